```python
import jax
import jax.numpy as jnp
from jax import lax
import numpy as np

D_MODEL = 1024
BATCH = 4
SEQ = 8192
DEPTH = 2
DEC_BATCH = 32
DEC_SEQ = 8
PAST_LEN = 16384
PAGE_SIZE = 128

F32 = jnp.float32
N_EVEN = (DEPTH + 1) // 2
N_ODD = DEPTH // 2
M_HEADS = 4
M_DK = 128
M_DV = 128
M_CHUNK = 64
NSA_HEADS = 8
NSA_GROUPS = 2
NSA_DH = 64
CMP_BLOCK = 32
SEL_BLOCK = 64
N_SEL = 16
WINDOW = 512
NSA_QB = 64
FOX_HEADS = 8
FOX_DH = 64
FOX_QB = 128
RWKV_HEADS = 8
RWKV_N = 64
W_LORA = 64
A_LORA = 64
G_LORA = 128
MEM_TOKENS = 256
MEM_HEADS = 4
MEM_DH = D_MODEL // MEM_HEADS
N_GROUPS = 4
EXP_PER_GROUP = 8
N_EXPERTS = N_GROUPS * EXP_PER_GROUP
TOP_K = 2
D_EXPERT = 256
MOE_BLOCK = 128

ROPE_THETA = 10000.0
ALPHA = (2.0 * DEPTH) ** 0.25
BETA = (8.0 * DEPTH) ** -0.25
LN_EPS = 1e-5
NEG = -1e30
FORCE = 1e4

MIX_A = M_HEADS * M_DV
MIX_B = NSA_HEADS * NSA_DH
MIX_C = FOX_HEADS * FOX_DH
MIX_D = RWKV_HEADS * RWKV_N
NSA_KV = NSA_GROUPS * NSA_DH
EVEN_SPLITS = (M_HEADS * M_DK, M_HEADS * M_DK, MIX_A, M_HEADS, M_HEADS, MIX_A,
               MIX_B, NSA_KV, NSA_KV, NSA_KV, NSA_KV, NSA_KV, NSA_KV, 3 * NSA_HEADS)
RWKV_SPLITS = (MIX_D, W_LORA, MIX_D, MIX_D, A_LORA, G_LORA)
RWKV_SHIFT = sum(RWKV_SPLITS)
ODD_SPLITS = (MIX_C, MIX_C, MIX_C, FOX_HEADS, RWKV_SHIFT)
EVEN_W = sum(EVEN_SPLITS)
ODD_W = sum(ODD_SPLITS)
EVEN_STATE = ('mlstm_C', 'mlstm_n', 'mlstm_m', 'nsa_kc', 'nsa_vc', 'nsa_ks', 'nsa_vs', 'nsa_kw', 'nsa_vw')
ODD_STATE = ('fox_k', 'fox_v', 'fox_logf', 'rwkv_S', 'rwkv_shift')

kernel_name = 'hybrid_mlstm_nsa_fox_rwkv7_hmoe_step'


def _split(a, sizes):
    return jnp.split(a, np.cumsum(sizes)[:-1].tolist(), axis=-1)


def _layer_norm(x, g, b):
    xf = x.astype(F32)
    mu = xf.mean(-1, keepdims=True)
    var = jnp.square(xf - mu).mean(-1, keepdims=True)
    return ((xf - mu) * lax.rsqrt(var + LN_EPS) * g + b).astype(x.dtype)


def _head_norm(x, g, b, eps):
    xf = x.astype(F32)
    mu = xf.mean(-1, keepdims=True)
    var = jnp.square(xf - mu).mean(-1, keepdims=True)
    y = (xf - mu) * lax.rsqrt(var + eps) * g
    return y if b is None else y + b


def _masked_softmax(s, mask):
    s = jnp.where(mask, s.astype(F32), NEG)
    e = jnp.where(mask, jnp.exp(s - s.max(-1, keepdims=True)), 0.0)
    return e / jnp.maximum(e.sum(-1, keepdims=True), 1e-30)


def _rope(x, pos):
    half = x.shape[-1] // 2
    freq = ROPE_THETA ** (-jnp.arange(half, dtype=F32) / half)
    ang = pos.astype(F32)[:, None] * freq[None, :]
    cos = jnp.cos(ang)[None, :, None, :]
    sin = jnp.sin(ang)[None, :, None, :]
    x1, x2 = x[..., :half], x[..., half:]
    return jnp.concatenate([x1 * cos - x2 * sin, x1 * sin + x2 * cos], -1).astype(x.dtype)


def _gather_pages(pool, page_table):
    g = pool[page_table]
    return g.reshape((page_table.shape[0], page_table.shape[1] * pool.shape[1]) + pool.shape[2:])


def _mlstm(q, k, v, ig, lf, C0, n0, m0):
    B, T, H, _ = q.shape
    L = M_CHUNK if T % M_CHUNK == 0 else T
    tri = jnp.tril(jnp.ones((L, L), bool))

    def chunks(a):
        a = a.astype(F32).reshape((B, T // L, L, H) + a.shape[3:])
        return jnp.moveaxis(jnp.moveaxis(a, 1, 0), 3, 2)

    def step(carry, inp):
        C, n, m = carry
        qc, kc, vc, ic, fc = inp
        b = jnp.cumsum(fc, -1)
        D = jnp.where(tri, b[..., :, None] - b[..., None, :] + ic[..., None, :], NEG)
        inter = b + m[..., None]
        mt = jnp.maximum(inter, D.max(-1))
        W = jnp.where(tri, jnp.exp(D - mt[..., None]), 0.0)
        a = jnp.exp(inter - mt)
        S = jnp.einsum('bhtk,bhsk->bhts', qc, kc) * W
        num = a[..., None] * jnp.einsum('bhvk,bhtk->bhtv', C, qc) + jnp.einsum('bhts,bhsv->bhtv', S, vc)
        den = a * jnp.einsum('bhk,bhtk->bht', n, qc) + S.sum(-1)
        h = num / jnp.maximum(jnp.abs(den), jnp.exp(-mt))[..., None]
        mL = mt[..., -1]
        gL = jnp.exp(b[..., -1:] - b + ic - mL[..., None])
        aL = jnp.exp(b[..., -1] + m - mL)
        C = aL[..., None, None] * C + jnp.einsum('bhs,bhsv,bhsk->bhvk', gL, vc, kc)
        n = aL[..., None] * n + jnp.einsum('bhs,bhsk->bhk', gL, kc)
        return (C, n, mL), h

    (C1, n1, m1), h = lax.scan(step, (C0.astype(F32), n0.astype(F32), m0.astype(F32)),
                               tuple(chunks(a) for a in (q, k, v, ig, lf)))
    h = jnp.moveaxis(jnp.moveaxis(h, 2, 3), 0, 1).reshape(B, T, H, -1)
    return h, (C1, n1, m1)


def _compress(kv, pe, w1, w2):
    B, L, G, d = kv.shape
    nc = L // CMP_BLOCK
    blk = kv[:, :nc * CMP_BLOCK].reshape(B, nc, CMP_BLOCK, G, d) + pe[None, None, :, None, :]
    blk = jnp.swapaxes(blk, 2, 3).reshape(B, nc, G, CMP_BLOCK * d)
    return jax.nn.gelu(blk @ w1) @ w2


def _nsa_branches(q, q0, kcmp, vcmp, ks, vs, kw_ext, vw_ext):
    B, T, H, dh = q.shape
    G = ks.shape[2]
    hpg = H // G
    nc = kcmp.shape[1]
    ns = ks.shape[1] // SEL_BLOCK
    ratio = SEL_BLOCK // CMP_BLOCK
    n_sel = min(N_SEL, ns)
    qb = NSA_QB if T % NSA_QB == 0 else T
    scale = dh ** -0.5
    cmp_end = jnp.arange(nc) * CMP_BLOCK + CMP_BLOCK - 1
    blk = jnp.arange(ns)
    offs = jnp.arange(SEL_BLOCK)
    ks_t = jnp.swapaxes(ks, 1, 2)
    vs_t = jnp.swapaxes(vs, 1, 2)
    bi = jnp.arange(B)[:, None, None]
    gi = jnp.arange(G)[None, :, None]

    def block(i):
        o = i * qb
        qq = lax.dynamic_slice_in_dim(q, o, qb, 1).reshape(B, qb, G, hpg, dh)
        qpos = q0 + o + jnp.arange(qb)
        s = jnp.einsum('bqghd,bcgd->bqghc', qq, kcmp) * scale
        p = _masked_softmax(s, (cmp_end[None, :] <= qpos[:, None])[None, :, None, None, :])
        o_cmp = jnp.einsum('bqghc,bcgd->bqghd', p, vcmp)
        imp = jnp.pad(p.sum(3), ((0, 0), (0, 0), (0, 0), (0, ns * ratio - nc)))
        imp = imp.reshape(B, qb, G, ns, ratio).sum(-1)
        cur = (qpos // SEL_BLOCK)[:, None]
        forced = (blk[None] == 0) | (blk[None] == cur) | (blk[None] == cur - 1)
        valid = (blk[None] * SEL_BLOCK) <= qpos[:, None]
        score = jnp.where(valid[None, :, None, :], imp + FORCE * forced[None, :, None, :], NEG)
        top_v, top_i = lax.top_k(score, n_sel)
        kpos = top_i[..., None] * SEL_BLOCK + offs
        ok = (top_v > 0.5 * NEG)[..., None] & (kpos <= qpos[None, :, None, None, None])
        idx = jnp.swapaxes(kpos, 1, 2).reshape(B, G, qb * n_sel * SEL_BLOCK)
        kg = ks_t[bi, gi, idx].reshape(B, G, qb, n_sel * SEL_BLOCK, dh)
        vg = vs_t[bi, gi, idx].reshape(B, G, qb, n_sel * SEL_BLOCK, dh)
        s = jnp.einsum('bqghd,bgqkd->bqghk', qq, kg) * scale
        p = _masked_softmax(s, ok.reshape(B, qb, G, 1, n_sel * SEL_BLOCK))
        o_slc = jnp.einsum('bqghk,bgqkd->bqghd', p, vg)
        kw = lax.dynamic_slice_in_dim(kw_ext, o, WINDOW + qb, 1)
        vw = lax.dynamic_slice_in_dim(vw_ext, o, WINDOW + qb, 1)
        wpos = q0 + o - WINDOW + jnp.arange(WINDOW + qb)
        wm = (wpos[None] <= qpos[:, None]) & (wpos[None] > qpos[:, None] - WINDOW) & (wpos[None] >= 0)
        s = jnp.einsum('bqghd,bkgd->bqghk', qq, kw) * scale
        p = _masked_softmax(s, wm[None, :, None, None, :])
        o_win = jnp.einsum('bqghk,bkgd->bqghd', p, vw)
        return o_cmp, o_slc, o_win

    outs = lax.map(block, jnp.arange(T // qb))
    return tuple(jnp.moveaxis(t, 0, 1).reshape(B, T, H, dh) for t in outs)


def _fox_attend(q, k, v, c, q0):
    B, T, H, dh = q.shape
    L = k.shape[1]
    qb = FOX_QB if T % FOX_QB == 0 else T
    kpos = jnp.arange(L)
    ck = jnp.swapaxes(c, 1, 2)
    scale = dh ** -0.5

    def block(i):
        o = i * qb
        qq = lax.dynamic_slice_in_dim(q, o, qb, 1)
        cq = lax.dynamic_slice_in_dim(ck, q0 + o, qb, 2)
        qpos = q0 + o + jnp.arange(qb)
        s = jnp.einsum('bqhd,bkhd->bhqk', qq, k).astype(F32) * scale + cq[..., :, None] - ck[..., None, :]
        p = _masked_softmax(s, kpos[None, :] <= qpos[:, None])
        return jnp.einsum('bhqk,bkhd->bqhd', p, v)

    out = lax.map(block, jnp.arange(T // qb))
    return jnp.moveaxis(out, 0, 1).reshape(B, T, H, dh)


def _rwkv7_scan(r, log_w, k, v, kk, a, S0):
    def step(S, inp):
        rt, wt, kt, vt, kkt, at = inp
        sa = jnp.einsum('bhvk,bhk->bhv', S, -kkt)
        S = S * jnp.exp(wt)[:, :, None, :] + sa[..., :, None] * (kkt * at)[..., None, :] + vt[..., :, None] * kt[..., None, :]
        return S, jnp.einsum('bhvk,bhk->bhv', S, rt)

    xs = tuple(jnp.moveaxis(t.astype(F32), 1, 0) for t in (r, log_w, k, v, kk, a))
    S1, y = lax.scan(step, S0.astype(F32), xs)
    return jnp.moveaxis(y, 0, 1), S1


def _even_mixer(x, q0, w_in, w_out, b_i, b_f, m_norm_g, pe_k, pe_v, ck1, ck2, cv1, cv2,
                C0, n0, m0, kc_past, vc_past, ks_past, vs_past, kw_buf, vw_buf):
    B, T, _ = x.shape
    (mq, mk, mv, mi, mf, mo, nq, nkc, nvc, nks, nvs, nkw, nvw, ng) = _split(x @ w_in, EVEN_SPLITS)
    q = mq.reshape(B, T, M_HEADS, M_DK)
    k = mk.reshape(B, T, M_HEADS, M_DK) * (M_DK ** -0.5)
    v = mv.reshape(B, T, M_HEADS, M_DV)
    ig = (mi + b_i).astype(F32)
    lf = jax.nn.log_sigmoid((mf + b_f).astype(F32))
    h, (C1, n1, m1) = _mlstm(q, k, v, ig, lf, C0, n0, m0)
    h = _head_norm(h, m_norm_g.reshape(M_HEADS, M_DV), None, 1e-6)
    h = h * jax.nn.sigmoid(mo.astype(F32)).reshape(B, T, M_HEADS, M_DV)
    pos = q0 + jnp.arange(T)
    shp = (B, T, NSA_GROUPS, NSA_DH)
    qn = _rope(nq.reshape(B, T, NSA_HEADS, NSA_DH), pos)
    kc = nkc.reshape(shp)
    vc = nvc.reshape(shp)
    ks = _rope(nks.reshape(shp), pos)
    vs = nvs.reshape(shp)
    kw = _rope(nkw.reshape(shp), pos)
    vw = nvw.reshape(shp)
    kc_f = jnp.concatenate([kc_past, kc], 1)
    vc_f = jnp.concatenate([vc_past, vc], 1)
    ks_f = jnp.concatenate([ks_past, ks], 1)
    vs_f = jnp.concatenate([vs_past, vs], 1)
    L = kc_f.shape[1]
    nc = L // CMP_BLOCK
    kcmp = _rope(_compress(kc_f, pe_k, ck1, ck2), jnp.arange(nc) * CMP_BLOCK + CMP_BLOCK - 1)
    vcmp = _compress(vc_f, pe_v, cv1, cv2)
    ns = -(-L // SEL_BLOCK)
    padw = ((0, 0), (0, ns * SEL_BLOCK - L), (0, 0), (0, 0))
    ks_p = jnp.pad(ks_f, padw)
    vs_p = jnp.pad(vs_f, padw)
    wb = kw_buf.shape[1]
    zpad = jnp.zeros((B, WINDOW - wb, NSA_GROUPS, NSA_DH), kw.dtype)
    kw_ext = jnp.concatenate([zpad, kw_buf.astype(kw.dtype), kw], 1)
    vw_ext = jnp.concatenate([zpad, vw_buf.astype(vw.dtype), vw], 1)
    o_cmp, o_slc, o_win = _nsa_branches(qn, q0, kcmp, vcmp, ks_p, vs_p, kw_ext, vw_ext)
    gate = jax.nn.sigmoid(ng.astype(F32)).reshape(B, T, NSA_HEADS, 3)
    o = gate[..., 0:1] * o_cmp + gate[..., 1:2] * o_slc + gate[..., 2:3] * o_win
    y = jnp.concatenate([h.reshape(B, T, MIX_A), o.reshape(B, T, MIX_B)], -1).astype(x.dtype) @ w_out
    keep = min(WINDOW, wb + T)
    kw_new = jnp.concatenate([kw_buf.astype(kw.dtype), kw], 1)[:, -keep:]
    vw_new = jnp.concatenate([vw_buf.astype(vw.dtype), vw], 1)[:, -keep:]
    return y, (C1, n1, m1, kc, vc, ks, vs, kw_new, vw_new)


def _odd_mixer(x, q0, w_in, w_out, fox_b_f, mu, w0, w2, a0, a2, g2, k_k, k_a, r_k, ln_g, ln_b,
               k_past, v_past, lf_past, S0, shift0):
    B, T, _ = x.shape
    fq, fk, fv, ff, rw = _split(x @ w_in, ODD_SPLITS)
    hs = (B, T, FOX_HEADS, FOX_DH)
    k = fk.reshape(hs)
    v = fv.reshape(hs)
    lf = jax.nn.log_sigmoid((ff + fox_b_f).astype(F32))
    k_f = jnp.concatenate([k_past, k], 1)
    v_f = jnp.concatenate([v_past, v], 1)
    c = jnp.cumsum(jnp.concatenate([lf_past.astype(F32), lf], 1), axis=1)
    o_c = _fox_attend(fq.reshape(hs), k_f, v_f, c, q0)
    prev = jnp.concatenate([shift0[:, None, :].astype(rw.dtype), rw[:, :-1]], axis=1)
    rw_m = (rw + mu * (prev - rw)).astype(F32)
    r, wl, kr, vr, al, gl = _split(rw_m, RWKV_SPLITS)
    w = -jax.nn.softplus(-(w0 + jnp.tanh(wl) @ w2)) - 0.5
    log_w = -jnp.exp(w)
    a = jax.nn.sigmoid(a0 + al @ a2)
    g = jax.nn.sigmoid(gl) @ g2
    hsr = (B, T, RWKV_HEADS, RWKV_N)
    kk = (kr * k_k).reshape(hsr)
    kk = kk / jnp.maximum(jnp.linalg.norm(kk, axis=-1, keepdims=True), 1e-12)
    kr = kr * (1.0 + (a - 1.0) * k_a)
    r_h, k_h, v_h, a_h = r.reshape(hsr), kr.reshape(hsr), vr.reshape(hsr), a.reshape(hsr)
    y, S1 = _rwkv7_scan(r_h, log_w.reshape(hsr), k_h, v_h, kk, a_h, S0)
    y = _head_norm(y, ln_g.reshape(RWKV_HEADS, RWKV_N), ln_b.reshape(RWKV_HEADS, RWKV_N), 64e-5)
    y = y + (r_h * k_h * r_k).sum(-1, keepdims=True) * v_h
    y = y.reshape(B, T, MIX_D) * g
    out = jnp.concatenate([o_c.reshape(B, T, MIX_C), y], -1).astype(x.dtype) @ w_out
    return out, (k, v, lf, S1, rw[:, -1])


def _mem_xattn(x, km, vm, wq, wo):
    B, T, _ = x.shape
    q = (x @ wq).reshape(B, T, MEM_HEADS, MEM_DH)
    s = jnp.einsum('bthd,bmhd->bhtm', q, km).astype(F32) * (MEM_DH ** -0.5)
    p = jax.nn.softmax(s, axis=-1)
    o = jnp.einsum('bhtm,bmhd->bthd', p, vm).reshape(B, T, D_MODEL)
    return o.astype(x.dtype) @ wo


def _hier_moe(x, w_group, b_group, w_expert, b_expert, w1, w3, w2):
    B, T, D = x.shape
    N = B * T
    xf = x.reshape(N, D)
    g_logit = (xf @ w_group + b_group).astype(F32)
    grp = jnp.argmax(g_logit, axis=-1)
    g_w = jnp.take_along_axis(jax.nn.softmax(g_logit, -1), grp[:, None], 1)[:, 0]
    e_logit = (xf @ w_expert + b_expert).astype(F32).reshape(N, N_GROUPS, EXP_PER_GROUP)
    e_logit = jnp.take_along_axis(e_logit, grp[:, None, None], 1)[:, 0]
    top_p, top_i = lax.top_k(jax.nn.softmax(e_logit, -1), TOP_K)
    gate = (g_w[:, None] * top_p / top_p.sum(-1, keepdims=True)).reshape(-1)
    eid = (grp[:, None] * EXP_PER_GROUP + top_i).reshape(-1)
    A = N * TOP_K
    tok = jnp.arange(A) // TOP_K
    order = jnp.argsort(eid)
    se, stok, sg = eid[order], tok[order], gate[order]
    counts = jnp.zeros((N_EXPERTS,), jnp.int32).at[eid].add(1)
    padded = (counts + MOE_BLOCK - 1) // MOE_BLOCK * MOE_BLOCK
    pad_end = jnp.cumsum(padded)
    start = jnp.cumsum(counts) - counts
    dest = (pad_end - padded)[se] + jnp.arange(A) - start[se]
    n_blk = -(-A // MOE_BLOCK) + N_EXPERTS
    slot_tok = jnp.full((n_blk * MOE_BLOCK,), N, jnp.int32).at[dest].set(stok)
    xs = jnp.concatenate([xf, jnp.zeros((1, D), xf.dtype)], 0)[slot_tok].reshape(n_blk, MOE_BLOCK, D)
    blk_e = jnp.minimum(jnp.searchsorted(pad_end, jnp.arange(n_blk) * MOE_BLOCK, side='right'), N_EXPERTS - 1)

    def expert_block(args):
        xb, e = args
        return (jax.nn.silu(xb @ w1[e]) * (xb @ w3[e])) @ w2[e]

    ys = lax.map(expert_block, (xs, blk_e)).reshape(n_blk * MOE_BLOCK, D)
    y = jnp.zeros((N, D), F32).at[stok].add(sg[:, None] * ys[dest].astype(F32))
    return y.astype(x.dtype).reshape(B, T, D)


def setup_inputs(seed: int = 0) -> dict:
    key = jax.random.key(seed)
    keys = iter(jax.random.split(key, 72))

    def nrm(shape, scale=1.0, shift=0.0):
        return shift + scale * jax.random.normal(next(keys), shape, F32)

    n_pages = PAST_LEN // PAGE_SIZE
    n_used = DEC_BATCH * n_pages
    pool = n_used + (n_used + 3) // 4
    wb = min(WINDOW, PAST_LEN)
    D = D_MODEL
    inp = {}
    inp['x_prompt'] = nrm((BATCH, SEQ, D))
    inp['x_sample'] = nrm((DEC_BATCH, DEC_SEQ, D))
    inp['state_mlstm_C'] = nrm((N_EVEN, DEC_BATCH, M_HEADS, M_DV, M_DK))
    inp['state_mlstm_n'] = nrm((N_EVEN, DEC_BATCH, M_HEADS, M_DK))
    inp['state_mlstm_m'] = nrm((N_EVEN, DEC_BATCH, M_HEADS))
    inp['cache_nsa_kc'] = nrm((N_EVEN, pool, PAGE_SIZE, NSA_GROUPS, NSA_DH))
    inp['cache_nsa_vc'] = nrm((N_EVEN, pool, PAGE_SIZE, NSA_GROUPS, NSA_DH))
    inp['cache_nsa_ks'] = nrm((N_EVEN, pool, PAGE_SIZE, NSA_GROUPS, NSA_DH))
    inp['cache_nsa_vs'] = nrm((N_EVEN, pool, PAGE_SIZE, NSA_GROUPS, NSA_DH))
    inp['cache_nsa_kw'] = nrm((N_EVEN, DEC_BATCH, wb, NSA_GROUPS, NSA_DH))
    inp['cache_nsa_vw'] = nrm((N_EVEN, DEC_BATCH, wb, NSA_GROUPS, NSA_DH))
    inp['cache_fox_k'] = nrm((N_ODD, pool, PAGE_SIZE, FOX_HEADS, FOX_DH))
    inp['cache_fox_v'] = nrm((N_ODD, pool, PAGE_SIZE, FOX_HEADS, FOX_DH))
    inp['cache_fox_logf'] = jax.nn.log_sigmoid(nrm((N_ODD, pool, PAGE_SIZE, FOX_HEADS), 0.5, 3.0))
    inp['state_rwkv_S'] = nrm((N_ODD, DEC_BATCH, RWKV_HEADS, RWKV_N, RWKV_N), 0.5)
    inp['state_rwkv_shift'] = nrm((N_ODD, DEC_BATCH, RWKV_SHIFT))
    inp['cache_mem_k'] = nrm((DEPTH, DEC_BATCH, MEM_TOKENS, MEM_HEADS, MEM_DH))
    inp['cache_mem_v'] = nrm((DEPTH, DEC_BATCH, MEM_TOKENS, MEM_HEADS, MEM_DH))
    inp['page_table'] = jax.random.permutation(next(keys), pool)[:n_used].reshape(DEC_BATCH, n_pages).astype(jnp.int32)
    inp['mem_prompt'] = nrm((BATCH, MEM_TOKENS, D))
    inp['even_w_in'] = nrm((N_EVEN, D, EVEN_W), D ** -0.5)
    inp['even_w_out'] = nrm((N_EVEN, MIX_A + MIX_B, D), BETA * (MIX_A + MIX_B) ** -0.5)
    inp['mlstm_b_i'] = nrm((N_EVEN, M_HEADS), 0.5)
    inp['mlstm_b_f'] = nrm((N_EVEN, M_HEADS), 0.5, 4.0)
    inp['mlstm_norm_g'] = nrm((N_EVEN, MIX_A), 0.02, 1.0)
    inp['nsa_pe_k'] = nrm((N_EVEN, CMP_BLOCK, NSA_DH), 0.1)
    inp['nsa_pe_v'] = nrm((N_EVEN, CMP_BLOCK, NSA_DH), 0.1)
    inp['nsa_cmp_k_w1'] = nrm((N_EVEN, CMP_BLOCK * NSA_DH, NSA_DH), (CMP_BLOCK * NSA_DH) ** -0.5)
    inp['nsa_cmp_k_w2'] = nrm((N_EVEN, NSA_DH, NSA_DH), NSA_DH ** -0.5)
    inp['nsa_cmp_v_w1'] = nrm((N_EVEN, CMP_BLOCK * NSA_DH, NSA_DH), (CMP_BLOCK * NSA_DH) ** -0.5)
    inp['nsa_cmp_v_w2'] = nrm((N_EVEN, NSA_DH, NSA_DH), NSA_DH ** -0.5)
    inp['odd_w_in'] = nrm((N_ODD, D, ODD_W), D ** -0.5)
    inp['odd_w_out'] = nrm((N_ODD, MIX_C + MIX_D, D), BETA * (MIX_C + MIX_D) ** -0.5)
    inp['fox_b_f'] = nrm((N_ODD, FOX_HEADS), 0.5, 3.0)
    inp['rwkv_mu'] = jax.random.uniform(next(keys), (N_ODD, RWKV_SHIFT), F32)
    inp['rwkv_w0'] = nrm((N_ODD, MIX_D), 0.5)
    inp['rwkv_w2'] = nrm((N_ODD, W_LORA, MIX_D), W_LORA ** -0.5)
    inp['rwkv_a0'] = nrm((N_ODD, MIX_D), 0.1)
    inp['rwkv_a2'] = nrm((N_ODD, A_LORA, MIX_D), A_LORA ** -0.5)
    inp['rwkv_g2'] = nrm((N_ODD, G_LORA, MIX_D), G_LORA ** -0.5)
    inp['rwkv_k_k'] = nrm((N_ODD, MIX_D), 0.1, 0.85)
    inp['rwkv_k_a'] = nrm((N_ODD, MIX_D), 0.1, 1.0)
    inp['rwkv_r_k'] = nrm((N_ODD, RWKV_HEADS, RWKV_N), 0.1)
    inp['rwkv_ln_g'] = nrm((N_ODD, MIX_D), 0.02, 1.0)
    inp['rwkv_ln_b'] = nrm((N_ODD, MIX_D), 0.02)
    inp['mem_wq'] = nrm((DEPTH, D, D), D ** -0.5)
    inp['mem_wk'] = nrm((DEPTH, D, D), D ** -0.5)
    inp['mem_wv'] = nrm((DEPTH, D, D), BETA * D ** -0.5)
    inp['mem_wo'] = nrm((DEPTH, D, D), BETA * D ** -0.5)
    inp['moe_w_group'] = nrm((DEPTH, D, N_GROUPS), D ** -0.5)
    inp['moe_b_group'] = nrm((DEPTH, N_GROUPS), 0.01)
    inp['moe_w_expert'] = nrm((DEPTH, D, N_EXPERTS), D ** -0.5)
    inp['moe_b_expert'] = nrm((DEPTH, N_EXPERTS), 0.01)
    inp['moe_w1'] = nrm((DEPTH, N_EXPERTS, D, D_EXPERT), D ** -0.5)
    inp['moe_w3'] = nrm((DEPTH, N_EXPERTS, D, D_EXPERT), D ** -0.5)
    inp['moe_w2'] = nrm((DEPTH, N_EXPERTS, D_EXPERT, D), BETA * D_EXPERT ** -0.5)
    inp['ln_g'] = nrm((DEPTH, 3, D), 0.02, 1.0)
    inp['ln_b'] = nrm((DEPTH, 3, D), 0.02)
    return inp


def reference(x_prompt, x_sample, state_mlstm_C, state_mlstm_n, state_mlstm_m,
              cache_nsa_kc, cache_nsa_vc, cache_nsa_ks, cache_nsa_vs, cache_nsa_kw, cache_nsa_vw,
              cache_fox_k, cache_fox_v, cache_fox_logf, state_rwkv_S, state_rwkv_shift,
              cache_mem_k, cache_mem_v, page_table, mem_prompt,
              even_w_in, even_w_out, mlstm_b_i, mlstm_b_f, mlstm_norm_g,
              nsa_pe_k, nsa_pe_v, nsa_cmp_k_w1, nsa_cmp_k_w2, nsa_cmp_v_w1, nsa_cmp_v_w2,
              odd_w_in, odd_w_out, fox_b_f, rwkv_mu, rwkv_w0, rwkv_w2, rwkv_a0, rwkv_a2, rwkv_g2,
              rwkv_k_k, rwkv_k_a, rwkv_r_k, rwkv_ln_g, rwkv_ln_b,
              mem_wq, mem_wk, mem_wv, mem_wo,
              moe_w_group, moe_b_group, moe_w_expert, moe_b_expert, moe_w1, moe_w3, moe_w2,
              ln_g, ln_b):
    bp = x_prompt.shape[0]
    past = page_table.shape[1] * cache_nsa_kc.shape[2]
    xp, xs = x_prompt, x_sample
    acc = {}

    def push(prefix, names, vals):
        for nm, val in zip(names, vals):
            acc.setdefault(prefix + nm, []).append(val)

    for layer in range(DEPTH):
        e = layer // 2
        if layer % 2 == 0:
            w = (even_w_in[e], even_w_out[e], mlstm_b_i[e], mlstm_b_f[e], mlstm_norm_g[e],
                 nsa_pe_k[e], nsa_pe_v[e], nsa_cmp_k_w1[e], nsa_cmp_k_w2[e], nsa_cmp_v_w1[e], nsa_cmp_v_w2[e])
            e0 = jnp.zeros((bp, 0, NSA_GROUPS, NSA_DH), xp.dtype)
            yp, sp = _even_mixer(xp, 0, *w, jnp.zeros((bp, M_HEADS, M_DV, M_DK), F32),
                                 jnp.zeros((bp, M_HEADS, M_DK), F32), jnp.zeros((bp, M_HEADS), F32),
                                 e0, e0, e0, e0, e0, e0)
            ys, ss = _even_mixer(xs, past, *w, state_mlstm_C[e], state_mlstm_n[e], state_mlstm_m[e],
                                 _gather_pages(cache_nsa_kc[e], page_table), _gather_pages(cache_nsa_vc[e], page_table),
                                 _gather_pages(cache_nsa_ks[e], page_table), _gather_pages(cache_nsa_vs[e], page_table),
                                 cache_nsa_kw[e], cache_nsa_vw[e])
            names = EVEN_STATE
        else:
            w = (odd_w_in[e], odd_w_out[e], fox_b_f[e], rwkv_mu[e], rwkv_w0[e], rwkv_w2[e], rwkv_a0[e],
                 rwkv_a2[e], rwkv_g2[e], rwkv_k_k[e], rwkv_k_a[e], rwkv_r_k[e], rwkv_ln_g[e], rwkv_ln_b[e])
            e0 = jnp.zeros((bp, 0, FOX_HEADS, FOX_DH), xp.dtype)
            yp, sp = _odd_mixer(xp, 0, *w, e0, e0, jnp.zeros((bp, 0, FOX_HEADS), F32),
                                jnp.zeros((bp, RWKV_HEADS, RWKV_N, RWKV_N), F32),
                                jnp.zeros((bp, RWKV_SHIFT), xp.dtype))
            ys, ss = _odd_mixer(xs, past, *w, _gather_pages(cache_fox_k[e], page_table),
                                _gather_pages(cache_fox_v[e], page_table), _gather_pages(cache_fox_logf[e], page_table),
                                state_rwkv_S[e], state_rwkv_shift[e])
            names = ODD_STATE
        push('p_', names, sp)
        push('s_', names, ss)
        xp = _layer_norm(ALPHA * xp + yp, ln_g[layer, 0], ln_b[layer, 0])
        xs = _layer_norm(ALPHA * xs + ys, ln_g[layer, 0], ln_b[layer, 0])
        kmp = (mem_prompt @ mem_wk[layer]).reshape(bp, -1, MEM_HEADS, MEM_DH)
        vmp = (mem_prompt @ mem_wv[layer]).reshape(bp, -1, MEM_HEADS, MEM_DH)
        push('p_', ('mem_k', 'mem_v'), (kmp, vmp))
        xp = _layer_norm(ALPHA * xp + _mem_xattn(xp, kmp, vmp, mem_wq[layer], mem_wo[layer]), ln_g[layer, 1], ln_b[layer, 1])
        xs = _layer_norm(ALPHA * xs + _mem_xattn(xs, cache_mem_k[layer], cache_mem_v[layer], mem_wq[layer], mem_wo[layer]),
                         ln_g[layer, 1], ln_b[layer, 1])
        mw = (moe_w_group[layer], moe_b_group[layer], moe_w_expert[layer], moe_b_expert[layer],
              moe_w1[layer], moe_w3[layer], moe_w2[layer])
        xp = _layer_norm(ALPHA * xp + _hier_moe(xp, *mw), ln_g[layer, 2], ln_b[layer, 2])
        xs = _layer_norm(ALPHA * xs + _hier_moe(xs, *mw), ln_g[layer, 2], ln_b[layer, 2])

    st = {name: jnp.stack(vals) for name, vals in acc.items()}
    return (xp, xs,
            st['p_mlstm_C'], st['p_mlstm_n'], st['p_mlstm_m'],
            st['p_nsa_kc'], st['p_nsa_vc'], st['p_nsa_ks'], st['p_nsa_vs'], st['p_nsa_kw'], st['p_nsa_vw'],
            st['p_fox_k'], st['p_fox_v'], st['p_fox_logf'], st['p_rwkv_S'], st['p_rwkv_shift'],
            st['p_mem_k'], st['p_mem_v'],
            st['s_mlstm_C'], st['s_mlstm_n'], st['s_mlstm_m'],
            st['s_nsa_kc'], st['s_nsa_vc'], st['s_nsa_ks'], st['s_nsa_vs'], st['s_nsa_kw'], st['s_nsa_vw'],
            st['s_fox_k'], st['s_fox_v'], st['s_fox_logf'], st['s_rwkv_S'], st['s_rwkv_shift'])
```

```python
import functools

import numpy as np
import jax
import jax.numpy as jnp
from jax import lax
from jax.experimental import pallas as pl
from jax.experimental.pallas import tpu as pltpu

F32 = jnp.float32
BF16 = jnp.bfloat16
HI = lax.Precision.HIGHEST

D_MODEL = 1024
DEPTH = 2
PAGE = 128
M_HEADS, M_DK, M_DV, M_CHUNK = 4, 128, 128, 64
NSA_HEADS, NSA_GROUPS, NSA_DH = 8, 2, 64
CMP_BLOCK, SEL_BLOCK, N_SEL, WINDOW = 32, 64, 16, 512
FOX_HEADS, FOX_DH = 8, 64
RWKV_HEADS, RWKV_N, W_LORA, A_LORA, G_LORA = 8, 64, 64, 64, 128
RWKV_CHUNK = 64
MEM_HEADS = 4
MEM_DH = D_MODEL // MEM_HEADS
N_GROUPS, EXP_PER_GROUP, TOP_K, D_EXPERT = 4, 8, 2, 256
N_EXPERTS = N_GROUPS * EXP_PER_GROUP
ROPE_THETA = 10000.0
ALPHA = (2.0 * DEPTH) ** 0.25
LN_EPS = 1e-5
NEG = -1e30
FORCE = 1e4
MIX = 512
RWKV_SHIFT = 1792

LANES = 128
VMEM_LIMIT = 48 * 1024 * 1024


def _cp(*sem):
    return pltpu.CompilerParams(dimension_semantics=sem, vmem_limit_bytes=VMEM_LIMIT)


def _dot(a, b, prec=None):
    return jnp.dot(a, b, preferred_element_type=F32, precision=prec)


def _dot_nt(a, b, prec=None):
    return lax.dot_general(a, b, (((1,), (1,)), ((), ())), preferred_element_type=F32, precision=prec)


def _dot_tn(a, b, prec=None):
    return lax.dot_general(a, b, (((0,), (0,)), ((), ())), preferred_element_type=F32, precision=prec)


def _iota(shape, dim):
    return lax.broadcasted_iota(jnp.int32, shape, dim)


def _log_sigmoid(x):
    return jnp.minimum(x, 0.0) - jnp.log1p(jnp.exp(-jnp.abs(x)))


def _sigmoid(x):
    return 1.0 / (1.0 + jnp.exp(-x))


def _softplus(x):
    return jnp.maximum(x, 0.0) + jnp.log1p(jnp.exp(-jnp.abs(x)))


def _layer_norm_rows(z, g, b):
    mu = jnp.mean(z, axis=-1, keepdims=True)
    zc = z - mu
    var = jnp.mean(zc * zc, axis=-1, keepdims=True)
    return zc * lax.rsqrt(var + LN_EPS) * g + b


def _rope128(y, cos, sin_signed):
    lane = _iota(y.shape, 1)
    lo = (lane & 63) < 32
    sw = jnp.where(lo, pltpu.roll(y, 96, 1), pltpu.roll(y, 32, 1))
    return y * cos + sw * sin_signed


def _rope_tables(pos):
    half = NSA_DH // 2
    freq = ROPE_THETA ** (-jnp.arange(half, dtype=F32) / half)
    ang = pos.astype(F32)[:, None] * freq[None, :]
    cos, sin = jnp.cos(ang), jnp.sin(ang)
    return jnp.tile(cos, (1, 4)), jnp.tile(jnp.concatenate([-sin, sin], -1), (1, 2))


def _proj_kernel(*refs, segs, use_rope):
    if use_rope:
        x_ref, w_ref, cos_ref, sin_ref = refs[:4]
        outs = refs[4:]
    else:
        x_ref, w_ref = refs[:2]
        outs = refs[2:]
    xb = x_ref[...].astype(BF16)
    off = 0
    for (wd, rp), o_ref in zip(segs, outs):
        y = _dot(xb, w_ref[:, off:off + wd])
        if rp:
            for c in range(wd // LANES):
                o_ref[:, c * LANES:(c + 1) * LANES] = _rope128(y[:, c * LANES:(c + 1) * LANES], cos_ref[...], sin_ref[...])
        else:
            o_ref[...] = y
        off += wd


def _proj_split(x, w, segs, rope=None):
    n, k = x.shape
    tm = min(256, n)
    wtot = sum(s[0] for s in segs)
    use_rope = rope is not None
    in_specs = [pl.BlockSpec((tm, k), lambda i: (i, 0)), pl.BlockSpec((k, wtot), lambda i: (0, 0))]
    args = [x, w]
    if use_rope:
        in_specs += [pl.BlockSpec((tm, LANES), lambda i: (i, 0))] * 2
        args += list(rope)
    return pl.pallas_call(
        functools.partial(_proj_kernel, segs=tuple(segs), use_rope=use_rope),
        out_shape=[jax.ShapeDtypeStruct((n, wd), F32) for wd, _ in segs],
        grid=(n // tm,),
        in_specs=in_specs,
        out_specs=[pl.BlockSpec((tm, wd), lambda i: (i, 0)) for wd, _ in segs],
        compiler_params=_cp("parallel"),
    )(*args)


def _mm_res_ln_kernel(*refs, n_in):
    a_refs = refs[:n_in]
    w_refs = refs[n_in:2 * n_in]
    x_ref, g_ref, b_ref, o_ref = refs[2 * n_in:]
    y = _dot(a_refs[0][...].astype(BF16), w_refs[0][...])
    for a_ref, w_ref in zip(a_refs[1:], w_refs[1:]):
        y = y + _dot(a_ref[...].astype(BF16), w_ref[...])
    o_ref[...] = _layer_norm_rows(ALPHA * x_ref[...] + y, g_ref[...], b_ref[...])


def _mm_res_ln(a_list, w_list, x, g, b):
    n, d = x.shape
    tm = min(512, n)
    n_in = len(a_list)
    in_specs = ([pl.BlockSpec((tm, a.shape[1]), lambda i: (i, 0)) for a in a_list]
                + [pl.BlockSpec(w.shape, lambda i: (0, 0)) for w in w_list]
                + [pl.BlockSpec((tm, d), lambda i: (i, 0)), pl.BlockSpec((1, d), lambda i: (0, 0)),
                   pl.BlockSpec((1, d), lambda i: (0, 0))])
    return pl.pallas_call(
        functools.partial(_mm_res_ln_kernel, n_in=n_in),
        out_shape=jax.ShapeDtypeStruct((n, d), F32),
        grid=(n // tm,),
        in_specs=in_specs,
        out_specs=pl.BlockSpec((tm, d), lambda i: (i, 0)),
        compiler_params=_cp("parallel"),
    )(*a_list, *w_list, x, g.reshape(1, d), b.reshape(1, d))


def _mlstm_kernel(q_ref, k_ref, v_ref, o_ref, sm_ref, bias_ref, g_ref, c0_ref, n0_ref, m0_ref,
                  h_ref, c1_ref, n1_ref, m1_ref, c_scr, n_scr, m_scr, *, L):
    ci = pl.program_id(1)

    @pl.when(ci == 0)
    def _():
        c_scr[...] = c0_ref[...]
        n_scr[...] = n0_ref[...]
        m_scr[...] = m0_ref[...]

    sm = sm_ref[...] + bias_ref[...]
    lane = _iota(sm.shape, 1)
    gates = jnp.where(lane < M_HEADS, sm, _log_sigmoid(sm))
    eye8 = (_iota((8, LANES), 0) == _iota((8, LANES), 1)).astype(F32)
    gates_row = _dot_nt(eye8, gates, HI)
    r_i = _iota((L, L), 0)
    c_i = _iota((L, L), 1)
    tri = c_i <= r_i
    scale = M_DK ** -0.5
    for h in range(M_HEADS):
        ic_col = gates[:, h:h + 1]
        f_col = gates[:, M_HEADS + h:M_HEADS + h + 1]
        ic_row = gates_row[h:h + 1, :]
        f_row = gates_row[M_HEADS + h:M_HEADS + h + 1, :]
        b_col = jnp.sum(jnp.where(tri, jnp.broadcast_to(f_row, (L, L)), 0.0), axis=1, keepdims=True)
        b_row = jnp.sum(jnp.where(r_i <= c_i, jnp.broadcast_to(f_col, (L, L)), 0.0), axis=0, keepdims=True)
        m_prev = m_scr[h:h + 1, 0:1]
        dmat = jnp.where(tri, b_col - b_row + ic_row, NEG)
        inter = b_col + m_prev
        mt = jnp.maximum(inter, jnp.max(dmat, axis=1, keepdims=True))
        wmat = jnp.where(tri, jnp.exp(dmat - mt), 0.0)
        a = jnp.exp(inter - mt)
        qh = q_ref[:, h * M_DK:(h + 1) * M_DK]
        kh = k_ref[:, h * M_DK:(h + 1) * M_DK] * scale
        vh = v_ref[:, h * M_DV:(h + 1) * M_DV]
        qb, kb, vb = qh.astype(BF16), kh.astype(BF16), vh.astype(BF16)
        c_old = c_scr[h]
        n_old = n_scr[h:h + 1, :]
        s = _dot_nt(qb, kb) * wmat
        num = a * _dot_nt(qb, c_old.astype(BF16)) + _dot(s.astype(BF16), vb)
        den = a * jnp.sum(qh * n_old, axis=1, keepdims=True) + jnp.sum(s, axis=1, keepdims=True)
        hh = num / jnp.maximum(jnp.abs(den), jnp.exp(-mt))
        m_l = mt[L - 1:L, :]
        b_l = b_col[L - 1:L, :]
        g_l = jnp.exp(b_l - b_col + ic_col - m_l)
        a_l = jnp.exp(b_l + m_prev - m_l)
        c_scr[h] = a_l * c_old + _dot_tn((g_l * vh).astype(BF16), kb)
        n_scr[h:h + 1, :] = a_l * n_old + jnp.sum(g_l * kh, axis=0, keepdims=True)
        m_scr[h:h + 1, :] = jnp.broadcast_to(m_l, (1, LANES))
        mu = jnp.mean(hh, axis=1, keepdims=True)
        hc = hh - mu
        var = jnp.mean(hc * hc, axis=1, keepdims=True)
        hn = hc * lax.rsqrt(var + 1e-6) * g_ref[:, h * M_DV:(h + 1) * M_DV]
        h_ref[:, h * M_DV:(h + 1) * M_DV] = hn * _sigmoid(o_ref[:, h * M_DV:(h + 1) * M_DV])

    @pl.when(ci == pl.num_programs(1) - 1)
    def _():
        c1_ref[...] = c_scr[...]
        n1_ref[...] = n_scr[...]
        m1_ref[...] = m_scr[...]


def _mlstm(mq, mk, mv, mo, small, bias_row, norm_g, c0, n0, m0, B, T):
    L = M_CHUNK if T % M_CHUNK == 0 else T
    nch = T // L
    row = lambda b, c: (b * nch + c, 0)
    st4 = lambda b, c: (b, 0, 0, 0)
    st3 = lambda b, c: (b, 0, 0)
    m0b = jnp.broadcast_to(m0[:, :, None], (B, M_HEADS, LANES)).astype(F32)
    h, c1, n1, m1 = pl.pallas_call(
        functools.partial(_mlstm_kernel, L=L),
        out_shape=[jax.ShapeDtypeStruct((B * T, MIX), F32),
                   jax.ShapeDtypeStruct((B, M_HEADS, M_DV, M_DK), F32),
                   jax.ShapeDtypeStruct((B, M_HEADS, M_DK), F32),
                   jax.ShapeDtypeStruct((B, M_HEADS, LANES), F32)],
        grid=(B, nch),
        in_specs=[pl.BlockSpec((L, MIX), row)] * 4
        + [pl.BlockSpec((L, LANES), row), pl.BlockSpec((1, LANES), lambda b, c: (0, 0)),
           pl.BlockSpec((1, MIX), lambda b, c: (0, 0)),
           pl.BlockSpec((None, M_HEADS, M_DV, M_DK), st4), pl.BlockSpec((None, M_HEADS, M_DK), st3),
           pl.BlockSpec((None, M_HEADS, LANES), st3)],
        out_specs=[pl.BlockSpec((L, MIX), row), pl.BlockSpec((None, M_HEADS, M_DV, M_DK), st4),
                   pl.BlockSpec((None, M_HEADS, M_DK), st3), pl.BlockSpec((None, M_HEADS, LANES), st3)],
        scratch_shapes=[pltpu.VMEM((M_HEADS, M_DV, M_DK), F32), pltpu.VMEM((M_HEADS, M_DK), F32),
                        pltpu.VMEM((M_HEADS, LANES), F32)],
        compiler_params=_cp("parallel", "arbitrary"),
    )(mq, mk, mv, mo, small, bias_row, norm_g.reshape(1, MIX), c0.astype(F32), n0.astype(F32), m0b)
    return h, c1, n1, m1[:, :, 0]


def _compress_kernel(x_ref, pe_ref, w1_ref, w2_ref, o_ref):
    xb = (x_ref[...] + pe_ref[...]).astype(BF16)
    hid = jax.nn.gelu(_dot(xb, w1_ref[...]))
    o_ref[...] = _dot(hid.astype(BF16), w2_ref[...])


def _compress(blocks, pe, w1, w2):
    r = blocks.shape[0]
    tb = min(256, r)
    width = CMP_BLOCK * LANES
    eye = jnp.eye(NSA_GROUPS, dtype=F32)
    w1b = jnp.einsum('jdo,gh->jgdho', w1.reshape(CMP_BLOCK, NSA_DH, NSA_DH), eye).reshape(width, LANES).astype(BF16)
    w2b = jnp.einsum('do,gh->gdho', w2, eye).reshape(LANES, LANES).astype(BF16)
    pe_flat = jnp.broadcast_to(pe[:, None, :], (CMP_BLOCK, NSA_GROUPS, NSA_DH)).reshape(1, width)
    return pl.pallas_call(
        _compress_kernel,
        out_shape=jax.ShapeDtypeStruct((r, LANES), F32),
        grid=(pl.cdiv(r, tb),),
        in_specs=[pl.BlockSpec((tb, width), lambda i: (i, 0)), pl.BlockSpec((1, width), lambda i: (0, 0)),
                  pl.BlockSpec((width, LANES), lambda i: (0, 0)), pl.BlockSpec((LANES, LANES), lambda i: (0, 0))],
        out_specs=pl.BlockSpec((tb, LANES), lambda i: (i, 0)),
        compiler_params=_cp("parallel"),
    )(blocks, pe_flat, w1b, w2b)


def _rope_rows_kernel(x_ref, cos_ref, sin_ref, o_ref):
    o_ref[...] = _rope128(x_ref[...], cos_ref[...], sin_ref[...])


def _rope_rows(x, cos, sin):
    r = x.shape[0]
    tb = min(512, r)
    spec = pl.BlockSpec((tb, LANES), lambda i: (i, 0))
    return pl.pallas_call(_rope_rows_kernel, out_shape=jax.ShapeDtypeStruct((r, LANES), F32), grid=(r // tb,),
                          in_specs=[spec] * 3, out_specs=spec, compiler_params=_cp("parallel"))(x, cos, sin)


def _stack_heads(qblk, heads, scale):
    rows = []
    for src_blk, src_half, dst_half in heads:
        x = qblk[:, src_blk * LANES:(src_blk + 1) * LANES] * scale
        if src_half != dst_half:
            x = pltpu.roll(x, 64, 1)
        lane = _iota(x.shape, 1)
        keep = (lane >= 64) if dst_half else (lane < 64)
        rows.append(jnp.where(keep, x, 0.0))
    return jnp.concatenate(rows, axis=0).astype(BF16)


def _unstack_heads(o_full_list, heads_per_kb, tq, n_out_blocks):
    outs = [None] * n_out_blocks
    for o_full, heads in zip(o_full_list, heads_per_kb):
        for idx, (src_blk, src_half, dst_half) in enumerate(heads):
            piece = o_full[idx * tq:(idx + 1) * tq]
            if src_half != dst_half:
                piece = pltpu.roll(piece, 64, 1)
            lane = _iota(piece.shape, 1)
            keep = (lane >= 64) if src_half else (lane < 64)
            prev = outs[src_blk]
            outs[src_blk] = jnp.where(keep, piece, 0.0 if prev is None else prev)
    return outs


NSA_HEAD_MAP = [[(h // 2, h % 2, h // (NSA_HEADS // NSA_GROUPS)) for h in range(NSA_HEADS)]]
FOX_HEAD_MAP = [[(kb, 0, 0), (kb, 1, 1)] for kb in range(FOX_HEADS // 2)]


def _cmp_sel_kernel(q_ref, kc_ref, vc_ref, cend_ref, o_ref, sel_ref, *, tq, q0, nc, ns, ns_pad):
    qi = pl.program_id(1)
    heads = NSA_HEAD_MAP[0]
    hpg = NSA_HEADS // NSA_GROUPS
    qbd = _stack_heads(q_ref[...], heads, NSA_DH ** -0.5)
    s = _dot_nt(qbd, kc_ref[...].astype(BF16))
    rows = NSA_HEADS * tq
    qpos = q0 + qi * tq + (_iota((rows, 1), 0) & (tq - 1))
    mask = cend_ref[...] <= qpos
    s = jnp.where(mask, s, NEG)
    e = jnp.where(mask, jnp.exp(s - jnp.max(s, axis=1, keepdims=True)), 0.0)
    p = e / jnp.maximum(jnp.sum(e, axis=1, keepdims=True), 1e-30)
    o_full = _dot(p.astype(BF16), vc_ref[...].astype(BF16))
    outs = _unstack_heads([o_full], NSA_HEAD_MAP, tq, MIX // LANES)
    for c, blk in enumerate(outs):
        o_ref[:, c * LANES:(c + 1) * LANES] = blk
    half = nc // 2
    qp = q0 + qi * tq + _iota((tq, 1), 0)
    blk_i = _iota((tq, ns_pad), 1)
    blk_f = blk_i.astype(F32)
    cur = qp >> 6
    forced = (blk_i == 0) | (blk_i == cur) | (blk_i == cur - 1)
    valid = (blk_i * SEL_BLOCK <= qp) & (blk_i < ns)
    n_sel = min(N_SEL, ns)
    for g in range(NSA_GROUPS):
        pg = p[g * hpg * tq:(g * hpg + 1) * tq]
        for j in range(1, hpg):
            pg = pg + p[(g * hpg + j) * tq:(g * hpg + j + 1) * tq]
        imp = pg[:, :half] + pg[:, half:]
        if ns_pad > half:
            imp = jnp.concatenate([imp, jnp.zeros((tq, ns_pad - half), F32)], axis=1)
        score = jnp.where(valid, imp + FORCE * forced.astype(F32), NEG)
        score = jnp.where(blk_i < ns, score, -jnp.inf)
        chosen = jnp.zeros((tq, ns_pad), F32)
        for _ in range(n_sel):
            mx = jnp.max(score, axis=1, keepdims=True)
            first = jnp.min(jnp.where(score == mx, blk_f, 1e9), axis=1, keepdims=True)
            hit = blk_f == first
            chosen = jnp.where(hit, 1.0, chosen)
            score = jnp.where(hit, -jnp.inf, score)
        sel_ref[:, g * ns_pad:(g + 1) * ns_pad] = jnp.where(valid, chosen, 0.0)


def _cmp_sel(q3, kcmp, vcmp, cend, q0, ns, ns_pad):
    B, T, _ = q3.shape
    nc = kcmp.shape[1]
    tq = min(64, T)
    return pl.pallas_call(
        functools.partial(_cmp_sel_kernel, tq=tq, q0=q0, nc=nc, ns=ns, ns_pad=ns_pad),
        out_shape=[jax.ShapeDtypeStruct((B, T, MIX), F32), jax.ShapeDtypeStruct((B, T, NSA_GROUPS * ns_pad), F32)],
        grid=(B, T // tq),
        in_specs=[pl.BlockSpec((None, tq, MIX), lambda b, i: (b, i, 0)),
                  pl.BlockSpec((None, nc, LANES), lambda b, i: (b, 0, 0)),
                  pl.BlockSpec((None, nc, LANES), lambda b, i: (b, 0, 0)),
                  pl.BlockSpec((1, nc), lambda b, i: (0, 0))],
        out_specs=[pl.BlockSpec((None, tq, MIX), lambda b, i: (b, i, 0)),
                   pl.BlockSpec((None, tq, NSA_GROUPS * ns_pad), lambda b, i: (b, i, 0))],
        compiler_params=_cp("parallel", "arbitrary"),
    )(q3, kcmp, vcmp, cend)


def _flash_kernel(*refs, cfg):
    (mode, head_map, tq, tk, nkb, q0, kbase, pps, has_tail, tail_pos0, has_bias, ns_pad, n_pref, qw) = cfg
    qi_ref, kj_ref, fl_ref = refs[:3]
    pos = n_pref
    q_ref = refs[pos]; pos += 1
    npg = max(pps, 1)
    k_refs = refs[pos:pos + npg]; pos += npg
    v_refs = refs[pos:pos + npg]; pos += npg
    if has_tail:
        tk_ref, tv_ref = refs[pos:pos + 2]; pos += 2
    if has_bias:
        c_ref = refs[pos]; pos += 1
        if has_tail:
            ct_ref = refs[pos]; pos += 1
    if mode == 'sel':
        sel_ref = refs[pos]; pos += 1
    o_ref = refs[pos]; pos += 1
    qbd_scr, m_scr, l_scr, acc_scr = refs[pos:pos + 4]

    s_id = pl.program_id(1)
    qi = qi_ref[s_id]
    kj = kj_ref[s_id]
    fl = fl_ref[s_id]
    rows = len(head_map[0]) * tq

    @pl.when((fl & 1) != 0)
    def _init():
        qblk = q_ref[...]
        for kb in range(nkb):
            qbd_scr[kb] = _stack_heads(qblk, head_map[kb], NSA_DH ** -0.5)
        m_scr[...] = jnp.full(m_scr.shape, NEG, F32)
        l_scr[...] = jnp.zeros(l_scr.shape, F32)
        acc_scr[...] = jnp.zeros(acc_scr.shape, F32)

    def update(k_tile, v_tile, kpos0, c_tile, width):
        qpos = q0 + qi * tq + (_iota((rows, 1), 0) & (tq - 1))
        kpos = kpos0 + _iota((1, width), 1)
        mask = kpos <= qpos
        if mode == 'band':
            mask = mask & (kpos > qpos - WINDOW)
        if mode == 'sel':
            selblk = sel_ref[...]
            blk0 = kpos0 >> 6
            expand = (_iota((ns_pad, width), 0) == blk0 + (_iota((ns_pad, width), 1) >> 6)).astype(BF16)
            parts = []
            for g in range(NSA_GROUPS):
                mg = _dot(selblk[:, g * ns_pad:(g + 1) * ns_pad].astype(BF16), expand)
                parts += [mg] * (NSA_HEADS // NSA_GROUPS)
            mask = mask & (jnp.concatenate(parts, axis=0) > 0.5)
        for kb in range(nkb):
            kt = k_tile[:, kb * LANES:(kb + 1) * LANES].astype(BF16)
            sc = _dot_nt(qbd_scr[kb], kt)
            if c_tile is not None:
                bias = jnp.concatenate([jnp.broadcast_to(c_tile[2 * kb + j:2 * kb + j + 1, :], (tq, width))
                                        for j in range(2)], axis=0)
                sc = sc - bias
            sc = jnp.where(mask, sc, NEG)
            m_old = m_scr[kb]
            m_new = jnp.maximum(m_old, jnp.max(sc, axis=1, keepdims=True))
            alpha = jnp.exp(m_old - m_new)
            p = jnp.where(mask, jnp.exp(sc - m_new), 0.0)
            l_scr[kb] = alpha * l_scr[kb] + jnp.sum(p, axis=1, keepdims=True)
            vt = v_tile[:, kb * LANES:(kb + 1) * LANES].astype(BF16)
            acc_scr[kb] = alpha * acc_scr[kb] + _dot(p.astype(BF16), vt)
            m_scr[kb] = m_new

    def main_update():
        if pps:
            k_tile = jnp.concatenate([r[...] for r in k_refs], axis=0) if pps > 1 else k_refs[0][...]
            v_tile = jnp.concatenate([r[...] for r in v_refs], axis=0) if pps > 1 else v_refs[0][...]
        else:
            k_tile, v_tile = k_refs[0][...], v_refs[0][...]
        update(k_tile, v_tile, kbase + kj * tk, c_ref[...] if has_bias else None, tk)

    if has_tail:
        is_tail = (fl & 4) != 0

        @pl.when(is_tail)
        def _():
            update(tk_ref[...], tv_ref[...], tail_pos0, ct_ref[...] if has_bias else None, PAGE)

        @pl.when(jnp.logical_not(is_tail))
        def _():
            main_update()
    else:
        main_update()

    @pl.when((fl & 2) != 0)
    def _fin():
        fulls = [acc_scr[kb] / jnp.maximum(l_scr[kb], 1e-30) for kb in range(nkb)]
        outs = _unstack_heads(fulls, head_map, tq, qw // LANES)
        for c, blk in enumerate(outs):
            o_ref[:, c * LANES:(c + 1) * LANES] = blk


def _schedule(nq, tq, tk, q0, kbase, n_ktiles, mode, has_tail):
    qi_l, kj_l, fl_l = [], [], []
    for qi in range(nq):
        q_lo, q_hi = q0 + qi * tq, q0 + (qi + 1) * tq - 1
        ks = []
        for kj in range(n_ktiles):
            k_lo, k_hi = kbase + kj * tk, kbase + (kj + 1) * tk - 1
            if k_lo > q_hi:
                continue
            if mode == 'band' and k_hi <= q_lo - WINDOW:
                continue
            ks.append(kj)
        entries = [(kj, 0) for kj in ks]
        if has_tail:
            entries.append((ks[-1] if ks else 0, 4))
        for n, (kj, f) in enumerate(entries):
            f |= (1 if n == 0 else 0) | (2 if n == len(entries) - 1 else 0)
            qi_l.append(qi); kj_l.append(kj); fl_l.append(f)
    return (np.asarray(qi_l, np.int32), np.asarray(kj_l, np.int32), np.asarray(fl_l, np.int32))


def _flash(q3, k3, v3, *, mode, head_map, tq, tk, q0, kbase=0, page_table=None, pps=0, tail=None, tail_pos0=0,
           bias=None, bias_tail=None, sel=None, ns_pad=0):
    B, Tq, qw = q3.shape
    kw = k3.shape[-1]
    nkb = kw // LANES
    nq = Tq // tq
    paged = page_table is not None
    has_tail = tail is not None
    has_bias = bias is not None
    if paged:
        npages = page_table.shape[1]
        tk = pps * PAGE
        n_ktiles = npages // pps
    else:
        n_ktiles = k3.shape[1] // tk
    qi_t, kj_t, fl_t = _schedule(nq, tq, tk, q0, kbase, n_ktiles, mode, has_tail)
    nsteps = len(qi_t)
    prefetch = [jnp.asarray(qi_t), jnp.asarray(kj_t), jnp.asarray(fl_t)]
    if paged:
        prefetch.append(page_table.reshape(-1).astype(jnp.int32))
    n_pref = len(prefetch)
    rows = len(head_map[0]) * tq

    in_specs = [pl.BlockSpec((None, tq, qw), lambda b, s, qi, kj, fl, *_: (b, qi[s], 0))]
    args = [q3]
    if paged:
        def page_map(i):
            return lambda b, s, qi, kj, fl, pt: (pt[b * npages + kj[s] * pps + i], 0, 0)
        for arr in (k3, v3):
            for i in range(pps):
                in_specs.append(pl.BlockSpec((None, PAGE, kw), page_map(i)))
                args.append(arr)
    else:
        for arr in (k3, v3):
            in_specs.append(pl.BlockSpec((None, tk, kw), lambda b, s, qi, kj, fl, *_: (b, kj[s], 0)))
            args.append(arr)
    if has_tail:
        for arr in tail:
            in_specs.append(pl.BlockSpec((None, PAGE, kw), lambda b, s, qi, kj, fl, *_: (b, 0, 0)))
            args.append(arr)
    if has_bias:
        in_specs.append(pl.BlockSpec((None, 8, tk), lambda b, s, qi, kj, fl, *_: (b, 0, kj[s])))
        args.append(bias)
        if has_tail:
            in_specs.append(pl.BlockSpec((None, 8, PAGE), lambda b, s, qi, kj, fl, *_: (b, 0, 0)))
            args.append(bias_tail)
    if mode == 'sel':
        in_specs.append(pl.BlockSpec((None, tq, NSA_GROUPS * ns_pad), lambda b, s, qi, kj, fl, *_: (b, qi[s], 0)))
        args.append(sel)
    cfg = (mode, head_map, tq, tk, nkb, q0, kbase, pps if paged else 0, has_tail, tail_pos0, has_bias, ns_pad,
           n_pref, qw)
    return pl.pallas_call(
        functools.partial(_flash_kernel, cfg=cfg),
        out_shape=jax.ShapeDtypeStruct((B, Tq, qw), F32),
        grid_spec=pltpu.PrefetchScalarGridSpec(
            num_scalar_prefetch=n_pref,
            grid=(B, nsteps),
            in_specs=in_specs,
            out_specs=pl.BlockSpec((None, tq, qw), lambda b, s, qi, kj, fl, *_: (b, qi[s], 0)),
            scratch_shapes=[pltpu.VMEM((nkb, rows, LANES), BF16), pltpu.VMEM((nkb, rows, 1), F32),
                            pltpu.VMEM((nkb, rows, 1), F32), pltpu.VMEM((nkb, rows, LANES), F32)]),
        compiler_params=_cp("parallel", "arbitrary"),
    )(*prefetch, *args)


def _nsa_combine_kernel(oc_ref, os_ref, ow_ref, sm_ref, e_ref, o_ref):
    gate = _sigmoid(sm_ref[...])
    acc = None
    for j, br in enumerate((oc_ref, os_ref, ow_ref)):
        gexp = _dot(gate, e_ref[j], HI)
        acc = gexp * br[...] if acc is None else acc + gexp * br[...]
    o_ref[...] = acc


def _nsa_combine(o_cmp, o_slc, o_win, small):
    n = o_cmp.shape[0]
    tm = min(512, n)
    src = np.zeros((3, LANES, MIX), np.float32)
    for h in range(NSA_HEADS):
        for j in range(3):
            src[j, 2 * M_HEADS + 3 * h + j, h * NSA_DH:(h + 1) * NSA_DH] = 1.0
    spec = pl.BlockSpec((tm, MIX), lambda i: (i, 0))
    return pl.pallas_call(
        _nsa_combine_kernel, out_shape=jax.ShapeDtypeStruct((n, MIX), F32), grid=(n // tm,),
        in_specs=[spec, spec, spec, pl.BlockSpec((tm, LANES), lambda i: (i, 0)),
                  pl.BlockSpec((3, LANES, MIX), lambda i: (0, 0, 0))],
        out_specs=spec, compiler_params=_cp("parallel"))(o_cmp, o_slc, o_win, small, jnp.asarray(src))


def _fox_cumsum_kernel(sm_ref, bias_ref, lf_ref, c_ref, carry, *, tt):
    @pl.when(pl.program_id(1) == 0)
    def _():
        carry[...] = jnp.zeros(carry.shape, F32)

    lf = _log_sigmoid(sm_ref[...] + bias_ref[...])
    lf_ref[...] = lf
    eye8 = (_iota((8, LANES), 0) == _iota((8, LANES), 1)).astype(F32)
    lf_row = _dot_nt(eye8, lf, HI)
    upper = (_iota((tt, tt), 0) <= _iota((tt, tt), 1)).astype(F32)
    c = _dot(lf_row, upper, HI) + carry[:, 0:1]
    c_ref[...] = c
    carry[...] = jnp.broadcast_to(c[:, tt - 1:tt], carry.shape)


def _fox_cumsum(small, bias_row, B, T):
    tt = min(256, T)
    row = lambda b, i: (b * (T // tt) + i, 0)
    return pl.pallas_call(
        functools.partial(_fox_cumsum_kernel, tt=tt),
        out_shape=[jax.ShapeDtypeStruct((B * T, LANES), F32), jax.ShapeDtypeStruct((B, 8, T), F32)],
        grid=(B, T // tt),
        in_specs=[pl.BlockSpec((tt, LANES), row), pl.BlockSpec((1, LANES), lambda b, i: (0, 0))],
        out_specs=[pl.BlockSpec((tt, LANES), row), pl.BlockSpec((None, 8, tt), lambda b, i: (b, 0, i))],
        scratch_shapes=[pltpu.VMEM((8, LANES), F32)],
        compiler_params=_cp("parallel", "arbitrary"),
    )(small, bias_row)


def _fox_cumsum_paged_kernel(pt_ref, *refs, pps, n_main):
    lf_refs = refs[:pps]
    sm_ref, bias_ref, c_ref, lfn_ref, carry = refs[pps:]
    j = pl.program_id(1)

    @pl.when(j == 0)
    def _():
        carry[...] = jnp.zeros(carry.shape, F32)

    upper = (_iota((PAGE, PAGE), 0) <= _iota((PAGE, PAGE), 1)).astype(F32)

    @pl.when(j < n_main)
    def _():
        run = carry[:, 0:1]
        for i in range(pps):
            c = _dot(lf_refs[i][...], upper, HI) + run
            c_ref[:, i * PAGE:(i + 1) * PAGE] = c
            run = c[:, PAGE - 1:PAGE]
        carry[...] = jnp.broadcast_to(run, carry.shape)

    @pl.when(j == n_main)
    def _():
        lf = _log_sigmoid(sm_ref[...] + bias_ref[...])
        lfn_ref[...] = lf
        eye8 = (_iota((8, LANES), 0) == _iota((8, LANES), 1)).astype(F32)
        c_new = _dot(_dot_nt(eye8, lf, HI), upper, HI) + carry[:, 0:1]
        for i in range(pps):
            c_ref[:, i * PAGE:(i + 1) * PAGE] = c_new


def _fox_cumsum_paged(lf_pool_t, page_table, small_pad, bias_row, pps):
    B, npages = page_table.shape
    n_main = npages // pps

    def page_map(i):
        return lambda b, j, pt: (pt[b * npages + jnp.minimum(j, n_main - 1) * pps + i], 0, 0)

    return pl.pallas_call(
        functools.partial(_fox_cumsum_paged_kernel, pps=pps, n_main=n_main),
        out_shape=[jax.ShapeDtypeStruct((B, 8, (n_main + 1) * pps * PAGE), F32),
                   jax.ShapeDtypeStruct((B, PAGE, LANES), F32)],
        grid_spec=pltpu.PrefetchScalarGridSpec(
            num_scalar_prefetch=1, grid=(B, n_main + 1),
            in_specs=[pl.BlockSpec((None, 8, PAGE), page_map(i)) for i in range(pps)]
            + [pl.BlockSpec((None, PAGE, LANES), lambda b, j, pt: (b, 0, 0)),
               pl.BlockSpec((1, LANES), lambda b, j, pt: (0, 0))],
            out_specs=[pl.BlockSpec((None, 8, pps * PAGE), lambda b, j, pt: (b, 0, j)),
                       pl.BlockSpec((None, PAGE, LANES), lambda b, j, pt: (b, 0, 0))],
            scratch_shapes=[pltpu.VMEM((8, LANES), F32)]),
        compiler_params=_cp("parallel", "arbitrary"),
    )(page_table.reshape(-1).astype(jnp.int32), *([lf_pool_t] * pps), small_pad, bias_row)


def _rwkv_prep_kernel(rw_ref, prev_ref, sh_ref, mu_ref, w0_ref, w2_ref, a0_ref, a2_ref, g2_ref, kk_ref, ka_ref,
                      bd_ref, r_o, lw_o, k_o, v_o, kk_o, a_o, g_o, *, tt):
    i = pl.program_id(1)
    rw = rw_ref[...]
    first_prev = jnp.where(i == 0, sh_ref[...], prev_ref[7:8, :])
    prev = jnp.where(_iota(rw.shape, 0) == 0, first_prev, pltpu.roll(rw, 1, 0))
    m = rw + mu_ref[...] * (prev - rw)
    r, kr, vr = m[:, 0:MIX], m[:, MIX:2 * MIX], m[:, 2 * MIX:3 * MIX]
    lora = m[:, 3 * MIX:3 * MIX + LANES]
    gl = m[:, 3 * MIX + LANES:3 * MIX + 2 * LANES]
    w = -_softplus(-(w0_ref[...] + _dot(jnp.tanh(lora).astype(BF16), w2_ref[...]))) - 0.5
    a = _sigmoid(a0_ref[...] + _dot(lora.astype(BF16), a2_ref[...]))
    kk = kr * kk_ref[...]
    ss = _dot(kk * kk, bd_ref[...], HI)
    r_o[...] = r
    lw_o[...] = -jnp.exp(w)
    k_o[...] = kr * (1.0 + (a - 1.0) * ka_ref[...])
    v_o[...] = vr
    kk_o[...] = kk / jnp.maximum(jnp.sqrt(ss), 1e-12)
    a_o[...] = a
    g_o[...] = _dot(_sigmoid(gl).astype(BF16), g2_ref[...])


def _rwkv_prep(rw3, shift0, P):
    B, T, wdt = rw3.shape
    tt = min(256, T)
    nblk8 = tt // 8
    row3 = lambda b, i: (b, i, 0)
    vec = lambda n: pl.BlockSpec((1, n), lambda b, i: (0, 0))
    outs = pl.pallas_call(
        functools.partial(_rwkv_prep_kernel, tt=tt),
        out_shape=[jax.ShapeDtypeStruct((B, T, MIX), F32)] * 7,
        grid=(B, T // tt),
        in_specs=[pl.BlockSpec((None, tt, wdt), row3),
                  pl.BlockSpec((None, 8, wdt), lambda b, i: (b, jnp.maximum(i * nblk8 - 1, 0), 0)),
                  pl.BlockSpec((None, 1, wdt), lambda b, i: (b, 0, 0)),
                  vec(wdt), vec(MIX), pl.BlockSpec((LANES, MIX), lambda b, i: (0, 0)),
                  vec(MIX), pl.BlockSpec((LANES, MIX), lambda b, i: (0, 0)),
                  pl.BlockSpec((LANES, MIX), lambda b, i: (0, 0)), vec(MIX), vec(MIX),
                  pl.BlockSpec((MIX, MIX), lambda b, i: (0, 0))],
        out_specs=[pl.BlockSpec((None, tt, MIX), row3)] * 7,
        compiler_params=_cp("parallel", "arbitrary"),
    )(rw3, rw3, shift0[:, None, :], P['mu'], P['w0'], P['w2'], P['a0'], P['a2'], P['g2'], P['k_k'], P['k_a'], P['bd64'])
    return outs


def _rwkv_scan_kernel(r_ref, lw_ref, k_ref, v_ref, kk_ref, a_ref, g_ref, s0_ref, rk_ref, lng_ref, lnb_ref,
                      y_ref, s1_ref, s_scr, *, L):
    ci = pl.program_id(1)

    @pl.when(ci == 0)
    def _():
        s_scr[...] = s0_ref[...]

    r_i = _iota((L, L), 0)
    c_i = _iota((L, L), 1)
    lower = (c_i <= r_i).astype(F32)
    strict = c_i < r_i
    incl = c_i <= r_i
    eye_l = (c_i == r_i).astype(F32)
    lane = _iota((L, LANES), 1)
    half_masks = (lane < 64, lane >= 64)
    rr = _iota((LANES, LANES), 0)
    cc = _iota((LANES, LANES), 1)
    bdmask = (rr >= 64) == (cc >= 64)
    eye128 = rr == cc
    bd_f = bdmask.astype(F32)
    for pr in range(RWKV_HEADS // 2):
        sl = slice(pr * LANES, (pr + 1) * LANES)
        lw = lw_ref[:, sl]
        cl = _dot(lower, lw, HI)
        w_in = jnp.exp(cl)
        w_out = jnp.exp(-cl)
        kk = kk_ref[:, sl]
        kap = kk * jnp.exp(cl - lw)
        bet = kk * a_ref[:, sl] * w_out
        kt = k_ref[:, sl] * w_out
        rt = r_ref[:, sl] * w_in
        v = v_ref[:, sl]
        sbd = s_scr[pr]
        rhs = _dot(kap, sbd, HI)
        ys = _dot(rt, sbd, HI)
        u = jnp.zeros((L, LANES), F32)
        tinvs, rbs, rks = [], [], []
        for hm in half_masks:
            kap_h = jnp.where(hm, kap, 0.0)
            rt_h = jnp.where(hm, rt, 0.0)
            amat = jnp.where(strict, _dot_nt(kap_h, bet, HI), 0.0)
            cmat = jnp.where(strict, _dot_nt(kap_h, kt, HI), 0.0)
            tinv = eye_l - amat
            pw = amat
            n = 2
            while n < L:
                pw = _dot(pw, pw, HI)
                tinv = tinv + _dot(tinv, pw, HI)
                n *= 2
            v_h = jnp.where(hm, v, 0.0)
            rhs_h = jnp.where(hm, rhs, 0.0) + _dot(cmat, v_h, HI)
            u = u - _dot(tinv, rhs_h, HI)
            rbs.append(jnp.where(incl, _dot_nt(rt_h, bet, HI), 0.0))
            rks.append(jnp.where(incl, _dot_nt(rt_h, kt, HI), 0.0))
        y = ys
        for hm, rb, rkm in zip(half_masks, rbs, rks):
            y = y + _dot(rb, jnp.where(hm, u, 0.0), HI) + _dot(rkm, jnp.where(hm, v, 0.0), HI)
        upd = jnp.where(bdmask, _dot_tn(bet, u, HI) + _dot_tn(kt, v, HI), 0.0)
        w_end = jnp.broadcast_to(w_in[L - 1:L, :], (LANES, LANES))
        w_end_col = jnp.sum(jnp.where(eye128, w_end, 0.0), axis=1, keepdims=True)
        s_scr[pr] = (sbd + upd) * w_end_col
        mean = _dot(y, bd_f, HI) * (1.0 / RWKV_N)
        yc = y - mean
        var = _dot(yc * yc, bd_f, HI) * (1.0 / RWKV_N)
        yn = yc * lax.rsqrt(var + 64e-5) * lng_ref[:, sl] + lnb_ref[:, sl]
        r_raw = r_ref[:, sl]
        bonus = _dot(r_raw * k_ref[:, sl] * rk_ref[:, sl], bd_f, HI)
        y_ref[:, sl] = (yn + bonus * v) * g_ref[:, sl]

    @pl.when(ci == pl.num_programs(1) - 1)
    def _():
        s1_ref[...] = s_scr[...]


def _rwkv_scan(r, lw, k, v, kk, a, g, s0, P, B, T):
    L = RWKV_CHUNK
    tp = -(-T // L) * L
    if tp != T:
        pad = lambda t: jnp.pad(t, ((0, 0), (0, tp - T), (0, 0)))
        r, lw, k, v, kk, a, g = (pad(t) for t in (r, lw, k, v, kk, a, g))
    nch = tp // L
    st = jnp.swapaxes(s0.astype(F32), -1, -2).reshape(B, RWKV_HEADS // 2, 2, RWKV_N, RWKV_N)
    z = jnp.zeros_like(st[:, :, 0])
    sbd0 = jnp.concatenate([jnp.concatenate([st[:, :, 0], z], -1), jnp.concatenate([z, st[:, :, 1]], -1)], -2)
    row3 = lambda b, c: (b, c, 0)
    st_spec = pl.BlockSpec((None, RWKV_HEADS // 2, LANES, LANES), lambda b, c: (b, 0, 0, 0))
    vec = pl.BlockSpec((1, MIX), lambda b, c: (0, 0))
    y, sbd1 = pl.pallas_call(
        functools.partial(_rwkv_scan_kernel, L=L),
        out_shape=[jax.ShapeDtypeStruct((B, tp, MIX), F32),
                   jax.ShapeDtypeStruct((B, RWKV_HEADS // 2, LANES, LANES), F32)],
        grid=(B, nch),
        in_specs=[pl.BlockSpec((None, L, MIX), row3)] * 7 + [st_spec, vec, vec, vec],
        out_specs=[pl.BlockSpec((None, L, MIX), row3), st_spec],
        scratch_shapes=[pltpu.VMEM((RWKV_HEADS // 2, LANES, LANES), F32)],
        compiler_params=_cp("parallel", "arbitrary"),
    )(r, lw, k, v, kk, a, g, sbd0, P['r_k'], P['ln_g'], P['ln_b'])
    s_e = sbd1[:, :, :RWKV_N, :RWKV_N]
    s_o = sbd1[:, :, RWKV_N:, RWKV_N:]
    s1 = jnp.swapaxes(jnp.stack([s_e, s_o], 2).reshape(B, RWKV_HEADS, RWKV_N, RWKV_N), -1, -2)
    return y[:, :T], s1


def _mem_kernel(x_ref, km_ref, vm_ref, wq_ref, wo_ref, g_ref, b_ref, o_ref):
    x = x_ref[...]
    q = _dot(x.astype(BF16), wq_ref[...])
    heads = []
    for h in range(MEM_HEADS):
        sl = slice(h * MEM_DH, (h + 1) * MEM_DH)
        s = _dot_nt((q[:, sl] * (MEM_DH ** -0.5)).astype(BF16), km_ref[:, sl].astype(BF16))
        e = jnp.exp(s - jnp.max(s, axis=1, keepdims=True))
        p = e / jnp.sum(e, axis=1, keepdims=True)
        heads.append(_dot(p.astype(BF16), vm_ref[:, sl].astype(BF16)))
    o = jnp.concatenate(heads, axis=1)
    y = _dot(o.astype(BF16), wo_ref[...])
    o_ref[...] = _layer_norm_rows(ALPHA * x + y, g_ref[...], b_ref[...])


def _mem_layer(x3, km, vm, wq, wo, g, b):
    B, T, d = x3.shape
    tq = min(512, T)
    mt = km.shape[1]
    full = lambda shape: pl.BlockSpec(shape, lambda bb, i: (0,) * len(shape))
    return pl.pallas_call(
        _mem_kernel, out_shape=jax.ShapeDtypeStruct((B, T, d), F32), grid=(B, T // tq),
        in_specs=[pl.BlockSpec((None, tq, d), lambda bb, i: (bb, i, 0)),
                  pl.BlockSpec((None, mt, d), lambda bb, i: (bb, 0, 0)),
                  pl.BlockSpec((None, mt, d), lambda bb, i: (bb, 0, 0)),
                  full((d, d)), full((d, d)), full((1, d)), full((1, d))],
        out_specs=pl.BlockSpec((None, tq, d), lambda bb, i: (bb, i, 0)),
        compiler_params=_cp("parallel", "arbitrary"),
    )(x3, km, vm, wq, wo, g.reshape(1, d), b.reshape(1, d))


def _moe_kernel(x_ref, wr_ref, br_ref, w1_ref, w3_ref, w2_ref, g_ref, b_ref, o_ref, xb_scr, gw_scr, acc_scr):
    e = pl.program_id(1)

    @pl.when(e == 0)
    def _route():
        x = x_ref[...]
        xb_scr[...] = x.astype(BF16)
        logit = _dot(x, wr_ref[...], HI) + br_ref[...]
        lane = _iota(logit.shape, 1)
        lane_f = lane.astype(F32)
        is_g = lane < N_GROUPS
        gl = jnp.where(is_g, logit, -jnp.inf)
        gmax = jnp.max(gl, axis=1, keepdims=True)
        grp = jnp.min(jnp.where(gl == gmax, lane_f, 1e9), axis=1, keepdims=True)
        g_w = 1.0 / jnp.sum(jnp.where(is_g, jnp.exp(gl - gmax), 0.0), axis=1, keepdims=True)
        lo = N_GROUPS + EXP_PER_GROUP * grp
        in_grp = (lane_f >= lo) & (lane_f < lo + EXP_PER_GROUP)
        el = jnp.where(in_grp, logit, -jnp.inf)
        emax = jnp.max(el, axis=1, keepdims=True)
        ee = jnp.where(in_grp, jnp.exp(el - emax), 0.0)
        p = jnp.where(in_grp, ee / jnp.sum(ee, axis=1, keepdims=True), -1.0)
        p1 = jnp.max(p, axis=1, keepdims=True)
        i1 = jnp.min(jnp.where(p == p1, lane_f, 1e9), axis=1, keepdims=True)
        pr = jnp.where(lane_f == i1, -1.0, p)
        p2 = jnp.max(pr, axis=1, keepdims=True)
        i2 = jnp.min(jnp.where(pr == p2, lane_f, 1e9), axis=1, keepdims=True)
        den = p1 + p2
        gw_scr[...] = jnp.where(lane_f == i1, g_w * p1 / den, 0.0) + jnp.where(lane_f == i2, g_w * p2 / den, 0.0)
        acc_scr[...] = jnp.zeros(acc_scr.shape, F32)

    gw = gw_scr[...]
    gcol = jnp.sum(jnp.where(_iota(gw.shape, 1) == e + N_GROUPS, gw, 0.0), axis=1, keepdims=True)
    xb = xb_scr[...]
    h1 = _dot(xb, w1_ref[...])
    h3 = _dot(xb, w3_ref[...])
    hid = h1 * _sigmoid(h1) * h3
    acc_scr[...] += gcol * _dot(hid.astype(BF16), w2_ref[...])

    @pl.when(e == pl.num_programs(1) - 1)
    def _():
        o_ref[...] = _layer_norm_rows(ALPHA * x_ref[...] + acc_scr[...], g_ref[...], b_ref[...])


def _moe_layer(x, M, g, b):
    n, d = x.shape
    tm = min(1024, n)
    full2 = lambda shape: pl.BlockSpec(shape, lambda i, e: (0, 0))
    return pl.pallas_call(
        _moe_kernel, out_shape=jax.ShapeDtypeStruct((n, d), F32), grid=(n // tm, N_EXPERTS),
        in_specs=[pl.BlockSpec((tm, d), lambda i, e: (i, 0)), full2((d, LANES)), full2((1, LANES)),
                  pl.BlockSpec((None, d, D_EXPERT), lambda i, e: (e, 0, 0)),
                  pl.BlockSpec((None, d, D_EXPERT), lambda i, e: (e, 0, 0)),
                  pl.BlockSpec((None, D_EXPERT, d), lambda i, e: (e, 0, 0)),
                  full2((1, d)), full2((1, d))],
        out_specs=pl.BlockSpec((tm, d), lambda i, e: (i, 0)),
        scratch_shapes=[pltpu.VMEM((tm, d), BF16), pltpu.VMEM((tm, LANES), F32), pltpu.VMEM((tm, d), F32)],
        compiler_params=_cp("parallel", "arbitrary"),
    )(x, M['wr'], M['br'], M['w1'], M['w3'], M['w2'], g.reshape(1, d), b.reshape(1, d))


EVEN_SEGS = ([(MIX, False)] * 4 + [(MIX, True), (LANES, False), (LANES, False), (LANES, True), (LANES, False),
                                   (LANES, True), (LANES, False), (LANES, False)])
ODD_SEGS = [(MIX, False)] * 3 + [(RWKV_SHIFT, False), (LANES, False)]
RW_PERM = np.concatenate([np.arange(0, 512), np.arange(576, 1088), np.arange(1088, 1600), np.arange(512, 576),
                          np.arange(1600, 1664), np.arange(1664, 1792)])
RW_INV = np.argsort(RW_PERM)


def _prep_even(w_in, w_out, b_i, b_f):
    zeros = jnp.zeros((D_MODEL, LANES - 2 * M_HEADS - 3 * NSA_HEADS), F32)
    w = jnp.concatenate([w_in[:, 0:1536], w_in[:, 1544:2056], w_in[:, 2056:3336], w_in[:, 1536:1544],
                         w_in[:, 3336:3360], zeros], axis=1).astype(BF16)
    bias = jnp.concatenate([b_i, b_f, jnp.zeros((LANES - 2 * M_HEADS,), F32)]).reshape(1, LANES)
    return dict(w_in=w, w_out_a=w_out[:MIX].astype(BF16), w_out_b=w_out[MIX:].astype(BF16), gate_bias=bias)


def _prep_odd(w_in, w_out, fox_b_f, mu, w0, w2, a0, a2, g2, k_k, k_a, r_k, ln_g, ln_b):
    rw_cols = w_in[:, 1544:3336][:, RW_PERM]
    zeros = jnp.zeros((D_MODEL, LANES - FOX_HEADS), F32)
    w = jnp.concatenate([w_in[:, 0:1536], rw_cols, w_in[:, 1536:1544], zeros], axis=1).astype(BF16)
    bias = jnp.concatenate([fox_b_f, jnp.zeros((LANES - FOX_HEADS,), F32)]).reshape(1, LANES)
    z64 = jnp.zeros((64, MIX), F32)
    hd = np.arange(MIX) // RWKV_N
    bd64 = jnp.asarray((hd[:, None] == hd[None, :]).astype(np.float32))
    P = dict(mu=mu[RW_PERM].reshape(1, -1), w0=w0.reshape(1, -1), a0=a0.reshape(1, -1),
             w2=jnp.concatenate([w2, z64], 0).astype(BF16), a2=jnp.concatenate([z64, a2], 0).astype(BF16),
             g2=g2.astype(BF16), k_k=k_k.reshape(1, -1), k_a=k_a.reshape(1, -1), r_k=r_k.reshape(1, -1),
             ln_g=ln_g.reshape(1, -1), ln_b=ln_b.reshape(1, -1), bd64=bd64)
    return dict(w_in=w, w_out_a=w_out[:MIX].astype(BF16), w_out_b=w_out[MIX:].astype(BF16), fox_bias=bias, P=P)


def _prep_moe(w_group, b_group, w_expert, b_expert, w1, w3, w2):
    padw = LANES - N_GROUPS - N_EXPERTS
    wr = jnp.concatenate([w_group, w_expert, jnp.zeros((D_MODEL, padw), F32)], axis=1)
    br = jnp.concatenate([b_group, b_expert, jnp.zeros((padw,), F32)]).reshape(1, LANES)
    return dict(wr=wr, br=br, w1=w1.astype(BF16), w3=w3.astype(BF16), w2=w2.astype(BF16))


def _even_evenodd(kcmp):
    return jnp.concatenate([kcmp[:, 0::2], kcmp[:, 1::2]], axis=1)


def _even_mixer(x3, q0, W, nsa_w, norm_g, c0, n0, m0, past, ln_g, ln_b):
    B, T, D = x3.shape
    N = B * T
    x2 = x3.reshape(N, D)
    pos = q0 + jnp.arange(T)
    cos, sin = _rope_tables(pos)
    rope = (jnp.tile(cos, (B, 1)), jnp.tile(sin, (B, 1)))
    mq, mk, mv, mo, nq, kc, vc, ks, vs, kw, vw, small = _proj_split(x2, W['w_in'], EVEN_SEGS, rope)
    h, c1, n1, m1 = _mlstm(mq, mk, mv, mo, small, W['gate_bias'], norm_g, c0, n0, m0, B, T)
    pe_k, pe_v, ck1, ck2, cv1, cv2 = nsa_w
    blk_w = CMP_BLOCK * LANES
    q3 = nq.reshape(B, T, MIX)
    if past is None:
        L = T
        nc = L // CMP_BLOCK
        kcmp = _compress(kc.reshape(B * nc, blk_w), pe_k, ck1, ck2).reshape(B, nc, LANES)
        vcmp = _compress(vc.reshape(B * nc, blk_w), pe_v, cv1, cv2).reshape(B, nc, LANES)
    else:
        pt = past['page_table']
        npages = pt.shape[1]
        L = npages * PAGE + T
        nc = L // CMP_BLOCK
        per_page = PAGE // CMP_BLOCK
        pool = past['kc'].shape[0]
        kcp = _compress(past['kc'].reshape(pool * per_page, blk_w), pe_k, ck1, ck2).reshape(pool, per_page * LANES)
        vcp = _compress(past['vc'].reshape(pool * per_page, blk_w), pe_v, cv1, cv2).reshape(pool, per_page * LANES)
        kcmp = jnp.take(kcp, pt, axis=0).reshape(B, npages * per_page, LANES)[:, :nc]
        vcmp = jnp.take(vcp, pt, axis=0).reshape(B, npages * per_page, LANES)[:, :nc]
    cpos = jnp.arange(nc) * CMP_BLOCK + CMP_BLOCK - 1
    ccos, csin = _rope_tables(cpos)
    kcmp = _rope_rows(kcmp.reshape(B * nc, LANES), jnp.tile(ccos, (B, 1)), jnp.tile(csin, (B, 1))).reshape(B, nc, LANES)
    ns = -(-L // SEL_BLOCK)
    ns_pad = -(-ns // LANES) * LANES
    cend = jnp.concatenate([cpos[0::2], cpos[1::2]]).astype(jnp.int32).reshape(1, nc)
    o_cmp, sel = _cmp_sel(q3, _even_evenodd(kcmp), _even_evenodd(vcmp), cend, q0, ns, ns_pad)
    ks3, vs3 = ks.reshape(B, T, LANES), vs.reshape(B, T, LANES)
    kw3, vw3 = kw.reshape(B, T, LANES), vw.reshape(B, T, LANES)
    if past is None:
        tq = min(128, T)
        o_slc = _flash(q3, ks3, vs3, mode='sel', head_map=NSA_HEAD_MAP, tq=tq, tk=min(512, T), q0=0, sel=sel,
                       ns_pad=ns_pad)
        o_win = _flash(q3, kw3, vw3, mode='band', head_map=NSA_HEAD_MAP, tq=tq, tk=min(128, T), q0=0)
        kw_new, vw_new = kw3[:, -min(WINDOW, T):], vw3[:, -min(WINDOW, T):]
    else:
        padrows = ((0, 0), (0, PAGE - T), (0, 0))
        o_slc = _flash(q3, past['ks'], past['vs'], mode='sel', head_map=NSA_HEAD_MAP, tq=T, tk=0, q0=q0,
                       page_table=pt, pps=_pages_per_step(npages), tail=(jnp.pad(ks3, padrows), jnp.pad(vs3, padrows)),
                       tail_pos0=q0, sel=sel, ns_pad=ns_pad)
        kw_ext = jnp.concatenate([past['kw_buf'], kw3], axis=1)
        vw_ext = jnp.concatenate([past['vw_buf'], vw3], axis=1)
        wb = past['kw_buf'].shape[1]
        o_win = _flash(q3, kw_ext, vw_ext, mode='band', head_map=NSA_HEAD_MAP, tq=T, tk=wb + T, q0=q0, kbase=q0 - wb)
        keep = min(WINDOW, wb + T)
        kw_new, vw_new = kw_ext[:, -keep:], vw_ext[:, -keep:]
    o = _nsa_combine(o_cmp.reshape(N, MIX), o_slc.reshape(N, MIX), o_win.reshape(N, MIX), small)
    x_new = _mm_res_ln([h, o], [W['w_out_a'], W['w_out_b']], x2, ln_g, ln_b).reshape(B, T, D)
    sh = (B, T, NSA_GROUPS, NSA_DH)
    wsh = (B, -1, NSA_GROUPS, NSA_DH)
    state = (c1, n1, m1, kc.reshape(sh), vc.reshape(sh), ks.reshape(sh), vs.reshape(sh),
             kw_new.reshape(wsh), vw_new.reshape(wsh))
    return x_new, state


def _pages_per_step(npages):
    for p in (8, 4, 2, 1):
        if npages % p == 0:
            return p


def _odd_mixer(x3, q0, W, s0, shift0, past, ln_g, ln_b):
    B, T, D = x3.shape
    N = B * T
    x2 = x3.reshape(N, D)
    fq, fk, fv, rw, small = _proj_split(x2, W['w_in'], ODD_SEGS)
    q3, k3, v3 = (t.reshape(B, T, MIX) for t in (fq, fk, fv))
    if past is None:
        lf_full, c_row = _fox_cumsum(small, W['fox_bias'], B, T)
        lf = lf_full[:, :FOX_HEADS].reshape(B, T, FOX_HEADS)
        tq = min(256, T)
        o_c = _flash(q3, k3, v3, mode='causal', head_map=FOX_HEAD_MAP, tq=tq, tk=min(512, T), q0=0, bias=c_row)
    else:
        pt = past['page_table']
        npages = pt.shape[1]
        pps = _pages_per_step(npages)
        padrows = ((0, 0), (0, PAGE - T), (0, 0))
        lf_pool_t = jnp.swapaxes(past['lf'], 1, 2)
        c_all, lf_new = _fox_cumsum_paged(lf_pool_t, pt, jnp.pad(small.reshape(B, T, LANES), padrows), W['fox_bias'], pps)
        lf = lf_new[:, :T, :FOX_HEADS]
        n_main = npages * PAGE
        o_c = _flash(q3, past['k'], past['v'], mode='causal', head_map=FOX_HEAD_MAP, tq=T, tk=0, q0=q0,
                     page_table=pt, pps=pps, tail=(jnp.pad(k3, padrows), jnp.pad(v3, padrows)), tail_pos0=q0,
                     bias=c_all, bias_tail=c_all[:, :, n_main:n_main + PAGE])
    rw3 = rw.reshape(B, T, RWKV_SHIFT)
    r, lw, k, v, kk, a, g = _rwkv_prep(rw3, shift0[:, RW_PERM], W['P'])
    y, s1 = _rwkv_scan(r, lw, k, v, kk, a, g, s0, W['P'], B, T)
    x_new = _mm_res_ln([o_c.reshape(N, MIX), y.reshape(N, MIX)], [W['w_out_a'], W['w_out_b']], x2, ln_g, ln_b)
    hs = (B, T, FOX_HEADS, FOX_DH)
    state = (fk.reshape(hs), fv.reshape(hs), lf, s1, rw3[:, -1][:, RW_INV])
    return x_new.reshape(B, T, D), state


def kernel(x_prompt, x_sample, state_mlstm_C, state_mlstm_n, state_mlstm_m, cache_nsa_kc, cache_nsa_vc, cache_nsa_ks, cache_nsa_vs, cache_nsa_kw, cache_nsa_vw, cache_fox_k, cache_fox_v, cache_fox_logf, state_rwkv_S, state_rwkv_shift, cache_mem_k, cache_mem_v, page_table, mem_prompt, even_w_in, even_w_out, mlstm_b_i, mlstm_b_f, mlstm_norm_g, nsa_pe_k, nsa_pe_v, nsa_cmp_k_w1, nsa_cmp_k_w2, nsa_cmp_v_w1, nsa_cmp_v_w2, odd_w_in, odd_w_out, fox_b_f, rwkv_mu, rwkv_w0, rwkv_w2, rwkv_a0, rwkv_a2, rwkv_g2, rwkv_k_k, rwkv_k_a, rwkv_r_k, rwkv_ln_g, rwkv_ln_b, mem_wq, mem_wk, mem_wv, mem_wo, moe_w_group, moe_b_group, moe_w_expert, moe_b_expert, moe_w1, moe_w3, moe_w2, ln_g, ln_b):
    bp, tp, d = x_prompt.shape
    bs, ts, _ = x_sample.shape
    depth = ln_g.shape[0]
    past = page_table.shape[1] * cache_nsa_kc.shape[2]
    pool = cache_nsa_kc.shape[1]
    xp, xs = x_prompt, x_sample
    acc = {}

    def push(prefix, names, vals):
        for nm, val in zip(names, vals):
            acc.setdefault(prefix + nm, []).append(val)

    even_names = ('mlstm_C', 'mlstm_n', 'mlstm_m', 'nsa_kc', 'nsa_vc', 'nsa_ks', 'nsa_vs', 'nsa_kw', 'nsa_vw')
    odd_names = ('fox_k', 'fox_v', 'fox_logf', 'rwkv_S', 'rwkv_shift')
    for layer in range(depth):
        e = layer // 2
        if layer % 2 == 0:
            W = _prep_even(even_w_in[e], even_w_out[e], mlstm_b_i[e], mlstm_b_f[e])
            nsa_w = (nsa_pe_k[e], nsa_pe_v[e], nsa_cmp_k_w1[e], nsa_cmp_k_w2[e], nsa_cmp_v_w1[e], nsa_cmp_v_w2[e])
            xp, sp = _even_mixer(xp, 0, W, nsa_w, mlstm_norm_g[e], jnp.zeros((bp, M_HEADS, M_DV, M_DK), F32),
                                 jnp.zeros((bp, M_HEADS, M_DK), F32), jnp.zeros((bp, M_HEADS), F32), None,
                                 ln_g[layer, 0], ln_b[layer, 0])
            pg = lambda c: c[e].reshape(pool, PAGE, LANES)
            wbuf = lambda c: c[e].reshape(bs, -1, LANES)
            past_d = dict(kc=pg(cache_nsa_kc), vc=pg(cache_nsa_vc), ks=pg(cache_nsa_ks), vs=pg(cache_nsa_vs),
                          page_table=page_table, kw_buf=wbuf(cache_nsa_kw), vw_buf=wbuf(cache_nsa_vw))
            xs, ss = _even_mixer(xs, past, W, nsa_w, mlstm_norm_g[e], state_mlstm_C[e], state_mlstm_n[e],
                                 state_mlstm_m[e], past_d, ln_g[layer, 0], ln_b[layer, 0])
            names = even_names
        else:
            W = _prep_odd(odd_w_in[e], odd_w_out[e], fox_b_f[e], rwkv_mu[e], rwkv_w0[e], rwkv_w2[e], rwkv_a0[e],
                          rwkv_a2[e], rwkv_g2[e], rwkv_k_k[e], rwkv_k_a[e], rwkv_r_k[e], rwkv_ln_g[e], rwkv_ln_b[e])
            xp, sp = _odd_mixer(xp, 0, W, jnp.zeros((bp, RWKV_HEADS, RWKV_N, RWKV_N), F32),
                                jnp.zeros((bp, RWKV_SHIFT), F32), None, ln_g[layer, 0], ln_b[layer, 0])
            past_d = dict(k=cache_fox_k[e].reshape(pool, PAGE, MIX), v=cache_fox_v[e].reshape(pool, PAGE, MIX),
                          lf=cache_fox_logf[e], page_table=page_table)
            xs, ss = _odd_mixer(xs, past, W, state_rwkv_S[e], state_rwkv_shift[e], past_d,
                                ln_g[layer, 0], ln_b[layer, 0])
            names = odd_names
        push('p_', names, sp)
        push('s_', names, ss)
        wkv = jnp.concatenate([mem_wk[layer], mem_wv[layer]], axis=1).astype(BF16)
        mt = mem_prompt.shape[1]
        kmp, vmp = _proj_split(mem_prompt.reshape(bp * mt, d), wkv, [(d, False), (d, False)])
        push('p_', ('mem_k', 'mem_v'), (kmp.reshape(bp, mt, MEM_HEADS, MEM_DH), vmp.reshape(bp, mt, MEM_HEADS, MEM_DH)))
        wq, wo = mem_wq[layer].astype(BF16), mem_wo[layer].astype(BF16)
        xp = _mem_layer(xp, kmp.reshape(bp, mt, d), vmp.reshape(bp, mt, d), wq, wo, ln_g[layer, 1], ln_b[layer, 1])
        xs = _mem_layer(xs, cache_mem_k[layer].reshape(bs, -1, d), cache_mem_v[layer].reshape(bs, -1, d), wq, wo,
                        ln_g[layer, 1], ln_b[layer, 1])
        M = _prep_moe(moe_w_group[layer], moe_b_group[layer], moe_w_expert[layer], moe_b_expert[layer],
                      moe_w1[layer], moe_w3[layer], moe_w2[layer])
        xp = _moe_layer(xp.reshape(bp * tp, d), M, ln_g[layer, 2], ln_b[layer, 2]).reshape(bp, tp, d)
        xs = _moe_layer(xs.reshape(bs * ts, d), M, ln_g[layer, 2], ln_b[layer, 2]).reshape(bs, ts, d)

    st = {name: jnp.stack(vals) for name, vals in acc.items()}
    return (xp, xs,
            st['p_mlstm_C'], st['p_mlstm_n'], st['p_mlstm_m'],
            st['p_nsa_kc'], st['p_nsa_vc'], st['p_nsa_ks'], st['p_nsa_vs'], st['p_nsa_kw'], st['p_nsa_vw'],
            st['p_fox_k'], st['p_fox_v'], st['p_fox_logf'], st['p_rwkv_S'], st['p_rwkv_shift'],
            st['p_mem_k'], st['p_mem_v'],
            st['s_mlstm_C'], st['s_mlstm_n'], st['s_mlstm_m'],
            st['s_nsa_kc'], st['s_nsa_vc'], st['s_nsa_ks'], st['s_nsa_vs'], st['s_nsa_kw'], st['s_nsa_vw'],
            st['s_fox_k'], st['s_fox_v'], st['s_fox_logf'], st['s_rwkv_S'], st['s_rwkv_shift'])
```

```python
import functools

import numpy as np
import jax
import jax.numpy as jnp
from jax import lax
from jax.experimental import pallas as pl
from jax.experimental.pallas import tpu as pltpu

F32 = jnp.float32
BF16 = jnp.bfloat16
HI = lax.Precision.HIGHEST

D_MODEL = 1024
DEPTH = 2
PAGE = 128
M_HEADS, M_DK, M_DV, M_CHUNK = 4, 128, 128, 64
NSA_HEADS, NSA_GROUPS, NSA_DH = 8, 2, 64
CMP_BLOCK, SEL_BLOCK, N_SEL, WINDOW = 32, 64, 16, 512
FOX_HEADS, FOX_DH = 8, 64
RWKV_HEADS, RWKV_N, W_LORA, A_LORA, G_LORA = 8, 64, 64, 64, 128
RWKV_CHUNK = 64
MEM_HEADS = 4
MEM_DH = D_MODEL // MEM_HEADS
N_GROUPS, EXP_PER_GROUP, TOP_K, D_EXPERT = 4, 8, 2, 256
N_EXPERTS = N_GROUPS * EXP_PER_GROUP
ROPE_THETA = 10000.0
ALPHA = (2.0 * DEPTH) ** 0.25
LN_EPS = 1e-5
NEG = -1e30
FORCE = 1e4
MIX = 512
RWKV_SHIFT = 1792
SEL_OFF = 2.0 ** 30

LANES = 128
VMEM_LIMIT = 48 * 1024 * 1024
TQ_FOX, TK_FOX = 256, 512
TQ_NSA, TK_SLC, TK_WIN = 128, 512, 128


def _cp(*sem):
    return pltpu.CompilerParams(dimension_semantics=sem, vmem_limit_bytes=VMEM_LIMIT)


def _dot(a, b, prec=None):
    return jnp.dot(a, b, preferred_element_type=F32, precision=prec)


def _dot_nt(a, b, prec=None):
    return lax.dot_general(a, b, (((1,), (1,)), ((), ())), preferred_element_type=F32, precision=prec)


def _dot_tn(a, b, prec=None):
    return lax.dot_general(a, b, (((0,), (0,)), ((), ())), preferred_element_type=F32, precision=prec)


def _bdot(a, b):
    return _dot(a.astype(BF16), b.astype(BF16))


def _bdot_nt(a, b):
    return _dot_nt(a.astype(BF16), b.astype(BF16))


def _bdot_tn(a, b):
    return _dot_tn(a.astype(BF16), b.astype(BF16))


def _split_bf16(a):
    hi = a.astype(BF16)
    return hi, (a - hi.astype(F32)).astype(BF16)


def _dot3(a, b):
    ah, al = _split_bf16(a)
    bh, bl = _split_bf16(b)
    return _dot(ah, bh) + (_dot(ah, bl) + _dot(al, bh))


def _iota(shape, dim):
    return lax.broadcasted_iota(jnp.int32, shape, dim)


def _log_sigmoid(x):
    return jnp.minimum(x, 0.0) - jnp.log1p(jnp.exp(-jnp.abs(x)))


def _sigmoid(x):
    return 1.0 / (1.0 + jnp.exp(-x))


def _softplus(x):
    return jnp.maximum(x, 0.0) + jnp.log1p(jnp.exp(-jnp.abs(x)))


def _layer_norm_rows(z, g, b):
    mu = jnp.mean(z, axis=-1, keepdims=True)
    zc = z - mu
    var = jnp.mean(zc * zc, axis=-1, keepdims=True)
    return zc * lax.rsqrt(var + LN_EPS) * g + b


def _rope128(y, cos, sin_signed):
    lane = _iota(y.shape, 1)
    lo = (lane & 63) < 32
    sw = jnp.where(lo, pltpu.roll(y, 96, 1), pltpu.roll(y, 32, 1))
    return y * cos + sw * sin_signed


def _rope_tables(pos):
    half = NSA_DH // 2
    freq = ROPE_THETA ** (-jnp.arange(half, dtype=F32) / half)
    ang = pos.astype(F32)[:, None] * freq[None, :]
    cos, sin = jnp.cos(ang), jnp.sin(ang)
    return jnp.tile(cos, (1, 4)), jnp.tile(jnp.concatenate([-sin, sin], -1), (1, 2))


def _largest_divisor(n, options):
    for p in options:
        if n % p == 0:
            return p


def _proj_kernel(*refs, segs, use_rope):
    if use_rope:
        x_ref, w_ref, cos_ref, sin_ref = refs[:4]
        outs = refs[4:]
    else:
        x_ref, w_ref = refs[:2]
        outs = refs[2:]
    xb = x_ref[...].astype(BF16)
    off = 0
    for (wd, rp), o_ref in zip(segs, outs):
        y = _dot(xb, w_ref[:, off:off + wd])
        if rp:
            for c in range(wd // LANES):
                o_ref[:, c * LANES:(c + 1) * LANES] = _rope128(y[:, c * LANES:(c + 1) * LANES], cos_ref[...], sin_ref[...])
        else:
            o_ref[...] = y
        off += wd


def _proj_split(x, w, segs, rope=None):
    n, k = x.shape
    tm = min(256, n)
    wtot = sum(s[0] for s in segs)
    use_rope = rope is not None
    in_specs = [pl.BlockSpec((tm, k), lambda i: (i, 0)), pl.BlockSpec((k, wtot), lambda i: (0, 0))]
    args = [x, w]
    if use_rope:
        in_specs += [pl.BlockSpec((tm, LANES), lambda i: (i, 0))] * 2
        args += list(rope)
    return pl.pallas_call(
        functools.partial(_proj_kernel, segs=tuple(segs), use_rope=use_rope),
        out_shape=[jax.ShapeDtypeStruct((n, wd), F32) for wd, _ in segs],
        grid=(n // tm,),
        in_specs=in_specs,
        out_specs=[pl.BlockSpec((tm, wd), lambda i: (i, 0)) for wd, _ in segs],
        compiler_params=_cp("parallel"),
    )(*args)


def _mm_res_ln_kernel(*refs, n_in):
    a_refs = refs[:n_in]
    w_refs = refs[n_in:2 * n_in]
    x_ref, g_ref, b_ref, o_ref = refs[2 * n_in:]
    y = _dot(a_refs[0][...].astype(BF16), w_refs[0][...])
    for a_ref, w_ref in zip(a_refs[1:], w_refs[1:]):
        y = y + _dot(a_ref[...].astype(BF16), w_ref[...])
    o_ref[...] = _layer_norm_rows(ALPHA * x_ref[...] + y, g_ref[...], b_ref[...])


def _mm_res_ln(a_list, w_list, x, g, b):
    n, d = x.shape
    tm = min(512, n)
    n_in = len(a_list)
    in_specs = ([pl.BlockSpec((tm, a.shape[1]), lambda i: (i, 0)) for a in a_list]
                + [pl.BlockSpec(w.shape, lambda i: (0, 0)) for w in w_list]
                + [pl.BlockSpec((tm, d), lambda i: (i, 0)), pl.BlockSpec((1, d), lambda i: (0, 0)),
                   pl.BlockSpec((1, d), lambda i: (0, 0))])
    return pl.pallas_call(
        functools.partial(_mm_res_ln_kernel, n_in=n_in),
        out_shape=jax.ShapeDtypeStruct((n, d), F32),
        grid=(n // tm,),
        in_specs=in_specs,
        out_specs=pl.BlockSpec((tm, d), lambda i: (i, 0)),
        compiler_params=_cp("parallel"),
    )(*a_list, *w_list, x, g.reshape(1, d), b.reshape(1, d))


def _mlstm_kernel(q_ref, k_ref, v_ref, o_ref, sm_ref, bias_ref, g_ref, c0_ref, n0_ref, m0_ref,
                  h_ref, c1_ref, n1_ref, m1_ref, c_scr, n_scr, m_scr, *, L):
    ci = pl.program_id(1)

    @pl.when(ci == 0)
    def _():
        c_scr[...] = c0_ref[...]
        n_scr[...] = n0_ref[...]
        m_scr[...] = m0_ref[...]

    sm = sm_ref[...] + bias_ref[...]
    lane = _iota(sm.shape, 1)
    gates = jnp.where(lane < M_HEADS, sm, _log_sigmoid(sm))
    eye8 = (_iota((8, LANES), 0) == _iota((8, LANES), 1)).astype(F32)
    gates_row = _dot_nt(eye8, gates, HI)
    r_i = _iota((L, L), 0)
    c_i = _iota((L, L), 1)
    tri = c_i <= r_i
    scale = M_DK ** -0.5
    for h in range(M_HEADS):
        ic_col = gates[:, h:h + 1]
        f_col = gates[:, M_HEADS + h:M_HEADS + h + 1]
        ic_row = gates_row[h:h + 1, :]
        f_row = gates_row[M_HEADS + h:M_HEADS + h + 1, :]
        b_col = jnp.sum(jnp.where(tri, jnp.broadcast_to(f_row, (L, L)), 0.0), axis=1, keepdims=True)
        b_row = jnp.sum(jnp.where(r_i <= c_i, jnp.broadcast_to(f_col, (L, L)), 0.0), axis=0, keepdims=True)
        m_prev = m_scr[h:h + 1, 0:1]
        dmat = jnp.where(tri, b_col - b_row + ic_row, NEG)
        inter = b_col + m_prev
        mt = jnp.maximum(inter, jnp.max(dmat, axis=1, keepdims=True))
        wmat = jnp.where(tri, jnp.exp(dmat - mt), 0.0)
        a = jnp.exp(inter - mt)
        qh = q_ref[:, h * M_DK:(h + 1) * M_DK]
        kh = k_ref[:, h * M_DK:(h + 1) * M_DK] * scale
        vh = v_ref[:, h * M_DV:(h + 1) * M_DV]
        qb, kb, vb = qh.astype(BF16), kh.astype(BF16), vh.astype(BF16)
        c_old = c_scr[h]
        n_old = n_scr[h:h + 1, :]
        s = _dot_nt(qb, kb) * wmat
        num = a * _dot_nt(qb, c_old.astype(BF16)) + _dot(s.astype(BF16), vb)
        den = a * jnp.sum(qh * n_old, axis=1, keepdims=True) + jnp.sum(s, axis=1, keepdims=True)
        hh = num / jnp.maximum(jnp.abs(den), jnp.exp(-mt))
        m_l = mt[L - 1:L, :]
        b_l = b_col[L - 1:L, :]
        g_l = jnp.exp(b_l - b_col + ic_col - m_l)
        a_l = jnp.exp(b_l + m_prev - m_l)
        c_scr[h] = a_l * c_old + _dot_tn((g_l * vh).astype(BF16), kb)
        n_scr[h:h + 1, :] = a_l * n_old + jnp.sum(g_l * kh, axis=0, keepdims=True)
        m_scr[h:h + 1, :] = jnp.broadcast_to(m_l, (1, LANES))
        mu = jnp.mean(hh, axis=1, keepdims=True)
        hc = hh - mu
        var = jnp.mean(hc * hc, axis=1, keepdims=True)
        hn = hc * lax.rsqrt(var + 1e-6) * g_ref[:, h * M_DV:(h + 1) * M_DV]
        h_ref[:, h * M_DV:(h + 1) * M_DV] = hn * _sigmoid(o_ref[:, h * M_DV:(h + 1) * M_DV])

    @pl.when(ci == pl.num_programs(1) - 1)
    def _():
        c1_ref[...] = c_scr[...]
        n1_ref[...] = n_scr[...]
        m1_ref[...] = m_scr[...]


def _mlstm(mq, mk, mv, mo, small, bias_row, norm_g, c0, n0, m0, B, T):
    L = M_CHUNK if T % M_CHUNK == 0 else T
    nch = T // L
    row = lambda b, c: (b * nch + c, 0)
    st4 = lambda b, c: (b, 0, 0, 0)
    st3 = lambda b, c: (b, 0, 0)
    m0b = jnp.broadcast_to(m0[:, :, None], (B, M_HEADS, LANES)).astype(F32)
    h, c1, n1, m1 = pl.pallas_call(
        functools.partial(_mlstm_kernel, L=L),
        out_shape=[jax.ShapeDtypeStruct((B * T, MIX), F32),
                   jax.ShapeDtypeStruct((B, M_HEADS, M_DV, M_DK), F32),
                   jax.ShapeDtypeStruct((B, M_HEADS, M_DK), F32),
                   jax.ShapeDtypeStruct((B, M_HEADS, LANES), F32)],
        grid=(B, nch),
        in_specs=[pl.BlockSpec((L, MIX), row)] * 4
        + [pl.BlockSpec((L, LANES), row), pl.BlockSpec((1, LANES), lambda b, c: (0, 0)),
           pl.BlockSpec((1, MIX), lambda b, c: (0, 0)),
           pl.BlockSpec((None, M_HEADS, M_DV, M_DK), st4), pl.BlockSpec((None, M_HEADS, M_DK), st3),
           pl.BlockSpec((None, M_HEADS, LANES), st3)],
        out_specs=[pl.BlockSpec((L, MIX), row), pl.BlockSpec((None, M_HEADS, M_DV, M_DK), st4),
                   pl.BlockSpec((None, M_HEADS, M_DK), st3), pl.BlockSpec((None, M_HEADS, LANES), st3)],
        scratch_shapes=[pltpu.VMEM((M_HEADS, M_DV, M_DK), F32), pltpu.VMEM((M_HEADS, M_DK), F32),
                        pltpu.VMEM((M_HEADS, LANES), F32)],
        compiler_params=_cp("parallel", "arbitrary"),
    )(mq, mk, mv, mo, small, bias_row, norm_g.reshape(1, MIX), c0.astype(F32), n0.astype(F32), m0b)
    return h, c1, n1, m1[:, :, 0]


def _compress_weights(pe, w1, w2):
    eye = jnp.eye(NSA_GROUPS, dtype=F32)
    w1j = jnp.einsum('jdo,gh->jgdho', w1.reshape(CMP_BLOCK, NSA_DH, NSA_DH), eye)
    w1j = w1j.reshape(CMP_BLOCK, LANES, LANES).astype(BF16)
    w2b = jnp.einsum('do,gh->gdho', w2, eye).reshape(LANES, LANES).astype(BF16)
    pe2 = jnp.tile(pe, (1, NSA_GROUPS))
    return pe2, w1j, w2b


def _compress_kernel(x_ref, pe_ref, w1_ref, w2_ref, o_ref):
    xb = (x_ref[...] + pe_ref[...]).astype(BF16)
    hid = jax.nn.gelu(_dot(xb, w1_ref[...]))
    o_ref[...] = _dot(hid.astype(BF16), w2_ref[...])


def _compress(blocks, pe, w1, w2):
    r = blocks.shape[0]
    tb = min(256, r)
    width = CMP_BLOCK * LANES
    pe2, w1j, w2b = _compress_weights(pe, w1, w2)
    return pl.pallas_call(
        _compress_kernel,
        out_shape=jax.ShapeDtypeStruct((r, LANES), F32),
        grid=(pl.cdiv(r, tb),),
        in_specs=[pl.BlockSpec((tb, width), lambda i: (i, 0)), pl.BlockSpec((1, width), lambda i: (0, 0)),
                  pl.BlockSpec((width, LANES), lambda i: (0, 0)), pl.BlockSpec((LANES, LANES), lambda i: (0, 0))],
        out_specs=pl.BlockSpec((tb, LANES), lambda i: (i, 0)),
        compiler_params=_cp("parallel"),
    )(blocks, pe2.reshape(1, width), w1j.reshape(width, LANES), w2b)


def _compress_pool_kernel(xt_ref, pe_ref, w1_ref, w2_ref, o_ref, xs_scr, *, pb):
    per_page = PAGE // CMP_BLOCK
    for p in range(pb):
        xs_scr[p * PAGE:(p + 1) * PAGE, :] = xt_ref[p].T + pe_ref[...]
    acc = None
    for j in range(CMP_BLOCK):
        rows = xs_scr[pl.ds(j, per_page * pb, stride=CMP_BLOCK), :]
        t = _dot(rows.astype(BF16), w1_ref[j])
        acc = t if acc is None else acc + t
    o_ref[...] = _dot(jax.nn.gelu(acc).astype(BF16), w2_ref[...])


def _compress_pool(pool_t, pe, w1, w2):
    npool = pool_t.shape[0]
    per_page = PAGE // CMP_BLOCK
    pb = _largest_divisor(npool, (16, 8, 4, 2))
    pe2, w1j, w2b = _compress_weights(pe, w1, w2)
    pe_tile = jnp.tile(pe2, (per_page, 1))
    return pl.pallas_call(
        functools.partial(_compress_pool_kernel, pb=pb),
        out_shape=jax.ShapeDtypeStruct((npool * per_page, LANES), F32),
        grid=(npool // pb,),
        in_specs=[pl.BlockSpec((pb, LANES, PAGE), lambda i: (i, 0, 0)), pl.BlockSpec((PAGE, LANES), lambda i: (0, 0)),
                  pl.BlockSpec((CMP_BLOCK, LANES, LANES), lambda i: (0, 0, 0)),
                  pl.BlockSpec((LANES, LANES), lambda i: (0, 0))],
        out_specs=pl.BlockSpec((per_page * pb, LANES), lambda i: (i, 0)),
        scratch_shapes=[pltpu.VMEM((pb * PAGE, LANES), F32)],
        compiler_params=_cp("parallel"),
    )(pool_t, pe_tile, w1j, w2b)


def _rope_rows_kernel(x_ref, cos_ref, sin_ref, o_ref):
    o_ref[...] = _rope128(x_ref[...], cos_ref[...], sin_ref[...])


def _rope_rows(x, cos, sin):
    r = x.shape[0]
    tb = min(512, r)
    spec = pl.BlockSpec((tb, LANES), lambda i: (i, 0))
    return pl.pallas_call(_rope_rows_kernel, out_shape=jax.ShapeDtypeStruct((r, LANES), F32), grid=(r // tb,),
                          in_specs=[spec] * 3, out_specs=spec, compiler_params=_cp("parallel"))(x, cos, sin)


def _stack_heads(qblk, heads, scale):
    rows = []
    for src_blk, src_half, dst_half in heads:
        x = qblk[:, src_blk * LANES:(src_blk + 1) * LANES] * scale
        if src_half != dst_half:
            x = pltpu.roll(x, 64, 1)
        lane = _iota(x.shape, 1)
        keep = (lane >= 64) if dst_half else (lane < 64)
        rows.append(jnp.where(keep, x, 0.0))
    return jnp.concatenate(rows, axis=0)


def _unstack_heads(o_full_list, heads_per_kb, tq, n_out_blocks):
    outs = [None] * n_out_blocks
    for o_full, heads in zip(o_full_list, heads_per_kb):
        for idx, (src_blk, src_half, dst_half) in enumerate(heads):
            piece = o_full[idx * tq:(idx + 1) * tq]
            if src_half != dst_half:
                piece = pltpu.roll(piece, 64, 1)
            lane = _iota(piece.shape, 1)
            keep = (lane >= 64) if src_half else (lane < 64)
            prev = outs[src_blk]
            outs[src_blk] = jnp.where(keep, piece, 0.0 if prev is None else prev)
    return outs


def _stack_wide(qblk, n_heads, scale):
    head_of_lane = _iota(qblk.shape, 1) >> 6
    return jnp.concatenate([jnp.where(head_of_lane == h, qblk * scale, 0.0) for h in range(n_heads)], axis=0)


def _unstack_wide(o_full, n_heads, tq):
    head_of_lane = _iota((tq, o_full.shape[1]), 1) >> 6
    out = jnp.zeros((tq, o_full.shape[1]), F32)
    for h in range(n_heads):
        out = jnp.where(head_of_lane == h, o_full[h * tq:(h + 1) * tq], out)
    return out


NSA_HEAD_MAP = [[(h // 2, h % 2, h // (NSA_HEADS // NSA_GROUPS)) for h in range(NSA_HEADS)]]
FOX_HEAD_MAP = [[(kb, 0, 0), (kb, 1, 1)] for kb in range(FOX_HEADS // 2)]


def _cmp_sel_kernel(q_ref, kc_ref, vc_ref, cend_ref, o_ref, sel_ref, *, tq, q0, nc, ns, ns_pad):
    qi = pl.program_id(1)
    heads = NSA_HEAD_MAP[0]
    hpg = NSA_HEADS // NSA_GROUPS
    qbd = _stack_heads(q_ref[...], heads, NSA_DH ** -0.5).astype(BF16)
    s = _dot_nt(qbd, kc_ref[...].astype(BF16))
    rows = NSA_HEADS * tq
    qpos = q0 + qi * tq + (_iota((rows, 1), 0) & (tq - 1))
    mask = cend_ref[...] <= qpos
    s = jnp.where(mask, s, NEG)
    e = jnp.where(mask, jnp.exp(s - jnp.max(s, axis=1, keepdims=True)), 0.0)
    p = e / jnp.maximum(jnp.sum(e, axis=1, keepdims=True), 1e-30)
    o_full = _dot(p.astype(BF16), vc_ref[...].astype(BF16))
    outs = _unstack_heads([o_full], NSA_HEAD_MAP, tq, MIX // LANES)
    for c, blk in enumerate(outs):
        o_ref[:, c * LANES:(c + 1) * LANES] = blk
    half = nc // 2
    qp = q0 + qi * tq + _iota((tq, 1), 0)
    blk_i = _iota((tq, ns_pad), 1)
    blk_f = blk_i.astype(F32)
    cur = qp >> 6
    forced = (blk_i == 0) | (blk_i == cur) | (blk_i == cur - 1)
    valid = (blk_i * SEL_BLOCK <= qp) & (blk_i < ns)
    n_sel = min(N_SEL, ns)
    for g in range(NSA_GROUPS):
        pg = p[g * hpg * tq:(g * hpg + 1) * tq]
        for j in range(1, hpg):
            pg = pg + p[(g * hpg + j) * tq:(g * hpg + j + 1) * tq]
        imp = pg[:, :half] + pg[:, half:]
        if ns_pad > half:
            imp = jnp.concatenate([imp, jnp.zeros((tq, ns_pad - half), F32)], axis=1)
        score = jnp.where(valid, imp + FORCE * forced.astype(F32), NEG)
        score = jnp.where(blk_i < ns, score, -jnp.inf)
        chosen = jnp.zeros((tq, ns_pad), F32)
        for _ in range(n_sel):
            mx = jnp.max(score, axis=1, keepdims=True)
            first = jnp.min(jnp.where(score == mx, blk_f, 1e9), axis=1, keepdims=True)
            hit = blk_f == first
            chosen = jnp.where(hit, 1.0, chosen)
            score = jnp.where(hit, -jnp.inf, score)
        sel_ref[:, g * ns_pad:(g + 1) * ns_pad] = jnp.where(valid, chosen, 0.0)


def _cmp_sel(q3, kcmp, vcmp, cend, q0, ns, ns_pad):
    B, T, _ = q3.shape
    nc = kcmp.shape[1]
    tq = min(64, T)
    return pl.pallas_call(
        functools.partial(_cmp_sel_kernel, tq=tq, q0=q0, nc=nc, ns=ns, ns_pad=ns_pad),
        out_shape=[jax.ShapeDtypeStruct((B, T, MIX), F32), jax.ShapeDtypeStruct((B, T, NSA_GROUPS * ns_pad), F32)],
        grid=(B, T // tq),
        in_specs=[pl.BlockSpec((None, tq, MIX), lambda b, i: (b, i, 0)),
                  pl.BlockSpec((None, nc, LANES), lambda b, i: (b, 0, 0)),
                  pl.BlockSpec((None, nc, LANES), lambda b, i: (b, 0, 0)),
                  pl.BlockSpec((1, nc), lambda b, i: (0, 0))],
        out_specs=[pl.BlockSpec((None, tq, MIX), lambda b, i: (b, i, 0)),
                   pl.BlockSpec((None, tq, NSA_GROUPS * ns_pad), lambda b, i: (b, i, 0))],
        compiler_params=_cp("parallel", "arbitrary"),
    )(q3, kcmp, vcmp, cend)


def _flash_kernel(*refs, cfg):
    mode, layout, tq, tk = cfg['mode'], cfg['layout'], cfg['tq'], cfg['tk']
    q0, kbase, pps, has_tail, tail_pos0 = cfg['q0'], cfg['kbase'], cfg['pps'], cfg['has_tail'], cfg['tail_pos0']
    ns_pad, n_pref, qw, kv_t = cfg['ns_pad'], cfg['n_pref'], cfg['qw'], cfg['kv_t']
    head_map = NSA_HEAD_MAP if layout == 'nsa' else FOX_HEAD_MAP
    nkb = 1 if layout in ('nsa', 'fox_wide') else FOX_HEADS // 2
    n_heads_kb = FOX_HEADS if layout == 'fox_wide' else len(head_map[0])
    rows = n_heads_kb * tq
    qi_ref, kj_ref, fl_ref = refs[:3]
    pos = n_pref
    q_ref = refs[pos]; pos += 1
    npg = max(pps, 1)
    k_refs = refs[pos:pos + npg]; pos += npg
    v_refs = refs[pos:pos + npg]; pos += npg
    if has_tail:
        tk_ref, tv_ref = refs[pos:pos + 2]; pos += 2
    c_ref = ct_ref = sel_ref = None
    if cfg['has_caug'] or cfg['has_crow']:
        c_ref = refs[pos]; pos += 1
        if has_tail:
            ct_ref = refs[pos]; pos += 1
    if mode == 'sel':
        sel_ref = refs[pos]; pos += 1
    o_ref = refs[pos]; pos += 1
    qst_scr, m_scr, l_scr, acc_scr = refs[pos:pos + 4]

    s_id = pl.program_id(1)
    qi = qi_ref[s_id]
    kj = kj_ref[s_id]
    fl = fl_ref[s_id]
    scale = NSA_DH ** -0.5

    @pl.when((fl & 1) != 0)
    def _init():
        qblk = q_ref[...]
        if layout == 'fox_wide':
            qst_scr[0] = _stack_wide(qblk, FOX_HEADS, scale).astype(BF16)
        elif layout == 'nsa':
            qbd = _stack_heads(qblk, head_map[0], scale).astype(BF16)
            if mode == 'sel':
                selblk = sel_ref[...]
                hpg = NSA_HEADS // NSA_GROUPS
                parts = []
                for g in range(NSA_GROUPS):
                    off = ((selblk[:, g * ns_pad:(g + 1) * ns_pad] - 1.0) * SEL_OFF).astype(BF16)
                    parts += [off] * hpg
                qbd = jnp.concatenate([qbd, jnp.concatenate(parts, axis=0)], axis=1)
            qst_scr[0] = qbd
        else:
            for kb in range(nkb):
                qbd = _stack_heads(qblk, head_map[kb], scale).astype(BF16)
                if cfg['has_caug']:
                    lane = _iota((rows, LANES), 1)
                    head = 2 * kb + (_iota((rows, LANES), 0) >= tq).astype(jnp.int32)
                    aug = jnp.where((lane >= 3 * head) & (lane < 3 * head + 3), -1.0, 0.0).astype(BF16)
                    qbd = jnp.concatenate([qbd, aug], axis=1)
                qst_scr[kb] = qbd
        m_scr[...] = jnp.full(m_scr.shape, NEG, F32)
        l_scr[...] = jnp.zeros(l_scr.shape, F32)
        acc_scr[...] = jnp.zeros(acc_scr.shape, F32)

    def update(k_tile, v_tile, kpos0, c_tile, width, masked):
        if masked:
            qpos = q0 + qi * tq + (_iota((rows, 1), 0) & (tq - 1))
            kpos = kpos0 + _iota((1, width), 1)
            mask = kpos <= qpos
            if mode == 'band':
                mask = mask & (kpos > qpos - WINDOW)
        for kb in range(nkb):
            if kv_t:
                kt = k_tile.astype(BF16)
                if mode == 'sel':
                    onehot = (_iota((ns_pad, width), 0) == (kpos0 >> 6) + (_iota((ns_pad, width), 1) >> 6))
                    kt = jnp.concatenate([kt, onehot.astype(BF16)], axis=0)
                sc = _dot(qst_scr[kb], kt)
                if c_tile is not None:
                    sc = sc - jnp.concatenate([jnp.broadcast_to(c_tile[h:h + 1, :], (tq, width))
                                               for h in range(n_heads_kb)], axis=0)
            else:
                kt = k_tile[:, kb * LANES:(kb + 1) * LANES].astype(BF16)
                if mode == 'sel':
                    onehot = (_iota((width, ns_pad), 1) == (kpos0 >> 6) + (_iota((width, ns_pad), 0) >> 6))
                    kt = jnp.concatenate([kt, onehot.astype(BF16)], axis=1)
                if c_tile is not None:
                    kt = jnp.concatenate([kt, c_tile], axis=1)
                sc = _dot_nt(qst_scr[kb], kt)
            if masked:
                sc = jnp.where(mask, sc, NEG)
            m_old = m_scr[kb]
            m_new = jnp.maximum(m_old, jnp.max(sc, axis=1, keepdims=True))
            alpha = jnp.exp(m_old - m_new)
            p = jnp.exp(sc - m_new)
            l_scr[kb] = alpha * l_scr[kb] + jnp.sum(p, axis=1, keepdims=True)
            if kv_t:
                pv = _dot_nt(p.astype(BF16), v_tile.astype(BF16))
            else:
                pv = _dot(p.astype(BF16), v_tile[:, kb * LANES:(kb + 1) * LANES].astype(BF16))
            acc_scr[kb] = alpha * acc_scr[kb] + pv
            m_scr[kb] = m_new

    def main_update(masked):
        if pps > 1:
            k_tile = jnp.concatenate([r[...] for r in k_refs], axis=1)
            v_tile = jnp.concatenate([r[...] for r in v_refs], axis=1)
        else:
            k_tile, v_tile = k_refs[0][...], v_refs[0][...]
        update(k_tile, v_tile, kbase + kj * tk, c_ref[...] if c_ref is not None else None, tk, masked)

    is_tail = (fl & 4) != 0
    need_mask = (fl & 8) != 0
    if has_tail:
        @pl.when(is_tail)
        def _():
            update(tk_ref[...], tv_ref[...], tail_pos0, ct_ref[...] if ct_ref is not None else None, PAGE, True)

    @pl.when(jnp.logical_and(jnp.logical_not(is_tail), need_mask))
    def _():
        main_update(True)

    @pl.when(jnp.logical_and(jnp.logical_not(is_tail), jnp.logical_not(need_mask)))
    def _():
        main_update(False)

    @pl.when((fl & 2) != 0)
    def _fin():
        fulls = [acc_scr[kb] / jnp.maximum(l_scr[kb], 1e-30) for kb in range(nkb)]
        if layout == 'fox_wide':
            o_ref[...] = _unstack_wide(fulls[0], FOX_HEADS, tq)
        else:
            outs = _unstack_heads(fulls, head_map, tq, qw // LANES)
            for c, blk in enumerate(outs):
                o_ref[:, c * LANES:(c + 1) * LANES] = blk


def _schedule(nq, tq, tk, q0, kbase, n_ktiles, mode, has_tail):
    qi_l, kj_l, fl_l = [], [], []
    for qi in range(nq):
        q_lo, q_hi = q0 + qi * tq, q0 + (qi + 1) * tq - 1
        entries = []
        for kj in range(n_ktiles):
            k_lo, k_hi = kbase + kj * tk, kbase + (kj + 1) * tk - 1
            if k_lo > q_hi:
                continue
            if mode == 'band' and k_hi <= q_lo - WINDOW:
                continue
            all_visible = k_hi <= q_lo and (mode != 'band' or k_lo > q_hi - WINDOW)
            entries.append((kj, 0 if all_visible else 8))
        if has_tail:
            entries.append((entries[-1][0] if entries else 0, 4 | 8))
        for n, (kj, f) in enumerate(entries):
            f |= (1 if n == 0 else 0) | (2 if n == len(entries) - 1 else 0)
            qi_l.append(qi); kj_l.append(kj); fl_l.append(f)
    return (np.asarray(qi_l, np.int32), np.asarray(kj_l, np.int32), np.asarray(fl_l, np.int32))


def _flash(q3, k3, v3, *, mode, layout, tq, tk, q0, kbase=0, page_table=None, pps=0, tail=None, tail_pos0=0,
           caug=None, crow=None, crow_tail=None, sel=None, ns_pad=0):
    B, Tq, qw = q3.shape
    nq = Tq // tq
    paged = page_table is not None
    has_tail = tail is not None
    if paged:
        kw = k3.shape[1]
        npages = page_table.shape[1]
        tk = pps * PAGE
        n_ktiles = npages // pps
    else:
        kw = k3.shape[-1]
        n_ktiles = k3.shape[1] // tk
    qi_t, kj_t, fl_t = _schedule(nq, tq, tk, q0, kbase, n_ktiles, mode, has_tail)
    nsteps = len(qi_t)
    prefetch = [jnp.asarray(qi_t), jnp.asarray(kj_t), jnp.asarray(fl_t)]
    if paged:
        prefetch.append(page_table.reshape(-1).astype(jnp.int32))
    n_pref = len(prefetch)
    nkb = 1 if layout in ('nsa', 'fox_wide') else FOX_HEADS // 2
    n_heads_kb = FOX_HEADS if layout == 'fox_wide' else (NSA_HEADS if layout == 'nsa' else 2)
    rows = n_heads_kb * tq
    cq = kw if layout == 'fox_wide' else LANES
    if mode == 'sel':
        cq += ns_pad
    if caug is not None:
        cq += LANES
    cv = kw if layout == 'fox_wide' else LANES

    in_specs = [pl.BlockSpec((None, tq, qw), lambda b, s, qi, kj, fl, *_: (b, qi[s], 0))]
    args = [q3]
    if paged:
        def page_map(i):
            return lambda b, s, qi, kj, fl, pt: (pt[b * npages + kj[s] * pps + i], 0, 0)
        for arr in (k3, v3):
            for i in range(pps):
                in_specs.append(pl.BlockSpec((None, kw, PAGE), page_map(i)))
                args.append(arr)
    else:
        for arr in (k3, v3):
            in_specs.append(pl.BlockSpec((None, tk, kw), lambda b, s, qi, kj, fl, *_: (b, kj[s], 0)))
            args.append(arr)
    if has_tail:
        for arr in tail:
            in_specs.append(pl.BlockSpec((None, kw, PAGE), lambda b, s, qi, kj, fl, *_: (b, 0, 0)))
            args.append(arr)
    if caug is not None:
        in_specs.append(pl.BlockSpec((None, tk, LANES), lambda b, s, qi, kj, fl, *_: (b, kj[s], 0)))
        args.append(caug)
    if crow is not None:
        in_specs.append(pl.BlockSpec((None, 8, tk), lambda b, s, qi, kj, fl, *_: (b, 0, kj[s])))
        args.append(crow)
        if has_tail:
            in_specs.append(pl.BlockSpec((None, 8, PAGE), lambda b, s, qi, kj, fl, *_: (b, 0, 0)))
            args.append(crow_tail)
    if mode == 'sel':
        in_specs.append(pl.BlockSpec((None, tq, NSA_GROUPS * ns_pad), lambda b, s, qi, kj, fl, *_: (b, qi[s], 0)))
        args.append(sel)
    cfg = dict(mode=mode, layout=layout, tq=tq, tk=tk, q0=q0, kbase=kbase, pps=pps if paged else 0,
               has_tail=has_tail, tail_pos0=tail_pos0, ns_pad=ns_pad, n_pref=n_pref, qw=qw, kv_t=paged,
               has_caug=caug is not None, has_crow=crow is not None)
    return pl.pallas_call(
        functools.partial(_flash_kernel, cfg=cfg),
        out_shape=jax.ShapeDtypeStruct((B, Tq, qw), F32),
        grid_spec=pltpu.PrefetchScalarGridSpec(
            num_scalar_prefetch=n_pref,
            grid=(B, nsteps),
            in_specs=in_specs,
            out_specs=pl.BlockSpec((None, tq, qw), lambda b, s, qi, kj, fl, *_: (b, qi[s], 0)),
            scratch_shapes=[pltpu.VMEM((nkb, rows, cq), BF16), pltpu.VMEM((nkb, rows, 1), F32),
                            pltpu.VMEM((nkb, rows, 1), F32), pltpu.VMEM((nkb, rows, cv), F32)]),
        compiler_params=_cp("parallel", "arbitrary"),
    )(*prefetch, *args)


def _nsa_combine_kernel(oc_ref, os_ref, ow_ref, sm_ref, e_ref, o_ref):
    gate = _sigmoid(sm_ref[...])
    acc = None
    for j, br in enumerate((oc_ref, os_ref, ow_ref)):
        gexp = _dot(gate, e_ref[j], HI)
        acc = gexp * br[...] if acc is None else acc + gexp * br[...]
    o_ref[...] = acc


def _nsa_combine(o_cmp, o_slc, o_win, small):
    n = o_cmp.shape[0]
    tm = min(512, n)
    src = np.zeros((3, LANES, MIX), np.float32)
    for h in range(NSA_HEADS):
        for j in range(3):
            src[j, 2 * M_HEADS + 3 * h + j, h * NSA_DH:(h + 1) * NSA_DH] = 1.0
    spec = pl.BlockSpec((tm, MIX), lambda i: (i, 0))
    return pl.pallas_call(
        _nsa_combine_kernel, out_shape=jax.ShapeDtypeStruct((n, MIX), F32), grid=(n // tm,),
        in_specs=[spec, spec, spec, pl.BlockSpec((tm, LANES), lambda i: (i, 0)),
                  pl.BlockSpec((3, LANES, MIX), lambda i: (0, 0, 0))],
        out_specs=spec, compiler_params=_cp("parallel"))(o_cmp, o_slc, o_win, small, jnp.asarray(src))


def _fox_cumsum_kernel(sm_ref, bias_ref, place_ref, lf_ref, caug_ref, carry, *, tt):
    @pl.when(pl.program_id(1) == 0)
    def _():
        carry[...] = jnp.zeros(carry.shape, F32)

    lf = _log_sigmoid(sm_ref[...] + bias_ref[...])
    lf_ref[...] = lf
    lower = (_iota((tt, tt), 1) <= _iota((tt, tt), 0)).astype(F32)
    c = _dot(lower, lf, HI) + carry[0:1, :]
    carry[...] = jnp.broadcast_to(c[tt - 1:tt, :], carry.shape)
    c_hi = c.astype(BF16)
    r1 = c - c_hi.astype(F32)
    c_mid = r1.astype(BF16)
    c_lo = (r1 - c_mid.astype(F32)).astype(BF16)
    caug = _dot(c_hi, place_ref[0]) + _dot(c_mid, place_ref[1]) + _dot(c_lo, place_ref[2])
    caug_ref[...] = caug.astype(BF16)


def _fox_cumsum(small, bias_row, B, T):
    tt = min(256, T)
    row = lambda b, i: (b * (T // tt) + i, 0)
    place = np.zeros((3, LANES, LANES), np.float32)
    for h in range(FOX_HEADS):
        for j in range(3):
            place[j, h, 3 * h + j] = 1.0
    return pl.pallas_call(
        functools.partial(_fox_cumsum_kernel, tt=tt),
        out_shape=[jax.ShapeDtypeStruct((B * T, LANES), F32), jax.ShapeDtypeStruct((B * T, LANES), BF16)],
        grid=(B, T // tt),
        in_specs=[pl.BlockSpec((tt, LANES), row), pl.BlockSpec((1, LANES), lambda b, i: (0, 0)),
                  pl.BlockSpec((3, LANES, LANES), lambda b, i: (0, 0, 0))],
        out_specs=[pl.BlockSpec((tt, LANES), row), pl.BlockSpec((tt, LANES), row)],
        scratch_shapes=[pltpu.VMEM((8, LANES), F32)],
        compiler_params=_cp("parallel", "arbitrary"),
    )(small, bias_row, jnp.asarray(place, BF16))


def _fox_cumsum_paged_kernel(pt_ref, *refs, pps, n_main):
    lf_refs = refs[:pps]
    sm_ref, bias_ref, c_ref, lfn_ref, carry = refs[pps:]
    j = pl.program_id(1)

    @pl.when(j == 0)
    def _():
        carry[...] = jnp.zeros(carry.shape, F32)

    upper = (_iota((PAGE, PAGE), 0) <= _iota((PAGE, PAGE), 1)).astype(F32)

    @pl.when(j < n_main)
    def _():
        run = carry[:, 0:1]
        for i in range(pps):
            c = _dot(lf_refs[i][...], upper, HI) + run
            c_ref[:, i * PAGE:(i + 1) * PAGE] = c
            run = c[:, PAGE - 1:PAGE]
        carry[...] = jnp.broadcast_to(run, carry.shape)

    @pl.when(j == n_main)
    def _():
        lf = _log_sigmoid(sm_ref[...] + bias_ref[...])
        lfn_ref[...] = lf
        eye8 = (_iota((8, LANES), 0) == _iota((8, LANES), 1)).astype(F32)
        c_new = _dot(_dot_nt(eye8, lf, HI), upper, HI) + carry[:, 0:1]
        for i in range(pps):
            c_ref[:, i * PAGE:(i + 1) * PAGE] = c_new


def _fox_cumsum_paged(lf_pool_t, page_table, small_pad, bias_row, pps):
    B, npages = page_table.shape
    n_main = npages // pps

    def page_map(i):
        return lambda b, j, pt: (pt[b * npages + jnp.minimum(j, n_main - 1) * pps + i], 0, 0)

    return pl.pallas_call(
        functools.partial(_fox_cumsum_paged_kernel, pps=pps, n_main=n_main),
        out_shape=[jax.ShapeDtypeStruct((B, 8, (n_main + 1) * pps * PAGE), F32),
                   jax.ShapeDtypeStruct((B, PAGE, LANES), F32)],
        grid_spec=pltpu.PrefetchScalarGridSpec(
            num_scalar_prefetch=1, grid=(B, n_main + 1),
            in_specs=[pl.BlockSpec((None, 8, PAGE), page_map(i)) for i in range(pps)]
            + [pl.BlockSpec((None, PAGE, LANES), lambda b, j, pt: (b, 0, 0)),
               pl.BlockSpec((1, LANES), lambda b, j, pt: (0, 0))],
            out_specs=[pl.BlockSpec((None, 8, pps * PAGE), lambda b, j, pt: (b, 0, j)),
                       pl.BlockSpec((None, PAGE, LANES), lambda b, j, pt: (b, 0, 0))],
            scratch_shapes=[pltpu.VMEM((8, LANES), F32)]),
        compiler_params=_cp("parallel", "arbitrary"),
    )(page_table.reshape(-1).astype(jnp.int32), *([lf_pool_t] * pps), small_pad, bias_row)


def _rwkv_prep_kernel(rw_ref, prev_ref, sh_ref, mu_ref, w0_ref, w2_ref, a0_ref, a2_ref, g2_ref, kk_ref, ka_ref,
                      bd_ref, r_o, lw_o, k_o, v_o, kk_o, a_o, g_o, *, tt):
    i = pl.program_id(1)
    rw = rw_ref[...]
    first_prev = jnp.where(i == 0, sh_ref[...], prev_ref[7:8, :])
    prev = jnp.where(_iota(rw.shape, 0) == 0, first_prev, pltpu.roll(rw, 1, 0))
    m = rw + mu_ref[...] * (prev - rw)
    r, kr, vr = m[:, 0:MIX], m[:, MIX:2 * MIX], m[:, 2 * MIX:3 * MIX]
    lora = m[:, 3 * MIX:3 * MIX + LANES]
    gl = m[:, 3 * MIX + LANES:3 * MIX + 2 * LANES]
    w = -_softplus(-(w0_ref[...] + _dot(jnp.tanh(lora).astype(BF16), w2_ref[...]))) - 0.5
    a = _sigmoid(a0_ref[...] + _dot(lora.astype(BF16), a2_ref[...]))
    kk = kr * kk_ref[...]
    ss = _dot(kk * kk, bd_ref[...], HI)
    r_o[...] = r
    lw_o[...] = -jnp.exp(w)
    k_o[...] = kr * (1.0 + (a - 1.0) * ka_ref[...])
    v_o[...] = vr
    kk_o[...] = kk / jnp.maximum(jnp.sqrt(ss), 1e-12)
    a_o[...] = a
    g_o[...] = _dot(_sigmoid(gl).astype(BF16), g2_ref[...])


def _rwkv_prep(rw3, shift0, P):
    B, T, wdt = rw3.shape
    tt = min(256, T)
    nblk8 = tt // 8
    row3 = lambda b, i: (b, i, 0)
    vec = lambda n: pl.BlockSpec((1, n), lambda b, i: (0, 0))
    outs = pl.pallas_call(
        functools.partial(_rwkv_prep_kernel, tt=tt),
        out_shape=[jax.ShapeDtypeStruct((B, T, MIX), F32)] * 7,
        grid=(B, T // tt),
        in_specs=[pl.BlockSpec((None, tt, wdt), row3),
                  pl.BlockSpec((None, 8, wdt), lambda b, i: (b, jnp.maximum(i * nblk8 - 1, 0), 0)),
                  pl.BlockSpec((None, 1, wdt), lambda b, i: (b, 0, 0)),
                  vec(wdt), vec(MIX), pl.BlockSpec((LANES, MIX), lambda b, i: (0, 0)),
                  vec(MIX), pl.BlockSpec((LANES, MIX), lambda b, i: (0, 0)),
                  pl.BlockSpec((LANES, MIX), lambda b, i: (0, 0)), vec(MIX), vec(MIX),
                  pl.BlockSpec((MIX, MIX), lambda b, i: (0, 0))],
        out_specs=[pl.BlockSpec((None, tt, MIX), row3)] * 7,
        compiler_params=_cp("parallel", "arbitrary"),
    )(rw3, rw3, shift0[:, None, :], P['mu'], P['w0'], P['w2'], P['a0'], P['a2'], P['g2'], P['k_k'], P['k_a'], P['bd64'])
    return outs


def _rwkv_scan_kernel(r_ref, lw_ref, k_ref, v_ref, kk_ref, a_ref, g_ref, s0_ref, rk_ref, lng_ref, lnb_ref,
                      y_ref, s1_ref, s_scr, *, L):
    ci = pl.program_id(1)

    @pl.when(ci == 0)
    def _():
        s_scr[...] = s0_ref[...]

    r_i = _iota((L, L), 0)
    c_i = _iota((L, L), 1)
    lower = (c_i <= r_i).astype(F32)
    strict = c_i < r_i
    incl = c_i <= r_i
    eye_l = (c_i == r_i).astype(F32)
    lane = _iota((L, LANES), 1)
    half_masks = (lane < 64, lane >= 64)
    rr = _iota((LANES, LANES), 0)
    cc = _iota((LANES, LANES), 1)
    bdmask = (rr >= 64) == (cc >= 64)
    eye128 = rr == cc
    bd_f = bdmask.astype(F32)
    for pr in range(RWKV_HEADS // 2):
        sl = slice(pr * LANES, (pr + 1) * LANES)
        lw = lw_ref[:, sl]
        cl = _dot(lower, lw, HI)
        w_in = jnp.exp(cl)
        w_out = jnp.exp(-cl)
        kk = kk_ref[:, sl]
        kap = kk * jnp.exp(cl - lw)
        bet = kk * a_ref[:, sl] * w_out
        kt = k_ref[:, sl] * w_out
        rt = r_ref[:, sl] * w_in
        v = v_ref[:, sl]
        sbd = s_scr[pr]
        rhs = _bdot(kap, sbd)
        ys = _bdot(rt, sbd)
        u = jnp.zeros((L, LANES), F32)
        rbs, rks = [], []
        for hm in half_masks:
            kap_h = jnp.where(hm, kap, 0.0)
            rt_h = jnp.where(hm, rt, 0.0)
            amat = jnp.where(strict, _bdot_nt(kap_h, bet), 0.0)
            cmat = jnp.where(strict, _bdot_nt(kap_h, kt), 0.0)
            tinv = eye_l - amat
            pw = amat
            n = 2
            while n < L:
                pw = _dot3(pw, pw)
                tinv = tinv + _dot3(tinv, pw)
                n *= 2
            v_h = jnp.where(hm, v, 0.0)
            rhs_h = jnp.where(hm, rhs, 0.0) + _bdot(cmat, v_h)
            u = u - _dot3(tinv, rhs_h)
            rbs.append(jnp.where(incl, _bdot_nt(rt_h, bet), 0.0))
            rks.append(jnp.where(incl, _bdot_nt(rt_h, kt), 0.0))
        y = ys
        for hm, rb, rkm in zip(half_masks, rbs, rks):
            y = y + _bdot(rb, jnp.where(hm, u, 0.0)) + _bdot(rkm, jnp.where(hm, v, 0.0))
        upd = jnp.where(bdmask, _bdot_tn(bet, u) + _bdot_tn(kt, v), 0.0)
        w_end = jnp.broadcast_to(w_in[L - 1:L, :], (LANES, LANES))
        w_end_col = jnp.sum(jnp.where(eye128, w_end, 0.0), axis=1, keepdims=True)
        s_scr[pr] = (sbd + upd) * w_end_col
        mean = _dot3(y, bd_f) * (1.0 / RWKV_N)
        yc = y - mean
        var = _dot3(yc * yc, bd_f) * (1.0 / RWKV_N)
        yn = yc * lax.rsqrt(var + 64e-5) * lng_ref[:, sl] + lnb_ref[:, sl]
        r_raw = r_ref[:, sl]
        bonus = _dot3(r_raw * k_ref[:, sl] * rk_ref[:, sl], bd_f)
        y_ref[:, sl] = (yn + bonus * v) * g_ref[:, sl]

    @pl.when(ci == pl.num_programs(1) - 1)
    def _():
        s1_ref[...] = s_scr[...]


def _rwkv_scan(r, lw, k, v, kk, a, g, s0, P, B, T):
    L = RWKV_CHUNK
    tp = -(-T // L) * L
    if tp != T:
        pad = lambda t: jnp.pad(t, ((0, 0), (0, tp - T), (0, 0)))
        r, lw, k, v, kk, a, g = (pad(t) for t in (r, lw, k, v, kk, a, g))
    nch = tp // L
    st = jnp.swapaxes(s0.astype(F32), -1, -2).reshape(B, RWKV_HEADS // 2, 2, RWKV_N, RWKV_N)
    z = jnp.zeros_like(st[:, :, 0])
    sbd0 = jnp.concatenate([jnp.concatenate([st[:, :, 0], z], -1), jnp.concatenate([z, st[:, :, 1]], -1)], -2)
    row3 = lambda b, c: (b, c, 0)
    st_spec = pl.BlockSpec((None, RWKV_HEADS // 2, LANES, LANES), lambda b, c: (b, 0, 0, 0))
    vec = pl.BlockSpec((1, MIX), lambda b, c: (0, 0))
    y, sbd1 = pl.pallas_call(
        functools.partial(_rwkv_scan_kernel, L=L),
        out_shape=[jax.ShapeDtypeStruct((B, tp, MIX), F32),
                   jax.ShapeDtypeStruct((B, RWKV_HEADS // 2, LANES, LANES), F32)],
        grid=(B, nch),
        in_specs=[pl.BlockSpec((None, L, MIX), row3)] * 7 + [st_spec, vec, vec, vec],
        out_specs=[pl.BlockSpec((None, L, MIX), row3), st_spec],
        scratch_shapes=[pltpu.VMEM((RWKV_HEADS // 2, LANES, LANES), F32)],
        compiler_params=_cp("parallel", "arbitrary"),
    )(r, lw, k, v, kk, a, g, sbd0, P['r_k'], P['ln_g'], P['ln_b'])
    s_e = sbd1[:, :, :RWKV_N, :RWKV_N]
    s_o = sbd1[:, :, RWKV_N:, RWKV_N:]
    s1 = jnp.swapaxes(jnp.stack([s_e, s_o], 2).reshape(B, RWKV_HEADS, RWKV_N, RWKV_N), -1, -2)
    return y[:, :T], s1


def _mem_kernel(x_ref, km_ref, vm_ref, wq_ref, wo_ref, g_ref, b_ref, o_ref):
    x = x_ref[...]
    q = _dot(x.astype(BF16), wq_ref[...])
    heads = []
    for h in range(MEM_HEADS):
        sl = slice(h * MEM_DH, (h + 1) * MEM_DH)
        s = _dot_nt((q[:, sl] * (MEM_DH ** -0.5)).astype(BF16), km_ref[:, sl].astype(BF16))
        e = jnp.exp(s - jnp.max(s, axis=1, keepdims=True))
        p = e / jnp.sum(e, axis=1, keepdims=True)
        heads.append(_dot(p.astype(BF16), vm_ref[:, sl].astype(BF16)))
    o = jnp.concatenate(heads, axis=1)
    y = _dot(o.astype(BF16), wo_ref[...])
    o_ref[...] = _layer_norm_rows(ALPHA * x + y, g_ref[...], b_ref[...])


def _mem_layer(x3, km, vm, wq, wo, g, b):
    B, T, d = x3.shape
    tq = min(512, T)
    mt = km.shape[1]
    full = lambda shape: pl.BlockSpec(shape, lambda bb, i: (0,) * len(shape))
    return pl.pallas_call(
        _mem_kernel, out_shape=jax.ShapeDtypeStruct((B, T, d), F32), grid=(B, T // tq),
        in_specs=[pl.BlockSpec((None, tq, d), lambda bb, i: (bb, i, 0)),
                  pl.BlockSpec((None, mt, d), lambda bb, i: (bb, 0, 0)),
                  pl.BlockSpec((None, mt, d), lambda bb, i: (bb, 0, 0)),
                  full((d, d)), full((d, d)), full((1, d)), full((1, d))],
        out_specs=pl.BlockSpec((None, tq, d), lambda bb, i: (bb, i, 0)),
        compiler_params=_cp("parallel", "arbitrary"),
    )(x3, km, vm, wq, wo, g.reshape(1, d), b.reshape(1, d))


def _moe_kernel(x_ref, wr_ref, br_ref, w1_ref, w3_ref, w2_ref, g_ref, b_ref, o_ref, xb_scr, gw_scr, acc_scr):
    e = pl.program_id(1)

    @pl.when(e == 0)
    def _route():
        x = x_ref[...]
        xb_scr[...] = x.astype(BF16)
        logit = _dot(x, wr_ref[...], HI) + br_ref[...]
        lane = _iota(logit.shape, 1)
        lane_f = lane.astype(F32)
        is_g = lane < N_GROUPS
        gl = jnp.where(is_g, logit, -jnp.inf)
        gmax = jnp.max(gl, axis=1, keepdims=True)
        grp = jnp.min(jnp.where(gl == gmax, lane_f, 1e9), axis=1, keepdims=True)
        g_w = 1.0 / jnp.sum(jnp.where(is_g, jnp.exp(gl - gmax), 0.0), axis=1, keepdims=True)
        lo = N_GROUPS + EXP_PER_GROUP * grp
        in_grp = (lane_f >= lo) & (lane_f < lo + EXP_PER_GROUP)
        el = jnp.where(in_grp, logit, -jnp.inf)
        emax = jnp.max(el, axis=1, keepdims=True)
        ee = jnp.where(in_grp, jnp.exp(el - emax), 0.0)
        p = jnp.where(in_grp, ee / jnp.sum(ee, axis=1, keepdims=True), -1.0)
        p1 = jnp.max(p, axis=1, keepdims=True)
        i1 = jnp.min(jnp.where(p == p1, lane_f, 1e9), axis=1, keepdims=True)
        pr = jnp.where(lane_f == i1, -1.0, p)
        p2 = jnp.max(pr, axis=1, keepdims=True)
        i2 = jnp.min(jnp.where(pr == p2, lane_f, 1e9), axis=1, keepdims=True)
        den = p1 + p2
        gw_scr[...] = jnp.where(lane_f == i1, g_w * p1 / den, 0.0) + jnp.where(lane_f == i2, g_w * p2 / den, 0.0)
        acc_scr[...] = jnp.zeros(acc_scr.shape, F32)

    gw = gw_scr[...]
    gcol = jnp.sum(jnp.where(_iota(gw.shape, 1) == e + N_GROUPS, gw, 0.0), axis=1, keepdims=True)
    xb = xb_scr[...]
    h1 = _dot(xb, w1_ref[...])
    h3 = _dot(xb, w3_ref[...])
    hid = h1 * _sigmoid(h1) * h3
    acc_scr[...] += gcol * _dot(hid.astype(BF16), w2_ref[...])

    @pl.when(e == pl.num_programs(1) - 1)
    def _():
        o_ref[...] = _layer_norm_rows(ALPHA * x_ref[...] + acc_scr[...], g_ref[...], b_ref[...])


def _moe_layer(x, M, g, b):
    n, d = x.shape
    tm = min(1024, n)
    full2 = lambda shape: pl.BlockSpec(shape, lambda i, e: (0, 0))
    return pl.pallas_call(
        _moe_kernel, out_shape=jax.ShapeDtypeStruct((n, d), F32), grid=(n // tm, N_EXPERTS),
        in_specs=[pl.BlockSpec((tm, d), lambda i, e: (i, 0)), full2((d, LANES)), full2((1, LANES)),
                  pl.BlockSpec((None, d, D_EXPERT), lambda i, e: (e, 0, 0)),
                  pl.BlockSpec((None, d, D_EXPERT), lambda i, e: (e, 0, 0)),
                  pl.BlockSpec((None, D_EXPERT, d), lambda i, e: (e, 0, 0)),
                  full2((1, d)), full2((1, d))],
        out_specs=pl.BlockSpec((tm, d), lambda i, e: (i, 0)),
        scratch_shapes=[pltpu.VMEM((tm, d), BF16), pltpu.VMEM((tm, LANES), F32), pltpu.VMEM((tm, d), F32)],
        compiler_params=_cp("parallel", "arbitrary"),
    )(x, M['wr'], M['br'], M['w1'], M['w3'], M['w2'], g.reshape(1, d), b.reshape(1, d))


EVEN_SEGS = ([(MIX, False)] * 4 + [(MIX, True), (LANES, False), (LANES, False), (LANES, True), (LANES, False),
                                   (LANES, True), (LANES, False), (LANES, False)])
ODD_SEGS = [(MIX, False)] * 3 + [(RWKV_SHIFT, False), (LANES, False)]
RW_PERM = np.concatenate([np.arange(0, 512), np.arange(576, 1088), np.arange(1088, 1600), np.arange(512, 576),
                          np.arange(1600, 1664), np.arange(1664, 1792)])
RW_INV = np.argsort(RW_PERM)


def _prep_even(w_in, w_out, b_i, b_f):
    zeros = jnp.zeros((D_MODEL, LANES - 2 * M_HEADS - 3 * NSA_HEADS), F32)
    w = jnp.concatenate([w_in[:, 0:1536], w_in[:, 1544:2056], w_in[:, 2056:3336], w_in[:, 1536:1544],
                         w_in[:, 3336:3360], zeros], axis=1).astype(BF16)
    bias = jnp.concatenate([b_i, b_f, jnp.zeros((LANES - 2 * M_HEADS,), F32)]).reshape(1, LANES)
    return dict(w_in=w, w_out_a=w_out[:MIX].astype(BF16), w_out_b=w_out[MIX:].astype(BF16), gate_bias=bias)


def _prep_odd(w_in, w_out, fox_b_f, mu, w0, w2, a0, a2, g2, k_k, k_a, r_k, ln_g, ln_b):
    rw_cols = w_in[:, 1544:3336][:, RW_PERM]
    zeros = jnp.zeros((D_MODEL, LANES - FOX_HEADS), F32)
    w = jnp.concatenate([w_in[:, 0:1536], rw_cols, w_in[:, 1536:1544], zeros], axis=1).astype(BF16)
    bias = jnp.concatenate([fox_b_f, jnp.zeros((LANES - FOX_HEADS,), F32)]).reshape(1, LANES)
    z64 = jnp.zeros((64, MIX), F32)
    hd = np.arange(MIX) // RWKV_N
    bd64 = jnp.asarray((hd[:, None] == hd[None, :]).astype(np.float32))
    P = dict(mu=mu[RW_PERM].reshape(1, -1), w0=w0.reshape(1, -1), a0=a0.reshape(1, -1),
             w2=jnp.concatenate([w2, z64], 0).astype(BF16), a2=jnp.concatenate([z64, a2], 0).astype(BF16),
             g2=g2.astype(BF16), k_k=k_k.reshape(1, -1), k_a=k_a.reshape(1, -1), r_k=r_k.reshape(1, -1),
             ln_g=ln_g.reshape(1, -1), ln_b=ln_b.reshape(1, -1), bd64=bd64)
    return dict(w_in=w, w_out_a=w_out[:MIX].astype(BF16), w_out_b=w_out[MIX:].astype(BF16), fox_bias=bias, P=P)


def _prep_moe(w_group, b_group, w_expert, b_expert, w1, w3, w2):
    padw = LANES - N_GROUPS - N_EXPERTS
    wr = jnp.concatenate([w_group, w_expert, jnp.zeros((D_MODEL, padw), F32)], axis=1)
    br = jnp.concatenate([b_group, b_expert, jnp.zeros((padw,), F32)]).reshape(1, LANES)
    return dict(wr=wr, br=br, w1=w1.astype(BF16), w3=w3.astype(BF16), w2=w2.astype(BF16))


def _even_odd_blocks(kcmp):
    return jnp.concatenate([kcmp[:, 0::2], kcmp[:, 1::2]], axis=1)


def _pages_t(cache):
    pool, page, h, dh = cache.shape
    return jnp.transpose(cache, (0, 2, 3, 1)).reshape(pool, h * dh, page)


def _tail_t(rows3):
    B, T, _ = rows3.shape
    return jnp.swapaxes(jnp.pad(rows3, ((0, 0), (0, PAGE - T), (0, 0))), 1, 2)


def _even_mixer(x3, q0, W, nsa_w, norm_g, c0, n0, m0, past, ln_g, ln_b):
    B, T, D = x3.shape
    N = B * T
    x2 = x3.reshape(N, D)
    pos = q0 + jnp.arange(T)
    cos, sin = _rope_tables(pos)
    rope = (jnp.tile(cos, (B, 1)), jnp.tile(sin, (B, 1)))
    mq, mk, mv, mo, nq, kc, vc, ks, vs, kw, vw, small = _proj_split(x2, W['w_in'], EVEN_SEGS, rope)
    h, c1, n1, m1 = _mlstm(mq, mk, mv, mo, small, W['gate_bias'], norm_g, c0, n0, m0, B, T)
    pe_k, pe_v, ck1, ck2, cv1, cv2 = nsa_w
    blk_w = CMP_BLOCK * LANES
    q3 = nq.reshape(B, T, MIX)
    if past is None:
        L = T
        nc = L // CMP_BLOCK
        kcmp = _compress(kc.reshape(B * nc, blk_w), pe_k, ck1, ck2).reshape(B, nc, LANES)
        vcmp = _compress(vc.reshape(B * nc, blk_w), pe_v, cv1, cv2).reshape(B, nc, LANES)
    else:
        pt = past['page_table']
        npages = pt.shape[1]
        L = npages * PAGE + T
        nc = L // CMP_BLOCK
        per_page = PAGE // CMP_BLOCK
        pool = past['kc'].shape[0]
        kcp = _compress_pool(_pages_t(past['kc']), pe_k, ck1, ck2).reshape(pool, per_page * LANES)
        vcp = _compress_pool(_pages_t(past['vc']), pe_v, cv1, cv2).reshape(pool, per_page * LANES)
        kcmp = jnp.take(kcp, pt, axis=0).reshape(B, npages * per_page, LANES)[:, :nc]
        vcmp = jnp.take(vcp, pt, axis=0).reshape(B, npages * per_page, LANES)[:, :nc]
    cpos = jnp.arange(nc) * CMP_BLOCK + CMP_BLOCK - 1
    ccos, csin = _rope_tables(cpos)
    kcmp = _rope_rows(kcmp.reshape(B * nc, LANES), jnp.tile(ccos, (B, 1)), jnp.tile(csin, (B, 1))).reshape(B, nc, LANES)
    ns = -(-L // SEL_BLOCK)
    ns_pad = -(-ns // LANES) * LANES
    cend = jnp.concatenate([cpos[0::2], cpos[1::2]]).astype(jnp.int32).reshape(1, nc)
    o_cmp, sel = _cmp_sel(q3, _even_odd_blocks(kcmp), _even_odd_blocks(vcmp), cend, q0, ns, ns_pad)
    ks3, vs3 = ks.reshape(B, T, LANES), vs.reshape(B, T, LANES)
    kw3, vw3 = kw.reshape(B, T, LANES), vw.reshape(B, T, LANES)
    if past is None:
        tq = min(TQ_NSA, T)
        o_slc = _flash(q3, ks3, vs3, mode='sel', layout='nsa', tq=tq, tk=min(TK_SLC, T), q0=0, sel=sel, ns_pad=ns_pad)
        o_win = _flash(q3, kw3, vw3, mode='band', layout='nsa', tq=tq, tk=min(TK_WIN, T), q0=0)
        kw_new, vw_new = kw3[:, -min(WINDOW, T):], vw3[:, -min(WINDOW, T):]
    else:
        o_slc = _flash(q3, _pages_t(past['ks']), _pages_t(past['vs']), mode='sel', layout='nsa', tq=T, tk=0, q0=q0,
                       page_table=pt, pps=_largest_divisor(npages, (8, 4, 2, 1)), tail=(_tail_t(ks3), _tail_t(vs3)),
                       tail_pos0=q0, sel=sel, ns_pad=ns_pad)
        kw_ext = jnp.concatenate([past['kw_buf'], kw3], axis=1)
        vw_ext = jnp.concatenate([past['vw_buf'], vw3], axis=1)
        wb = past['kw_buf'].shape[1]
        o_win = _flash(q3, kw_ext, vw_ext, mode='band', layout='nsa', tq=T, tk=wb + T, q0=q0, kbase=q0 - wb)
        keep = min(WINDOW, wb + T)
        kw_new, vw_new = kw_ext[:, -keep:], vw_ext[:, -keep:]
    o = _nsa_combine(o_cmp.reshape(N, MIX), o_slc.reshape(N, MIX), o_win.reshape(N, MIX), small)
    x_new = _mm_res_ln([h, o], [W['w_out_a'], W['w_out_b']], x2, ln_g, ln_b).reshape(B, T, D)
    sh = (B, T, NSA_GROUPS, NSA_DH)
    wsh = (B, -1, NSA_GROUPS, NSA_DH)
    state = (c1, n1, m1, kc.reshape(sh), vc.reshape(sh), ks.reshape(sh), vs.reshape(sh),
             kw_new.reshape(wsh), vw_new.reshape(wsh))
    return x_new, state


def _odd_mixer(x3, q0, W, s0, shift0, past, ln_g, ln_b):
    B, T, D = x3.shape
    N = B * T
    x2 = x3.reshape(N, D)
    fq, fk, fv, rw, small = _proj_split(x2, W['w_in'], ODD_SEGS)
    q3, k3, v3 = (t.reshape(B, T, MIX) for t in (fq, fk, fv))
    if past is None:
        lf_full, caug = _fox_cumsum(small, W['fox_bias'], B, T)
        lf = lf_full[:, :FOX_HEADS].reshape(B, T, FOX_HEADS)
        o_c = _flash(q3, k3, v3, mode='causal', layout='fox', tq=min(TQ_FOX, T), tk=min(TK_FOX, T), q0=0,
                     caug=caug.reshape(B, T, LANES))
    else:
        pt = past['page_table']
        npages = pt.shape[1]
        pps = _largest_divisor(npages, (8, 4, 2, 1))
        padrows = ((0, 0), (0, PAGE - T), (0, 0))
        lf_pool_t = jnp.swapaxes(past['lf'], 1, 2)
        c_all, lf_new = _fox_cumsum_paged(lf_pool_t, pt, jnp.pad(small.reshape(B, T, LANES), padrows), W['fox_bias'], pps)
        lf = lf_new[:, :T, :FOX_HEADS]
        n_main = npages * PAGE
        o_c = _flash(q3, _pages_t(past['k']), _pages_t(past['v']), mode='causal', layout='fox_wide', tq=T, tk=0, q0=q0,
                     page_table=pt, pps=pps, tail=(_tail_t(k3), _tail_t(v3)), tail_pos0=q0,
                     crow=c_all, crow_tail=c_all[:, :, n_main:n_main + PAGE])
    rw3 = rw.reshape(B, T, RWKV_SHIFT)
    r, lw, k, v, kk, a, g = _rwkv_prep(rw3, shift0[:, RW_PERM], W['P'])
    y, s1 = _rwkv_scan(r, lw, k, v, kk, a, g, s0, W['P'], B, T)
    x_new = _mm_res_ln([o_c.reshape(N, MIX), y.reshape(N, MIX)], [W['w_out_a'], W['w_out_b']], x2, ln_g, ln_b)
    hs = (B, T, FOX_HEADS, FOX_DH)
    state = (fk.reshape(hs), fv.reshape(hs), lf, s1, rw3[:, -1][:, RW_INV])
    return x_new.reshape(B, T, D), state


def kernel(x_prompt, x_sample, state_mlstm_C, state_mlstm_n, state_mlstm_m, cache_nsa_kc, cache_nsa_vc, cache_nsa_ks, cache_nsa_vs, cache_nsa_kw, cache_nsa_vw, cache_fox_k, cache_fox_v, cache_fox_logf, state_rwkv_S, state_rwkv_shift, cache_mem_k, cache_mem_v, page_table, mem_prompt, even_w_in, even_w_out, mlstm_b_i, mlstm_b_f, mlstm_norm_g, nsa_pe_k, nsa_pe_v, nsa_cmp_k_w1, nsa_cmp_k_w2, nsa_cmp_v_w1, nsa_cmp_v_w2, odd_w_in, odd_w_out, fox_b_f, rwkv_mu, rwkv_w0, rwkv_w2, rwkv_a0, rwkv_a2, rwkv_g2, rwkv_k_k, rwkv_k_a, rwkv_r_k, rwkv_ln_g, rwkv_ln_b, mem_wq, mem_wk, mem_wv, mem_wo, moe_w_group, moe_b_group, moe_w_expert, moe_b_expert, moe_w1, moe_w3, moe_w2, ln_g, ln_b):
    bp, tp, d = x_prompt.shape
    bs, ts, _ = x_sample.shape
    depth = ln_g.shape[0]
    past = page_table.shape[1] * cache_nsa_kc.shape[2]
    xp, xs = x_prompt, x_sample
    acc = {}

    def push(prefix, names, vals):
        for nm, val in zip(names, vals):
            acc.setdefault(prefix + nm, []).append(val)

    even_names = ('mlstm_C', 'mlstm_n', 'mlstm_m', 'nsa_kc', 'nsa_vc', 'nsa_ks', 'nsa_vs', 'nsa_kw', 'nsa_vw')
    odd_names = ('fox_k', 'fox_v', 'fox_logf', 'rwkv_S', 'rwkv_shift')
    for layer in range(depth):
        e = layer // 2
        if layer % 2 == 0:
            W = _prep_even(even_w_in[e], even_w_out[e], mlstm_b_i[e], mlstm_b_f[e])
            nsa_w = (nsa_pe_k[e], nsa_pe_v[e], nsa_cmp_k_w1[e], nsa_cmp_k_w2[e], nsa_cmp_v_w1[e], nsa_cmp_v_w2[e])
            xp, sp = _even_mixer(xp, 0, W, nsa_w, mlstm_norm_g[e], jnp.zeros((bp, M_HEADS, M_DV, M_DK), F32),
                                 jnp.zeros((bp, M_HEADS, M_DK), F32), jnp.zeros((bp, M_HEADS), F32), None,
                                 ln_g[layer, 0], ln_b[layer, 0])
            wbuf = lambda c: c[e].reshape(bs, -1, LANES)
            past_d = dict(kc=cache_nsa_kc[e], vc=cache_nsa_vc[e], ks=cache_nsa_ks[e], vs=cache_nsa_vs[e],
                          page_table=page_table, kw_buf=wbuf(cache_nsa_kw), vw_buf=wbuf(cache_nsa_vw))
            xs, ss = _even_mixer(xs, past, W, nsa_w, mlstm_norm_g[e], state_mlstm_C[e], state_mlstm_n[e],
                                 state_mlstm_m[e], past_d, ln_g[layer, 0], ln_b[layer, 0])
            names = even_names
        else:
            W = _prep_odd(odd_w_in[e], odd_w_out[e], fox_b_f[e], rwkv_mu[e], rwkv_w0[e], rwkv_w2[e], rwkv_a0[e],
                          rwkv_a2[e], rwkv_g2[e], rwkv_k_k[e], rwkv_k_a[e], rwkv_r_k[e], rwkv_ln_g[e], rwkv_ln_b[e])
            xp, sp = _odd_mixer(xp, 0, W, jnp.zeros((bp, RWKV_HEADS, RWKV_N, RWKV_N), F32),
                                jnp.zeros((bp, RWKV_SHIFT), F32), None, ln_g[layer, 0], ln_b[layer, 0])
            past_d = dict(k=cache_fox_k[e], v=cache_fox_v[e], lf=cache_fox_logf[e], page_table=page_table)
            xs, ss = _odd_mixer(xs, past, W, state_rwkv_S[e], state_rwkv_shift[e], past_d,
                                ln_g[layer, 0], ln_b[layer, 0])
            names = odd_names
        push('p_', names, sp)
        push('s_', names, ss)
        wkv = jnp.concatenate([mem_wk[layer], mem_wv[layer]], axis=1).astype(BF16)
        mt = mem_prompt.shape[1]
        kmp, vmp = _proj_split(mem_prompt.reshape(bp * mt, d), wkv, [(d, False), (d, False)])
        push('p_', ('mem_k', 'mem_v'), (kmp.reshape(bp, mt, MEM_HEADS, MEM_DH), vmp.reshape(bp, mt, MEM_HEADS, MEM_DH)))
        wq, wo = mem_wq[layer].astype(BF16), mem_wo[layer].astype(BF16)
        xp = _mem_layer(xp, kmp.reshape(bp, mt, d), vmp.reshape(bp, mt, d), wq, wo, ln_g[layer, 1], ln_b[layer, 1])
        xs = _mem_layer(xs, cache_mem_k[layer].reshape(bs, -1, d), cache_mem_v[layer].reshape(bs, -1, d), wq, wo,
                        ln_g[layer, 1], ln_b[layer, 1])
        M = _prep_moe(moe_w_group[layer], moe_b_group[layer], moe_w_expert[layer], moe_b_expert[layer],
                      moe_w1[layer], moe_w3[layer], moe_w2[layer])
        xp = _moe_layer(xp.reshape(bp * tp, d), M, ln_g[layer, 2], ln_b[layer, 2]).reshape(bp, tp, d)
        xs = _moe_layer(xs.reshape(bs * ts, d), M, ln_g[layer, 2], ln_b[layer, 2]).reshape(bs, ts, d)

    st = {name: jnp.stack(vals) for name, vals in acc.items()}
    return (xp, xs,
            st['p_mlstm_C'], st['p_mlstm_n'], st['p_mlstm_m'],
            st['p_nsa_kc'], st['p_nsa_vc'], st['p_nsa_ks'], st['p_nsa_vs'], st['p_nsa_kw'], st['p_nsa_vw'],
            st['p_fox_k'], st['p_fox_v'], st['p_fox_logf'], st['p_rwkv_S'], st['p_rwkv_shift'],
            st['p_mem_k'], st['p_mem_v'],
            st['s_mlstm_C'], st['s_mlstm_n'], st['s_mlstm_m'],
            st['s_nsa_kc'], st['s_nsa_vc'], st['s_nsa_ks'], st['s_nsa_vs'], st['s_nsa_kw'], st['s_nsa_vw'],
            st['s_fox_k'], st['s_fox_v'], st['s_fox_logf'], st['s_rwkv_S'], st['s_rwkv_shift'])
```

```python
import functools

import numpy as np
import jax
import jax.numpy as jnp
from jax import lax
from jax.experimental import pallas as pl
from jax.experimental.pallas import tpu as pltpu

F32 = jnp.float32
BF16 = jnp.bfloat16
HI = lax.Precision.HIGHEST

D_MODEL = 1024
DEPTH = 2
PAGE = 128
M_HEADS, M_DK, M_DV, M_CHUNK = 4, 128, 128, 64
NSA_HEADS, NSA_GROUPS, NSA_DH = 8, 2, 64
CMP_BLOCK, SEL_BLOCK, N_SEL, WINDOW = 32, 64, 16, 512
FOX_HEADS, FOX_DH = 8, 64
RWKV_HEADS, RWKV_N, W_LORA, A_LORA, G_LORA = 8, 64, 64, 64, 128
RWKV_CHUNK = 64
MEM_HEADS = 4
MEM_DH = D_MODEL // MEM_HEADS
N_GROUPS, EXP_PER_GROUP, TOP_K, D_EXPERT = 4, 8, 2, 256
N_EXPERTS = N_GROUPS * EXP_PER_GROUP
ROPE_THETA = 10000.0
ALPHA = (2.0 * DEPTH) ** 0.25
LN_EPS = 1e-5
NEG = -1e30
FORCE = 1e4
MIX = 512
RWKV_SHIFT = 1792
SEL_OFF = 2.0 ** 30

LANES = 128
VMEM_LIMIT = 48 * 1024 * 1024
TQ_FOX, TK_FOX = 256, 512
TQ_NSA, TK_SLC, TK_WIN = 128, 512, 128
TQ_CMP = 256


def _cp(*sem):
    return pltpu.CompilerParams(dimension_semantics=sem, vmem_limit_bytes=VMEM_LIMIT)


def _dot(a, b, prec=None):
    return jnp.dot(a, b, preferred_element_type=F32, precision=prec)


def _dot_nt(a, b, prec=None):
    return lax.dot_general(a, b, (((1,), (1,)), ((), ())), preferred_element_type=F32, precision=prec)


def _dot_tn(a, b, prec=None):
    return lax.dot_general(a, b, (((0,), (0,)), ((), ())), preferred_element_type=F32, precision=prec)


def _bdot(a, b):
    return _dot(a.astype(BF16), b.astype(BF16))


def _bdot_nt(a, b):
    return _dot_nt(a.astype(BF16), b.astype(BF16))


def _bdot_tn(a, b):
    return _dot_tn(a.astype(BF16), b.astype(BF16))


def _split_bf16(a):
    hi = a.astype(BF16)
    return hi, (a - hi.astype(F32)).astype(BF16)


def _dot3(a, b):
    ah, al = _split_bf16(a)
    bh, bl = _split_bf16(b)
    return _dot(ah, bh) + (_dot(ah, bl) + _dot(al, bh))


def _iota(shape, dim):
    return lax.broadcasted_iota(jnp.int32, shape, dim)


def _log_sigmoid(x):
    return jnp.minimum(x, 0.0) - jnp.log1p(jnp.exp(-jnp.abs(x)))


def _sigmoid(x):
    return 1.0 / (1.0 + jnp.exp(-x))


def _softplus(x):
    return jnp.maximum(x, 0.0) + jnp.log1p(jnp.exp(-jnp.abs(x)))


def _layer_norm_rows(z, g, b):
    mu = jnp.mean(z, axis=-1, keepdims=True)
    zc = z - mu
    var = jnp.mean(zc * zc, axis=-1, keepdims=True)
    return zc * lax.rsqrt(var + LN_EPS) * g + b


def _rope128(y, cos, sin_signed):
    lane = _iota(y.shape, 1)
    lo = (lane & 63) < 32
    sw = jnp.where(lo, pltpu.roll(y, 96, 1), pltpu.roll(y, 32, 1))
    return y * cos + sw * sin_signed


def _rope_tables(pos):
    half = NSA_DH // 2
    freq = ROPE_THETA ** (-jnp.arange(half, dtype=F32) / half)
    ang = pos.astype(F32)[:, None] * freq[None, :]
    cos, sin = jnp.cos(ang), jnp.sin(ang)
    return jnp.tile(cos, (1, 4)), jnp.tile(jnp.concatenate([-sin, sin], -1), (1, 2))


def _largest_divisor(n, options):
    for p in options:
        if n % p == 0:
            return p


def _proj_kernel(*refs, segs, use_rope):
    if use_rope:
        x_ref, w_ref, cos_ref, sin_ref = refs[:4]
        outs = refs[4:]
    else:
        x_ref, w_ref = refs[:2]
        outs = refs[2:]
    xb = x_ref[...].astype(BF16)
    off = 0
    for (wd, rp), o_ref in zip(segs, outs):
        y = _dot(xb, w_ref[:, off:off + wd])
        if rp:
            for c in range(wd // LANES):
                o_ref[:, c * LANES:(c + 1) * LANES] = _rope128(y[:, c * LANES:(c + 1) * LANES], cos_ref[...], sin_ref[...])
        else:
            o_ref[...] = y
        off += wd


def _proj_split(x, w, segs, rope=None):
    n, k = x.shape
    tm = min(256, n)
    wtot = sum(s[0] for s in segs)
    use_rope = rope is not None
    in_specs = [pl.BlockSpec((tm, k), lambda i: (i, 0)), pl.BlockSpec((k, wtot), lambda i: (0, 0))]
    args = [x, w]
    if use_rope:
        in_specs += [pl.BlockSpec((tm, LANES), lambda i: (i, 0))] * 2
        args += list(rope)
    return pl.pallas_call(
        functools.partial(_proj_kernel, segs=tuple(segs), use_rope=use_rope),
        out_shape=[jax.ShapeDtypeStruct((n, wd), F32) for wd, _ in segs],
        grid=(n // tm,),
        in_specs=in_specs,
        out_specs=[pl.BlockSpec((tm, wd), lambda i: (i, 0)) for wd, _ in segs],
        compiler_params=_cp("parallel"),
    )(*args)


def _mm_res_ln_kernel(*refs, n_in):
    a_refs = refs[:n_in]
    w_refs = refs[n_in:2 * n_in]
    x_ref, g_ref, b_ref, o_ref = refs[2 * n_in:]
    y = _dot(a_refs[0][...].astype(BF16), w_refs[0][...])
    for a_ref, w_ref in zip(a_refs[1:], w_refs[1:]):
        y = y + _dot(a_ref[...].astype(BF16), w_ref[...])
    o_ref[...] = _layer_norm_rows(ALPHA * x_ref[...] + y, g_ref[...], b_ref[...])


def _mm_res_ln(a_list, w_list, x, g, b):
    n, d = x.shape
    tm = min(512, n)
    n_in = len(a_list)
    in_specs = ([pl.BlockSpec((tm, a.shape[1]), lambda i: (i, 0)) for a in a_list]
                + [pl.BlockSpec(w.shape, lambda i: (0, 0)) for w in w_list]
                + [pl.BlockSpec((tm, d), lambda i: (i, 0)), pl.BlockSpec((1, d), lambda i: (0, 0)),
                   pl.BlockSpec((1, d), lambda i: (0, 0))])
    return pl.pallas_call(
        functools.partial(_mm_res_ln_kernel, n_in=n_in),
        out_shape=jax.ShapeDtypeStruct((n, d), F32),
        grid=(n // tm,),
        in_specs=in_specs,
        out_specs=pl.BlockSpec((tm, d), lambda i: (i, 0)),
        compiler_params=_cp("parallel"),
    )(*a_list, *w_list, x, g.reshape(1, d), b.reshape(1, d))


def _mlstm_kernel(q_ref, k_ref, v_ref, o_ref, sm_ref, bias_ref, g_ref, c0_ref, n0_ref, m0_ref,
                  h_ref, c1_ref, n1_ref, m1_ref, c_scr, n_scr, m_scr, *, L):
    ci = pl.program_id(1)

    @pl.when(ci == 0)
    def _():
        c_scr[...] = c0_ref[...]
        n_scr[...] = n0_ref[...]
        m_scr[...] = m0_ref[...]

    sm = sm_ref[...] + bias_ref[...]
    lane = _iota(sm.shape, 1)
    gates = jnp.where(lane < M_HEADS, sm, _log_sigmoid(sm))
    eye8 = (_iota((8, LANES), 0) == _iota((8, LANES), 1)).astype(F32)
    gates_row = _dot_nt(eye8, gates, HI)
    r_i = _iota((L, L), 0)
    c_i = _iota((L, L), 1)
    tri = c_i <= r_i
    scale = M_DK ** -0.5
    for h in range(M_HEADS):
        ic_col = gates[:, h:h + 1]
        f_col = gates[:, M_HEADS + h:M_HEADS + h + 1]
        ic_row = gates_row[h:h + 1, :]
        f_row = gates_row[M_HEADS + h:M_HEADS + h + 1, :]
        b_col = jnp.sum(jnp.where(tri, jnp.broadcast_to(f_row, (L, L)), 0.0), axis=1, keepdims=True)
        b_row = jnp.sum(jnp.where(r_i <= c_i, jnp.broadcast_to(f_col, (L, L)), 0.0), axis=0, keepdims=True)
        m_prev = m_scr[h:h + 1, 0:1]
        dmat = jnp.where(tri, b_col - b_row + ic_row, NEG)
        inter = b_col + m_prev
        mt = jnp.maximum(inter, jnp.max(dmat, axis=1, keepdims=True))
        wmat = jnp.where(tri, jnp.exp(dmat - mt), 0.0)
        a = jnp.exp(inter - mt)
        qh = q_ref[:, h * M_DK:(h + 1) * M_DK]
        kh = k_ref[:, h * M_DK:(h + 1) * M_DK] * scale
        vh = v_ref[:, h * M_DV:(h + 1) * M_DV]
        qb, kb, vb = qh.astype(BF16), kh.astype(BF16), vh.astype(BF16)
        c_old = c_scr[h]
        n_old = n_scr[h:h + 1, :]
        s = _dot_nt(qb, kb) * wmat
        num = a * _dot_nt(qb, c_old.astype(BF16)) + _dot(s.astype(BF16), vb)
        den = a * jnp.sum(qh * n_old, axis=1, keepdims=True) + jnp.sum(s, axis=1, keepdims=True)
        hh = num / jnp.maximum(jnp.abs(den), jnp.exp(-mt))
        m_l = mt[L - 1:L, :]
        b_l = b_col[L - 1:L, :]
        g_l = jnp.exp(b_l - b_col + ic_col - m_l)
        a_l = jnp.exp(b_l + m_prev - m_l)
        c_scr[h] = a_l * c_old + _dot_tn((g_l * vh).astype(BF16), kb)
        n_scr[h:h + 1, :] = a_l * n_old + jnp.sum(g_l * kh, axis=0, keepdims=True)
        m_scr[h:h + 1, :] = jnp.broadcast_to(m_l, (1, LANES))
        mu = jnp.mean(hh, axis=1, keepdims=True)
        hc = hh - mu
        var = jnp.mean(hc * hc, axis=1, keepdims=True)
        hn = hc * lax.rsqrt(var + 1e-6) * g_ref[:, h * M_DV:(h + 1) * M_DV]
        h_ref[:, h * M_DV:(h + 1) * M_DV] = hn * _sigmoid(o_ref[:, h * M_DV:(h + 1) * M_DV])

    @pl.when(ci == pl.num_programs(1) - 1)
    def _():
        c1_ref[...] = c_scr[...]
        n1_ref[...] = n_scr[...]
        m1_ref[...] = m_scr[...]


def _mlstm(mq, mk, mv, mo, small, bias_row, norm_g, c0, n0, m0, B, T):
    L = M_CHUNK if T % M_CHUNK == 0 else T
    nch = T // L
    row = lambda b, c: (b * nch + c, 0)
    st4 = lambda b, c: (b, 0, 0, 0)
    st3 = lambda b, c: (b, 0, 0)
    m0b = jnp.broadcast_to(m0[:, :, None], (B, M_HEADS, LANES)).astype(F32)
    h, c1, n1, m1 = pl.pallas_call(
        functools.partial(_mlstm_kernel, L=L),
        out_shape=[jax.ShapeDtypeStruct((B * T, MIX), F32),
                   jax.ShapeDtypeStruct((B, M_HEADS, M_DV, M_DK), F32),
                   jax.ShapeDtypeStruct((B, M_HEADS, M_DK), F32),
                   jax.ShapeDtypeStruct((B, M_HEADS, LANES), F32)],
        grid=(B, nch),
        in_specs=[pl.BlockSpec((L, MIX), row)] * 4
        + [pl.BlockSpec((L, LANES), row), pl.BlockSpec((1, LANES), lambda b, c: (0, 0)),
           pl.BlockSpec((1, MIX), lambda b, c: (0, 0)),
           pl.BlockSpec((None, M_HEADS, M_DV, M_DK), st4), pl.BlockSpec((None, M_HEADS, M_DK), st3),
           pl.BlockSpec((None, M_HEADS, LANES), st3)],
        out_specs=[pl.BlockSpec((L, MIX), row), pl.BlockSpec((None, M_HEADS, M_DV, M_DK), st4),
                   pl.BlockSpec((None, M_HEADS, M_DK), st3), pl.BlockSpec((None, M_HEADS, LANES), st3)],
        scratch_shapes=[pltpu.VMEM((M_HEADS, M_DV, M_DK), F32), pltpu.VMEM((M_HEADS, M_DK), F32),
                        pltpu.VMEM((M_HEADS, LANES), F32)],
        compiler_params=_cp("parallel", "arbitrary"),
    )(mq, mk, mv, mo, small, bias_row, norm_g.reshape(1, MIX), c0.astype(F32), n0.astype(F32), m0b)
    return h, c1, n1, m1[:, :, 0]


def _compress_weights(pe, w1, w2):
    eye = jnp.eye(NSA_GROUPS, dtype=F32)
    w1j = jnp.einsum('jdo,gh->jgdho', w1.reshape(CMP_BLOCK, NSA_DH, NSA_DH), eye)
    w1j = w1j.reshape(CMP_BLOCK, LANES, LANES).astype(BF16)
    w2b = jnp.einsum('do,gh->gdho', w2, eye).reshape(LANES, LANES).astype(BF16)
    pe2 = jnp.tile(pe, (1, NSA_GROUPS))
    return pe2, w1j, w2b


def _compress_kernel(x_ref, pe_ref, w1_ref, w2_ref, o_ref):
    xb = (x_ref[...] + pe_ref[...]).astype(BF16)
    hid = jax.nn.gelu(_dot(xb, w1_ref[...]))
    o_ref[...] = _dot(hid.astype(BF16), w2_ref[...])


def _compress(blocks, pe, w1, w2):
    r = blocks.shape[0]
    tb = min(256, r)
    width = CMP_BLOCK * LANES
    pe2, w1j, w2b = _compress_weights(pe, w1, w2)
    return pl.pallas_call(
        _compress_kernel,
        out_shape=jax.ShapeDtypeStruct((r, LANES), F32),
        grid=(pl.cdiv(r, tb),),
        in_specs=[pl.BlockSpec((tb, width), lambda i: (i, 0)), pl.BlockSpec((1, width), lambda i: (0, 0)),
                  pl.BlockSpec((width, LANES), lambda i: (0, 0)), pl.BlockSpec((LANES, LANES), lambda i: (0, 0))],
        out_specs=pl.BlockSpec((tb, LANES), lambda i: (i, 0)),
        compiler_params=_cp("parallel"),
    )(blocks, pe2.reshape(1, width), w1j.reshape(width, LANES), w2b)


def _compress_pool_kernel(xt_ref, pe_ref, w1_ref, w2_ref, o_ref, xs_scr, *, pb):
    per_page = PAGE // CMP_BLOCK
    for p in range(pb):
        xs_scr[p * PAGE:(p + 1) * PAGE, :] = xt_ref[p].T + pe_ref[...]
    acc = None
    for j in range(CMP_BLOCK):
        rows = xs_scr[pl.ds(j, per_page * pb, stride=CMP_BLOCK), :]
        t = _dot(rows.astype(BF16), w1_ref[j])
        acc = t if acc is None else acc + t
    o_ref[...] = _dot(jax.nn.gelu(acc).astype(BF16), w2_ref[...])


def _compress_pool(pool_t, pe, w1, w2):
    npool = pool_t.shape[0]
    per_page = PAGE // CMP_BLOCK
    pb = _largest_divisor(npool, (16, 8, 4, 2))
    pe2, w1j, w2b = _compress_weights(pe, w1, w2)
    pe_tile = jnp.tile(pe2, (per_page, 1))
    return pl.pallas_call(
        functools.partial(_compress_pool_kernel, pb=pb),
        out_shape=jax.ShapeDtypeStruct((npool * per_page, LANES), F32),
        grid=(npool // pb,),
        in_specs=[pl.BlockSpec((pb, LANES, PAGE), lambda i: (i, 0, 0)), pl.BlockSpec((PAGE, LANES), lambda i: (0, 0)),
                  pl.BlockSpec((CMP_BLOCK, LANES, LANES), lambda i: (0, 0, 0)),
                  pl.BlockSpec((LANES, LANES), lambda i: (0, 0))],
        out_specs=pl.BlockSpec((per_page * pb, LANES), lambda i: (i, 0)),
        scratch_shapes=[pltpu.VMEM((pb * PAGE, LANES), F32)],
        compiler_params=_cp("parallel"),
    )(pool_t, pe_tile, w1j, w2b)


def _rope_rows_kernel(x_ref, cos_ref, sin_ref, o_ref):
    o_ref[...] = _rope128(x_ref[...], cos_ref[...], sin_ref[...])


def _rope_rows(x, cos, sin):
    r = x.shape[0]
    tb = min(512, r)
    spec = pl.BlockSpec((tb, LANES), lambda i: (i, 0))
    return pl.pallas_call(_rope_rows_kernel, out_shape=jax.ShapeDtypeStruct((r, LANES), F32), grid=(r // tb,),
                          in_specs=[spec] * 3, out_specs=spec, compiler_params=_cp("parallel"))(x, cos, sin)


def _stack_heads(qblk, heads, scale):
    rows = []
    for src_blk, src_half, dst_half in heads:
        x = qblk[:, src_blk * LANES:(src_blk + 1) * LANES] * scale
        if src_half != dst_half:
            x = pltpu.roll(x, 64, 1)
        lane = _iota(x.shape, 1)
        keep = (lane >= 64) if dst_half else (lane < 64)
        rows.append(jnp.where(keep, x, 0.0))
    return jnp.concatenate(rows, axis=0)


def _unstack_heads(o_full_list, heads_per_kb, tq, n_out_blocks):
    outs = [None] * n_out_blocks
    for o_full, heads in zip(o_full_list, heads_per_kb):
        for idx, (src_blk, src_half, dst_half) in enumerate(heads):
            piece = o_full[idx * tq:(idx + 1) * tq]
            if src_half != dst_half:
                piece = pltpu.roll(piece, 64, 1)
            lane = _iota(piece.shape, 1)
            keep = (lane >= 64) if src_half else (lane < 64)
            prev = outs[src_blk]
            outs[src_blk] = jnp.where(keep, piece, 0.0 if prev is None else prev)
    return outs


def _lane_repeat(x, width):
    if width == LANES:
        return x
    if width % LANES == 0:
        return pltpu.repeat(x, width // LANES, axis=1)
    return jnp.broadcast_to(x[:, 0:1], (x.shape[0], width))


def _stack_wide(qblk, n_heads, scale):
    head_of_lane = _iota(qblk.shape, 1) >> 6
    return jnp.concatenate([jnp.where(head_of_lane == h, qblk * scale, 0.0) for h in range(n_heads)], axis=0)


def _unstack_wide(o_full, n_heads, tq):
    head_of_lane = _iota((tq, o_full.shape[1]), 1) >> 6
    out = jnp.zeros((tq, o_full.shape[1]), F32)
    for h in range(n_heads):
        out = jnp.where(head_of_lane == h, o_full[h * tq:(h + 1) * tq], out)
    return out


NSA_HEAD_MAP = [[(h // 2, h % 2, h // (NSA_HEADS // NSA_GROUPS)) for h in range(NSA_HEADS)]]
FOX_HEAD_MAP = [[(kb, 0, 0), (kb, 1, 1)] for kb in range(FOX_HEADS // 2)]


def _cmp_sel_kernel(q_ref, kc_ref, vc_ref, cend_ref, o_ref, sel_ref, *, tq, q0, nc, ns, ns_pad):
    qi = pl.program_id(1)
    heads = NSA_HEAD_MAP[0]
    hpg = NSA_HEADS // NSA_GROUPS
    qbd = _stack_heads(q_ref[...], heads, NSA_DH ** -0.5).astype(BF16)
    s = _dot_nt(qbd, kc_ref[...].astype(BF16))
    rows = NSA_HEADS * tq
    qpos = q0 + qi * tq + (_iota((rows, 1), 0) & (tq - 1))
    mask = cend_ref[...] <= qpos
    s = jnp.where(mask, s, NEG)
    e = jnp.where(mask, jnp.exp(s - jnp.max(s, axis=1, keepdims=True)), 0.0)
    p = e / jnp.maximum(jnp.sum(e, axis=1, keepdims=True), 1e-30)
    o_full = _dot(p.astype(BF16), vc_ref[...].astype(BF16))
    outs = _unstack_heads([o_full], NSA_HEAD_MAP, tq, MIX // LANES)
    for c, blk in enumerate(outs):
        o_ref[:, c * LANES:(c + 1) * LANES] = blk
    half = nc // 2
    qp = q0 + qi * tq + _iota((tq, 1), 0)
    blk_i = _iota((tq, ns_pad), 1)
    blk_f = blk_i.astype(F32)
    cur = qp >> 6
    forced = (blk_i == 0) | (blk_i == cur) | (blk_i == cur - 1)
    valid = (blk_i * SEL_BLOCK <= qp) & (blk_i < ns)
    n_sel = min(N_SEL, ns)
    scores, chosen = [], []
    for g in range(NSA_GROUPS):
        pg = p[g * hpg * tq:(g * hpg + 1) * tq]
        for j in range(1, hpg):
            pg = pg + p[(g * hpg + j) * tq:(g * hpg + j + 1) * tq]
        imp = pg[:, :half] + pg[:, half:]
        if ns_pad > half:
            imp = jnp.concatenate([imp, jnp.zeros((tq, ns_pad - half), F32)], axis=1)
        score = jnp.where(valid, imp + FORCE * forced.astype(F32), NEG)
        scores.append(jnp.where(blk_i < ns, score, -jnp.inf))
        chosen.append(jnp.zeros((tq, ns_pad), F32))
    for _ in range(n_sel):
        for g in range(NSA_GROUPS):
            mx = jnp.max(scores[g], axis=1, keepdims=True)
            first = jnp.min(jnp.where(scores[g] == mx, blk_f, 1e9), axis=1, keepdims=True)
            hit = blk_f == first
            chosen[g] = jnp.where(hit, 1.0, chosen[g])
            scores[g] = jnp.where(hit, -jnp.inf, scores[g])
    for g in range(NSA_GROUPS):
        sel_ref[:, g * ns_pad:(g + 1) * ns_pad] = jnp.where(valid, chosen[g], 0.0)


def _cmp_sel(q3, kcmp, vcmp, cend, q0, ns, ns_pad):
    B, T, _ = q3.shape
    nc = kcmp.shape[1]
    tq = min(TQ_CMP, T)
    return pl.pallas_call(
        functools.partial(_cmp_sel_kernel, tq=tq, q0=q0, nc=nc, ns=ns, ns_pad=ns_pad),
        out_shape=[jax.ShapeDtypeStruct((B, T, MIX), F32), jax.ShapeDtypeStruct((B, T, NSA_GROUPS * ns_pad), F32)],
        grid=(B, T // tq),
        in_specs=[pl.BlockSpec((None, tq, MIX), lambda b, i: (b, i, 0)),
                  pl.BlockSpec((None, nc, LANES), lambda b, i: (b, 0, 0)),
                  pl.BlockSpec((None, nc, LANES), lambda b, i: (b, 0, 0)),
                  pl.BlockSpec((1, nc), lambda b, i: (0, 0))],
        out_specs=[pl.BlockSpec((None, tq, MIX), lambda b, i: (b, i, 0)),
                   pl.BlockSpec((None, tq, NSA_GROUPS * ns_pad), lambda b, i: (b, i, 0))],
        compiler_params=_cp("parallel", "arbitrary"),
    )(q3, kcmp, vcmp, cend)


def _flash_kernel(*refs, cfg):
    mode, layout, tq, tk = cfg['mode'], cfg['layout'], cfg['tq'], cfg['tk']
    q0, kbase, pps, has_tail, tail_pos0 = cfg['q0'], cfg['kbase'], cfg['pps'], cfg['has_tail'], cfg['tail_pos0']
    ns_pad, n_pref, qw, kv_t = cfg['ns_pad'], cfg['n_pref'], cfg['qw'], cfg['kv_t']
    head_map = NSA_HEAD_MAP if layout == 'nsa' else FOX_HEAD_MAP
    nkb = 1 if layout in ('nsa', 'fox_wide') else FOX_HEADS // 2
    n_heads_kb = FOX_HEADS if layout == 'fox_wide' else len(head_map[0])
    rows = n_heads_kb * tq
    qi_ref, kj_ref, fl_ref = refs[:3]
    pos = n_pref
    q_ref = refs[pos]; pos += 1
    npg = max(pps, 1)
    k_refs = refs[pos:pos + npg]; pos += npg
    v_refs = refs[pos:pos + npg]; pos += npg
    if has_tail:
        tk_ref, tv_ref = refs[pos:pos + 2]; pos += 2
    c_ref = ct_ref = sel_ref = None
    if cfg['has_caug'] or cfg['has_crow']:
        c_ref = refs[pos]; pos += 1
        if has_tail:
            ct_ref = refs[pos]; pos += 1
    if mode == 'sel':
        sel_ref = refs[pos]; pos += 1
    o_ref = refs[pos]; pos += 1
    qst_scr, m_scr, l_scr, acc_scr = refs[pos:pos + 4]

    s_id = pl.program_id(1)
    qi = qi_ref[s_id]
    kj = kj_ref[s_id]
    fl = fl_ref[s_id]
    scale = NSA_DH ** -0.5

    @pl.when((fl & 1) != 0)
    def _init():
        qblk = q_ref[...]
        if layout == 'fox_wide':
            qst_scr[0] = _stack_wide(qblk, FOX_HEADS, scale).astype(BF16)
        elif layout == 'nsa':
            qbd = _stack_heads(qblk, head_map[0], scale).astype(BF16)
            if mode == 'sel':
                selblk = sel_ref[...]
                hpg = NSA_HEADS // NSA_GROUPS
                parts = []
                for g in range(NSA_GROUPS):
                    off = ((selblk[:, g * ns_pad:(g + 1) * ns_pad] - 1.0) * SEL_OFF).astype(BF16)
                    parts += [off] * hpg
                qbd = jnp.concatenate([qbd, jnp.concatenate(parts, axis=0)], axis=1)
            qst_scr[0] = qbd
        else:
            for kb in range(nkb):
                qbd = _stack_heads(qblk, head_map[kb], scale).astype(BF16)
                if cfg['has_caug']:
                    lane = _iota((rows, LANES), 1)
                    head = 2 * kb + (_iota((rows, LANES), 0) >= tq).astype(jnp.int32)
                    aug = jnp.where((lane >= 3 * head) & (lane < 3 * head + 3), -1.0, 0.0).astype(BF16)
                    qbd = jnp.concatenate([qbd, aug], axis=1)
                qst_scr[kb] = qbd
        m_scr[...] = jnp.full(m_scr.shape, NEG, F32)
        l_scr[...] = jnp.zeros(l_scr.shape, F32)
        acc_scr[...] = jnp.zeros(acc_scr.shape, F32)

    def update(k_tile, v_tile, kpos0, c_tile, width, masked):
        if masked:
            qpos = q0 + qi * tq + (_iota((rows, 1), 0) & (tq - 1))
            kpos = kpos0 + _iota((1, width), 1)
            mask = kpos <= qpos
            if mode == 'band':
                mask = mask & (kpos > qpos - WINDOW)
        for kb in range(nkb):
            if kv_t:
                kt = k_tile.astype(BF16)
                if mode == 'sel':
                    onehot = (_iota((ns_pad, width), 0) == (kpos0 >> 6) + (_iota((ns_pad, width), 1) >> 6))
                    kt = jnp.concatenate([kt, onehot.astype(BF16)], axis=0)
                sc = _dot(qst_scr[kb], kt)
                if c_tile is not None:
                    sc = sc - jnp.concatenate([jnp.broadcast_to(c_tile[h:h + 1, :], (tq, width))
                                               for h in range(n_heads_kb)], axis=0)
            else:
                kt = k_tile[:, kb * LANES:(kb + 1) * LANES].astype(BF16)
                if mode == 'sel':
                    onehot = (_iota((width, ns_pad), 1) == (kpos0 >> 6) + (_iota((width, ns_pad), 0) >> 6))
                    kt = jnp.concatenate([kt, onehot.astype(BF16)], axis=1)
                if c_tile is not None:
                    kt = jnp.concatenate([kt, c_tile], axis=1)
                sc = _dot_nt(qst_scr[kb], kt)
            if masked:
                sc = jnp.where(mask, sc, NEG)
            m_old = m_scr[kb]
            m_new = jnp.maximum(m_old, jnp.max(sc, axis=1, keepdims=True))
            alpha = jnp.exp(m_old - m_new)
            p = jnp.exp(sc - _lane_repeat(m_new, width))
            l_scr[kb] = alpha * l_scr[kb] + jnp.sum(p, axis=1, keepdims=True)
            if kv_t:
                pv = _dot_nt(p.astype(BF16), v_tile.astype(BF16))
            else:
                pv = _dot(p.astype(BF16), v_tile[:, kb * LANES:(kb + 1) * LANES].astype(BF16))
            acc_scr[kb] = _lane_repeat(alpha, pv.shape[1]) * acc_scr[kb] + pv
            m_scr[kb] = m_new

    def main_update(masked):
        if pps > 1:
            k_tile = jnp.concatenate([r[...] for r in k_refs], axis=1)
            v_tile = jnp.concatenate([r[...] for r in v_refs], axis=1)
        else:
            k_tile, v_tile = k_refs[0][...], v_refs[0][...]
        update(k_tile, v_tile, kbase + kj * tk, c_ref[...] if c_ref is not None else None, tk, masked)

    is_tail = (fl & 4) != 0
    need_mask = (fl & 8) != 0
    if has_tail:
        @pl.when(is_tail)
        def _():
            update(tk_ref[...], tv_ref[...], tail_pos0, ct_ref[...] if ct_ref is not None else None, PAGE, True)

    @pl.when(jnp.logical_and(jnp.logical_not(is_tail), need_mask))
    def _():
        main_update(True)

    @pl.when(jnp.logical_and(jnp.logical_not(is_tail), jnp.logical_not(need_mask)))
    def _():
        main_update(False)

    @pl.when((fl & 2) != 0)
    def _fin():
        fulls = [acc_scr[kb] / _lane_repeat(jnp.maximum(l_scr[kb], 1e-30), acc_scr.shape[2]) for kb in range(nkb)]
        if layout == 'fox_wide':
            o_ref[...] = _unstack_wide(fulls[0], FOX_HEADS, tq)
        else:
            outs = _unstack_heads(fulls, head_map, tq, qw // LANES)
            for c, blk in enumerate(outs):
                o_ref[:, c * LANES:(c + 1) * LANES] = blk


def _schedule(nq, tq, tk, q0, kbase, n_ktiles, mode, has_tail):
    qi_l, kj_l, fl_l = [], [], []
    for qi in range(nq):
        q_lo, q_hi = q0 + qi * tq, q0 + (qi + 1) * tq - 1
        entries = []
        for kj in range(n_ktiles):
            k_lo, k_hi = kbase + kj * tk, kbase + (kj + 1) * tk - 1
            if k_lo > q_hi:
                continue
            if mode == 'band' and k_hi <= q_lo - WINDOW:
                continue
            all_visible = k_hi <= q_lo and (mode != 'band' or k_lo > q_hi - WINDOW)
            entries.append((kj, 0 if all_visible else 8))
        if has_tail:
            entries.append((entries[-1][0] if entries else 0, 4 | 8))
        for n, (kj, f) in enumerate(entries):
            f |= (1 if n == 0 else 0) | (2 if n == len(entries) - 1 else 0)
            qi_l.append(qi); kj_l.append(kj); fl_l.append(f)
    return (np.asarray(qi_l, np.int32), np.asarray(kj_l, np.int32), np.asarray(fl_l, np.int32))


def _flash(q3, k3, v3, *, mode, layout, tq, tk, q0, kbase=0, page_table=None, pps=0, tail=None, tail_pos0=0,
           caug=None, crow=None, crow_tail=None, sel=None, ns_pad=0):
    B, Tq, qw = q3.shape
    nq = Tq // tq
    paged = page_table is not None
    has_tail = tail is not None
    if paged:
        kw = k3.shape[1]
        npages = page_table.shape[1]
        tk = pps * PAGE
        n_ktiles = npages // pps
    else:
        kw = k3.shape[-1]
        n_ktiles = k3.shape[1] // tk
    qi_t, kj_t, fl_t = _schedule(nq, tq, tk, q0, kbase, n_ktiles, mode, has_tail)
    nsteps = len(qi_t)
    prefetch = [jnp.asarray(qi_t), jnp.asarray(kj_t), jnp.asarray(fl_t)]
    if paged:
        prefetch.append(page_table.reshape(-1).astype(jnp.int32))
    n_pref = len(prefetch)
    nkb = 1 if layout in ('nsa', 'fox_wide') else FOX_HEADS // 2
    n_heads_kb = FOX_HEADS if layout == 'fox_wide' else (NSA_HEADS if layout == 'nsa' else 2)
    rows = n_heads_kb * tq
    cq = kw if layout == 'fox_wide' else LANES
    if mode == 'sel':
        cq += ns_pad
    if caug is not None:
        cq += LANES
    cv = kw if layout == 'fox_wide' else LANES

    in_specs = [pl.BlockSpec((None, tq, qw), lambda b, s, qi, kj, fl, *_: (b, qi[s], 0))]
    args = [q3]
    if paged:
        def page_map(i):
            return lambda b, s, qi, kj, fl, pt: (pt[b * npages + kj[s] * pps + i], 0, 0)
        for arr in (k3, v3):
            for i in range(pps):
                in_specs.append(pl.BlockSpec((None, kw, PAGE), page_map(i)))
                args.append(arr)
    else:
        for arr in (k3, v3):
            in_specs.append(pl.BlockSpec((None, tk, kw), lambda b, s, qi, kj, fl, *_: (b, kj[s], 0)))
            args.append(arr)
    if has_tail:
        for arr in tail:
            in_specs.append(pl.BlockSpec((None, kw, PAGE), lambda b, s, qi, kj, fl, *_: (b, 0, 0)))
            args.append(arr)
    if caug is not None:
        in_specs.append(pl.BlockSpec((None, tk, LANES), lambda b, s, qi, kj, fl, *_: (b, kj[s], 0)))
        args.append(caug)
    if crow is not None:
        in_specs.append(pl.BlockSpec((None, 8, tk), lambda b, s, qi, kj, fl, *_: (b, 0, kj[s])))
        args.append(crow)
        if has_tail:
            in_specs.append(pl.BlockSpec((None, 8, PAGE), lambda b, s, qi, kj, fl, *_: (b, 0, 0)))
            args.append(crow_tail)
    if mode == 'sel':
        in_specs.append(pl.BlockSpec((None, tq, NSA_GROUPS * ns_pad), lambda b, s, qi, kj, fl, *_: (b, qi[s], 0)))
        args.append(sel)
    cfg = dict(mode=mode, layout=layout, tq=tq, tk=tk, q0=q0, kbase=kbase, pps=pps if paged else 0,
               has_tail=has_tail, tail_pos0=tail_pos0, ns_pad=ns_pad, n_pref=n_pref, qw=qw, kv_t=paged,
               has_caug=caug is not None, has_crow=crow is not None)
    return pl.pallas_call(
        functools.partial(_flash_kernel, cfg=cfg),
        out_shape=jax.ShapeDtypeStruct((B, Tq, qw), F32),
        grid_spec=pltpu.PrefetchScalarGridSpec(
            num_scalar_prefetch=n_pref,
            grid=(B, nsteps),
            in_specs=in_specs,
            out_specs=pl.BlockSpec((None, tq, qw), lambda b, s, qi, kj, fl, *_: (b, qi[s], 0)),
            scratch_shapes=[pltpu.VMEM((nkb, rows, cq), BF16), pltpu.VMEM((nkb, rows, LANES), F32),
                            pltpu.VMEM((nkb, rows, LANES), F32), pltpu.VMEM((nkb, rows, cv), F32)]),
        compiler_params=_cp("parallel", "arbitrary"),
    )(*prefetch, *args)


def _nsa_combine_kernel(oc_ref, os_ref, ow_ref, sm_ref, e_ref, o_ref):
    gate = _sigmoid(sm_ref[...])
    acc = None
    for j, br in enumerate((oc_ref, os_ref, ow_ref)):
        gexp = _dot(gate, e_ref[j], HI)
        acc = gexp * br[...] if acc is None else acc + gexp * br[...]
    o_ref[...] = acc


def _nsa_combine(o_cmp, o_slc, o_win, small):
    n = o_cmp.shape[0]
    tm = min(512, n)
    src = np.zeros((3, LANES, MIX), np.float32)
    for h in range(NSA_HEADS):
        for j in range(3):
            src[j, 2 * M_HEADS + 3 * h + j, h * NSA_DH:(h + 1) * NSA_DH] = 1.0
    spec = pl.BlockSpec((tm, MIX), lambda i: (i, 0))
    return pl.pallas_call(
        _nsa_combine_kernel, out_shape=jax.ShapeDtypeStruct((n, MIX), F32), grid=(n // tm,),
        in_specs=[spec, spec, spec, pl.BlockSpec((tm, LANES), lambda i: (i, 0)),
                  pl.BlockSpec((3, LANES, MIX), lambda i: (0, 0, 0))],
        out_specs=spec, compiler_params=_cp("parallel"))(o_cmp, o_slc, o_win, small, jnp.asarray(src))


def _fox_cumsum_kernel(sm_ref, bias_ref, place_ref, lf_ref, caug_ref, carry, *, tt):
    @pl.when(pl.program_id(1) == 0)
    def _():
        carry[...] = jnp.zeros(carry.shape, F32)

    lf = _log_sigmoid(sm_ref[...] + bias_ref[...])
    lf_ref[...] = lf
    lower = (_iota((tt, tt), 1) <= _iota((tt, tt), 0)).astype(F32)
    c = _dot(lower, lf, HI) + carry[0:1, :]
    carry[...] = jnp.broadcast_to(c[tt - 1:tt, :], carry.shape)
    c_hi = c.astype(BF16)
    r1 = c - c_hi.astype(F32)
    c_mid = r1.astype(BF16)
    c_lo = (r1 - c_mid.astype(F32)).astype(BF16)
    caug = _dot(c_hi, place_ref[0]) + _dot(c_mid, place_ref[1]) + _dot(c_lo, place_ref[2])
    caug_ref[...] = caug.astype(BF16)


def _fox_cumsum(small, bias_row, B, T):
    tt = min(256, T)
    row = lambda b, i: (b * (T // tt) + i, 0)
    place = np.zeros((3, LANES, LANES), np.float32)
    for h in range(FOX_HEADS):
        for j in range(3):
            place[j, h, 3 * h + j] = 1.0
    return pl.pallas_call(
        functools.partial(_fox_cumsum_kernel, tt=tt),
        out_shape=[jax.ShapeDtypeStruct((B * T, LANES), F32), jax.ShapeDtypeStruct((B * T, LANES), BF16)],
        grid=(B, T // tt),
        in_specs=[pl.BlockSpec((tt, LANES), row), pl.BlockSpec((1, LANES), lambda b, i: (0, 0)),
                  pl.BlockSpec((3, LANES, LANES), lambda b, i: (0, 0, 0))],
        out_specs=[pl.BlockSpec((tt, LANES), row), pl.BlockSpec((tt, LANES), row)],
        scratch_shapes=[pltpu.VMEM((8, LANES), F32)],
        compiler_params=_cp("parallel", "arbitrary"),
    )(small, bias_row, jnp.asarray(place, BF16))


def _fox_cumsum_paged_kernel(pt_ref, *refs, pps, n_main):
    lf_refs = refs[:pps]
    sm_ref, bias_ref, c_ref, lfn_ref, carry = refs[pps:]
    j = pl.program_id(1)

    @pl.when(j == 0)
    def _():
        carry[...] = jnp.zeros(carry.shape, F32)

    upper = (_iota((PAGE, PAGE), 0) <= _iota((PAGE, PAGE), 1)).astype(F32)

    @pl.when(j < n_main)
    def _():
        run = carry[:, 0:1]
        for i in range(pps):
            c = _dot(lf_refs[i][...], upper, HI) + run
            c_ref[:, i * PAGE:(i + 1) * PAGE] = c
            run = c[:, PAGE - 1:PAGE]
        carry[...] = jnp.broadcast_to(run, carry.shape)

    @pl.when(j == n_main)
    def _():
        lf = _log_sigmoid(sm_ref[...] + bias_ref[...])
        lfn_ref[...] = lf
        eye8 = (_iota((8, LANES), 0) == _iota((8, LANES), 1)).astype(F32)
        c_new = _dot(_dot_nt(eye8, lf, HI), upper, HI) + carry[:, 0:1]
        for i in range(pps):
            c_ref[:, i * PAGE:(i + 1) * PAGE] = c_new


def _fox_cumsum_paged(lf_pool_t, page_table, small_pad, bias_row, pps):
    B, npages = page_table.shape
    n_main = npages // pps

    def page_map(i):
        return lambda b, j, pt: (pt[b * npages + jnp.minimum(j, n_main - 1) * pps + i], 0, 0)

    return pl.pallas_call(
        functools.partial(_fox_cumsum_paged_kernel, pps=pps, n_main=n_main),
        out_shape=[jax.ShapeDtypeStruct((B, 8, (n_main + 1) * pps * PAGE), F32),
                   jax.ShapeDtypeStruct((B, PAGE, LANES), F32)],
        grid_spec=pltpu.PrefetchScalarGridSpec(
            num_scalar_prefetch=1, grid=(B, n_main + 1),
            in_specs=[pl.BlockSpec((None, 8, PAGE), page_map(i)) for i in range(pps)]
            + [pl.BlockSpec((None, PAGE, LANES), lambda b, j, pt: (b, 0, 0)),
               pl.BlockSpec((1, LANES), lambda b, j, pt: (0, 0))],
            out_specs=[pl.BlockSpec((None, 8, pps * PAGE), lambda b, j, pt: (b, 0, j)),
                       pl.BlockSpec((None, PAGE, LANES), lambda b, j, pt: (b, 0, 0))],
            scratch_shapes=[pltpu.VMEM((8, LANES), F32)]),
        compiler_params=_cp("parallel", "arbitrary"),
    )(page_table.reshape(-1).astype(jnp.int32), *([lf_pool_t] * pps), small_pad, bias_row)


def _rwkv_prep_kernel(rw_ref, prev_ref, sh_ref, mu_ref, w0_ref, w2_ref, a0_ref, a2_ref, g2_ref, kk_ref, ka_ref,
                      bd_ref, r_o, lw_o, k_o, v_o, kk_o, a_o, g_o, *, tt):
    i = pl.program_id(1)
    rw = rw_ref[...]
    first_prev = jnp.where(i == 0, sh_ref[...], prev_ref[7:8, :])
    prev = jnp.where(_iota(rw.shape, 0) == 0, first_prev, pltpu.roll(rw, 1, 0))
    m = rw + mu_ref[...] * (prev - rw)
    r, kr, vr = m[:, 0:MIX], m[:, MIX:2 * MIX], m[:, 2 * MIX:3 * MIX]
    lora = m[:, 3 * MIX:3 * MIX + LANES]
    gl = m[:, 3 * MIX + LANES:3 * MIX + 2 * LANES]
    w = -_softplus(-(w0_ref[...] + _dot(jnp.tanh(lora).astype(BF16), w2_ref[...]))) - 0.5
    a = _sigmoid(a0_ref[...] + _dot(lora.astype(BF16), a2_ref[...]))
    kk = kr * kk_ref[...]
    ss = _dot(kk * kk, bd_ref[...], HI)
    r_o[...] = r
    lw_o[...] = -jnp.exp(w)
    k_o[...] = kr * (1.0 + (a - 1.0) * ka_ref[...])
    v_o[...] = vr
    kk_o[...] = kk / jnp.maximum(jnp.sqrt(ss), 1e-12)
    a_o[...] = a
    g_o[...] = _dot(_sigmoid(gl).astype(BF16), g2_ref[...])


def _rwkv_prep(rw3, shift0, P):
    B, T, wdt = rw3.shape
    tt = min(256, T)
    nblk8 = tt // 8
    row3 = lambda b, i: (b, i, 0)
    vec = lambda n: pl.BlockSpec((1, n), lambda b, i: (0, 0))
    outs = pl.pallas_call(
        functools.partial(_rwkv_prep_kernel, tt=tt),
        out_shape=[jax.ShapeDtypeStruct((B, T, MIX), F32)] * 7,
        grid=(B, T // tt),
        in_specs=[pl.BlockSpec((None, tt, wdt), row3),
                  pl.BlockSpec((None, 8, wdt), lambda b, i: (b, jnp.maximum(i * nblk8 - 1, 0), 0)),
                  pl.BlockSpec((None, 1, wdt), lambda b, i: (b, 0, 0)),
                  vec(wdt), vec(MIX), pl.BlockSpec((LANES, MIX), lambda b, i: (0, 0)),
                  vec(MIX), pl.BlockSpec((LANES, MIX), lambda b, i: (0, 0)),
                  pl.BlockSpec((LANES, MIX), lambda b, i: (0, 0)), vec(MIX), vec(MIX),
                  pl.BlockSpec((MIX, MIX), lambda b, i: (0, 0))],
        out_specs=[pl.BlockSpec((None, tt, MIX), row3)] * 7,
        compiler_params=_cp("parallel", "arbitrary"),
    )(rw3, rw3, shift0[:, None, :], P['mu'], P['w0'], P['w2'], P['a0'], P['a2'], P['g2'], P['k_k'], P['k_a'], P['bd64'])
    return outs


def _rwkv_scan_kernel(r_ref, lw_ref, k_ref, v_ref, kk_ref, a_ref, g_ref, s0_ref, rk_ref, lng_ref, lnb_ref,
                      y_ref, s1_ref, s_scr, *, L):
    ci = pl.program_id(1)

    @pl.when(ci == 0)
    def _():
        s_scr[...] = s0_ref[...]

    r_i = _iota((L, L), 0)
    c_i = _iota((L, L), 1)
    lower = (c_i <= r_i).astype(F32)
    strict = c_i < r_i
    incl = c_i <= r_i
    eye_l = (c_i == r_i).astype(F32)
    lane = _iota((L, LANES), 1)
    half_masks = (lane < 64, lane >= 64)
    rr = _iota((LANES, LANES), 0)
    cc = _iota((LANES, LANES), 1)
    bdmask = (rr >= 64) == (cc >= 64)
    eye128 = rr == cc
    bd_f = bdmask.astype(F32)
    pairs = range(RWKV_HEADS // 2)
    sls = [slice(pr * LANES, (pr + 1) * LANES) for pr in pairs]
    P = []
    for sl in sls:
        lw = lw_ref[:, sl]
        cl = _dot(lower, lw, HI)
        w_in = jnp.exp(cl)
        w_out = jnp.exp(-cl)
        kk = kk_ref[:, sl]
        P.append(dict(w_in=w_in, kap=kk * jnp.exp(cl - lw), bet=kk * a_ref[:, sl] * w_out,
                      kt=k_ref[:, sl] * w_out, rt=r_ref[:, sl] * w_in, v=v_ref[:, sl]))
    for pr, d in zip(pairs, P):
        d['sbd'] = s_scr[pr]
        d['rhs'] = _bdot(d['kap'], d['sbd'])
        d['ys'] = _bdot(d['rt'], d['sbd'])
    H = []
    for d in P:
        for hm in half_masks:
            kap_h = jnp.where(hm, d['kap'], 0.0)
            rt_h = jnp.where(hm, d['rt'], 0.0)
            amat = jnp.where(strict, _bdot_nt(kap_h, d['bet']), 0.0)
            H.append(dict(d=d, hm=hm, pw=amat, tinv=eye_l - amat,
                          cmat=jnp.where(strict, _bdot_nt(kap_h, d['kt']), 0.0),
                          rb=jnp.where(incl, _bdot_nt(rt_h, d['bet']), 0.0),
                          rk=jnp.where(incl, _bdot_nt(rt_h, d['kt']), 0.0)))
    n = 2
    while n < L:
        for h in H:
            h['pw'] = _dot3(h['pw'], h['pw'])
        for h in H:
            h['tinv'] = h['tinv'] + _dot3(h['tinv'], h['pw'])
        n *= 2
    for h in H:
        d, hm = h['d'], h['hm']
        h['v_h'] = jnp.where(hm, d['v'], 0.0)
        h['rhs_h'] = jnp.where(hm, d['rhs'], 0.0) + _bdot(h['cmat'], h['v_h'])
    for h in H:
        h['u_h'] = -_dot3(h['tinv'], h['rhs_h'])
    for h in H:
        h['y_h'] = _bdot(h['rb'], h['u_h']) + _bdot(h['rk'], h['v_h'])
    for pr, (d, sl) in enumerate(zip(P, sls)):
        h0, h1 = H[2 * pr], H[2 * pr + 1]
        u = h0['u_h'] + h1['u_h']
        y = d['ys'] + h0['y_h'] + h1['y_h']
        upd = jnp.where(bdmask, _bdot_tn(d['bet'], u) + _bdot_tn(d['kt'], d['v']), 0.0)
        w_end = jnp.broadcast_to(d['w_in'][L - 1:L, :], (LANES, LANES))
        w_end_col = jnp.sum(jnp.where(eye128, w_end, 0.0), axis=1, keepdims=True)
        s_scr[pr] = (d['sbd'] + upd) * w_end_col
        d['y'] = y
    for d, sl in zip(P, sls):
        y = d['y']
        mean = _dot3(y, bd_f) * (1.0 / RWKV_N)
        yc = y - mean
        var = _dot3(yc * yc, bd_f) * (1.0 / RWKV_N)
        yn = yc * lax.rsqrt(var + 64e-5) * lng_ref[:, sl] + lnb_ref[:, sl]
        bonus = _dot3(r_ref[:, sl] * k_ref[:, sl] * rk_ref[:, sl], bd_f)
        y_ref[:, sl] = (yn + bonus * d['v']) * g_ref[:, sl]

    @pl.when(ci == pl.num_programs(1) - 1)
    def _():
        s1_ref[...] = s_scr[...]


def _rwkv_scan(r, lw, k, v, kk, a, g, s0, P, B, T):
    L = RWKV_CHUNK
    tp = -(-T // L) * L
    if tp != T:
        pad = lambda t: jnp.pad(t, ((0, 0), (0, tp - T), (0, 0)))
        r, lw, k, v, kk, a, g = (pad(t) for t in (r, lw, k, v, kk, a, g))
    nch = tp // L
    st = jnp.swapaxes(s0.astype(F32), -1, -2).reshape(B, RWKV_HEADS // 2, 2, RWKV_N, RWKV_N)
    z = jnp.zeros_like(st[:, :, 0])
    sbd0 = jnp.concatenate([jnp.concatenate([st[:, :, 0], z], -1), jnp.concatenate([z, st[:, :, 1]], -1)], -2)
    row3 = lambda b, c: (b, c, 0)
    st_spec = pl.BlockSpec((None, RWKV_HEADS // 2, LANES, LANES), lambda b, c: (b, 0, 0, 0))
    vec = pl.BlockSpec((1, MIX), lambda b, c: (0, 0))
    y, sbd1 = pl.pallas_call(
        functools.partial(_rwkv_scan_kernel, L=L),
        out_shape=[jax.ShapeDtypeStruct((B, tp, MIX), F32),
                   jax.ShapeDtypeStruct((B, RWKV_HEADS // 2, LANES, LANES), F32)],
        grid=(B, nch),
        in_specs=[pl.BlockSpec((None, L, MIX), row3)] * 7 + [st_spec, vec, vec, vec],
        out_specs=[pl.BlockSpec((None, L, MIX), row3), st_spec],
        scratch_shapes=[pltpu.VMEM((RWKV_HEADS // 2, LANES, LANES), F32)],
        compiler_params=_cp("parallel", "arbitrary"),
    )(r, lw, k, v, kk, a, g, sbd0, P['r_k'], P['ln_g'], P['ln_b'])
    s_e = sbd1[:, :, :RWKV_N, :RWKV_N]
    s_o = sbd1[:, :, RWKV_N:, RWKV_N:]
    s1 = jnp.swapaxes(jnp.stack([s_e, s_o], 2).reshape(B, RWKV_HEADS, RWKV_N, RWKV_N), -1, -2)
    return y[:, :T], s1


def _mem_kernel(x_ref, km_ref, vm_ref, wq_ref, wo_ref, g_ref, b_ref, o_ref):
    x = x_ref[...]
    q = _dot(x.astype(BF16), wq_ref[...])
    heads = []
    for h in range(MEM_HEADS):
        sl = slice(h * MEM_DH, (h + 1) * MEM_DH)
        s = _dot_nt((q[:, sl] * (MEM_DH ** -0.5)).astype(BF16), km_ref[:, sl].astype(BF16))
        e = jnp.exp(s - jnp.max(s, axis=1, keepdims=True))
        p = e / jnp.sum(e, axis=1, keepdims=True)
        heads.append(_dot(p.astype(BF16), vm_ref[:, sl].astype(BF16)))
    o = jnp.concatenate(heads, axis=1)
    y = _dot(o.astype(BF16), wo_ref[...])
    o_ref[...] = _layer_norm_rows(ALPHA * x + y, g_ref[...], b_ref[...])


def _mem_layer(x3, km, vm, wq, wo, g, b):
    B, T, d = x3.shape
    tq = min(512, T)
    mt = km.shape[1]
    full = lambda shape: pl.BlockSpec(shape, lambda bb, i: (0,) * len(shape))
    return pl.pallas_call(
        _mem_kernel, out_shape=jax.ShapeDtypeStruct((B, T, d), F32), grid=(B, T // tq),
        in_specs=[pl.BlockSpec((None, tq, d), lambda bb, i: (bb, i, 0)),
                  pl.BlockSpec((None, mt, d), lambda bb, i: (bb, 0, 0)),
                  pl.BlockSpec((None, mt, d), lambda bb, i: (bb, 0, 0)),
                  full((d, d)), full((d, d)), full((1, d)), full((1, d))],
        out_specs=pl.BlockSpec((None, tq, d), lambda bb, i: (bb, i, 0)),
        compiler_params=_cp("parallel", "arbitrary"),
    )(x3, km, vm, wq, wo, g.reshape(1, d), b.reshape(1, d))


def _moe_kernel(x_ref, wr_ref, br_ref, w1_ref, w3_ref, w2_ref, g_ref, b_ref, o_ref, xb_scr, gw_scr, acc_scr):
    e = pl.program_id(1)

    @pl.when(e == 0)
    def _route():
        x = x_ref[...]
        xb_scr[...] = x.astype(BF16)
        logit = _dot(x, wr_ref[...], HI) + br_ref[...]
        lane = _iota(logit.shape, 1)
        lane_f = lane.astype(F32)
        is_g = lane < N_GROUPS
        gl = jnp.where(is_g, logit, -jnp.inf)
        gmax = jnp.max(gl, axis=1, keepdims=True)
        grp = jnp.min(jnp.where(gl == gmax, lane_f, 1e9), axis=1, keepdims=True)
        g_w = 1.0 / jnp.sum(jnp.where(is_g, jnp.exp(gl - gmax), 0.0), axis=1, keepdims=True)
        lo = N_GROUPS + EXP_PER_GROUP * grp
        in_grp = (lane_f >= lo) & (lane_f < lo + EXP_PER_GROUP)
        el = jnp.where(in_grp, logit, -jnp.inf)
        emax = jnp.max(el, axis=1, keepdims=True)
        ee = jnp.where(in_grp, jnp.exp(el - emax), 0.0)
        p = jnp.where(in_grp, ee / jnp.sum(ee, axis=1, keepdims=True), -1.0)
        p1 = jnp.max(p, axis=1, keepdims=True)
        i1 = jnp.min(jnp.where(p == p1, lane_f, 1e9), axis=1, keepdims=True)
        pr = jnp.where(lane_f == i1, -1.0, p)
        p2 = jnp.max(pr, axis=1, keepdims=True)
        i2 = jnp.min(jnp.where(pr == p2, lane_f, 1e9), axis=1, keepdims=True)
        den = p1 + p2
        gw_scr[...] = jnp.where(lane_f == i1, g_w * p1 / den, 0.0) + jnp.where(lane_f == i2, g_w * p2 / den, 0.0)
        acc_scr[...] = jnp.zeros(acc_scr.shape, F32)

    gw = gw_scr[...]
    gcol = jnp.sum(jnp.where(_iota(gw.shape, 1) == e + N_GROUPS, gw, 0.0), axis=1, keepdims=True)
    xb = xb_scr[...]
    h1 = _dot(xb, w1_ref[...])
    h3 = _dot(xb, w3_ref[...])
    hid = h1 * _sigmoid(h1) * h3
    acc_scr[...] += gcol * _dot(hid.astype(BF16), w2_ref[...])

    @pl.when(e == pl.num_programs(1) - 1)
    def _():
        o_ref[...] = _layer_norm_rows(ALPHA * x_ref[...] + acc_scr[...], g_ref[...], b_ref[...])


def _moe_layer(x, M, g, b):
    n, d = x.shape
    tm = min(1024, n)
    full2 = lambda shape: pl.BlockSpec(shape, lambda i, e: (0, 0))
    return pl.pallas_call(
        _moe_kernel, out_shape=jax.ShapeDtypeStruct((n, d), F32), grid=(n // tm, N_EXPERTS),
        in_specs=[pl.BlockSpec((tm, d), lambda i, e: (i, 0)), full2((d, LANES)), full2((1, LANES)),
                  pl.BlockSpec((None, d, D_EXPERT), lambda i, e: (e, 0, 0)),
                  pl.BlockSpec((None, d, D_EXPERT), lambda i, e: (e, 0, 0)),
                  pl.BlockSpec((None, D_EXPERT, d), lambda i, e: (e, 0, 0)),
                  full2((1, d)), full2((1, d))],
        out_specs=pl.BlockSpec((tm, d), lambda i, e: (i, 0)),
        scratch_shapes=[pltpu.VMEM((tm, d), BF16), pltpu.VMEM((tm, LANES), F32), pltpu.VMEM((tm, d), F32)],
        compiler_params=_cp("parallel", "arbitrary"),
    )(x, M['wr'], M['br'], M['w1'], M['w3'], M['w2'], g.reshape(1, d), b.reshape(1, d))


EVEN_SEGS = ([(MIX, False)] * 4 + [(MIX, True), (LANES, False), (LANES, False), (LANES, True), (LANES, False),
                                   (LANES, True), (LANES, False), (LANES, False)])
ODD_SEGS = [(MIX, False)] * 3 + [(RWKV_SHIFT, False), (LANES, False)]
RW_PERM = np.concatenate([np.arange(0, 512), np.arange(576, 1088), np.arange(1088, 1600), np.arange(512, 576),
                          np.arange(1600, 1664), np.arange(1664, 1792)])
RW_INV = np.argsort(RW_PERM)


def _prep_even(w_in, w_out, b_i, b_f):
    zeros = jnp.zeros((D_MODEL, LANES - 2 * M_HEADS - 3 * NSA_HEADS), F32)
    w = jnp.concatenate([w_in[:, 0:1536], w_in[:, 1544:2056], w_in[:, 2056:3336], w_in[:, 1536:1544],
                         w_in[:, 3336:3360], zeros], axis=1).astype(BF16)
    bias = jnp.concatenate([b_i, b_f, jnp.zeros((LANES - 2 * M_HEADS,), F32)]).reshape(1, LANES)
    return dict(w_in=w, w_out_a=w_out[:MIX].astype(BF16), w_out_b=w_out[MIX:].astype(BF16), gate_bias=bias)


def _prep_odd(w_in, w_out, fox_b_f, mu, w0, w2, a0, a2, g2, k_k, k_a, r_k, ln_g, ln_b):
    rw_cols = w_in[:, 1544:3336][:, RW_PERM]
    zeros = jnp.zeros((D_MODEL, LANES - FOX_HEADS), F32)
    w = jnp.concatenate([w_in[:, 0:1536], rw_cols, w_in[:, 1536:1544], zeros], axis=1).astype(BF16)
    bias = jnp.concatenate([fox_b_f, jnp.zeros((LANES - FOX_HEADS,), F32)]).reshape(1, LANES)
    z64 = jnp.zeros((64, MIX), F32)
    hd = np.arange(MIX) // RWKV_N
    bd64 = jnp.asarray((hd[:, None] == hd[None, :]).astype(np.float32))
    P = dict(mu=mu[RW_PERM].reshape(1, -1), w0=w0.reshape(1, -1), a0=a0.reshape(1, -1),
             w2=jnp.concatenate([w2, z64], 0).astype(BF16), a2=jnp.concatenate([z64, a2], 0).astype(BF16),
             g2=g2.astype(BF16), k_k=k_k.reshape(1, -1), k_a=k_a.reshape(1, -1), r_k=r_k.reshape(1, -1),
             ln_g=ln_g.reshape(1, -1), ln_b=ln_b.reshape(1, -1), bd64=bd64)
    return dict(w_in=w, w_out_a=w_out[:MIX].astype(BF16), w_out_b=w_out[MIX:].astype(BF16), fox_bias=bias, P=P)


def _prep_moe(w_group, b_group, w_expert, b_expert, w1, w3, w2):
    padw = LANES - N_GROUPS - N_EXPERTS
    wr = jnp.concatenate([w_group, w_expert, jnp.zeros((D_MODEL, padw), F32)], axis=1)
    br = jnp.concatenate([b_group, b_expert, jnp.zeros((padw,), F32)]).reshape(1, LANES)
    return dict(wr=wr, br=br, w1=w1.astype(BF16), w3=w3.astype(BF16), w2=w2.astype(BF16))


def _even_odd_blocks(kcmp):
    return jnp.concatenate([kcmp[:, 0::2], kcmp[:, 1::2]], axis=1)


def _pages_t(cache):
    pool, page, h, dh = cache.shape
    return jnp.transpose(cache, (0, 2, 3, 1)).reshape(pool, h * dh, page)


def _tail_t(rows3):
    B, T, _ = rows3.shape
    return jnp.swapaxes(jnp.pad(rows3, ((0, 0), (0, PAGE - T), (0, 0))), 1, 2)


def _even_mixer(x3, q0, W, nsa_w, norm_g, c0, n0, m0, past, ln_g, ln_b):
    B, T, D = x3.shape
    N = B * T
    x2 = x3.reshape(N, D)
    pos = q0 + jnp.arange(T)
    cos, sin = _rope_tables(pos)
    rope = (jnp.tile(cos, (B, 1)), jnp.tile(sin, (B, 1)))
    mq, mk, mv, mo, nq, kc, vc, ks, vs, kw, vw, small = _proj_split(x2, W['w_in'], EVEN_SEGS, rope)
    h, c1, n1, m1 = _mlstm(mq, mk, mv, mo, small, W['gate_bias'], norm_g, c0, n0, m0, B, T)
    pe_k, pe_v, ck1, ck2, cv1, cv2 = nsa_w
    blk_w = CMP_BLOCK * LANES
    q3 = nq.reshape(B, T, MIX)
    if past is None:
        L = T
        nc = L // CMP_BLOCK
        kcmp = _compress(kc.reshape(B * nc, blk_w), pe_k, ck1, ck2).reshape(B, nc, LANES)
        vcmp = _compress(vc.reshape(B * nc, blk_w), pe_v, cv1, cv2).reshape(B, nc, LANES)
    else:
        pt = past['page_table']
        npages = pt.shape[1]
        L = npages * PAGE + T
        nc = L // CMP_BLOCK
        per_page = PAGE // CMP_BLOCK
        pool = past['kc'].shape[0]
        kcp = _compress_pool(_pages_t(past['kc']), pe_k, ck1, ck2).reshape(pool, per_page * LANES)
        vcp = _compress_pool(_pages_t(past['vc']), pe_v, cv1, cv2).reshape(pool, per_page * LANES)
        kcmp = jnp.take(kcp, pt, axis=0).reshape(B, npages * per_page, LANES)[:, :nc]
        vcmp = jnp.take(vcp, pt, axis=0).reshape(B, npages * per_page, LANES)[:, :nc]
    cpos = jnp.arange(nc) * CMP_BLOCK + CMP_BLOCK - 1
    ccos, csin = _rope_tables(cpos)
    kcmp = _rope_rows(kcmp.reshape(B * nc, LANES), jnp.tile(ccos, (B, 1)), jnp.tile(csin, (B, 1))).reshape(B, nc, LANES)
    ns = -(-L // SEL_BLOCK)
    ns_pad = -(-ns // LANES) * LANES
    cend = jnp.concatenate([cpos[0::2], cpos[1::2]]).astype(jnp.int32).reshape(1, nc)
    o_cmp, sel = _cmp_sel(q3, _even_odd_blocks(kcmp), _even_odd_blocks(vcmp), cend, q0, ns, ns_pad)
    ks3, vs3 = ks.reshape(B, T, LANES), vs.reshape(B, T, LANES)
    kw3, vw3 = kw.reshape(B, T, LANES), vw.reshape(B, T, LANES)
    if past is None:
        tq = min(TQ_NSA, T)
        o_slc = _flash(q3, ks3, vs3, mode='sel', layout='nsa', tq=tq, tk=min(TK_SLC, T), q0=0, sel=sel, ns_pad=ns_pad)
        o_win = _flash(q3, kw3, vw3, mode='band', layout='nsa', tq=tq, tk=min(TK_WIN, T), q0=0)
        kw_new, vw_new = kw3[:, -min(WINDOW, T):], vw3[:, -min(WINDOW, T):]
    else:
        o_slc = _flash(q3, _pages_t(past['ks']), _pages_t(past['vs']), mode='sel', layout='nsa', tq=T, tk=0, q0=q0,
                       page_table=pt, pps=_largest_divisor(npages, (8, 4, 2, 1)), tail=(_tail_t(ks3), _tail_t(vs3)),
                       tail_pos0=q0, sel=sel, ns_pad=ns_pad)
        kw_ext = jnp.concatenate([past['kw_buf'], kw3], axis=1)
        vw_ext = jnp.concatenate([past['vw_buf'], vw3], axis=1)
        wb = past['kw_buf'].shape[1]
        o_win = _flash(q3, kw_ext, vw_ext, mode='band', layout='nsa', tq=T, tk=wb + T, q0=q0, kbase=q0 - wb)
        keep = min(WINDOW, wb + T)
        kw_new, vw_new = kw_ext[:, -keep:], vw_ext[:, -keep:]
    o = _nsa_combine(o_cmp.reshape(N, MIX), o_slc.reshape(N, MIX), o_win.reshape(N, MIX), small)
    x_new = _mm_res_ln([h, o], [W['w_out_a'], W['w_out_b']], x2, ln_g, ln_b).reshape(B, T, D)
    sh = (B, T, NSA_GROUPS, NSA_DH)
    wsh = (B, -1, NSA_GROUPS, NSA_DH)
    state = (c1, n1, m1, kc.reshape(sh), vc.reshape(sh), ks.reshape(sh), vs.reshape(sh),
             kw_new.reshape(wsh), vw_new.reshape(wsh))
    return x_new, state


def _odd_mixer(x3, q0, W, s0, shift0, past, ln_g, ln_b):
    B, T, D = x3.shape
    N = B * T
    x2 = x3.reshape(N, D)
    fq, fk, fv, rw, small = _proj_split(x2, W['w_in'], ODD_SEGS)
    q3, k3, v3 = (t.reshape(B, T, MIX) for t in (fq, fk, fv))
    if past is None:
        lf_full, caug = _fox_cumsum(small, W['fox_bias'], B, T)
        lf = lf_full[:, :FOX_HEADS].reshape(B, T, FOX_HEADS)
        o_c = _flash(q3, k3, v3, mode='causal', layout='fox', tq=min(TQ_FOX, T), tk=min(TK_FOX, T), q0=0,
                     caug=caug.reshape(B, T, LANES))
    else:
        pt = past['page_table']
        npages = pt.shape[1]
        pps = _largest_divisor(npages, (8, 4, 2, 1))
        padrows = ((0, 0), (0, PAGE - T), (0, 0))
        lf_pool_t = jnp.swapaxes(past['lf'], 1, 2)
        c_all, lf_new = _fox_cumsum_paged(lf_pool_t, pt, jnp.pad(small.reshape(B, T, LANES), padrows), W['fox_bias'], pps)
        lf = lf_new[:, :T, :FOX_HEADS]
        n_main = npages * PAGE
        o_c = _flash(q3, _pages_t(past['k']), _pages_t(past['v']), mode='causal', layout='fox_wide', tq=T, tk=0, q0=q0,
                     page_table=pt, pps=pps, tail=(_tail_t(k3), _tail_t(v3)), tail_pos0=q0,
                     crow=c_all, crow_tail=c_all[:, :, n_main:n_main + PAGE])
    rw3 = rw.reshape(B, T, RWKV_SHIFT)
    r, lw, k, v, kk, a, g = _rwkv_prep(rw3, shift0[:, RW_PERM], W['P'])
    y, s1 = _rwkv_scan(r, lw, k, v, kk, a, g, s0, W['P'], B, T)
    x_new = _mm_res_ln([o_c.reshape(N, MIX), y.reshape(N, MIX)], [W['w_out_a'], W['w_out_b']], x2, ln_g, ln_b)
    hs = (B, T, FOX_HEADS, FOX_DH)
    state = (fk.reshape(hs), fv.reshape(hs), lf, s1, rw3[:, -1][:, RW_INV])
    return x_new.reshape(B, T, D), state


def kernel(x_prompt, x_sample, state_mlstm_C, state_mlstm_n, state_mlstm_m, cache_nsa_kc, cache_nsa_vc, cache_nsa_ks, cache_nsa_vs, cache_nsa_kw, cache_nsa_vw, cache_fox_k, cache_fox_v, cache_fox_logf, state_rwkv_S, state_rwkv_shift, cache_mem_k, cache_mem_v, page_table, mem_prompt, even_w_in, even_w_out, mlstm_b_i, mlstm_b_f, mlstm_norm_g, nsa_pe_k, nsa_pe_v, nsa_cmp_k_w1, nsa_cmp_k_w2, nsa_cmp_v_w1, nsa_cmp_v_w2, odd_w_in, odd_w_out, fox_b_f, rwkv_mu, rwkv_w0, rwkv_w2, rwkv_a0, rwkv_a2, rwkv_g2, rwkv_k_k, rwkv_k_a, rwkv_r_k, rwkv_ln_g, rwkv_ln_b, mem_wq, mem_wk, mem_wv, mem_wo, moe_w_group, moe_b_group, moe_w_expert, moe_b_expert, moe_w1, moe_w3, moe_w2, ln_g, ln_b):
    bp, tp, d = x_prompt.shape
    bs, ts, _ = x_sample.shape
    depth = ln_g.shape[0]
    past = page_table.shape[1] * cache_nsa_kc.shape[2]
    xp, xs = x_prompt, x_sample
    acc = {}

    def push(prefix, names, vals):
        for nm, val in zip(names, vals):
            acc.setdefault(prefix + nm, []).append(val)

    even_names = ('mlstm_C', 'mlstm_n', 'mlstm_m', 'nsa_kc', 'nsa_vc', 'nsa_ks', 'nsa_vs', 'nsa_kw', 'nsa_vw')
    odd_names = ('fox_k', 'fox_v', 'fox_logf', 'rwkv_S', 'rwkv_shift')
    for layer in range(depth):
        e = layer // 2
        if layer % 2 == 0:
            W = _prep_even(even_w_in[e], even_w_out[e], mlstm_b_i[e], mlstm_b_f[e])
            nsa_w = (nsa_pe_k[e], nsa_pe_v[e], nsa_cmp_k_w1[e], nsa_cmp_k_w2[e], nsa_cmp_v_w1[e], nsa_cmp_v_w2[e])
            xp, sp = _even_mixer(xp, 0, W, nsa_w, mlstm_norm_g[e], jnp.zeros((bp, M_HEADS, M_DV, M_DK), F32),
                                 jnp.zeros((bp, M_HEADS, M_DK), F32), jnp.zeros((bp, M_HEADS), F32), None,
                                 ln_g[layer, 0], ln_b[layer, 0])
            wbuf = lambda c: c[e].reshape(bs, -1, LANES)
            past_d = dict(kc=cache_nsa_kc[e], vc=cache_nsa_vc[e], ks=cache_nsa_ks[e], vs=cache_nsa_vs[e],
                          page_table=page_table, kw_buf=wbuf(cache_nsa_kw), vw_buf=wbuf(cache_nsa_vw))
            xs, ss = _even_mixer(xs, past, W, nsa_w, mlstm_norm_g[e], state_mlstm_C[e], state_mlstm_n[e],
                                 state_mlstm_m[e], past_d, ln_g[layer, 0], ln_b[layer, 0])
            names = even_names
        else:
            W = _prep_odd(odd_w_in[e], odd_w_out[e], fox_b_f[e], rwkv_mu[e], rwkv_w0[e], rwkv_w2[e], rwkv_a0[e],
                          rwkv_a2[e], rwkv_g2[e], rwkv_k_k[e], rwkv_k_a[e], rwkv_r_k[e], rwkv_ln_g[e], rwkv_ln_b[e])
            xp, sp = _odd_mixer(xp, 0, W, jnp.zeros((bp, RWKV_HEADS, RWKV_N, RWKV_N), F32),
                                jnp.zeros((bp, RWKV_SHIFT), F32), None, ln_g[layer, 0], ln_b[layer, 0])
            past_d = dict(k=cache_fox_k[e], v=cache_fox_v[e], lf=cache_fox_logf[e], page_table=page_table)
            xs, ss = _odd_mixer(xs, past, W, state_rwkv_S[e], state_rwkv_shift[e], past_d,
                                ln_g[layer, 0], ln_b[layer, 0])
            names = odd_names
        push('p_', names, sp)
        push('s_', names, ss)
        wkv = jnp.concatenate([mem_wk[layer], mem_wv[layer]], axis=1).astype(BF16)
        mt = mem_prompt.shape[1]
        kmp, vmp = _proj_split(mem_prompt.reshape(bp * mt, d), wkv, [(d, False), (d, False)])
        push('p_', ('mem_k', 'mem_v'), (kmp.reshape(bp, mt, MEM_HEADS, MEM_DH), vmp.reshape(bp, mt, MEM_HEADS, MEM_DH)))
        wq, wo = mem_wq[layer].astype(BF16), mem_wo[layer].astype(BF16)
        xp = _mem_layer(xp, kmp.reshape(bp, mt, d), vmp.reshape(bp, mt, d), wq, wo, ln_g[layer, 1], ln_b[layer, 1])
        xs = _mem_layer(xs, cache_mem_k[layer].reshape(bs, -1, d), cache_mem_v[layer].reshape(bs, -1, d), wq, wo,
                        ln_g[layer, 1], ln_b[layer, 1])
        M = _prep_moe(moe_w_group[layer], moe_b_group[layer], moe_w_expert[layer], moe_b_expert[layer],
                      moe_w1[layer], moe_w3[layer], moe_w2[layer])
        xp = _moe_layer(xp.reshape(bp * tp, d), M, ln_g[layer, 2], ln_b[layer, 2]).reshape(bp, tp, d)
        xs = _moe_layer(xs.reshape(bs * ts, d), M, ln_g[layer, 2], ln_b[layer, 2]).reshape(bs, ts, d)

    st = {name: jnp.stack(vals) for name, vals in acc.items()}
    return (xp, xs,
            st['p_mlstm_C'], st['p_mlstm_n'], st['p_mlstm_m'],
            st['p_nsa_kc'], st['p_nsa_vc'], st['p_nsa_ks'], st['p_nsa_vs'], st['p_nsa_kw'], st['p_nsa_vw'],
            st['p_fox_k'], st['p_fox_v'], st['p_fox_logf'], st['p_rwkv_S'], st['p_rwkv_shift'],
            st['p_mem_k'], st['p_mem_v'],
            st['s_mlstm_C'], st['s_mlstm_n'], st['s_mlstm_m'],
            st['s_nsa_kc'], st['s_nsa_vc'], st['s_nsa_ks'], st['s_nsa_vs'], st['s_nsa_kw'], st['s_nsa_vw'],
            st['s_fox_k'], st['s_fox_v'], st['s_fox_logf'], st['s_rwkv_S'], st['s_rwkv_shift'])
```

```python
import functools

import numpy as np
import jax
import jax.numpy as jnp
from jax import lax
from jax.experimental import pallas as pl
from jax.experimental.pallas import tpu as pltpu

F32 = jnp.float32
BF16 = jnp.bfloat16
HI = lax.Precision.HIGHEST

D_MODEL = 1024
DEPTH = 2
PAGE = 128
M_HEADS, M_DK, M_DV, M_CHUNK = 4, 128, 128, 64
NSA_HEADS, NSA_GROUPS, NSA_DH = 8, 2, 64
CMP_BLOCK, SEL_BLOCK, N_SEL, WINDOW = 32, 64, 16, 512
FOX_HEADS, FOX_DH = 8, 64
RWKV_HEADS, RWKV_N, W_LORA, A_LORA, G_LORA = 8, 64, 64, 64, 128
RWKV_CHUNK = 64
MEM_HEADS = 4
MEM_DH = D_MODEL // MEM_HEADS
N_GROUPS, EXP_PER_GROUP, TOP_K, D_EXPERT = 4, 8, 2, 256
N_EXPERTS = N_GROUPS * EXP_PER_GROUP
ROPE_THETA = 10000.0
ALPHA = (2.0 * DEPTH) ** 0.25
LN_EPS = 1e-5
NEG = -1e30
FORCE = 1e4
MIX = 512
RWKV_SHIFT = 1792
SEL_OFF = 2.0 ** 30
LOG2E = 1.4426950408889634

LANES = 128
VMEM_LIMIT = 48 * 1024 * 1024
TQ_FOX, TK_FOX = 512, 512
TQ_SLC, TK_SLC = 256, 512
TQ_WIN, TK_WIN = 128, 128
TQ_CMP = 256


def _cp(*sem):
    return pltpu.CompilerParams(dimension_semantics=sem, vmem_limit_bytes=VMEM_LIMIT)


def _dot(a, b, prec=None):
    return jnp.dot(a, b, preferred_element_type=F32, precision=prec)


def _dot_nt(a, b, prec=None):
    return lax.dot_general(a, b, (((1,), (1,)), ((), ())), preferred_element_type=F32, precision=prec)


def _dot_tn(a, b, prec=None):
    return lax.dot_general(a, b, (((0,), (0,)), ((), ())), preferred_element_type=F32, precision=prec)


def _bdot(a, b):
    return _dot(a.astype(BF16), b.astype(BF16))


def _bdot_nt(a, b):
    return _dot_nt(a.astype(BF16), b.astype(BF16))


def _bdot_tn(a, b):
    return _dot_tn(a.astype(BF16), b.astype(BF16))


def _split_bf16(a):
    hi = a.astype(BF16)
    return hi, (a - hi.astype(F32)).astype(BF16)


def _dot3(a, b):
    ah, al = _split_bf16(a)
    bh, bl = _split_bf16(b)
    return _dot(ah, bh) + (_dot(ah, bl) + _dot(al, bh))


def _split3_bf16(a):
    hi = a.astype(BF16)
    r1 = a - hi.astype(F32)
    mid = r1.astype(BF16)
    return hi, mid, (r1 - mid.astype(F32)).astype(BF16)


def _dot_r01(a, b01):
    b = b01.astype(BF16)
    hi, mid, lo = _split3_bf16(a)
    return _dot(hi, b) + (_dot(mid, b) + _dot(lo, b))


def _dot_l01(a01, b):
    a = a01.astype(BF16)
    hi, mid, lo = _split3_bf16(b)
    return _dot(a, hi) + (_dot(a, mid) + _dot(a, lo))


def _dot_nt_l01(a01, b):
    a = a01.astype(BF16)
    hi, mid, lo = _split3_bf16(b)
    return _dot_nt(a, hi) + (_dot_nt(a, mid) + _dot_nt(a, lo))


def _iota(shape, dim):
    return lax.broadcasted_iota(jnp.int32, shape, dim)


def _log_sigmoid(x):
    return jnp.minimum(x, 0.0) - jnp.log1p(jnp.exp(-jnp.abs(x)))


def _sigmoid(x):
    return 1.0 / (1.0 + jnp.exp(-x))


def _softplus(x):
    return jnp.maximum(x, 0.0) + jnp.log1p(jnp.exp(-jnp.abs(x)))


def _layer_norm_rows(z, g, b):
    mu = jnp.mean(z, axis=-1, keepdims=True)
    zc = z - mu
    var = jnp.mean(zc * zc, axis=-1, keepdims=True)
    return zc * lax.rsqrt(var + LN_EPS) * g + b


def _rope128(y, cos, sin_signed):
    lane = _iota(y.shape, 1)
    lo = (lane & 63) < 32
    sw = jnp.where(lo, pltpu.roll(y, 96, 1), pltpu.roll(y, 32, 1))
    return y * cos + sw * sin_signed


def _rope_tables(pos):
    half = NSA_DH // 2
    freq = ROPE_THETA ** (-jnp.arange(half, dtype=F32) / half)
    ang = pos.astype(F32)[:, None] * freq[None, :]
    cos, sin = jnp.cos(ang), jnp.sin(ang)
    return jnp.tile(cos, (1, 4)), jnp.tile(jnp.concatenate([-sin, sin], -1), (1, 2))


def _largest_divisor(n, options):
    for p in options:
        if n % p == 0:
            return p


def _proj_kernel(*refs, segs, use_rope):
    if use_rope:
        x_ref, w_ref, cos_ref, sin_ref = refs[:4]
        outs = refs[4:]
    else:
        x_ref, w_ref = refs[:2]
        outs = refs[2:]
    xb = x_ref[...].astype(BF16)
    off = 0
    for (wd, rp), o_ref in zip(segs, outs):
        y = _dot(xb, w_ref[:, off:off + wd])
        if rp:
            for c in range(wd // LANES):
                o_ref[:, c * LANES:(c + 1) * LANES] = _rope128(y[:, c * LANES:(c + 1) * LANES], cos_ref[...], sin_ref[...])
        else:
            o_ref[...] = y
        off += wd


def _proj_split(x, w, segs, rope=None):
    n, k = x.shape
    tm = min(256, n)
    wtot = sum(s[0] for s in segs)
    use_rope = rope is not None
    in_specs = [pl.BlockSpec((tm, k), lambda i: (i, 0)), pl.BlockSpec((k, wtot), lambda i: (0, 0))]
    args = [x, w]
    if use_rope:
        in_specs += [pl.BlockSpec((tm, LANES), lambda i: (i, 0))] * 2
        args += list(rope)
    return pl.pallas_call(
        functools.partial(_proj_kernel, segs=tuple(segs), use_rope=use_rope),
        out_shape=[jax.ShapeDtypeStruct((n, wd), F32) for wd, _ in segs],
        grid=(n // tm,),
        in_specs=in_specs,
        out_specs=[pl.BlockSpec((tm, wd), lambda i: (i, 0)) for wd, _ in segs],
        compiler_params=_cp("parallel"),
    )(*args)


def _mm_res_ln_kernel(*refs, n_in):
    a_refs = refs[:n_in]
    w_refs = refs[n_in:2 * n_in]
    x_ref, g_ref, b_ref, o_ref = refs[2 * n_in:]
    y = _dot(a_refs[0][...].astype(BF16), w_refs[0][...])
    for a_ref, w_ref in zip(a_refs[1:], w_refs[1:]):
        y = y + _dot(a_ref[...].astype(BF16), w_ref[...])
    o_ref[...] = _layer_norm_rows(ALPHA * x_ref[...] + y, g_ref[...], b_ref[...])


def _mm_res_ln(a_list, w_list, x, g, b):
    n, d = x.shape
    tm = min(512, n)
    n_in = len(a_list)
    in_specs = ([pl.BlockSpec((tm, a.shape[1]), lambda i: (i, 0)) for a in a_list]
                + [pl.BlockSpec(w.shape, lambda i: (0, 0)) for w in w_list]
                + [pl.BlockSpec((tm, d), lambda i: (i, 0)), pl.BlockSpec((1, d), lambda i: (0, 0)),
                   pl.BlockSpec((1, d), lambda i: (0, 0))])
    return pl.pallas_call(
        functools.partial(_mm_res_ln_kernel, n_in=n_in),
        out_shape=jax.ShapeDtypeStruct((n, d), F32),
        grid=(n // tm,),
        in_specs=in_specs,
        out_specs=pl.BlockSpec((tm, d), lambda i: (i, 0)),
        compiler_params=_cp("parallel"),
    )(*a_list, *w_list, x, g.reshape(1, d), b.reshape(1, d))


def _mlstm_kernel(q_ref, k_ref, v_ref, o_ref, sm_ref, bias_ref, g_ref, c0_ref, n0_ref, m0_ref,
                  h_ref, c1_ref, n1_ref, m1_ref, c_scr, n_scr, m_scr, *, L):
    ci = pl.program_id(1)

    @pl.when(ci == 0)
    def _():
        c_scr[...] = c0_ref[...]
        n_scr[...] = n0_ref[...]
        m_scr[...] = m0_ref[...]

    sm = sm_ref[...] + bias_ref[...]
    lane = _iota(sm.shape, 1)
    gates = jnp.where(lane < M_HEADS, sm, _log_sigmoid(sm))
    eye8 = (_iota((8, LANES), 0) == _iota((8, LANES), 1)).astype(F32)
    gates_row = _dot_nt_l01(eye8, gates)
    r_i = _iota((L, L), 0)
    c_i = _iota((L, L), 1)
    tri = c_i <= r_i
    scale = M_DK ** -0.5
    hs = []
    for h in range(M_HEADS):
        ic_col = gates[:, h:h + 1]
        f_col = gates[:, M_HEADS + h:M_HEADS + h + 1]
        ic_row = gates_row[h:h + 1, :]
        f_row = gates_row[M_HEADS + h:M_HEADS + h + 1, :]
        b_col = jnp.sum(jnp.where(tri, jnp.broadcast_to(f_row, (L, L)), 0.0), axis=1, keepdims=True)
        b_row = jnp.sum(jnp.where(r_i <= c_i, jnp.broadcast_to(f_col, (L, L)), 0.0), axis=0, keepdims=True)
        m_prev = m_scr[h:h + 1, 0:1]
        qh = q_ref[:, h * M_DK:(h + 1) * M_DK]
        kh = k_ref[:, h * M_DK:(h + 1) * M_DK] * scale
        vh = v_ref[:, h * M_DV:(h + 1) * M_DV]
        hs.append(dict(ic_col=ic_col, ic_row=ic_row, b_col=b_col, b_row=b_row, m_prev=m_prev, qh=qh, kh=kh, vh=vh,
                       qb=qh.astype(BF16), kb=kh.astype(BF16), vb=vh.astype(BF16),
                       c_old=c_scr[h], n_old=n_scr[h:h + 1, :]))
    for d in hs:
        dmat = jnp.where(tri, d['b_col'] - d['b_row'] + d['ic_row'], NEG)
        inter = d['b_col'] + d['m_prev']
        d['mt'] = jnp.maximum(inter, jnp.max(dmat, axis=1, keepdims=True))
        d['wmat'] = jnp.where(tri, jnp.exp(dmat - d['mt']), 0.0)
        d['a'] = jnp.exp(inter - d['mt'])
    for d in hs:
        d['s'] = _dot_nt(d['qb'], d['kb']) * d['wmat']
        d['cq'] = _dot_nt(d['qb'], d['c_old'].astype(BF16))
    for d in hs:
        num = d['a'] * d['cq'] + _dot(d['s'].astype(BF16), d['vb'])
        den = d['a'] * jnp.sum(d['qh'] * d['n_old'], axis=1, keepdims=True) + jnp.sum(d['s'], axis=1, keepdims=True)
        d['hh'] = num / jnp.maximum(jnp.abs(den), jnp.exp(-d['mt']))
    for h, d in enumerate(hs):
        m_l = d['mt'][L - 1:L, :]
        b_l = d['b_col'][L - 1:L, :]
        g_l = jnp.exp(b_l - d['b_col'] + d['ic_col'] - m_l)
        a_l = jnp.exp(b_l + d['m_prev'] - m_l)
        c_scr[h] = a_l * d['c_old'] + _dot_tn((g_l * d['vh']).astype(BF16), d['kb'])
        n_scr[h:h + 1, :] = a_l * d['n_old'] + jnp.sum(g_l * d['kh'], axis=0, keepdims=True)
        m_scr[h:h + 1, :] = jnp.broadcast_to(m_l, (1, LANES))
    for h, d in enumerate(hs):
        hh = d['hh']
        mu = jnp.mean(hh, axis=1, keepdims=True)
        hc = hh - mu
        var = jnp.mean(hc * hc, axis=1, keepdims=True)
        hn = hc * lax.rsqrt(var + 1e-6) * g_ref[:, h * M_DV:(h + 1) * M_DV]
        h_ref[:, h * M_DV:(h + 1) * M_DV] = hn * _sigmoid(o_ref[:, h * M_DV:(h + 1) * M_DV])

    @pl.when(ci == pl.num_programs(1) - 1)
    def _():
        c1_ref[...] = c_scr[...]
        n1_ref[...] = n_scr[...]
        m1_ref[...] = m_scr[...]


def _mlstm(mq, mk, mv, mo, small, bias_row, norm_g, c0, n0, m0, B, T):
    L = M_CHUNK if T % M_CHUNK == 0 else T
    nch = T // L
    row = lambda b, c: (b * nch + c, 0)
    st4 = lambda b, c: (b, 0, 0, 0)
    st3 = lambda b, c: (b, 0, 0)
    m0b = jnp.broadcast_to(m0[:, :, None], (B, M_HEADS, LANES)).astype(F32)
    h, c1, n1, m1 = pl.pallas_call(
        functools.partial(_mlstm_kernel, L=L),
        out_shape=[jax.ShapeDtypeStruct((B * T, MIX), F32),
                   jax.ShapeDtypeStruct((B, M_HEADS, M_DV, M_DK), F32),
                   jax.ShapeDtypeStruct((B, M_HEADS, M_DK), F32),
                   jax.ShapeDtypeStruct((B, M_HEADS, LANES), F32)],
        grid=(B, nch),
        in_specs=[pl.BlockSpec((L, MIX), row)] * 4
        + [pl.BlockSpec((L, LANES), row), pl.BlockSpec((1, LANES), lambda b, c: (0, 0)),
           pl.BlockSpec((1, MIX), lambda b, c: (0, 0)),
           pl.BlockSpec((None, M_HEADS, M_DV, M_DK), st4), pl.BlockSpec((None, M_HEADS, M_DK), st3),
           pl.BlockSpec((None, M_HEADS, LANES), st3)],
        out_specs=[pl.BlockSpec((L, MIX), row), pl.BlockSpec((None, M_HEADS, M_DV, M_DK), st4),
                   pl.BlockSpec((None, M_HEADS, M_DK), st3), pl.BlockSpec((None, M_HEADS, LANES), st3)],
        scratch_shapes=[pltpu.VMEM((M_HEADS, M_DV, M_DK), F32), pltpu.VMEM((M_HEADS, M_DK), F32),
                        pltpu.VMEM((M_HEADS, LANES), F32)],
        compiler_params=_cp("parallel", "arbitrary"),
    )(mq, mk, mv, mo, small, bias_row, norm_g.reshape(1, MIX), c0.astype(F32), n0.astype(F32), m0b)
    return h, c1, n1, m1[:, :, 0]


def _compress_weights(pe, w1, w2):
    eye = jnp.eye(NSA_GROUPS, dtype=F32)
    w1j = jnp.einsum('jdo,gh->jgdho', w1.reshape(CMP_BLOCK, NSA_DH, NSA_DH), eye)
    w1j = w1j.reshape(CMP_BLOCK, LANES, LANES).astype(BF16)
    w2b = jnp.einsum('do,gh->gdho', w2, eye).reshape(LANES, LANES).astype(BF16)
    pe2 = jnp.tile(pe, (1, NSA_GROUPS))
    return pe2, w1j, w2b


def _compress_kernel(x_ref, pe_ref, w1_ref, w2_ref, o_ref):
    xb = (x_ref[...] + pe_ref[...]).astype(BF16)
    hid = jax.nn.gelu(_dot(xb, w1_ref[...]))
    o_ref[...] = _dot(hid.astype(BF16), w2_ref[...])


def _compress(blocks, pe, w1, w2):
    r = blocks.shape[0]
    tb = min(256, r)
    width = CMP_BLOCK * LANES
    pe2, w1j, w2b = _compress_weights(pe, w1, w2)
    return pl.pallas_call(
        _compress_kernel,
        out_shape=jax.ShapeDtypeStruct((r, LANES), F32),
        grid=(pl.cdiv(r, tb),),
        in_specs=[pl.BlockSpec((tb, width), lambda i: (i, 0)), pl.BlockSpec((1, width), lambda i: (0, 0)),
                  pl.BlockSpec((width, LANES), lambda i: (0, 0)), pl.BlockSpec((LANES, LANES), lambda i: (0, 0))],
        out_specs=pl.BlockSpec((tb, LANES), lambda i: (i, 0)),
        compiler_params=_cp("parallel"),
    )(blocks, pe2.reshape(1, width), w1j.reshape(width, LANES), w2b)


def _compress_pool_kernel(xt_ref, pe_ref, w1_ref, w2_ref, o_ref, xs_scr, *, pb):
    per_page = PAGE // CMP_BLOCK
    for p in range(pb):
        xs_scr[p * PAGE:(p + 1) * PAGE, :] = xt_ref[p].T + pe_ref[...]
    acc = None
    for j in range(CMP_BLOCK):
        rows = xs_scr[pl.ds(j, per_page * pb, stride=CMP_BLOCK), :]
        t = _dot(rows.astype(BF16), w1_ref[j])
        acc = t if acc is None else acc + t
    o_ref[...] = _dot(jax.nn.gelu(acc).astype(BF16), w2_ref[...])


def _compress_pool(pool_t, pe, w1, w2):
    npool = pool_t.shape[0]
    per_page = PAGE // CMP_BLOCK
    pb = _largest_divisor(npool, (16, 8, 4, 2))
    pe2, w1j, w2b = _compress_weights(pe, w1, w2)
    pe_tile = jnp.tile(pe2, (per_page, 1))
    return pl.pallas_call(
        functools.partial(_compress_pool_kernel, pb=pb),
        out_shape=jax.ShapeDtypeStruct((npool * per_page, LANES), F32),
        grid=(npool // pb,),
        in_specs=[pl.BlockSpec((pb, LANES, PAGE), lambda i: (i, 0, 0)), pl.BlockSpec((PAGE, LANES), lambda i: (0, 0)),
                  pl.BlockSpec((CMP_BLOCK, LANES, LANES), lambda i: (0, 0, 0)),
                  pl.BlockSpec((LANES, LANES), lambda i: (0, 0))],
        out_specs=pl.BlockSpec((per_page * pb, LANES), lambda i: (i, 0)),
        scratch_shapes=[pltpu.VMEM((pb * PAGE, LANES), F32)],
        compiler_params=_cp("parallel"),
    )(pool_t, pe_tile, w1j, w2b)


def _rope_rows_kernel(x_ref, cos_ref, sin_ref, o_ref):
    o_ref[...] = _rope128(x_ref[...], cos_ref[...], sin_ref[...])


def _rope_rows(x, cos, sin):
    r = x.shape[0]
    tb = min(512, r)
    spec = pl.BlockSpec((tb, LANES), lambda i: (i, 0))
    return pl.pallas_call(_rope_rows_kernel, out_shape=jax.ShapeDtypeStruct((r, LANES), F32), grid=(r // tb,),
                          in_specs=[spec] * 3, out_specs=spec, compiler_params=_cp("parallel"))(x, cos, sin)


def _stack_heads(qblk, heads, scale):
    rows = []
    for src_blk, src_half, dst_half in heads:
        x = qblk[:, src_blk * LANES:(src_blk + 1) * LANES] * scale
        if src_half != dst_half:
            x = pltpu.roll(x, 64, 1)
        lane = _iota(x.shape, 1)
        keep = (lane >= 64) if dst_half else (lane < 64)
        rows.append(jnp.where(keep, x, 0.0))
    return jnp.concatenate(rows, axis=0)


def _unstack_heads(o_full_list, heads_per_kb, tq, n_out_blocks):
    outs = [None] * n_out_blocks
    for o_full, heads in zip(o_full_list, heads_per_kb):
        for idx, (src_blk, src_half, dst_half) in enumerate(heads):
            piece = o_full[idx * tq:(idx + 1) * tq]
            if src_half != dst_half:
                piece = pltpu.roll(piece, 64, 1)
            lane = _iota(piece.shape, 1)
            keep = (lane >= 64) if src_half else (lane < 64)
            prev = outs[src_blk]
            outs[src_blk] = jnp.where(keep, piece, 0.0 if prev is None else prev)
    return outs


def _lane_repeat(x, width):
    if width == LANES:
        return x
    if width % LANES == 0:
        return pltpu.repeat(x, width // LANES, axis=1)
    return jnp.broadcast_to(x[:, 0:1], (x.shape[0], width))


def _stack_wide(qblk, n_heads, scale):
    head_of_lane = _iota(qblk.shape, 1) >> 6
    return jnp.concatenate([jnp.where(head_of_lane == h, qblk * scale, 0.0) for h in range(n_heads)], axis=0)


def _unstack_wide(o_full, n_heads, tq):
    head_of_lane = _iota((tq, o_full.shape[1]), 1) >> 6
    out = jnp.zeros((tq, o_full.shape[1]), F32)
    for h in range(n_heads):
        out = jnp.where(head_of_lane == h, o_full[h * tq:(h + 1) * tq], out)
    return out


NSA_HEAD_MAP = [[(h // 2, h % 2, h // (NSA_HEADS // NSA_GROUPS)) for h in range(NSA_HEADS)]]
FOX_HEAD_MAP = [[(kb, 0, 0), (kb, 1, 1)] for kb in range(FOX_HEADS // 2)]


def _cmp_sel_kernel(q_ref, kc_ref, vc_ref, cend_ref, o_ref, sel_ref, *, tq, q0, nc, ns, ns_pad):
    qi = pl.program_id(1)
    heads = NSA_HEAD_MAP[0]
    hpg = NSA_HEADS // NSA_GROUPS
    qbd = _stack_heads(q_ref[...], heads, NSA_DH ** -0.5).astype(BF16)
    s = _dot_nt(qbd, kc_ref[...].astype(BF16))
    rows = NSA_HEADS * tq
    qpos = q0 + qi * tq + (_iota((rows, 1), 0) & (tq - 1))
    mask = cend_ref[...] <= qpos
    s = jnp.where(mask, s, NEG)
    e = jnp.where(mask, jnp.exp(s - jnp.max(s, axis=1, keepdims=True)), 0.0)
    p = e / jnp.maximum(jnp.sum(e, axis=1, keepdims=True), 1e-30)
    o_full = _dot(p.astype(BF16), vc_ref[...].astype(BF16))
    outs = _unstack_heads([o_full], NSA_HEAD_MAP, tq, MIX // LANES)
    for c, blk in enumerate(outs):
        o_ref[:, c * LANES:(c + 1) * LANES] = blk
    half = nc // 2
    qp = q0 + qi * tq + _iota((tq, 1), 0)
    blk_i = _iota((tq, ns_pad), 1)
    blk_f = blk_i.astype(F32)
    cur = qp >> 6
    forced = (blk_i == 0) | (blk_i == cur) | (blk_i == cur - 1)
    valid = (blk_i * SEL_BLOCK <= qp) & (blk_i < ns)
    n_sel = min(N_SEL, ns)
    scores, chosen = [], []
    for g in range(NSA_GROUPS):
        pg = p[g * hpg * tq:(g * hpg + 1) * tq]
        for j in range(1, hpg):
            pg = pg + p[(g * hpg + j) * tq:(g * hpg + j + 1) * tq]
        imp = pg[:, :half] + pg[:, half:]
        if ns_pad > half:
            imp = jnp.concatenate([imp, jnp.zeros((tq, ns_pad - half), F32)], axis=1)
        score = jnp.where(valid, imp + FORCE * forced.astype(F32), NEG)
        scores.append(jnp.where(blk_i < ns, score, -jnp.inf))
        chosen.append(jnp.zeros((tq, ns_pad), F32))
    for _ in range(n_sel):
        for g in range(NSA_GROUPS):
            mx = jnp.max(scores[g], axis=1, keepdims=True)
            first = jnp.min(jnp.where(scores[g] == mx, blk_f, 1e9), axis=1, keepdims=True)
            hit = blk_f == first
            chosen[g] = jnp.where(hit, 1.0, chosen[g])
            scores[g] = jnp.where(hit, -jnp.inf, scores[g])
    for g in range(NSA_GROUPS):
        sel_ref[:, g * ns_pad:(g + 1) * ns_pad] = jnp.where(valid, chosen[g], 0.0)


def _cmp_sel(q3, kcmp, vcmp, cend, q0, ns, ns_pad):
    B, T, _ = q3.shape
    nc = kcmp.shape[1]
    tq = min(TQ_CMP, T)
    return pl.pallas_call(
        functools.partial(_cmp_sel_kernel, tq=tq, q0=q0, nc=nc, ns=ns, ns_pad=ns_pad),
        out_shape=[jax.ShapeDtypeStruct((B, T, MIX), F32), jax.ShapeDtypeStruct((B, T, NSA_GROUPS * ns_pad), F32)],
        grid=(B, T // tq),
        in_specs=[pl.BlockSpec((None, tq, MIX), lambda b, i: (b, i, 0)),
                  pl.BlockSpec((None, nc, LANES), lambda b, i: (b, 0, 0)),
                  pl.BlockSpec((None, nc, LANES), lambda b, i: (b, 0, 0)),
                  pl.BlockSpec((1, nc), lambda b, i: (0, 0))],
        out_specs=[pl.BlockSpec((None, tq, MIX), lambda b, i: (b, i, 0)),
                   pl.BlockSpec((None, tq, NSA_GROUPS * ns_pad), lambda b, i: (b, i, 0))],
        compiler_params=_cp("parallel", "arbitrary"),
    )(q3, kcmp, vcmp, cend)


def _flash_kernel(*refs, cfg):
    mode, layout, tq, tk = cfg['mode'], cfg['layout'], cfg['tq'], cfg['tk']
    q0, kbase, pps, has_tail, tail_pos0 = cfg['q0'], cfg['kbase'], cfg['pps'], cfg['has_tail'], cfg['tail_pos0']
    ns_pad, n_pref, qw, kv_t = cfg['ns_pad'], cfg['n_pref'], cfg['qw'], cfg['kv_t']
    head_map = NSA_HEAD_MAP if layout == 'nsa' else FOX_HEAD_MAP
    nkb = 1 if layout in ('nsa', 'fox_wide') else FOX_HEADS // 2
    n_heads_kb = FOX_HEADS if layout == 'fox_wide' else len(head_map[0])
    rows = n_heads_kb * tq
    qi_ref, kj_ref, fl_ref = refs[:3]
    pos = n_pref
    q_ref = refs[pos]; pos += 1
    npg = max(pps, 1)
    k_refs = refs[pos:pos + npg]; pos += npg
    v_refs = refs[pos:pos + npg]; pos += npg
    if has_tail:
        tk_ref, tv_ref = refs[pos:pos + 2]; pos += 2
    c_ref = ct_ref = sel_ref = None
    if cfg['has_caug'] or cfg['has_crow']:
        c_ref = refs[pos]; pos += 1
        if has_tail:
            ct_ref = refs[pos]; pos += 1
    if mode == 'sel':
        sel_ref = refs[pos]; pos += 1
    o_ref = refs[pos]; pos += 1
    qst_scr, m_scr, l_scr, acc_scr = refs[pos:pos + 4]

    s_id = pl.program_id(1)
    qi = qi_ref[s_id]
    kj = kj_ref[s_id]
    fl = fl_ref[s_id]
    scale = NSA_DH ** -0.5 * LOG2E

    @pl.when((fl & 1) != 0)
    def _init():
        qblk = q_ref[...]
        if layout == 'fox_wide':
            qst_scr[0] = _stack_wide(qblk, FOX_HEADS, scale).astype(BF16)
        elif layout == 'nsa':
            qbd = _stack_heads(qblk, head_map[0], scale).astype(BF16)
            if mode == 'sel':
                selblk = sel_ref[...]
                hpg = NSA_HEADS // NSA_GROUPS
                parts = []
                for g in range(NSA_GROUPS):
                    off = ((selblk[:, g * ns_pad:(g + 1) * ns_pad] - 1.0) * SEL_OFF).astype(BF16)
                    parts += [off] * hpg
                qbd = jnp.concatenate([qbd, jnp.concatenate(parts, axis=0)], axis=1)
            qst_scr[0] = qbd
        else:
            for kb in range(nkb):
                qbd = _stack_heads(qblk, head_map[kb], scale).astype(BF16)
                if cfg['has_caug']:
                    lane = _iota((rows, LANES), 1)
                    head = 2 * kb + (_iota((rows, LANES), 0) >= tq).astype(jnp.int32)
                    aug = jnp.where((lane >= 3 * head) & (lane < 3 * head + 3), -1.0, 0.0).astype(BF16)
                    qbd = jnp.concatenate([qbd, aug], axis=1)
                qst_scr[kb] = qbd
        m_scr[...] = jnp.full(m_scr.shape, NEG, F32)
        l_scr[...] = jnp.zeros(l_scr.shape, F32)
        acc_scr[...] = jnp.zeros(acc_scr.shape, F32)

    def update(k_tile, v_tile, kpos0, c_tile, width, masked):
        if masked:
            qpos = q0 + qi * tq + (_iota((rows, 1), 0) & (tq - 1))
            kpos = kpos0 + _iota((1, width), 1)
            mask = kpos <= qpos
            if mode == 'band':
                mask = mask & (kpos > qpos - WINDOW)
        for kb in range(nkb):
            if kv_t:
                kt = k_tile.astype(BF16)
                if mode == 'sel':
                    onehot = (_iota((ns_pad, width), 0) == (kpos0 >> 6) + (_iota((ns_pad, width), 1) >> 6))
                    kt = jnp.concatenate([kt, onehot.astype(BF16)], axis=0)
                sc = _dot(qst_scr[kb], kt)
                if c_tile is not None:
                    sc = sc - LOG2E * jnp.concatenate([jnp.broadcast_to(c_tile[h:h + 1, :], (tq, width))
                                                       for h in range(n_heads_kb)], axis=0)
            else:
                kt = k_tile[:, kb * LANES:(kb + 1) * LANES].astype(BF16)
                if mode == 'sel':
                    onehot = (_iota((width, ns_pad), 1) == (kpos0 >> 6) + (_iota((width, ns_pad), 0) >> 6))
                    kt = jnp.concatenate([kt, onehot.astype(BF16)], axis=1)
                if c_tile is not None:
                    kt = jnp.concatenate([kt, c_tile], axis=1)
                sc = _dot_nt(qst_scr[kb], kt)
            if masked:
                sc = jnp.where(mask, sc, NEG)
            m_old = m_scr[kb]
            m_new = jnp.maximum(m_old, jnp.max(sc, axis=1, keepdims=True))
            alpha = jnp.exp2(m_old - m_new)
            p = jnp.exp2(sc - _lane_repeat(m_new, width))
            l_scr[kb] = alpha * l_scr[kb] + jnp.sum(p, axis=1, keepdims=True)
            if kv_t:
                pv = _dot_nt(p.astype(BF16), v_tile.astype(BF16))
            else:
                pv = _dot(p.astype(BF16), v_tile[:, kb * LANES:(kb + 1) * LANES].astype(BF16))
            acc_scr[kb] = _lane_repeat(alpha, pv.shape[1]) * acc_scr[kb] + pv
            m_scr[kb] = m_new

    def main_update(masked):
        if pps > 1:
            k_tile = jnp.concatenate([r[...] for r in k_refs], axis=1)
            v_tile = jnp.concatenate([r[...] for r in v_refs], axis=1)
        else:
            k_tile, v_tile = k_refs[0][...], v_refs[0][...]
        update(k_tile, v_tile, kbase + kj * tk, c_ref[...] if c_ref is not None else None, tk, masked)

    is_tail = (fl & 4) != 0
    need_mask = (fl & 8) != 0
    if has_tail:
        @pl.when(is_tail)
        def _():
            update(tk_ref[...], tv_ref[...], tail_pos0, ct_ref[...] if ct_ref is not None else None, PAGE, True)

    @pl.when(jnp.logical_and(jnp.logical_not(is_tail), need_mask))
    def _():
        main_update(True)

    @pl.when(jnp.logical_and(jnp.logical_not(is_tail), jnp.logical_not(need_mask)))
    def _():
        main_update(False)

    @pl.when((fl & 2) != 0)
    def _fin():
        fulls = [acc_scr[kb] / _lane_repeat(jnp.maximum(l_scr[kb], 1e-30), acc_scr.shape[2]) for kb in range(nkb)]
        if layout == 'fox_wide':
            o_ref[...] = _unstack_wide(fulls[0], FOX_HEADS, tq)
        else:
            outs = _unstack_heads(fulls, head_map, tq, qw // LANES)
            for c, blk in enumerate(outs):
                o_ref[:, c * LANES:(c + 1) * LANES] = blk


def _schedule(nq, tq, tk, q0, kbase, n_ktiles, mode, has_tail):
    qi_l, kj_l, fl_l = [], [], []
    for qi in range(nq):
        q_lo, q_hi = q0 + qi * tq, q0 + (qi + 1) * tq - 1
        entries = []
        for kj in range(n_ktiles):
            k_lo, k_hi = kbase + kj * tk, kbase + (kj + 1) * tk - 1
            if k_lo > q_hi:
                continue
            if mode == 'band' and k_hi <= q_lo - WINDOW:
                continue
            all_visible = k_hi <= q_lo and (mode != 'band' or k_lo > q_hi - WINDOW)
            entries.append((kj, 0 if all_visible else 8))
        if has_tail:
            entries.append((entries[-1][0] if entries else 0, 4 | 8))
        for n, (kj, f) in enumerate(entries):
            f |= (1 if n == 0 else 0) | (2 if n == len(entries) - 1 else 0)
            qi_l.append(qi); kj_l.append(kj); fl_l.append(f)
    return (np.asarray(qi_l, np.int32), np.asarray(kj_l, np.int32), np.asarray(fl_l, np.int32))


def _flash(q3, k3, v3, *, mode, layout, tq, tk, q0, kbase=0, page_table=None, pps=0, tail=None, tail_pos0=0,
           caug=None, crow=None, crow_tail=None, sel=None, ns_pad=0):
    B, Tq, qw = q3.shape
    nq = Tq // tq
    paged = page_table is not None
    has_tail = tail is not None
    if paged:
        kw = k3.shape[1]
        npages = page_table.shape[1]
        tk = pps * PAGE
        n_ktiles = npages // pps
    else:
        kw = k3.shape[-1]
        n_ktiles = k3.shape[1] // tk
    qi_t, kj_t, fl_t = _schedule(nq, tq, tk, q0, kbase, n_ktiles, mode, has_tail)
    nsteps = len(qi_t)
    prefetch = [jnp.asarray(qi_t), jnp.asarray(kj_t), jnp.asarray(fl_t)]
    if paged:
        prefetch.append(page_table.reshape(-1).astype(jnp.int32))
    n_pref = len(prefetch)
    nkb = 1 if layout in ('nsa', 'fox_wide') else FOX_HEADS // 2
    n_heads_kb = FOX_HEADS if layout == 'fox_wide' else (NSA_HEADS if layout == 'nsa' else 2)
    rows = n_heads_kb * tq
    cq = kw if layout == 'fox_wide' else LANES
    if mode == 'sel':
        cq += ns_pad
    if caug is not None:
        cq += LANES
    cv = kw if layout == 'fox_wide' else LANES

    in_specs = [pl.BlockSpec((None, tq, qw), lambda b, s, qi, kj, fl, *_: (b, qi[s], 0))]
    args = [q3]
    if paged:
        def page_map(i):
            return lambda b, s, qi, kj, fl, pt: (pt[b * npages + kj[s] * pps + i], 0, 0)
        for arr in (k3, v3):
            for i in range(pps):
                in_specs.append(pl.BlockSpec((None, kw, PAGE), page_map(i)))
                args.append(arr)
    else:
        for arr in (k3, v3):
            in_specs.append(pl.BlockSpec((None, tk, kw), lambda b, s, qi, kj, fl, *_: (b, kj[s], 0)))
            args.append(arr)
    if has_tail:
        for arr in tail:
            in_specs.append(pl.BlockSpec((None, kw, PAGE), lambda b, s, qi, kj, fl, *_: (b, 0, 0)))
            args.append(arr)
    if caug is not None:
        in_specs.append(pl.BlockSpec((None, tk, LANES), lambda b, s, qi, kj, fl, *_: (b, kj[s], 0)))
        args.append(caug)
    if crow is not None:
        in_specs.append(pl.BlockSpec((None, 8, tk), lambda b, s, qi, kj, fl, *_: (b, 0, kj[s])))
        args.append(crow)
        if has_tail:
            in_specs.append(pl.BlockSpec((None, 8, PAGE), lambda b, s, qi, kj, fl, *_: (b, 0, 0)))
            args.append(crow_tail)
    if mode == 'sel':
        in_specs.append(pl.BlockSpec((None, tq, NSA_GROUPS * ns_pad), lambda b, s, qi, kj, fl, *_: (b, qi[s], 0)))
        args.append(sel)
    cfg = dict(mode=mode, layout=layout, tq=tq, tk=tk, q0=q0, kbase=kbase, pps=pps if paged else 0,
               has_tail=has_tail, tail_pos0=tail_pos0, ns_pad=ns_pad, n_pref=n_pref, qw=qw, kv_t=paged,
               has_caug=caug is not None, has_crow=crow is not None)
    return pl.pallas_call(
        functools.partial(_flash_kernel, cfg=cfg),
        out_shape=jax.ShapeDtypeStruct((B, Tq, qw), F32),
        grid_spec=pltpu.PrefetchScalarGridSpec(
            num_scalar_prefetch=n_pref,
            grid=(B, nsteps),
            in_specs=in_specs,
            out_specs=pl.BlockSpec((None, tq, qw), lambda b, s, qi, kj, fl, *_: (b, qi[s], 0)),
            scratch_shapes=[pltpu.VMEM((nkb, rows, cq), BF16), pltpu.VMEM((nkb, rows, LANES), F32),
                            pltpu.VMEM((nkb, rows, LANES), F32), pltpu.VMEM((nkb, rows, cv), F32)]),
        compiler_params=_cp("parallel", "arbitrary"),
    )(*prefetch, *args)


def _nsa_combine_kernel(oc_ref, os_ref, ow_ref, sm_ref, e_ref, o_ref):
    gate = _sigmoid(sm_ref[...])
    acc = None
    for j, br in enumerate((oc_ref, os_ref, ow_ref)):
        gexp = _dot_r01(gate, e_ref[j])
        acc = gexp * br[...] if acc is None else acc + gexp * br[...]
    o_ref[...] = acc


def _nsa_combine(o_cmp, o_slc, o_win, small):
    n = o_cmp.shape[0]
    tm = min(512, n)
    src = np.zeros((3, LANES, MIX), np.float32)
    for h in range(NSA_HEADS):
        for j in range(3):
            src[j, 2 * M_HEADS + 3 * h + j, h * NSA_DH:(h + 1) * NSA_DH] = 1.0
    spec = pl.BlockSpec((tm, MIX), lambda i: (i, 0))
    return pl.pallas_call(
        _nsa_combine_kernel, out_shape=jax.ShapeDtypeStruct((n, MIX), F32), grid=(n // tm,),
        in_specs=[spec, spec, spec, pl.BlockSpec((tm, LANES), lambda i: (i, 0)),
                  pl.BlockSpec((3, LANES, MIX), lambda i: (0, 0, 0))],
        out_specs=spec, compiler_params=_cp("parallel"))(o_cmp, o_slc, o_win, small, jnp.asarray(src, BF16))


def _fox_cumsum_kernel(sm_ref, bias_ref, place_ref, lf_ref, caug_ref, carry, *, tt):
    @pl.when(pl.program_id(1) == 0)
    def _():
        carry[...] = jnp.zeros(carry.shape, F32)

    lf = _log_sigmoid(sm_ref[...] + bias_ref[...])
    lf_ref[...] = lf
    lower = (_iota((tt, tt), 1) <= _iota((tt, tt), 0)).astype(F32)
    c = _dot_l01(lower, lf) + carry[0:1, :]
    carry[...] = jnp.broadcast_to(c[tt - 1:tt, :], carry.shape)
    c = c * LOG2E
    c_hi = c.astype(BF16)
    r1 = c - c_hi.astype(F32)
    c_mid = r1.astype(BF16)
    c_lo = (r1 - c_mid.astype(F32)).astype(BF16)
    caug = _dot(c_hi, place_ref[0]) + _dot(c_mid, place_ref[1]) + _dot(c_lo, place_ref[2])
    caug_ref[...] = caug.astype(BF16)


def _fox_cumsum(small, bias_row, B, T):
    tt = min(256, T)
    row = lambda b, i: (b * (T // tt) + i, 0)
    place = np.zeros((3, LANES, LANES), np.float32)
    for h in range(FOX_HEADS):
        for j in range(3):
            place[j, h, 3 * h + j] = 1.0
    return pl.pallas_call(
        functools.partial(_fox_cumsum_kernel, tt=tt),
        out_shape=[jax.ShapeDtypeStruct((B * T, LANES), F32), jax.ShapeDtypeStruct((B * T, LANES), BF16)],
        grid=(B, T // tt),
        in_specs=[pl.BlockSpec((tt, LANES), row), pl.BlockSpec((1, LANES), lambda b, i: (0, 0)),
                  pl.BlockSpec((3, LANES, LANES), lambda b, i: (0, 0, 0))],
        out_specs=[pl.BlockSpec((tt, LANES), row), pl.BlockSpec((tt, LANES), row)],
        scratch_shapes=[pltpu.VMEM((8, LANES), F32)],
        compiler_params=_cp("parallel", "arbitrary"),
    )(small, bias_row, jnp.asarray(place, BF16))


def _fox_cumsum_paged_kernel(pt_ref, *refs, pps, n_main):
    lf_refs = refs[:pps]
    sm_ref, bias_ref, c_ref, lfn_ref, carry = refs[pps:]
    j = pl.program_id(1)

    @pl.when(j == 0)
    def _():
        carry[...] = jnp.zeros(carry.shape, F32)

    upper = (_iota((PAGE, PAGE), 0) <= _iota((PAGE, PAGE), 1)).astype(F32)

    @pl.when(j < n_main)
    def _():
        run = carry[:, 0:1]
        for i in range(pps):
            c = _dot_r01(lf_refs[i][...], upper) + run
            c_ref[:, i * PAGE:(i + 1) * PAGE] = c
            run = c[:, PAGE - 1:PAGE]
        carry[...] = jnp.broadcast_to(run, carry.shape)

    @pl.when(j == n_main)
    def _():
        lf = _log_sigmoid(sm_ref[...] + bias_ref[...])
        lfn_ref[...] = lf
        eye8 = (_iota((8, LANES), 0) == _iota((8, LANES), 1)).astype(F32)
        c_new = _dot_r01(_dot_nt_l01(eye8, lf), upper) + carry[:, 0:1]
        for i in range(pps):
            c_ref[:, i * PAGE:(i + 1) * PAGE] = c_new


def _fox_cumsum_paged(lf_pool_t, page_table, small_pad, bias_row, pps):
    B, npages = page_table.shape
    n_main = npages // pps

    def page_map(i):
        return lambda b, j, pt: (pt[b * npages + jnp.minimum(j, n_main - 1) * pps + i], 0, 0)

    return pl.pallas_call(
        functools.partial(_fox_cumsum_paged_kernel, pps=pps, n_main=n_main),
        out_shape=[jax.ShapeDtypeStruct((B, 8, (n_main + 1) * pps * PAGE), F32),
                   jax.ShapeDtypeStruct((B, PAGE, LANES), F32)],
        grid_spec=pltpu.PrefetchScalarGridSpec(
            num_scalar_prefetch=1, grid=(B, n_main + 1),
            in_specs=[pl.BlockSpec((None, 8, PAGE), page_map(i)) for i in range(pps)]
            + [pl.BlockSpec((None, PAGE, LANES), lambda b, j, pt: (b, 0, 0)),
               pl.BlockSpec((1, LANES), lambda b, j, pt: (0, 0))],
            out_specs=[pl.BlockSpec((None, 8, pps * PAGE), lambda b, j, pt: (b, 0, j)),
                       pl.BlockSpec((None, PAGE, LANES), lambda b, j, pt: (b, 0, 0))],
            scratch_shapes=[pltpu.VMEM((8, LANES), F32)]),
        compiler_params=_cp("parallel", "arbitrary"),
    )(page_table.reshape(-1).astype(jnp.int32), *([lf_pool_t] * pps), small_pad, bias_row)


def _rwkv_prep_kernel(rw_ref, prev_ref, sh_ref, mu_ref, w0_ref, w2_ref, a0_ref, a2_ref, g2_ref, kk_ref, ka_ref,
                      bd_ref, r_o, lw_o, k_o, v_o, kk_o, a_o, g_o, *, tt):
    i = pl.program_id(1)
    rw = rw_ref[...]
    first_prev = jnp.where(i == 0, sh_ref[...], prev_ref[7:8, :])
    prev = jnp.where(_iota(rw.shape, 0) == 0, first_prev, pltpu.roll(rw, 1, 0))
    m = rw + mu_ref[...] * (prev - rw)
    r, kr, vr = m[:, 0:MIX], m[:, MIX:2 * MIX], m[:, 2 * MIX:3 * MIX]
    lora = m[:, 3 * MIX:3 * MIX + LANES]
    gl = m[:, 3 * MIX + LANES:3 * MIX + 2 * LANES]
    w = -_softplus(-(w0_ref[...] + _dot(jnp.tanh(lora).astype(BF16), w2_ref[...]))) - 0.5
    a = _sigmoid(a0_ref[...] + _dot(lora.astype(BF16), a2_ref[...]))
    kk = kr * kk_ref[...]
    ss = _dot_r01(kk * kk, bd_ref[...])
    r_o[...] = r
    lw_o[...] = -jnp.exp(w)
    k_o[...] = kr * (1.0 + (a - 1.0) * ka_ref[...])
    v_o[...] = vr
    kk_o[...] = kk / jnp.maximum(jnp.sqrt(ss), 1e-12)
    a_o[...] = a
    g_o[...] = _dot(_sigmoid(gl).astype(BF16), g2_ref[...])


def _rwkv_prep(rw3, shift0, P):
    B, T, wdt = rw3.shape
    tt = min(256, T)
    nblk8 = tt // 8
    row3 = lambda b, i: (b, i, 0)
    vec = lambda n: pl.BlockSpec((1, n), lambda b, i: (0, 0))
    outs = pl.pallas_call(
        functools.partial(_rwkv_prep_kernel, tt=tt),
        out_shape=[jax.ShapeDtypeStruct((B, T, MIX), F32)] * 7,
        grid=(B, T // tt),
        in_specs=[pl.BlockSpec((None, tt, wdt), row3),
                  pl.BlockSpec((None, 8, wdt), lambda b, i: (b, jnp.maximum(i * nblk8 - 1, 0), 0)),
                  pl.BlockSpec((None, 1, wdt), lambda b, i: (b, 0, 0)),
                  vec(wdt), vec(MIX), pl.BlockSpec((LANES, MIX), lambda b, i: (0, 0)),
                  vec(MIX), pl.BlockSpec((LANES, MIX), lambda b, i: (0, 0)),
                  pl.BlockSpec((LANES, MIX), lambda b, i: (0, 0)), vec(MIX), vec(MIX),
                  pl.BlockSpec((MIX, MIX), lambda b, i: (0, 0))],
        out_specs=[pl.BlockSpec((None, tt, MIX), row3)] * 7,
        compiler_params=_cp("parallel", "arbitrary"),
    )(rw3, rw3, shift0[:, None, :], P['mu'], P['w0'], P['w2'], P['a0'], P['a2'], P['g2'], P['k_k'], P['k_a'], P['bd64'])
    return outs


def _rwkv_scan_kernel(r_ref, lw_ref, k_ref, v_ref, kk_ref, a_ref, g_ref, s0_ref, rk_ref, lng_ref, lnb_ref,
                      y_ref, s1_ref, s_scr, *, L):
    ci = pl.program_id(1)

    @pl.when(ci == 0)
    def _():
        s_scr[...] = s0_ref[...]

    r_i = _iota((L, L), 0)
    c_i = _iota((L, L), 1)
    lower = (c_i <= r_i).astype(F32)
    strict = c_i < r_i
    incl = c_i <= r_i
    eye_l = (c_i == r_i).astype(F32)
    lane = _iota((L, LANES), 1)
    half_masks = (lane < 64, lane >= 64)
    rr = _iota((LANES, LANES), 0)
    cc = _iota((LANES, LANES), 1)
    bdmask = (rr >= 64) == (cc >= 64)
    eye128 = rr == cc
    bd_f = bdmask.astype(F32)
    pairs = range(RWKV_HEADS // 2)
    sls = [slice(pr * LANES, (pr + 1) * LANES) for pr in pairs]
    P = []
    for sl in sls:
        lw = lw_ref[:, sl]
        cl = _dot_l01(lower, lw)
        w_in = jnp.exp(cl)
        w_out = jnp.exp(-cl)
        kk = kk_ref[:, sl]
        P.append(dict(w_in=w_in, kap=kk * jnp.exp(cl - lw), bet=kk * a_ref[:, sl] * w_out,
                      kt=k_ref[:, sl] * w_out, rt=r_ref[:, sl] * w_in, v=v_ref[:, sl]))
    for pr, d in zip(pairs, P):
        d['sbd'] = s_scr[pr]
        d['rhs'] = _bdot(d['kap'], d['sbd'])
        d['ys'] = _bdot(d['rt'], d['sbd'])
    H = []
    for d in P:
        for hm in half_masks:
            kap_h = jnp.where(hm, d['kap'], 0.0)
            rt_h = jnp.where(hm, d['rt'], 0.0)
            amat = jnp.where(strict, _bdot_nt(kap_h, d['bet']), 0.0)
            H.append(dict(d=d, hm=hm, pw=amat, tinv=eye_l - amat,
                          cmat=jnp.where(strict, _bdot_nt(kap_h, d['kt']), 0.0),
                          rb=jnp.where(incl, _bdot_nt(rt_h, d['bet']), 0.0),
                          rk=jnp.where(incl, _bdot_nt(rt_h, d['kt']), 0.0)))
    def same_block(shift):
        return (r_i >> shift) == (c_i >> shift)

    for h in H:
        h['amat'] = h['pw']
        h['pw'] = jnp.where(same_block(3), h['amat'], 0.0)
        h['tinv'] = eye_l - h['pw']
    for _ in range(2):
        for h in H:
            h['pw'] = _dot3(h['pw'], h['pw'])
        for h in H:
            h['tinv'] = h['tinv'] + _dot3(h['tinv'], h['pw'])
    shift = 3
    while (1 << shift) < L:
        off_diag = jnp.logical_and(same_block(shift + 1), jnp.logical_not(same_block(shift)))
        for h in H:
            h['x'] = _dot3(jnp.where(off_diag, h['amat'], 0.0), h['tinv'])
        for h in H:
            h['tinv'] = h['tinv'] - _dot3(h['tinv'], h['x'])
        shift += 1
    for h in H:
        d, hm = h['d'], h['hm']
        h['v_h'] = jnp.where(hm, d['v'], 0.0)
        h['rhs_h'] = jnp.where(hm, d['rhs'], 0.0) + _bdot(h['cmat'], h['v_h'])
    for h in H:
        h['u_h'] = -_dot3(h['tinv'], h['rhs_h'])
    for h in H:
        h['y_h'] = _bdot(h['rb'], h['u_h']) + _bdot(h['rk'], h['v_h'])
    for pr, (d, sl) in enumerate(zip(P, sls)):
        h0, h1 = H[2 * pr], H[2 * pr + 1]
        u = h0['u_h'] + h1['u_h']
        y = d['ys'] + h0['y_h'] + h1['y_h']
        upd = jnp.where(bdmask, _bdot_tn(d['bet'], u) + _bdot_tn(d['kt'], d['v']), 0.0)
        w_end = jnp.broadcast_to(d['w_in'][L - 1:L, :], (LANES, LANES))
        w_end_col = jnp.sum(jnp.where(eye128, w_end, 0.0), axis=1, keepdims=True)
        s_scr[pr] = (d['sbd'] + upd) * w_end_col
        d['y'] = y
    for d, sl in zip(P, sls):
        y = d['y']
        mean = _dot_r01(y, bd_f) * (1.0 / RWKV_N)
        yc = y - mean
        var = _dot_r01(yc * yc, bd_f) * (1.0 / RWKV_N)
        yn = yc * lax.rsqrt(var + 64e-5) * lng_ref[:, sl] + lnb_ref[:, sl]
        bonus = _dot_r01(r_ref[:, sl] * k_ref[:, sl] * rk_ref[:, sl], bd_f)
        y_ref[:, sl] = (yn + bonus * d['v']) * g_ref[:, sl]

    @pl.when(ci == pl.num_programs(1) - 1)
    def _():
        s1_ref[...] = s_scr[...]


def _rwkv_scan(r, lw, k, v, kk, a, g, s0, P, B, T):
    L = RWKV_CHUNK
    tp = -(-T // L) * L
    if tp != T:
        pad = lambda t: jnp.pad(t, ((0, 0), (0, tp - T), (0, 0)))
        r, lw, k, v, kk, a, g = (pad(t) for t in (r, lw, k, v, kk, a, g))
    nch = tp // L
    st = jnp.swapaxes(s0.astype(F32), -1, -2).reshape(B, RWKV_HEADS // 2, 2, RWKV_N, RWKV_N)
    z = jnp.zeros_like(st[:, :, 0])
    sbd0 = jnp.concatenate([jnp.concatenate([st[:, :, 0], z], -1), jnp.concatenate([z, st[:, :, 1]], -1)], -2)
    row3 = lambda b, c: (b, c, 0)
    st_spec = pl.BlockSpec((None, RWKV_HEADS // 2, LANES, LANES), lambda b, c: (b, 0, 0, 0))
    vec = pl.BlockSpec((1, MIX), lambda b, c: (0, 0))
    y, sbd1 = pl.pallas_call(
        functools.partial(_rwkv_scan_kernel, L=L),
        out_shape=[jax.ShapeDtypeStruct((B, tp, MIX), F32),
                   jax.ShapeDtypeStruct((B, RWKV_HEADS // 2, LANES, LANES), F32)],
        grid=(B, nch),
        in_specs=[pl.BlockSpec((None, L, MIX), row3)] * 7 + [st_spec, vec, vec, vec],
        out_specs=[pl.BlockSpec((None, L, MIX), row3), st_spec],
        scratch_shapes=[pltpu.VMEM((RWKV_HEADS // 2, LANES, LANES), F32)],
        compiler_params=_cp("parallel", "arbitrary"),
    )(r, lw, k, v, kk, a, g, sbd0, P['r_k'], P['ln_g'], P['ln_b'])
    s_e = sbd1[:, :, :RWKV_N, :RWKV_N]
    s_o = sbd1[:, :, RWKV_N:, RWKV_N:]
    s1 = jnp.swapaxes(jnp.stack([s_e, s_o], 2).reshape(B, RWKV_HEADS, RWKV_N, RWKV_N), -1, -2)
    return y[:, :T], s1


def _mem_kernel(x_ref, km_ref, vm_ref, wq_ref, wo_ref, g_ref, b_ref, o_ref):
    x = x_ref[...]
    q = _dot(x.astype(BF16), wq_ref[...])
    heads = []
    for h in range(MEM_HEADS):
        sl = slice(h * MEM_DH, (h + 1) * MEM_DH)
        s = _dot_nt((q[:, sl] * (MEM_DH ** -0.5)).astype(BF16), km_ref[:, sl].astype(BF16))
        e = jnp.exp(s - jnp.max(s, axis=1, keepdims=True))
        p = e / jnp.sum(e, axis=1, keepdims=True)
        heads.append(_dot(p.astype(BF16), vm_ref[:, sl].astype(BF16)))
    o = jnp.concatenate(heads, axis=1)
    y = _dot(o.astype(BF16), wo_ref[...])
    o_ref[...] = _layer_norm_rows(ALPHA * x + y, g_ref[...], b_ref[...])


def _mem_layer(x3, km, vm, wq, wo, g, b):
    B, T, d = x3.shape
    tq = min(512, T)
    mt = km.shape[1]
    full = lambda shape: pl.BlockSpec(shape, lambda bb, i: (0,) * len(shape))
    return pl.pallas_call(
        _mem_kernel, out_shape=jax.ShapeDtypeStruct((B, T, d), F32), grid=(B, T // tq),
        in_specs=[pl.BlockSpec((None, tq, d), lambda bb, i: (bb, i, 0)),
                  pl.BlockSpec((None, mt, d), lambda bb, i: (bb, 0, 0)),
                  pl.BlockSpec((None, mt, d), lambda bb, i: (bb, 0, 0)),
                  full((d, d)), full((d, d)), full((1, d)), full((1, d))],
        out_specs=pl.BlockSpec((None, tq, d), lambda bb, i: (bb, i, 0)),
        compiler_params=_cp("parallel", "arbitrary"),
    )(x3, km, vm, wq, wo, g.reshape(1, d), b.reshape(1, d))


def _moe_kernel(x_ref, wr_ref, br_ref, w1_ref, w3_ref, w2_ref, g_ref, b_ref, o_ref, xb_scr, gw_scr, acc_scr):
    e = pl.program_id(1)

    @pl.when(e == 0)
    def _route():
        x = x_ref[...]
        xb_scr[...] = x.astype(BF16)
        logit = _dot(x, wr_ref[...], HI) + br_ref[...]
        lane = _iota(logit.shape, 1)
        lane_f = lane.astype(F32)
        is_g = lane < N_GROUPS
        gl = jnp.where(is_g, logit, -jnp.inf)
        gmax = jnp.max(gl, axis=1, keepdims=True)
        grp = jnp.min(jnp.where(gl == gmax, lane_f, 1e9), axis=1, keepdims=True)
        g_w = 1.0 / jnp.sum(jnp.where(is_g, jnp.exp(gl - gmax), 0.0), axis=1, keepdims=True)
        lo = N_GROUPS + EXP_PER_GROUP * grp
        in_grp = (lane_f >= lo) & (lane_f < lo + EXP_PER_GROUP)
        el = jnp.where(in_grp, logit, -jnp.inf)
        emax = jnp.max(el, axis=1, keepdims=True)
        ee = jnp.where(in_grp, jnp.exp(el - emax), 0.0)
        p = jnp.where(in_grp, ee / jnp.sum(ee, axis=1, keepdims=True), -1.0)
        p1 = jnp.max(p, axis=1, keepdims=True)
        i1 = jnp.min(jnp.where(p == p1, lane_f, 1e9), axis=1, keepdims=True)
        pr = jnp.where(lane_f == i1, -1.0, p)
        p2 = jnp.max(pr, axis=1, keepdims=True)
        i2 = jnp.min(jnp.where(pr == p2, lane_f, 1e9), axis=1, keepdims=True)
        den = p1 + p2
        gw_scr[...] = jnp.where(lane_f == i1, g_w * p1 / den, 0.0) + jnp.where(lane_f == i2, g_w * p2 / den, 0.0)
        acc_scr[...] = jnp.zeros(acc_scr.shape, F32)

    gw = gw_scr[...]
    gcol = jnp.sum(jnp.where(_iota(gw.shape, 1) == e + N_GROUPS, gw, 0.0), axis=1, keepdims=True)
    xb = xb_scr[...]
    h1 = _dot(xb, w1_ref[...])
    h3 = _dot(xb, w3_ref[...])
    hid = h1 * _sigmoid(h1) * h3
    acc_scr[...] += gcol * _dot(hid.astype(BF16), w2_ref[...])

    @pl.when(e == pl.num_programs(1) - 1)
    def _():
        o_ref[...] = _layer_norm_rows(ALPHA * x_ref[...] + acc_scr[...], g_ref[...], b_ref[...])


def _moe_layer(x, M, g, b):
    n, d = x.shape
    tm = min(1024, n)
    full2 = lambda shape: pl.BlockSpec(shape, lambda i, e: (0, 0))
    return pl.pallas_call(
        _moe_kernel, out_shape=jax.ShapeDtypeStruct((n, d), F32), grid=(n // tm, N_EXPERTS),
        in_specs=[pl.BlockSpec((tm, d), lambda i, e: (i, 0)), full2((d, LANES)), full2((1, LANES)),
                  pl.BlockSpec((None, d, D_EXPERT), lambda i, e: (e, 0, 0)),
                  pl.BlockSpec((None, d, D_EXPERT), lambda i, e: (e, 0, 0)),
                  pl.BlockSpec((None, D_EXPERT, d), lambda i, e: (e, 0, 0)),
                  full2((1, d)), full2((1, d))],
        out_specs=pl.BlockSpec((tm, d), lambda i, e: (i, 0)),
        scratch_shapes=[pltpu.VMEM((tm, d), BF16), pltpu.VMEM((tm, LANES), F32), pltpu.VMEM((tm, d), F32)],
        compiler_params=_cp("parallel", "arbitrary"),
    )(x, M['wr'], M['br'], M['w1'], M['w3'], M['w2'], g.reshape(1, d), b.reshape(1, d))


EVEN_SEGS = ([(MIX, False)] * 4 + [(MIX, True), (LANES, False), (LANES, False), (LANES, True), (LANES, False),
                                   (LANES, True), (LANES, False), (LANES, False)])
ODD_SEGS = [(MIX, False)] * 3 + [(RWKV_SHIFT, False), (LANES, False)]
RW_PERM = np.concatenate([np.arange(0, 512), np.arange(576, 1088), np.arange(1088, 1600), np.arange(512, 576),
                          np.arange(1600, 1664), np.arange(1664, 1792)])
RW_INV = np.argsort(RW_PERM)


def _prep_even(w_in, w_out, b_i, b_f):
    zeros = jnp.zeros((D_MODEL, LANES - 2 * M_HEADS - 3 * NSA_HEADS), F32)
    w = jnp.concatenate([w_in[:, 0:1536], w_in[:, 1544:2056], w_in[:, 2056:3336], w_in[:, 1536:1544],
                         w_in[:, 3336:3360], zeros], axis=1).astype(BF16)
    bias = jnp.concatenate([b_i, b_f, jnp.zeros((LANES - 2 * M_HEADS,), F32)]).reshape(1, LANES)
    return dict(w_in=w, w_out_a=w_out[:MIX].astype(BF16), w_out_b=w_out[MIX:].astype(BF16), gate_bias=bias)


def _prep_odd(w_in, w_out, fox_b_f, mu, w0, w2, a0, a2, g2, k_k, k_a, r_k, ln_g, ln_b):
    rw_cols = w_in[:, 1544:3336][:, RW_PERM]
    zeros = jnp.zeros((D_MODEL, LANES - FOX_HEADS), F32)
    w = jnp.concatenate([w_in[:, 0:1536], rw_cols, w_in[:, 1536:1544], zeros], axis=1).astype(BF16)
    bias = jnp.concatenate([fox_b_f, jnp.zeros((LANES - FOX_HEADS,), F32)]).reshape(1, LANES)
    z64 = jnp.zeros((64, MIX), F32)
    hd = np.arange(MIX) // RWKV_N
    bd64 = jnp.asarray((hd[:, None] == hd[None, :]).astype(np.float32), BF16)
    P = dict(mu=mu[RW_PERM].reshape(1, -1), w0=w0.reshape(1, -1), a0=a0.reshape(1, -1),
             w2=jnp.concatenate([w2, z64], 0).astype(BF16), a2=jnp.concatenate([z64, a2], 0).astype(BF16),
             g2=g2.astype(BF16), k_k=k_k.reshape(1, -1), k_a=k_a.reshape(1, -1), r_k=r_k.reshape(1, -1),
             ln_g=ln_g.reshape(1, -1), ln_b=ln_b.reshape(1, -1), bd64=bd64)
    return dict(w_in=w, w_out_a=w_out[:MIX].astype(BF16), w_out_b=w_out[MIX:].astype(BF16), fox_bias=bias, P=P)


def _prep_moe(w_group, b_group, w_expert, b_expert, w1, w3, w2):
    padw = LANES - N_GROUPS - N_EXPERTS
    wr = jnp.concatenate([w_group, w_expert, jnp.zeros((D_MODEL, padw), F32)], axis=1)
    br = jnp.concatenate([b_group, b_expert, jnp.zeros((padw,), F32)]).reshape(1, LANES)
    return dict(wr=wr, br=br, w1=w1.astype(BF16), w3=w3.astype(BF16), w2=w2.astype(BF16))


def _even_odd_blocks(kcmp):
    return jnp.concatenate([kcmp[:, 0::2], kcmp[:, 1::2]], axis=1)


def _pages_t(cache):
    pool, page, h, dh = cache.shape
    return jnp.transpose(cache, (0, 2, 3, 1)).reshape(pool, h * dh, page)


def _tail_t(rows3):
    B, T, _ = rows3.shape
    return jnp.swapaxes(jnp.pad(rows3, ((0, 0), (0, PAGE - T), (0, 0))), 1, 2)


def _even_mixer(x3, q0, W, nsa_w, norm_g, c0, n0, m0, past, ln_g, ln_b):
    B, T, D = x3.shape
    N = B * T
    x2 = x3.reshape(N, D)
    pos = q0 + jnp.arange(T)
    cos, sin = _rope_tables(pos)
    rope = (jnp.tile(cos, (B, 1)), jnp.tile(sin, (B, 1)))
    mq, mk, mv, mo, nq, kc, vc, ks, vs, kw, vw, small = _proj_split(x2, W['w_in'], EVEN_SEGS, rope)
    h, c1, n1, m1 = _mlstm(mq, mk, mv, mo, small, W['gate_bias'], norm_g, c0, n0, m0, B, T)
    pe_k, pe_v, ck1, ck2, cv1, cv2 = nsa_w
    blk_w = CMP_BLOCK * LANES
    q3 = nq.reshape(B, T, MIX)
    if past is None:
        L = T
        nc = L // CMP_BLOCK
        kcmp = _compress(kc.reshape(B * nc, blk_w), pe_k, ck1, ck2).reshape(B, nc, LANES)
        vcmp = _compress(vc.reshape(B * nc, blk_w), pe_v, cv1, cv2).reshape(B, nc, LANES)
    else:
        pt = past['page_table']
        npages = pt.shape[1]
        L = npages * PAGE + T
        nc = L // CMP_BLOCK
        per_page = PAGE // CMP_BLOCK
        pool = past['kc'].shape[0]
        kcp = _compress_pool(_pages_t(past['kc']), pe_k, ck1, ck2).reshape(pool, per_page * LANES)
        vcp = _compress_pool(_pages_t(past['vc']), pe_v, cv1, cv2).reshape(pool, per_page * LANES)
        kcmp = jnp.take(kcp, pt, axis=0).reshape(B, npages * per_page, LANES)[:, :nc]
        vcmp = jnp.take(vcp, pt, axis=0).reshape(B, npages * per_page, LANES)[:, :nc]
    cpos = jnp.arange(nc) * CMP_BLOCK + CMP_BLOCK - 1
    ccos, csin = _rope_tables(cpos)
    kcmp = _rope_rows(kcmp.reshape(B * nc, LANES), jnp.tile(ccos, (B, 1)), jnp.tile(csin, (B, 1))).reshape(B, nc, LANES)
    ns = -(-L // SEL_BLOCK)
    ns_pad = -(-ns // LANES) * LANES
    cend = jnp.concatenate([cpos[0::2], cpos[1::2]]).astype(jnp.int32).reshape(1, nc)
    o_cmp, sel = _cmp_sel(q3, _even_odd_blocks(kcmp), _even_odd_blocks(vcmp), cend, q0, ns, ns_pad)
    ks3, vs3 = ks.reshape(B, T, LANES), vs.reshape(B, T, LANES)
    kw3, vw3 = kw.reshape(B, T, LANES), vw.reshape(B, T, LANES)
    if past is None:
        o_slc = _flash(q3, ks3, vs3, mode='sel', layout='nsa', tq=min(TQ_SLC, T), tk=min(TK_SLC, T), q0=0, sel=sel,
                       ns_pad=ns_pad)
        o_win = _flash(q3, kw3, vw3, mode='band', layout='nsa', tq=min(TQ_WIN, T), tk=min(TK_WIN, T), q0=0)
        kw_new, vw_new = kw3[:, -min(WINDOW, T):], vw3[:, -min(WINDOW, T):]
    else:
        o_slc = _flash(q3, _pages_t(past['ks']), _pages_t(past['vs']), mode='sel', layout='nsa', tq=T, tk=0, q0=q0,
                       page_table=pt, pps=_largest_divisor(npages, (8, 4, 2, 1)), tail=(_tail_t(ks3), _tail_t(vs3)),
                       tail_pos0=q0, sel=sel, ns_pad=ns_pad)
        kw_ext = jnp.concatenate([past['kw_buf'], kw3], axis=1)
        vw_ext = jnp.concatenate([past['vw_buf'], vw3], axis=1)
        wb = past['kw_buf'].shape[1]
        o_win = _flash(q3, kw_ext, vw_ext, mode='band', layout='nsa', tq=T, tk=wb + T, q0=q0, kbase=q0 - wb)
        keep = min(WINDOW, wb + T)
        kw_new, vw_new = kw_ext[:, -keep:], vw_ext[:, -keep:]
    o = _nsa_combine(o_cmp.reshape(N, MIX), o_slc.reshape(N, MIX), o_win.reshape(N, MIX), small)
    x_new = _mm_res_ln([h, o], [W['w_out_a'], W['w_out_b']], x2, ln_g, ln_b).reshape(B, T, D)
    sh = (B, T, NSA_GROUPS, NSA_DH)
    wsh = (B, -1, NSA_GROUPS, NSA_DH)
    state = (c1, n1, m1, kc.reshape(sh), vc.reshape(sh), ks.reshape(sh), vs.reshape(sh),
             kw_new.reshape(wsh), vw_new.reshape(wsh))
    return x_new, state


def _odd_mixer(x3, q0, W, s0, shift0, past, ln_g, ln_b):
    B, T, D = x3.shape
    N = B * T
    x2 = x3.reshape(N, D)
    fq, fk, fv, rw, small = _proj_split(x2, W['w_in'], ODD_SEGS)
    q3, k3, v3 = (t.reshape(B, T, MIX) for t in (fq, fk, fv))
    if past is None:
        lf_full, caug = _fox_cumsum(small, W['fox_bias'], B, T)
        lf = lf_full[:, :FOX_HEADS].reshape(B, T, FOX_HEADS)
        o_c = _flash(q3, k3, v3, mode='causal', layout='fox', tq=min(TQ_FOX, T), tk=min(TK_FOX, T), q0=0,
                     caug=caug.reshape(B, T, LANES))
    else:
        pt = past['page_table']
        npages = pt.shape[1]
        pps = _largest_divisor(npages, (8, 4, 2, 1))
        padrows = ((0, 0), (0, PAGE - T), (0, 0))
        lf_pool_t = jnp.swapaxes(past['lf'], 1, 2)
        c_all, lf_new = _fox_cumsum_paged(lf_pool_t, pt, jnp.pad(small.reshape(B, T, LANES), padrows), W['fox_bias'], pps)
        lf = lf_new[:, :T, :FOX_HEADS]
        n_main = npages * PAGE
        o_c = _flash(q3, _pages_t(past['k']), _pages_t(past['v']), mode='causal', layout='fox_wide', tq=T, tk=0, q0=q0,
                     page_table=pt, pps=pps, tail=(_tail_t(k3), _tail_t(v3)), tail_pos0=q0,
                     crow=c_all, crow_tail=c_all[:, :, n_main:n_main + PAGE])
    rw3 = rw.reshape(B, T, RWKV_SHIFT)
    r, lw, k, v, kk, a, g = _rwkv_prep(rw3, shift0[:, RW_PERM], W['P'])
    y, s1 = _rwkv_scan(r, lw, k, v, kk, a, g, s0, W['P'], B, T)
    x_new = _mm_res_ln([o_c.reshape(N, MIX), y.reshape(N, MIX)], [W['w_out_a'], W['w_out_b']], x2, ln_g, ln_b)
    hs = (B, T, FOX_HEADS, FOX_DH)
    state = (fk.reshape(hs), fv.reshape(hs), lf, s1, rw3[:, -1][:, RW_INV])
    return x_new.reshape(B, T, D), state


def kernel(x_prompt, x_sample, state_mlstm_C, state_mlstm_n, state_mlstm_m, cache_nsa_kc, cache_nsa_vc, cache_nsa_ks, cache_nsa_vs, cache_nsa_kw, cache_nsa_vw, cache_fox_k, cache_fox_v, cache_fox_logf, state_rwkv_S, state_rwkv_shift, cache_mem_k, cache_mem_v, page_table, mem_prompt, even_w_in, even_w_out, mlstm_b_i, mlstm_b_f, mlstm_norm_g, nsa_pe_k, nsa_pe_v, nsa_cmp_k_w1, nsa_cmp_k_w2, nsa_cmp_v_w1, nsa_cmp_v_w2, odd_w_in, odd_w_out, fox_b_f, rwkv_mu, rwkv_w0, rwkv_w2, rwkv_a0, rwkv_a2, rwkv_g2, rwkv_k_k, rwkv_k_a, rwkv_r_k, rwkv_ln_g, rwkv_ln_b, mem_wq, mem_wk, mem_wv, mem_wo, moe_w_group, moe_b_group, moe_w_expert, moe_b_expert, moe_w1, moe_w3, moe_w2, ln_g, ln_b):
    bp, tp, d = x_prompt.shape
    bs, ts, _ = x_sample.shape
    depth = ln_g.shape[0]
    past = page_table.shape[1] * cache_nsa_kc.shape[2]
    xp, xs = x_prompt, x_sample
    acc = {}

    def push(prefix, names, vals):
        for nm, val in zip(names, vals):
            acc.setdefault(prefix + nm, []).append(val)

    even_names = ('mlstm_C', 'mlstm_n', 'mlstm_m', 'nsa_kc', 'nsa_vc', 'nsa_ks', 'nsa_vs', 'nsa_kw', 'nsa_vw')
    odd_names = ('fox_k', 'fox_v', 'fox_logf', 'rwkv_S', 'rwkv_shift')
    for layer in range(depth):
        e = layer // 2
        if layer % 2 == 0:
            W = _prep_even(even_w_in[e], even_w_out[e], mlstm_b_i[e], mlstm_b_f[e])
            nsa_w = (nsa_pe_k[e], nsa_pe_v[e], nsa_cmp_k_w1[e], nsa_cmp_k_w2[e], nsa_cmp_v_w1[e], nsa_cmp_v_w2[e])
            xp, sp = _even_mixer(xp, 0, W, nsa_w, mlstm_norm_g[e], jnp.zeros((bp, M_HEADS, M_DV, M_DK), F32),
                                 jnp.zeros((bp, M_HEADS, M_DK), F32), jnp.zeros((bp, M_HEADS), F32), None,
                                 ln_g[layer, 0], ln_b[layer, 0])
            wbuf = lambda c: c[e].reshape(bs, -1, LANES)
            past_d = dict(kc=cache_nsa_kc[e], vc=cache_nsa_vc[e], ks=cache_nsa_ks[e], vs=cache_nsa_vs[e],
                          page_table=page_table, kw_buf=wbuf(cache_nsa_kw), vw_buf=wbuf(cache_nsa_vw))
            xs, ss = _even_mixer(xs, past, W, nsa_w, mlstm_norm_g[e], state_mlstm_C[e], state_mlstm_n[e],
                                 state_mlstm_m[e], past_d, ln_g[layer, 0], ln_b[layer, 0])
            names = even_names
        else:
            W = _prep_odd(odd_w_in[e], odd_w_out[e], fox_b_f[e], rwkv_mu[e], rwkv_w0[e], rwkv_w2[e], rwkv_a0[e],
                          rwkv_a2[e], rwkv_g2[e], rwkv_k_k[e], rwkv_k_a[e], rwkv_r_k[e], rwkv_ln_g[e], rwkv_ln_b[e])
            xp, sp = _odd_mixer(xp, 0, W, jnp.zeros((bp, RWKV_HEADS, RWKV_N, RWKV_N), F32),
                                jnp.zeros((bp, RWKV_SHIFT), F32), None, ln_g[layer, 0], ln_b[layer, 0])
            past_d = dict(k=cache_fox_k[e], v=cache_fox_v[e], lf=cache_fox_logf[e], page_table=page_table)
            xs, ss = _odd_mixer(xs, past, W, state_rwkv_S[e], state_rwkv_shift[e], past_d,
                                ln_g[layer, 0], ln_b[layer, 0])
            names = odd_names
        push('p_', names, sp)
        push('s_', names, ss)
        wkv = jnp.concatenate([mem_wk[layer], mem_wv[layer]], axis=1).astype(BF16)
        mt = mem_prompt.shape[1]
        kmp, vmp = _proj_split(mem_prompt.reshape(bp * mt, d), wkv, [(d, False), (d, False)])
        push('p_', ('mem_k', 'mem_v'), (kmp.reshape(bp, mt, MEM_HEADS, MEM_DH), vmp.reshape(bp, mt, MEM_HEADS, MEM_DH)))
        wq, wo = mem_wq[layer].astype(BF16), mem_wo[layer].astype(BF16)
        xp = _mem_layer(xp, kmp.reshape(bp, mt, d), vmp.reshape(bp, mt, d), wq, wo, ln_g[layer, 1], ln_b[layer, 1])
        xs = _mem_layer(xs, cache_mem_k[layer].reshape(bs, -1, d), cache_mem_v[layer].reshape(bs, -1, d), wq, wo,
                        ln_g[layer, 1], ln_b[layer, 1])
        M = _prep_moe(moe_w_group[layer], moe_b_group[layer], moe_w_expert[layer], moe_b_expert[layer],
                      moe_w1[layer], moe_w3[layer], moe_w2[layer])
        xp = _moe_layer(xp.reshape(bp * tp, d), M, ln_g[layer, 2], ln_b[layer, 2]).reshape(bp, tp, d)
        xs = _moe_layer(xs.reshape(bs * ts, d), M, ln_g[layer, 2], ln_b[layer, 2]).reshape(bs, ts, d)

    st = {name: jnp.stack(vals) for name, vals in acc.items()}
    return (xp, xs,
            st['p_mlstm_C'], st['p_mlstm_n'], st['p_mlstm_m'],
            st['p_nsa_kc'], st['p_nsa_vc'], st['p_nsa_ks'], st['p_nsa_vs'], st['p_nsa_kw'], st['p_nsa_vw'],
            st['p_fox_k'], st['p_fox_v'], st['p_fox_logf'], st['p_rwkv_S'], st['p_rwkv_shift'],
            st['p_mem_k'], st['p_mem_v'],
            st['s_mlstm_C'], st['s_mlstm_n'], st['s_mlstm_m'],
            st['s_nsa_kc'], st['s_nsa_vc'], st['s_nsa_ks'], st['s_nsa_vs'], st['s_nsa_kw'], st['s_nsa_vw'],
            st['s_fox_k'], st['s_fox_v'], st['s_fox_logf'], st['s_rwkv_S'], st['s_rwkv_shift'])
```

```python
import functools

import numpy as np
import jax
import jax.numpy as jnp
from jax import lax
from jax.experimental import pallas as pl
from jax.experimental.pallas import tpu as pltpu

F32 = jnp.float32
BF16 = jnp.bfloat16
HI = lax.Precision.HIGHEST

D_MODEL = 1024
DEPTH = 2
PAGE = 128
M_HEADS, M_DK, M_DV, M_CHUNK = 4, 128, 128, 64
NSA_HEADS, NSA_GROUPS, NSA_DH = 8, 2, 64
CMP_BLOCK, SEL_BLOCK, N_SEL, WINDOW = 32, 64, 16, 512
FOX_HEADS, FOX_DH = 8, 64
RWKV_HEADS, RWKV_N, W_LORA, A_LORA, G_LORA = 8, 64, 64, 64, 128
RWKV_CHUNK = 64
MEM_HEADS = 4
MEM_DH = D_MODEL // MEM_HEADS
N_GROUPS, EXP_PER_GROUP, TOP_K, D_EXPERT = 4, 8, 2, 256
N_EXPERTS = N_GROUPS * EXP_PER_GROUP
ROPE_THETA = 10000.0
ALPHA = (2.0 * DEPTH) ** 0.25
LN_EPS = 1e-5
NEG = -1e30
FORCE = 1e4
MIX = 512
RWKV_SHIFT = 1792
SEL_OFF = 2.0 ** 30
LOG2E = 1.4426950408889634

LANES = 128
VMEM_LIMIT = 48 * 1024 * 1024
TQ_FOX, TK_FOX = 512, 512
TQ_SLC, TK_SLC = 256, 512
TQ_WIN, TK_WIN = 128, 128
TQ_CMP = 256


def _cp(*sem):
    return pltpu.CompilerParams(dimension_semantics=sem, vmem_limit_bytes=VMEM_LIMIT)


def _dot(a, b, prec=None):
    return jnp.dot(a, b, preferred_element_type=F32, precision=prec)


def _dot_nt(a, b, prec=None):
    return lax.dot_general(a, b, (((1,), (1,)), ((), ())), preferred_element_type=F32, precision=prec)


def _dot_tn(a, b, prec=None):
    return lax.dot_general(a, b, (((0,), (0,)), ((), ())), preferred_element_type=F32, precision=prec)


def _bdot(a, b):
    return _dot(a.astype(BF16), b.astype(BF16))


def _bdot_nt(a, b):
    return _dot_nt(a.astype(BF16), b.astype(BF16))


def _bdot_tn(a, b):
    return _dot_tn(a.astype(BF16), b.astype(BF16))


def _split_bf16(a):
    hi = a.astype(BF16)
    return hi, (a - hi.astype(F32)).astype(BF16)


def _dot3(a, b):
    ah, al = _split_bf16(a)
    bh, bl = _split_bf16(b)
    return _dot(ah, bh) + (_dot(ah, bl) + _dot(al, bh))


def _split3_bf16(a):
    hi = a.astype(BF16)
    r1 = a - hi.astype(F32)
    mid = r1.astype(BF16)
    return hi, mid, (r1 - mid.astype(F32)).astype(BF16)


def _dot_r01(a, b01):
    b = b01.astype(BF16)
    hi, mid, lo = _split3_bf16(a)
    return _dot(hi, b) + (_dot(mid, b) + _dot(lo, b))


def _dot_l01(a01, b):
    a = a01.astype(BF16)
    hi, mid, lo = _split3_bf16(b)
    return _dot(a, hi) + (_dot(a, mid) + _dot(a, lo))


def _dot_nt_l01(a01, b):
    a = a01.astype(BF16)
    hi, mid, lo = _split3_bf16(b)
    return _dot_nt(a, hi) + (_dot_nt(a, mid) + _dot_nt(a, lo))


def _iota(shape, dim):
    return lax.broadcasted_iota(jnp.int32, shape, dim)


def _log_sigmoid(x):
    return jnp.minimum(x, 0.0) - jnp.log1p(jnp.exp(-jnp.abs(x)))


def _sigmoid(x):
    return 1.0 / (1.0 + jnp.exp(-x))


def _softplus(x):
    return jnp.maximum(x, 0.0) + jnp.log1p(jnp.exp(-jnp.abs(x)))


def _layer_norm_rows(z, g, b):
    mu = jnp.mean(z, axis=-1, keepdims=True)
    zc = z - mu
    var = jnp.mean(zc * zc, axis=-1, keepdims=True)
    return zc * lax.rsqrt(var + LN_EPS) * g + b


def _rope128(y, cos, sin_signed):
    lane = _iota(y.shape, 1)
    lo = (lane & 63) < 32
    sw = jnp.where(lo, pltpu.roll(y, 96, 1), pltpu.roll(y, 32, 1))
    return y * cos + sw * sin_signed


def _rope_tables(pos):
    half = NSA_DH // 2
    freq = ROPE_THETA ** (-jnp.arange(half, dtype=F32) / half)
    ang = pos.astype(F32)[:, None] * freq[None, :]
    cos, sin = jnp.cos(ang), jnp.sin(ang)
    return jnp.tile(cos, (1, 4)), jnp.tile(jnp.concatenate([-sin, sin], -1), (1, 2))


def _largest_divisor(n, options):
    for p in options:
        if n % p == 0:
            return p


def _proj_kernel(*refs, segs, use_rope):
    if use_rope:
        x_ref, w_ref, cos_ref, sin_ref = refs[:4]
        outs = refs[4:]
    else:
        x_ref, w_ref = refs[:2]
        outs = refs[2:]
    xb = x_ref[...].astype(BF16)
    off = 0
    for (wd, rp), o_ref in zip(segs, outs):
        y = _dot(xb, w_ref[:, off:off + wd])
        if rp:
            for c in range(wd // LANES):
                o_ref[:, c * LANES:(c + 1) * LANES] = _rope128(y[:, c * LANES:(c + 1) * LANES], cos_ref[...], sin_ref[...])
        else:
            o_ref[...] = y
        off += wd


def _proj_split(x, w, segs, rope=None):
    n, k = x.shape
    tm = min(256, n)
    wtot = sum(s[0] for s in segs)
    use_rope = rope is not None
    in_specs = [pl.BlockSpec((tm, k), lambda i: (i, 0)), pl.BlockSpec((k, wtot), lambda i: (0, 0))]
    args = [x, w]
    if use_rope:
        in_specs += [pl.BlockSpec((tm, LANES), lambda i: (i, 0))] * 2
        args += list(rope)
    return pl.pallas_call(
        functools.partial(_proj_kernel, segs=tuple(segs), use_rope=use_rope),
        out_shape=[jax.ShapeDtypeStruct((n, wd), F32) for wd, _ in segs],
        grid=(n // tm,),
        in_specs=in_specs,
        out_specs=[pl.BlockSpec((tm, wd), lambda i: (i, 0)) for wd, _ in segs],
        compiler_params=_cp("parallel"),
    )(*args)


def _mm_res_ln_kernel(*refs, n_in):
    a_refs = refs[:n_in]
    w_refs = refs[n_in:2 * n_in]
    x_ref, g_ref, b_ref, o_ref = refs[2 * n_in:]
    y = _dot(a_refs[0][...].astype(BF16), w_refs[0][...])
    for a_ref, w_ref in zip(a_refs[1:], w_refs[1:]):
        y = y + _dot(a_ref[...].astype(BF16), w_ref[...])
    o_ref[...] = _layer_norm_rows(ALPHA * x_ref[...] + y, g_ref[...], b_ref[...])


def _mm_res_ln(a_list, w_list, x, g, b):
    n, d = x.shape
    tm = min(512, n)
    n_in = len(a_list)
    in_specs = ([pl.BlockSpec((tm, a.shape[1]), lambda i: (i, 0)) for a in a_list]
                + [pl.BlockSpec(w.shape, lambda i: (0, 0)) for w in w_list]
                + [pl.BlockSpec((tm, d), lambda i: (i, 0)), pl.BlockSpec((1, d), lambda i: (0, 0)),
                   pl.BlockSpec((1, d), lambda i: (0, 0))])
    return pl.pallas_call(
        functools.partial(_mm_res_ln_kernel, n_in=n_in),
        out_shape=jax.ShapeDtypeStruct((n, d), F32),
        grid=(n // tm,),
        in_specs=in_specs,
        out_specs=pl.BlockSpec((tm, d), lambda i: (i, 0)),
        compiler_params=_cp("parallel"),
    )(*a_list, *w_list, x, g.reshape(1, d), b.reshape(1, d))


def _mlstm_kernel(q_ref, k_ref, v_ref, o_ref, sm_ref, bias_ref, g_ref, c0_ref, n0_ref, m0_ref,
                  h_ref, c1_ref, n1_ref, m1_ref, c_scr, n_scr, m_scr, *, L):
    ci = pl.program_id(1)

    @pl.when(ci == 0)
    def _():
        c_scr[...] = c0_ref[...]
        n_scr[...] = n0_ref[...]
        m_scr[...] = m0_ref[...]

    sm = sm_ref[...] + bias_ref[...]
    lane = _iota(sm.shape, 1)
    gates = jnp.where(lane < M_HEADS, sm, _log_sigmoid(sm))
    eye8 = (_iota((8, LANES), 0) == _iota((8, LANES), 1)).astype(F32)
    gates_row = _dot_nt_l01(eye8, gates)
    r_i = _iota((L, L), 0)
    c_i = _iota((L, L), 1)
    tri = c_i <= r_i
    scale = M_DK ** -0.5
    hs = []
    for h in range(M_HEADS):
        ic_col = gates[:, h:h + 1]
        f_col = gates[:, M_HEADS + h:M_HEADS + h + 1]
        ic_row = gates_row[h:h + 1, :]
        f_row = gates_row[M_HEADS + h:M_HEADS + h + 1, :]
        b_col = jnp.sum(jnp.where(tri, jnp.broadcast_to(f_row, (L, L)), 0.0), axis=1, keepdims=True)
        b_row = jnp.sum(jnp.where(r_i <= c_i, jnp.broadcast_to(f_col, (L, L)), 0.0), axis=0, keepdims=True)
        m_prev = m_scr[h:h + 1, 0:1]
        qh = q_ref[:, h * M_DK:(h + 1) * M_DK]
        kh = k_ref[:, h * M_DK:(h + 1) * M_DK] * scale
        vh = v_ref[:, h * M_DV:(h + 1) * M_DV]
        hs.append(dict(ic_col=ic_col, ic_row=ic_row, b_col=b_col, b_row=b_row, m_prev=m_prev, qh=qh, kh=kh, vh=vh,
                       qb=qh.astype(BF16), kb=kh.astype(BF16), vb=vh.astype(BF16),
                       c_old=c_scr[h], n_old=n_scr[h:h + 1, :]))
    for d in hs:
        dmat = jnp.where(tri, d['b_col'] - d['b_row'] + d['ic_row'], NEG)
        inter = d['b_col'] + d['m_prev']
        d['mt'] = jnp.maximum(inter, jnp.max(dmat, axis=1, keepdims=True))
        d['wmat'] = jnp.where(tri, jnp.exp(dmat - d['mt']), 0.0)
        d['a'] = jnp.exp(inter - d['mt'])
    for d in hs:
        d['s'] = _dot_nt(d['qb'], d['kb']) * d['wmat']
        d['cq'] = _dot_nt(d['qb'], d['c_old'].astype(BF16))
    for d in hs:
        num = d['a'] * d['cq'] + _dot(d['s'].astype(BF16), d['vb'])
        den = d['a'] * jnp.sum(d['qh'] * d['n_old'], axis=1, keepdims=True) + jnp.sum(d['s'], axis=1, keepdims=True)
        d['hh'] = num / jnp.maximum(jnp.abs(den), jnp.exp(-d['mt']))
    for h, d in enumerate(hs):
        m_l = d['mt'][L - 1:L, :]
        b_l = d['b_col'][L - 1:L, :]
        g_l = jnp.exp(b_l - d['b_col'] + d['ic_col'] - m_l)
        a_l = jnp.exp(b_l + d['m_prev'] - m_l)
        c_scr[h] = a_l * d['c_old'] + _dot_tn((g_l * d['vh']).astype(BF16), d['kb'])
        n_scr[h:h + 1, :] = a_l * d['n_old'] + jnp.sum(g_l * d['kh'], axis=0, keepdims=True)
        m_scr[h:h + 1, :] = jnp.broadcast_to(m_l, (1, LANES))
    for h, d in enumerate(hs):
        hh = d['hh']
        mu = jnp.mean(hh, axis=1, keepdims=True)
        hc = hh - mu
        var = jnp.mean(hc * hc, axis=1, keepdims=True)
        hn = hc * lax.rsqrt(var + 1e-6) * g_ref[:, h * M_DV:(h + 1) * M_DV]
        h_ref[:, h * M_DV:(h + 1) * M_DV] = hn * _sigmoid(o_ref[:, h * M_DV:(h + 1) * M_DV])

    @pl.when(ci == pl.num_programs(1) - 1)
    def _():
        c1_ref[...] = c_scr[...]
        n1_ref[...] = n_scr[...]
        m1_ref[...] = m_scr[...]


def _mlstm(mq, mk, mv, mo, small, bias_row, norm_g, c0, n0, m0, B, T):
    L = M_CHUNK if T % M_CHUNK == 0 else T
    nch = T // L
    row = lambda b, c: (b * nch + c, 0)
    st4 = lambda b, c: (b, 0, 0, 0)
    st3 = lambda b, c: (b, 0, 0)
    m0b = jnp.broadcast_to(m0[:, :, None], (B, M_HEADS, LANES)).astype(F32)
    h, c1, n1, m1 = pl.pallas_call(
        functools.partial(_mlstm_kernel, L=L),
        out_shape=[jax.ShapeDtypeStruct((B * T, MIX), F32),
                   jax.ShapeDtypeStruct((B, M_HEADS, M_DV, M_DK), F32),
                   jax.ShapeDtypeStruct((B, M_HEADS, M_DK), F32),
                   jax.ShapeDtypeStruct((B, M_HEADS, LANES), F32)],
        grid=(B, nch),
        in_specs=[pl.BlockSpec((L, MIX), row)] * 4
        + [pl.BlockSpec((L, LANES), row), pl.BlockSpec((1, LANES), lambda b, c: (0, 0)),
           pl.BlockSpec((1, MIX), lambda b, c: (0, 0)),
           pl.BlockSpec((None, M_HEADS, M_DV, M_DK), st4), pl.BlockSpec((None, M_HEADS, M_DK), st3),
           pl.BlockSpec((None, M_HEADS, LANES), st3)],
        out_specs=[pl.BlockSpec((L, MIX), row), pl.BlockSpec((None, M_HEADS, M_DV, M_DK), st4),
                   pl.BlockSpec((None, M_HEADS, M_DK), st3), pl.BlockSpec((None, M_HEADS, LANES), st3)],
        scratch_shapes=[pltpu.VMEM((M_HEADS, M_DV, M_DK), F32), pltpu.VMEM((M_HEADS, M_DK), F32),
                        pltpu.VMEM((M_HEADS, LANES), F32)],
        compiler_params=_cp("parallel", "arbitrary"),
    )(mq, mk, mv, mo, small, bias_row, norm_g.reshape(1, MIX), c0.astype(F32), n0.astype(F32), m0b)
    return h, c1, n1, m1[:, :, 0]


def _compress_weights(pe, w1, w2):
    eye = jnp.eye(NSA_GROUPS, dtype=F32)
    w1j = jnp.einsum('jdo,gh->jgdho', w1.reshape(CMP_BLOCK, NSA_DH, NSA_DH), eye)
    w1j = w1j.reshape(CMP_BLOCK, LANES, LANES).astype(BF16)
    w2b = jnp.einsum('do,gh->gdho', w2, eye).reshape(LANES, LANES).astype(BF16)
    pe2 = jnp.tile(pe, (1, NSA_GROUPS))
    return pe2, w1j, w2b


def _compress_kernel(x_ref, pe_ref, w1_ref, w2_ref, o_ref):
    xb = (x_ref[...] + pe_ref[...]).astype(BF16)
    hid = jax.nn.gelu(_dot(xb, w1_ref[...]))
    o_ref[...] = _dot(hid.astype(BF16), w2_ref[...])


def _compress(blocks, pe, w1, w2):
    r = blocks.shape[0]
    tb = min(256, r)
    width = CMP_BLOCK * LANES
    pe2, w1j, w2b = _compress_weights(pe, w1, w2)
    return pl.pallas_call(
        _compress_kernel,
        out_shape=jax.ShapeDtypeStruct((r, LANES), F32),
        grid=(pl.cdiv(r, tb),),
        in_specs=[pl.BlockSpec((tb, width), lambda i: (i, 0)), pl.BlockSpec((1, width), lambda i: (0, 0)),
                  pl.BlockSpec((width, LANES), lambda i: (0, 0)), pl.BlockSpec((LANES, LANES), lambda i: (0, 0))],
        out_specs=pl.BlockSpec((tb, LANES), lambda i: (i, 0)),
        compiler_params=_cp("parallel"),
    )(blocks, pe2.reshape(1, width), w1j.reshape(width, LANES), w2b)


def _compress_pool_kernel(xt_ref, pe_ref, w1_ref, w2_ref, o_ref, xs_scr, *, pb):
    per_page = PAGE // CMP_BLOCK
    for p in range(pb):
        xs_scr[p * PAGE:(p + 1) * PAGE, :] = xt_ref[p].T + pe_ref[...]
    acc = None
    for j in range(CMP_BLOCK):
        rows = xs_scr[pl.ds(j, per_page * pb, stride=CMP_BLOCK), :]
        t = _dot(rows.astype(BF16), w1_ref[j])
        acc = t if acc is None else acc + t
    o_ref[...] = _dot(jax.nn.gelu(acc).astype(BF16), w2_ref[...])


def _compress_pool(pool_t, pe, w1, w2):
    npool = pool_t.shape[0]
    per_page = PAGE // CMP_BLOCK
    pb = _largest_divisor(npool, (16, 8, 4, 2))
    pe2, w1j, w2b = _compress_weights(pe, w1, w2)
    pe_tile = jnp.tile(pe2, (per_page, 1))
    return pl.pallas_call(
        functools.partial(_compress_pool_kernel, pb=pb),
        out_shape=jax.ShapeDtypeStruct((npool * per_page, LANES), F32),
        grid=(npool // pb,),
        in_specs=[pl.BlockSpec((pb, LANES, PAGE), lambda i: (i, 0, 0)), pl.BlockSpec((PAGE, LANES), lambda i: (0, 0)),
                  pl.BlockSpec((CMP_BLOCK, LANES, LANES), lambda i: (0, 0, 0)),
                  pl.BlockSpec((LANES, LANES), lambda i: (0, 0))],
        out_specs=pl.BlockSpec((per_page * pb, LANES), lambda i: (i, 0)),
        scratch_shapes=[pltpu.VMEM((pb * PAGE, LANES), F32)],
        compiler_params=_cp("parallel"),
    )(pool_t, pe_tile, w1j, w2b)


def _rope_rows_kernel(x_ref, cos_ref, sin_ref, o_ref):
    o_ref[...] = _rope128(x_ref[...], cos_ref[...], sin_ref[...])


def _rope_rows(x, cos, sin):
    r = x.shape[0]
    tb = min(512, r)
    spec = pl.BlockSpec((tb, LANES), lambda i: (i, 0))
    return pl.pallas_call(_rope_rows_kernel, out_shape=jax.ShapeDtypeStruct((r, LANES), F32), grid=(r // tb,),
                          in_specs=[spec] * 3, out_specs=spec, compiler_params=_cp("parallel"))(x, cos, sin)


def _stack_heads(qblk, heads, scale):
    rows = []
    for src_blk, src_half, dst_half in heads:
        x = qblk[:, src_blk * LANES:(src_blk + 1) * LANES] * scale
        if src_half != dst_half:
            x = pltpu.roll(x, 64, 1)
        lane = _iota(x.shape, 1)
        keep = (lane >= 64) if dst_half else (lane < 64)
        rows.append(jnp.where(keep, x, 0.0))
    return jnp.concatenate(rows, axis=0)


def _unstack_heads(o_full_list, heads_per_kb, tq, n_out_blocks):
    outs = [None] * n_out_blocks
    for o_full, heads in zip(o_full_list, heads_per_kb):
        for idx, (src_blk, src_half, dst_half) in enumerate(heads):
            piece = o_full[idx * tq:(idx + 1) * tq]
            if src_half != dst_half:
                piece = pltpu.roll(piece, 64, 1)
            lane = _iota(piece.shape, 1)
            keep = (lane >= 64) if src_half else (lane < 64)
            prev = outs[src_blk]
            outs[src_blk] = jnp.where(keep, piece, 0.0 if prev is None else prev)
    return outs


def _lane_repeat(x, width):
    if width == LANES:
        return x
    if width % LANES == 0:
        return pltpu.repeat(x, width // LANES, axis=1)
    return jnp.broadcast_to(x[:, 0:1], (x.shape[0], width))


def _stack_wide(qblk, n_heads, scale):
    head_of_lane = _iota(qblk.shape, 1) >> 6
    return jnp.concatenate([jnp.where(head_of_lane == h, qblk * scale, 0.0) for h in range(n_heads)], axis=0)


def _unstack_wide(o_full, n_heads, tq):
    head_of_lane = _iota((tq, o_full.shape[1]), 1) >> 6
    out = jnp.zeros((tq, o_full.shape[1]), F32)
    for h in range(n_heads):
        out = jnp.where(head_of_lane == h, o_full[h * tq:(h + 1) * tq], out)
    return out


NSA_HEAD_MAP = [[(h // 2, h % 2, h // (NSA_HEADS // NSA_GROUPS)) for h in range(NSA_HEADS)]]
FOX_HEAD_MAP = [[(kb, 0, 0), (kb, 1, 1)] for kb in range(FOX_HEADS // 2)]


def _cmp_sel_kernel(q_ref, kc_ref, vc_ref, cend_ref, o_ref, sel_ref, *, tq, q0, nc, ns, ns_pad):
    qi = pl.program_id(1)
    heads = NSA_HEAD_MAP[0]
    hpg = NSA_HEADS // NSA_GROUPS
    qbd = _stack_heads(q_ref[...], heads, NSA_DH ** -0.5).astype(BF16)
    s = _dot_nt(qbd, kc_ref[...].astype(BF16))
    rows = NSA_HEADS * tq
    qpos = q0 + qi * tq + (_iota((rows, 1), 0) & (tq - 1))
    mask = cend_ref[...] <= qpos
    s = jnp.where(mask, s, NEG)
    e = jnp.where(mask, jnp.exp(s - jnp.max(s, axis=1, keepdims=True)), 0.0)
    p = e / jnp.maximum(jnp.sum(e, axis=1, keepdims=True), 1e-30)
    o_full = _dot(p.astype(BF16), vc_ref[...].astype(BF16))
    outs = _unstack_heads([o_full], NSA_HEAD_MAP, tq, MIX // LANES)
    for c, blk in enumerate(outs):
        o_ref[:, c * LANES:(c + 1) * LANES] = blk
    half = nc // 2
    qp = q0 + qi * tq + _iota((tq, 1), 0)
    blk_i = _iota((tq, ns_pad), 1)
    blk_f = blk_i.astype(F32)
    cur = qp >> 6
    forced = (blk_i == 0) | (blk_i == cur) | (blk_i == cur - 1)
    valid = (blk_i * SEL_BLOCK <= qp) & (blk_i < ns)
    n_sel = min(N_SEL, ns)
    scores, chosen = [], []
    for g in range(NSA_GROUPS):
        pg = p[g * hpg * tq:(g * hpg + 1) * tq]
        for j in range(1, hpg):
            pg = pg + p[(g * hpg + j) * tq:(g * hpg + j + 1) * tq]
        imp = pg[:, :half] + pg[:, half:]
        if ns_pad > half:
            imp = jnp.concatenate([imp, jnp.zeros((tq, ns_pad - half), F32)], axis=1)
        score = jnp.where(valid, imp + FORCE * forced.astype(F32), NEG)
        scores.append(jnp.where(blk_i < ns, score, -jnp.inf))
        chosen.append(jnp.zeros((tq, ns_pad), F32))
    for _ in range(n_sel):
        for g in range(NSA_GROUPS):
            mx = jnp.max(scores[g], axis=1, keepdims=True)
            first = jnp.min(jnp.where(scores[g] == mx, blk_f, 1e9), axis=1, keepdims=True)
            hit = blk_f == first
            chosen[g] = jnp.where(hit, 1.0, chosen[g])
            scores[g] = jnp.where(hit, -jnp.inf, scores[g])
    for g in range(NSA_GROUPS):
        sel_ref[:, g * ns_pad:(g + 1) * ns_pad] = jnp.where(valid, chosen[g], 0.0)


def _cmp_sel(q3, kcmp, vcmp, cend, q0, ns, ns_pad):
    B, T, _ = q3.shape
    nc = kcmp.shape[1]
    tq = min(TQ_CMP, T)
    return pl.pallas_call(
        functools.partial(_cmp_sel_kernel, tq=tq, q0=q0, nc=nc, ns=ns, ns_pad=ns_pad),
        out_shape=[jax.ShapeDtypeStruct((B, T, MIX), F32), jax.ShapeDtypeStruct((B, T, NSA_GROUPS * ns_pad), F32)],
        grid=(B, T // tq),
        in_specs=[pl.BlockSpec((None, tq, MIX), lambda b, i: (b, i, 0)),
                  pl.BlockSpec((None, nc, LANES), lambda b, i: (b, 0, 0)),
                  pl.BlockSpec((None, nc, LANES), lambda b, i: (b, 0, 0)),
                  pl.BlockSpec((1, nc), lambda b, i: (0, 0))],
        out_specs=[pl.BlockSpec((None, tq, MIX), lambda b, i: (b, i, 0)),
                   pl.BlockSpec((None, tq, NSA_GROUPS * ns_pad), lambda b, i: (b, i, 0))],
        compiler_params=_cp("parallel", "arbitrary"),
    )(q3, kcmp, vcmp, cend)


def _flash_kernel(*refs, cfg):
    mode, layout, tq, tk = cfg['mode'], cfg['layout'], cfg['tq'], cfg['tk']
    q0, kbase, pps, has_tail, tail_pos0 = cfg['q0'], cfg['kbase'], cfg['pps'], cfg['has_tail'], cfg['tail_pos0']
    ns_pad, n_pref, qw, kv_t = cfg['ns_pad'], cfg['n_pref'], cfg['qw'], cfg['kv_t']
    head_map = NSA_HEAD_MAP if layout == 'nsa' else FOX_HEAD_MAP
    nkb = 1 if layout in ('nsa', 'fox_wide') else FOX_HEADS // 2
    n_heads_kb = FOX_HEADS if layout == 'fox_wide' else len(head_map[0])
    rows = n_heads_kb * tq
    qi_ref, kj_ref, fl_ref = refs[:3]
    pos = n_pref
    q_ref = refs[pos]; pos += 1
    npg = max(pps, 1)
    k_refs = refs[pos:pos + npg]; pos += npg
    v_refs = refs[pos:pos + npg]; pos += npg
    if has_tail:
        tk_ref, tv_ref = refs[pos:pos + 2]; pos += 2
    c_ref = ct_ref = sel_ref = None
    if cfg['has_caug'] or cfg['has_crow']:
        c_ref = refs[pos]; pos += 1
        if has_tail:
            ct_ref = refs[pos]; pos += 1
    if mode == 'sel':
        sel_ref = refs[pos]; pos += 1
    o_ref = refs[pos]; pos += 1
    qst_scr, m_scr, l_scr, acc_scr = refs[pos:pos + 4]

    s_id = pl.program_id(1)
    qi = qi_ref[s_id]
    kj = kj_ref[s_id]
    fl = fl_ref[s_id]
    scale = NSA_DH ** -0.5 * LOG2E

    @pl.when((fl & 1) != 0)
    def _init():
        qblk = q_ref[...]
        if layout == 'fox_wide':
            qst_scr[0] = _stack_wide(qblk, FOX_HEADS, scale).astype(BF16)
        elif layout == 'nsa':
            qbd = _stack_heads(qblk, head_map[0], scale).astype(BF16)
            if mode == 'sel':
                selblk = sel_ref[...]
                hpg = NSA_HEADS // NSA_GROUPS
                parts = []
                for g in range(NSA_GROUPS):
                    off = ((selblk[:, g * ns_pad:(g + 1) * ns_pad] - 1.0) * SEL_OFF).astype(BF16)
                    parts += [off] * hpg
                qbd = jnp.concatenate([qbd, jnp.concatenate(parts, axis=0)], axis=1)
            qst_scr[0] = qbd
        else:
            for kb in range(nkb):
                qbd = _stack_heads(qblk, head_map[kb], scale).astype(BF16)
                if cfg['has_caug']:
                    lane = _iota((rows, LANES), 1)
                    head = 2 * kb + (_iota((rows, LANES), 0) >= tq).astype(jnp.int32)
                    aug = jnp.where((lane >= 3 * head) & (lane < 3 * head + 3), -1.0, 0.0).astype(BF16)
                    qbd = jnp.concatenate([qbd, aug], axis=1)
                qst_scr[kb] = qbd
        m_scr[...] = jnp.full(m_scr.shape, NEG, F32)
        l_scr[...] = jnp.zeros(l_scr.shape, F32)
        acc_scr[...] = jnp.zeros(acc_scr.shape, F32)

    def update(k_tile, v_tile, kpos0, c_tile, width, masked):
        if masked:
            qpos = q0 + qi * tq + (_iota((rows, 1), 0) & (tq - 1))
            kpos = kpos0 + _iota((1, width), 1)
            mask = kpos <= qpos
            if mode == 'band':
                mask = mask & (kpos > qpos - WINDOW)
        for kb in range(nkb):
            if kv_t:
                kt = k_tile.astype(BF16)
                if mode == 'sel':
                    onehot = (_iota((ns_pad, width), 0) == (kpos0 >> 6) + (_iota((ns_pad, width), 1) >> 6))
                    kt = jnp.concatenate([kt, onehot.astype(BF16)], axis=0)
                sc = _dot(qst_scr[kb], kt)
                if c_tile is not None:
                    sc = sc - LOG2E * jnp.concatenate([jnp.broadcast_to(c_tile[h:h + 1, :], (tq, width))
                                                       for h in range(n_heads_kb)], axis=0)
            else:
                kt = k_tile[:, kb * LANES:(kb + 1) * LANES].astype(BF16)
                if mode == 'sel':
                    onehot = (_iota((width, ns_pad), 1) == (kpos0 >> 6) + (_iota((width, ns_pad), 0) >> 6))
                    kt = jnp.concatenate([kt, onehot.astype(BF16)], axis=1)
                if c_tile is not None:
                    kt = jnp.concatenate([kt, c_tile], axis=1)
                sc = _dot_nt(qst_scr[kb], kt)
            if masked:
                sc = jnp.where(mask, sc, NEG)
            m_old = m_scr[kb]
            m_new = jnp.maximum(m_old, jnp.max(sc, axis=1, keepdims=True))
            alpha = jnp.exp2(m_old - m_new)
            p = jnp.exp2(sc - _lane_repeat(m_new, width))
            l_scr[kb] = alpha * l_scr[kb] + jnp.sum(p, axis=1, keepdims=True)
            if kv_t:
                pv = _dot_nt(p.astype(BF16), v_tile.astype(BF16))
            else:
                pv = _dot(p.astype(BF16), v_tile[:, kb * LANES:(kb + 1) * LANES].astype(BF16))
            acc_scr[kb] = _lane_repeat(alpha, pv.shape[1]) * acc_scr[kb] + pv
            m_scr[kb] = m_new

    def main_update(masked):
        if pps > 1:
            k_tile = jnp.concatenate([r[...] for r in k_refs], axis=1)
            v_tile = jnp.concatenate([r[...] for r in v_refs], axis=1)
        else:
            k_tile, v_tile = k_refs[0][...], v_refs[0][...]
        update(k_tile, v_tile, kbase + kj * tk, c_ref[...] if c_ref is not None else None, tk, masked)

    is_tail = (fl & 4) != 0
    need_mask = (fl & 8) != 0
    if has_tail:
        @pl.when(is_tail)
        def _():
            update(tk_ref[...], tv_ref[...], tail_pos0, ct_ref[...] if ct_ref is not None else None, PAGE, True)

    @pl.when(jnp.logical_and(jnp.logical_not(is_tail), need_mask))
    def _():
        main_update(True)

    @pl.when(jnp.logical_and(jnp.logical_not(is_tail), jnp.logical_not(need_mask)))
    def _():
        main_update(False)

    @pl.when((fl & 2) != 0)
    def _fin():
        fulls = [acc_scr[kb] / _lane_repeat(jnp.maximum(l_scr[kb], 1e-30), acc_scr.shape[2]) for kb in range(nkb)]
        if layout == 'fox_wide':
            o_ref[...] = _unstack_wide(fulls[0], FOX_HEADS, tq)
        else:
            outs = _unstack_heads(fulls, head_map, tq, qw // LANES)
            for c, blk in enumerate(outs):
                o_ref[:, c * LANES:(c + 1) * LANES] = blk


def _schedule(nq, tq, tk, q0, kbase, n_ktiles, mode, has_tail):
    qi_l, kj_l, fl_l = [], [], []
    for qi in range(nq):
        q_lo, q_hi = q0 + qi * tq, q0 + (qi + 1) * tq - 1
        entries = []
        for kj in range(n_ktiles):
            k_lo, k_hi = kbase + kj * tk, kbase + (kj + 1) * tk - 1
            if k_lo > q_hi:
                continue
            if mode == 'band' and k_hi <= q_lo - WINDOW:
                continue
            all_visible = k_hi <= q_lo and (mode != 'band' or k_lo > q_hi - WINDOW)
            entries.append((kj, 0 if all_visible else 8))
        if has_tail:
            entries.append((entries[-1][0] if entries else 0, 4 | 8))
        for n, (kj, f) in enumerate(entries):
            f |= (1 if n == 0 else 0) | (2 if n == len(entries) - 1 else 0)
            qi_l.append(qi); kj_l.append(kj); fl_l.append(f)
    return (np.asarray(qi_l, np.int32), np.asarray(kj_l, np.int32), np.asarray(fl_l, np.int32))


def _flash(q3, k3, v3, *, mode, layout, tq, tk, q0, kbase=0, page_table=None, pps=0, tail=None, tail_pos0=0,
           caug=None, crow=None, crow_tail=None, sel=None, ns_pad=0):
    B, Tq, qw = q3.shape
    nq = Tq // tq
    paged = page_table is not None
    has_tail = tail is not None
    if paged:
        kw = k3.shape[1]
        npages = page_table.shape[1]
        tk = pps * PAGE
        n_ktiles = npages // pps
    else:
        kw = k3.shape[-1]
        n_ktiles = k3.shape[1] // tk
    qi_t, kj_t, fl_t = _schedule(nq, tq, tk, q0, kbase, n_ktiles, mode, has_tail)
    nsteps = len(qi_t)
    prefetch = [jnp.asarray(qi_t), jnp.asarray(kj_t), jnp.asarray(fl_t)]
    if paged:
        prefetch.append(page_table.reshape(-1).astype(jnp.int32))
    n_pref = len(prefetch)
    nkb = 1 if layout in ('nsa', 'fox_wide') else FOX_HEADS // 2
    n_heads_kb = FOX_HEADS if layout == 'fox_wide' else (NSA_HEADS if layout == 'nsa' else 2)
    rows = n_heads_kb * tq
    cq = kw if layout == 'fox_wide' else LANES
    if mode == 'sel':
        cq += ns_pad
    if caug is not None:
        cq += LANES
    cv = kw if layout == 'fox_wide' else LANES

    in_specs = [pl.BlockSpec((None, tq, qw), lambda b, s, qi, kj, fl, *_: (b, qi[s], 0))]
    args = [q3]
    if paged:
        def page_map(i):
            return lambda b, s, qi, kj, fl, pt: (pt[b * npages + kj[s] * pps + i], 0, 0)
        for arr in (k3, v3):
            for i in range(pps):
                in_specs.append(pl.BlockSpec((None, kw, PAGE), page_map(i)))
                args.append(arr)
    else:
        for arr in (k3, v3):
            in_specs.append(pl.BlockSpec((None, tk, kw), lambda b, s, qi, kj, fl, *_: (b, kj[s], 0)))
            args.append(arr)
    if has_tail:
        for arr in tail:
            in_specs.append(pl.BlockSpec((None, kw, PAGE), lambda b, s, qi, kj, fl, *_: (b, 0, 0)))
            args.append(arr)
    if caug is not None:
        in_specs.append(pl.BlockSpec((None, tk, LANES), lambda b, s, qi, kj, fl, *_: (b, kj[s], 0)))
        args.append(caug)
    if crow is not None:
        in_specs.append(pl.BlockSpec((None, 8, tk), lambda b, s, qi, kj, fl, *_: (b, 0, kj[s])))
        args.append(crow)
        if has_tail:
            in_specs.append(pl.BlockSpec((None, 8, PAGE), lambda b, s, qi, kj, fl, *_: (b, 0, 0)))
            args.append(crow_tail)
    if mode == 'sel':
        in_specs.append(pl.BlockSpec((None, tq, NSA_GROUPS * ns_pad), lambda b, s, qi, kj, fl, *_: (b, qi[s], 0)))
        args.append(sel)
    cfg = dict(mode=mode, layout=layout, tq=tq, tk=tk, q0=q0, kbase=kbase, pps=pps if paged else 0,
               has_tail=has_tail, tail_pos0=tail_pos0, ns_pad=ns_pad, n_pref=n_pref, qw=qw, kv_t=paged,
               has_caug=caug is not None, has_crow=crow is not None)
    return pl.pallas_call(
        functools.partial(_flash_kernel, cfg=cfg),
        out_shape=jax.ShapeDtypeStruct((B, Tq, qw), F32),
        grid_spec=pltpu.PrefetchScalarGridSpec(
            num_scalar_prefetch=n_pref,
            grid=(B, nsteps),
            in_specs=in_specs,
            out_specs=pl.BlockSpec((None, tq, qw), lambda b, s, qi, kj, fl, *_: (b, qi[s], 0)),
            scratch_shapes=[pltpu.VMEM((nkb, rows, cq), BF16), pltpu.VMEM((nkb, rows, LANES), F32),
                            pltpu.VMEM((nkb, rows, LANES), F32), pltpu.VMEM((nkb, rows, cv), F32)]),
        compiler_params=_cp("parallel", "arbitrary"),
    )(*prefetch, *args)


def _nsa_combine_kernel(oc_ref, os_ref, ow_ref, sm_ref, e_ref, o_ref):
    gate = _sigmoid(sm_ref[...])
    acc = None
    for j, br in enumerate((oc_ref, os_ref, ow_ref)):
        gexp = _dot_r01(gate, e_ref[j])
        acc = gexp * br[...] if acc is None else acc + gexp * br[...]
    o_ref[...] = acc


def _nsa_combine(o_cmp, o_slc, o_win, small):
    n = o_cmp.shape[0]
    tm = min(512, n)
    src = np.zeros((3, LANES, MIX), np.float32)
    for h in range(NSA_HEADS):
        for j in range(3):
            src[j, 2 * M_HEADS + 3 * h + j, h * NSA_DH:(h + 1) * NSA_DH] = 1.0
    spec = pl.BlockSpec((tm, MIX), lambda i: (i, 0))
    return pl.pallas_call(
        _nsa_combine_kernel, out_shape=jax.ShapeDtypeStruct((n, MIX), F32), grid=(n // tm,),
        in_specs=[spec, spec, spec, pl.BlockSpec((tm, LANES), lambda i: (i, 0)),
                  pl.BlockSpec((3, LANES, MIX), lambda i: (0, 0, 0))],
        out_specs=spec, compiler_params=_cp("parallel"))(o_cmp, o_slc, o_win, small, jnp.asarray(src, BF16))


def _fox_cumsum_kernel(sm_ref, bias_ref, place_ref, lf_ref, caug_ref, carry, *, tt):
    @pl.when(pl.program_id(1) == 0)
    def _():
        carry[...] = jnp.zeros(carry.shape, F32)

    lf = _log_sigmoid(sm_ref[...] + bias_ref[...])
    lf_ref[...] = lf
    lower = (_iota((tt, tt), 1) <= _iota((tt, tt), 0)).astype(F32)
    c = _dot_l01(lower, lf) + carry[0:1, :]
    carry[...] = jnp.broadcast_to(c[tt - 1:tt, :], carry.shape)
    c = c * LOG2E
    c_hi = c.astype(BF16)
    r1 = c - c_hi.astype(F32)
    c_mid = r1.astype(BF16)
    c_lo = (r1 - c_mid.astype(F32)).astype(BF16)
    caug = _dot(c_hi, place_ref[0]) + _dot(c_mid, place_ref[1]) + _dot(c_lo, place_ref[2])
    caug_ref[...] = caug.astype(BF16)


def _fox_cumsum(small, bias_row, B, T):
    tt = min(256, T)
    row = lambda b, i: (b * (T // tt) + i, 0)
    place = np.zeros((3, LANES, LANES), np.float32)
    for h in range(FOX_HEADS):
        for j in range(3):
            place[j, h, 3 * h + j] = 1.0
    return pl.pallas_call(
        functools.partial(_fox_cumsum_kernel, tt=tt),
        out_shape=[jax.ShapeDtypeStruct((B * T, LANES), F32), jax.ShapeDtypeStruct((B * T, LANES), BF16)],
        grid=(B, T // tt),
        in_specs=[pl.BlockSpec((tt, LANES), row), pl.BlockSpec((1, LANES), lambda b, i: (0, 0)),
                  pl.BlockSpec((3, LANES, LANES), lambda b, i: (0, 0, 0))],
        out_specs=[pl.BlockSpec((tt, LANES), row), pl.BlockSpec((tt, LANES), row)],
        scratch_shapes=[pltpu.VMEM((8, LANES), F32)],
        compiler_params=_cp("parallel", "arbitrary"),
    )(small, bias_row, jnp.asarray(place, BF16))


def _fox_cumsum_paged_kernel(pt_ref, *refs, pps, n_main):
    lf_refs = refs[:pps]
    sm_ref, bias_ref, c_ref, lfn_ref, carry = refs[pps:]
    j = pl.program_id(1)

    @pl.when(j == 0)
    def _():
        carry[...] = jnp.zeros(carry.shape, F32)

    upper = (_iota((PAGE, PAGE), 0) <= _iota((PAGE, PAGE), 1)).astype(F32)

    @pl.when(j < n_main)
    def _():
        run = carry[:, 0:1]
        for i in range(pps):
            c = _dot_r01(lf_refs[i][...], upper) + run
            c_ref[:, i * PAGE:(i + 1) * PAGE] = c
            run = c[:, PAGE - 1:PAGE]
        carry[...] = jnp.broadcast_to(run, carry.shape)

    @pl.when(j == n_main)
    def _():
        lf = _log_sigmoid(sm_ref[...] + bias_ref[...])
        lfn_ref[...] = lf
        eye8 = (_iota((8, LANES), 0) == _iota((8, LANES), 1)).astype(F32)
        c_new = _dot_r01(_dot_nt_l01(eye8, lf), upper) + carry[:, 0:1]
        for i in range(pps):
            c_ref[:, i * PAGE:(i + 1) * PAGE] = c_new


def _fox_cumsum_paged(lf_pool_t, page_table, small_pad, bias_row, pps):
    B, npages = page_table.shape
    n_main = npages // pps

    def page_map(i):
        return lambda b, j, pt: (pt[b * npages + jnp.minimum(j, n_main - 1) * pps + i], 0, 0)

    return pl.pallas_call(
        functools.partial(_fox_cumsum_paged_kernel, pps=pps, n_main=n_main),
        out_shape=[jax.ShapeDtypeStruct((B, 8, (n_main + 1) * pps * PAGE), F32),
                   jax.ShapeDtypeStruct((B, PAGE, LANES), F32)],
        grid_spec=pltpu.PrefetchScalarGridSpec(
            num_scalar_prefetch=1, grid=(B, n_main + 1),
            in_specs=[pl.BlockSpec((None, 8, PAGE), page_map(i)) for i in range(pps)]
            + [pl.BlockSpec((None, PAGE, LANES), lambda b, j, pt: (b, 0, 0)),
               pl.BlockSpec((1, LANES), lambda b, j, pt: (0, 0))],
            out_specs=[pl.BlockSpec((None, 8, pps * PAGE), lambda b, j, pt: (b, 0, j)),
                       pl.BlockSpec((None, PAGE, LANES), lambda b, j, pt: (b, 0, 0))],
            scratch_shapes=[pltpu.VMEM((8, LANES), F32)]),
        compiler_params=_cp("parallel", "arbitrary"),
    )(page_table.reshape(-1).astype(jnp.int32), *([lf_pool_t] * pps), small_pad, bias_row)


def _rwkv_prep_kernel(rw_ref, prev_ref, sh_ref, mu_ref, w0_ref, w2_ref, a0_ref, a2_ref, g2_ref, kk_ref, ka_ref,
                      bd_ref, r_o, lw_o, k_o, v_o, kk_o, a_o, g_o, *, tt):
    i = pl.program_id(1)
    rw = rw_ref[...]
    first_prev = jnp.where(i == 0, sh_ref[...], prev_ref[7:8, :])
    prev = jnp.where(_iota(rw.shape, 0) == 0, first_prev, pltpu.roll(rw, 1, 0))
    m = rw + mu_ref[...] * (prev - rw)
    r, kr, vr = m[:, 0:MIX], m[:, MIX:2 * MIX], m[:, 2 * MIX:3 * MIX]
    lora = m[:, 3 * MIX:3 * MIX + LANES]
    gl = m[:, 3 * MIX + LANES:3 * MIX + 2 * LANES]
    w = -_softplus(-(w0_ref[...] + _dot(jnp.tanh(lora).astype(BF16), w2_ref[...]))) - 0.5
    a = _sigmoid(a0_ref[...] + _dot(lora.astype(BF16), a2_ref[...]))
    kk = kr * kk_ref[...]
    ss = _dot_r01(kk * kk, bd_ref[...])
    r_o[...] = r
    lw_o[...] = -jnp.exp(w)
    k_o[...] = kr * (1.0 + (a - 1.0) * ka_ref[...])
    v_o[...] = vr
    kk_o[...] = kk / jnp.maximum(jnp.sqrt(ss), 1e-12)
    a_o[...] = a
    g_o[...] = _dot(_sigmoid(gl).astype(BF16), g2_ref[...])


def _rwkv_prep(rw3, shift0, P):
    B, T, wdt = rw3.shape
    tt = min(256, T)
    nblk8 = tt // 8
    row3 = lambda b, i: (b, i, 0)
    vec = lambda n: pl.BlockSpec((1, n), lambda b, i: (0, 0))
    outs = pl.pallas_call(
        functools.partial(_rwkv_prep_kernel, tt=tt),
        out_shape=[jax.ShapeDtypeStruct((B, T, MIX), F32)] * 7,
        grid=(B, T // tt),
        in_specs=[pl.BlockSpec((None, tt, wdt), row3),
                  pl.BlockSpec((None, 8, wdt), lambda b, i: (b, jnp.maximum(i * nblk8 - 1, 0), 0)),
                  pl.BlockSpec((None, 1, wdt), lambda b, i: (b, 0, 0)),
                  vec(wdt), vec(MIX), pl.BlockSpec((LANES, MIX), lambda b, i: (0, 0)),
                  vec(MIX), pl.BlockSpec((LANES, MIX), lambda b, i: (0, 0)),
                  pl.BlockSpec((LANES, MIX), lambda b, i: (0, 0)), vec(MIX), vec(MIX),
                  pl.BlockSpec((MIX, MIX), lambda b, i: (0, 0))],
        out_specs=[pl.BlockSpec((None, tt, MIX), row3)] * 7,
        compiler_params=_cp("parallel", "arbitrary"),
    )(rw3, rw3, shift0[:, None, :], P['mu'], P['w0'], P['w2'], P['a0'], P['a2'], P['g2'], P['k_k'], P['k_a'], P['bd64'])
    return outs


def _rwkv_scan_kernel(r_ref, lw_ref, k_ref, v_ref, kk_ref, a_ref, g_ref, s0_ref, rk_ref, lng_ref, lnb_ref,
                      y_ref, s1_ref, s_scr, *, L):
    ci = pl.program_id(1)

    @pl.when(ci == 0)
    def _():
        s_scr[...] = s0_ref[...]

    r_i = _iota((L, L), 0)
    c_i = _iota((L, L), 1)
    lower = (c_i <= r_i).astype(F32)
    strict = c_i < r_i
    incl = c_i <= r_i
    eye_l = (c_i == r_i).astype(F32)
    lane = _iota((L, LANES), 1)
    half_masks = (lane < 64, lane >= 64)
    rr = _iota((LANES, LANES), 0)
    cc = _iota((LANES, LANES), 1)
    bdmask = (rr >= 64) == (cc >= 64)
    eye128 = rr == cc
    bd_f = bdmask.astype(F32)
    pairs = range(RWKV_HEADS // 2)
    sls = [slice(pr * LANES, (pr + 1) * LANES) for pr in pairs]
    P = []
    for sl in sls:
        lw = lw_ref[:, sl]
        cl = _dot_l01(lower, lw)
        w_in = jnp.exp(cl)
        w_out = jnp.exp(-cl)
        kk = kk_ref[:, sl]
        P.append(dict(w_in=w_in, kap=kk * jnp.exp(cl - lw), bet=kk * a_ref[:, sl] * w_out,
                      kt=k_ref[:, sl] * w_out, rt=r_ref[:, sl] * w_in, v=v_ref[:, sl]))
    for pr, d in zip(pairs, P):
        d['sbd'] = s_scr[pr]
        d['rhs'] = _bdot(d['kap'], d['sbd'])
        d['ys'] = _bdot(d['rt'], d['sbd'])
    H = []
    for d in P:
        for hm in half_masks:
            kap_h = jnp.where(hm, d['kap'], 0.0)
            rt_h = jnp.where(hm, d['rt'], 0.0)
            amat = jnp.where(strict, _bdot_nt(kap_h, d['bet']), 0.0)
            H.append(dict(d=d, hm=hm, pw=amat, tinv=eye_l - amat,
                          cmat=jnp.where(strict, _bdot_nt(kap_h, d['kt']), 0.0),
                          rb=jnp.where(incl, _bdot_nt(rt_h, d['bet']), 0.0),
                          rk=jnp.where(incl, _bdot_nt(rt_h, d['kt']), 0.0)))
    def same_block(shift):
        return (r_i >> shift) == (c_i >> shift)

    for h in H:
        h['amat'] = h['pw']
        h['pw'] = jnp.where(same_block(3), h['amat'], 0.0)
        h['tinv'] = eye_l - h['pw']
    for _ in range(2):
        for h in H:
            h['pw'] = _dot3(h['pw'], h['pw'])
        for h in H:
            h['tinv'] = h['tinv'] + _dot3(h['tinv'], h['pw'])
    shift = 3
    while (1 << shift) < L:
        off_diag = jnp.logical_and(same_block(shift + 1), jnp.logical_not(same_block(shift)))
        for h in H:
            h['x'] = _dot3(jnp.where(off_diag, h['amat'], 0.0), h['tinv'])
        for h in H:
            h['tinv'] = h['tinv'] - _dot3(h['tinv'], h['x'])
        shift += 1
    for h in H:
        d, hm = h['d'], h['hm']
        h['v_h'] = jnp.where(hm, d['v'], 0.0)
        h['rhs_h'] = jnp.where(hm, d['rhs'], 0.0) + _bdot(h['cmat'], h['v_h'])
    for h in H:
        h['u_h'] = -_dot3(h['tinv'], h['rhs_h'])
    for h in H:
        h['y_h'] = _bdot(h['rb'], h['u_h']) + _bdot(h['rk'], h['v_h'])
    for pr, (d, sl) in enumerate(zip(P, sls)):
        h0, h1 = H[2 * pr], H[2 * pr + 1]
        u = h0['u_h'] + h1['u_h']
        y = d['ys'] + h0['y_h'] + h1['y_h']
        upd = jnp.where(bdmask, _bdot_tn(d['bet'], u) + _bdot_tn(d['kt'], d['v']), 0.0)
        w_end = jnp.broadcast_to(d['w_in'][L - 1:L, :], (LANES, LANES))
        w_end_col = jnp.sum(jnp.where(eye128, w_end, 0.0), axis=1, keepdims=True)
        s_scr[pr] = (d['sbd'] + upd) * w_end_col
        d['y'] = y
    for d, sl in zip(P, sls):
        y = d['y']
        mean = _dot_r01(y, bd_f) * (1.0 / RWKV_N)
        yc = y - mean
        var = _dot_r01(yc * yc, bd_f) * (1.0 / RWKV_N)
        yn = yc * lax.rsqrt(var + 64e-5) * lng_ref[:, sl] + lnb_ref[:, sl]
        bonus = _dot_r01(r_ref[:, sl] * k_ref[:, sl] * rk_ref[:, sl], bd_f)
        y_ref[:, sl] = (yn + bonus * d['v']) * g_ref[:, sl]

    @pl.when(ci == pl.num_programs(1) - 1)
    def _():
        s1_ref[...] = s_scr[...]


def _rwkv_scan(r, lw, k, v, kk, a, g, s0, P, B, T):
    L = RWKV_CHUNK
    tp = -(-T // L) * L
    if tp != T:
        pad = lambda t: jnp.pad(t, ((0, 0), (0, tp - T), (0, 0)))
        r, lw, k, v, kk, a, g = (pad(t) for t in (r, lw, k, v, kk, a, g))
    nch = tp // L
    st = jnp.swapaxes(s0.astype(F32), -1, -2).reshape(B, RWKV_HEADS // 2, 2, RWKV_N, RWKV_N)
    z = jnp.zeros_like(st[:, :, 0])
    sbd0 = jnp.concatenate([jnp.concatenate([st[:, :, 0], z], -1), jnp.concatenate([z, st[:, :, 1]], -1)], -2)
    row3 = lambda b, c: (b, c, 0)
    st_spec = pl.BlockSpec((None, RWKV_HEADS // 2, LANES, LANES), lambda b, c: (b, 0, 0, 0))
    vec = pl.BlockSpec((1, MIX), lambda b, c: (0, 0))
    y, sbd1 = pl.pallas_call(
        functools.partial(_rwkv_scan_kernel, L=L),
        out_shape=[jax.ShapeDtypeStruct((B, tp, MIX), F32),
                   jax.ShapeDtypeStruct((B, RWKV_HEADS // 2, LANES, LANES), F32)],
        grid=(B, nch),
        in_specs=[pl.BlockSpec((None, L, MIX), row3)] * 7 + [st_spec, vec, vec, vec],
        out_specs=[pl.BlockSpec((None, L, MIX), row3), st_spec],
        scratch_shapes=[pltpu.VMEM((RWKV_HEADS // 2, LANES, LANES), F32)],
        compiler_params=_cp("parallel", "arbitrary"),
    )(r, lw, k, v, kk, a, g, sbd0, P['r_k'], P['ln_g'], P['ln_b'])
    s_e = sbd1[:, :, :RWKV_N, :RWKV_N]
    s_o = sbd1[:, :, RWKV_N:, RWKV_N:]
    s1 = jnp.swapaxes(jnp.stack([s_e, s_o], 2).reshape(B, RWKV_HEADS, RWKV_N, RWKV_N), -1, -2)
    return y[:, :T], s1


def _mem_kernel(x_ref, km_ref, vm_ref, wq_ref, wo_ref, g_ref, b_ref, o_ref):
    x = x_ref[...]
    q = _dot(x.astype(BF16), wq_ref[...])
    heads = []
    for h in range(MEM_HEADS):
        sl = slice(h * MEM_DH, (h + 1) * MEM_DH)
        s = _dot_nt((q[:, sl] * (MEM_DH ** -0.5)).astype(BF16), km_ref[:, sl].astype(BF16))
        e = jnp.exp(s - jnp.max(s, axis=1, keepdims=True))
        p = e / jnp.sum(e, axis=1, keepdims=True)
        heads.append(_dot(p.astype(BF16), vm_ref[:, sl].astype(BF16)))
    o = jnp.concatenate(heads, axis=1)
    y = _dot(o.astype(BF16), wo_ref[...])
    o_ref[...] = _layer_norm_rows(ALPHA * x + y, g_ref[...], b_ref[...])


def _mem_layer(x3, km, vm, wq, wo, g, b):
    B, T, d = x3.shape
    tq = min(512, T)
    mt = km.shape[1]
    full = lambda shape: pl.BlockSpec(shape, lambda bb, i: (0,) * len(shape))
    return pl.pallas_call(
        _mem_kernel, out_shape=jax.ShapeDtypeStruct((B, T, d), F32), grid=(B, T // tq),
        in_specs=[pl.BlockSpec((None, tq, d), lambda bb, i: (bb, i, 0)),
                  pl.BlockSpec((None, mt, d), lambda bb, i: (bb, 0, 0)),
                  pl.BlockSpec((None, mt, d), lambda bb, i: (bb, 0, 0)),
                  full((d, d)), full((d, d)), full((1, d)), full((1, d))],
        out_specs=pl.BlockSpec((None, tq, d), lambda bb, i: (bb, i, 0)),
        compiler_params=_cp("parallel", "arbitrary"),
    )(x3, km, vm, wq, wo, g.reshape(1, d), b.reshape(1, d))


def _moe_kernel(x_ref, wr_ref, br_ref, w1_ref, w3_ref, w2_ref, g_ref, b_ref, o_ref, xb_scr, gw_scr, acc_scr):
    e = pl.program_id(1)

    @pl.when(e == 0)
    def _route():
        x = x_ref[...]
        xb_scr[...] = x.astype(BF16)
        logit = _dot(x, wr_ref[...], HI) + br_ref[...]
        lane = _iota(logit.shape, 1)
        lane_f = lane.astype(F32)
        is_g = lane < N_GROUPS
        gl = jnp.where(is_g, logit, -jnp.inf)
        gmax = jnp.max(gl, axis=1, keepdims=True)
        grp = jnp.min(jnp.where(gl == gmax, lane_f, 1e9), axis=1, keepdims=True)
        g_w = 1.0 / jnp.sum(jnp.where(is_g, jnp.exp(gl - gmax), 0.0), axis=1, keepdims=True)
        lo = N_GROUPS + EXP_PER_GROUP * grp
        in_grp = (lane_f >= lo) & (lane_f < lo + EXP_PER_GROUP)
        el = jnp.where(in_grp, logit, -jnp.inf)
        emax = jnp.max(el, axis=1, keepdims=True)
        ee = jnp.where(in_grp, jnp.exp(el - emax), 0.0)
        p = jnp.where(in_grp, ee / jnp.sum(ee, axis=1, keepdims=True), -1.0)
        p1 = jnp.max(p, axis=1, keepdims=True)
        i1 = jnp.min(jnp.where(p == p1, lane_f, 1e9), axis=1, keepdims=True)
        pr = jnp.where(lane_f == i1, -1.0, p)
        p2 = jnp.max(pr, axis=1, keepdims=True)
        i2 = jnp.min(jnp.where(pr == p2, lane_f, 1e9), axis=1, keepdims=True)
        den = p1 + p2
        gw_scr[...] = jnp.where(lane_f == i1, g_w * p1 / den, 0.0) + jnp.where(lane_f == i2, g_w * p2 / den, 0.0)
        acc_scr[...] = jnp.zeros(acc_scr.shape, F32)

    gw = gw_scr[...]
    gcol = jnp.sum(jnp.where(_iota(gw.shape, 1) == e + N_GROUPS, gw, 0.0), axis=1, keepdims=True)
    xb = xb_scr[...]
    h1 = _dot(xb, w1_ref[...])
    h3 = _dot(xb, w3_ref[...])
    hid = h1 * _sigmoid(h1) * h3
    acc_scr[...] += gcol * _dot(hid.astype(BF16), w2_ref[...])

    @pl.when(e == pl.num_programs(1) - 1)
    def _():
        o_ref[...] = _layer_norm_rows(ALPHA * x_ref[...] + acc_scr[...], g_ref[...], b_ref[...])


def _moe_layer(x, M, g, b):
    n, d = x.shape
    tm = min(1024, n)
    full2 = lambda shape: pl.BlockSpec(shape, lambda i, e: (0, 0))
    return pl.pallas_call(
        _moe_kernel, out_shape=jax.ShapeDtypeStruct((n, d), F32), grid=(n // tm, N_EXPERTS),
        in_specs=[pl.BlockSpec((tm, d), lambda i, e: (i, 0)), full2((d, LANES)), full2((1, LANES)),
                  pl.BlockSpec((None, d, D_EXPERT), lambda i, e: (e, 0, 0)),
                  pl.BlockSpec((None, d, D_EXPERT), lambda i, e: (e, 0, 0)),
                  pl.BlockSpec((None, D_EXPERT, d), lambda i, e: (e, 0, 0)),
                  full2((1, d)), full2((1, d))],
        out_specs=pl.BlockSpec((tm, d), lambda i, e: (i, 0)),
        scratch_shapes=[pltpu.VMEM((tm, d), BF16), pltpu.VMEM((tm, LANES), F32), pltpu.VMEM((tm, d), F32)],
        compiler_params=_cp("parallel", "arbitrary"),
    )(x, M['wr'], M['br'], M['w1'], M['w3'], M['w2'], g.reshape(1, d), b.reshape(1, d))


def _moe_route_kernel(x_ref, wr_ref, br_ref, gw_ref, cnt_ref):
    x = x_ref[...]
    logit = _dot(x, wr_ref[...], HI) + br_ref[...]
    lane_f = _iota(logit.shape, 1).astype(F32)
    is_g = lane_f < N_GROUPS
    gl = jnp.where(is_g, logit, -jnp.inf)
    gmax = jnp.max(gl, axis=1, keepdims=True)
    grp = jnp.min(jnp.where(gl == gmax, lane_f, 1e9), axis=1, keepdims=True)
    g_w = 1.0 / jnp.sum(jnp.where(is_g, jnp.exp(gl - gmax), 0.0), axis=1, keepdims=True)
    lo = N_GROUPS + EXP_PER_GROUP * grp
    in_grp = (lane_f >= lo) & (lane_f < lo + EXP_PER_GROUP)
    el = jnp.where(in_grp, logit, -jnp.inf)
    emax = jnp.max(el, axis=1, keepdims=True)
    ee = jnp.where(in_grp, jnp.exp(el - emax), 0.0)
    p = jnp.where(in_grp, ee / jnp.sum(ee, axis=1, keepdims=True), -1.0)
    p1 = jnp.max(p, axis=1, keepdims=True)
    i1 = jnp.min(jnp.where(p == p1, lane_f, 1e9), axis=1, keepdims=True)
    pr = jnp.where(lane_f == i1, -1.0, p)
    p2 = jnp.max(pr, axis=1, keepdims=True)
    i2 = jnp.min(jnp.where(pr == p2, lane_f, 1e9), axis=1, keepdims=True)
    den = p1 + p2
    member = jnp.where(lane_f == grp, 1.0, 0.0)
    gw_ref[...] = (member + jnp.where(lane_f == i1, g_w * p1 / den, 0.0)
                   + jnp.where(lane_f == i2, g_w * p2 / den, 0.0))
    cnt_ref[...] = jnp.broadcast_to(jnp.sum(member, axis=0, keepdims=True), cnt_ref.shape)


def _moe_group_kernel(cnt_ref, x_ref, gw_ref, w1_ref, w3_ref, w2_ref, g_ref, b_ref, o_ref,
                      xb_scr, gt_scr, rk_scr, xs_scr, gws_scr, ys_scr, acc_scr, *, tm, ch):
    i = pl.program_id(0)
    e = pl.program_id(1)
    grp = e // EXP_PER_GROUP
    cnt = cnt_ref[i * N_GROUPS + grp]
    slots = [slice(c * ch, (c + 1) * ch) for c in range(tm // ch)]

    @pl.when(e == 0)
    def _():
        xb_scr[...] = x_ref[...].astype(BF16)
        acc_scr[...] = jnp.zeros(acc_scr.shape, F32)
        member = gw_ref[...].T[0:8, :]
        gt_scr[...] = member
        before = (_iota((tm, tm), 0) < _iota((tm, tm), 1)).astype(BF16)
        rk_scr[...] = _dot(member.astype(BF16), before)

    def one_hot(c):
        member = gt_scr[pl.ds(grp, 1), :]
        rank = rk_scr[pl.ds(grp, 1), :]
        want = (c * ch + _iota((ch, 1), 0)).astype(F32)
        return jnp.where((member > 0.5) & (rank == want), 1.0, 0.0).astype(BF16)

    @pl.when(e % EXP_PER_GROUP == 0)
    def _():
        for c, rs in enumerate(slots):
            @pl.when(c * ch < cnt)
            def _():
                p = one_hot(c)
                xs_scr[rs, :] = _dot(p, xb_scr[...]).astype(BF16)
                gws_scr[rs, :] = _dot_l01(p, gw_ref[...])
                ys_scr[rs, :] = jnp.zeros((ch, ys_scr.shape[1]), F32)

    for c, rs in enumerate(slots):
        @pl.when(c * ch < cnt)
        def _():
            xs = xs_scr[rs, :]
            h1 = _dot(xs, w1_ref[...])
            h3 = _dot(xs, w3_ref[...])
            hid = h1 * _sigmoid(h1) * h3
            gws = gws_scr[rs, :]
            gcol = jnp.sum(jnp.where(_iota(gws.shape, 1) == e + N_GROUPS, gws, 0.0), axis=1, keepdims=True)
            ys_scr[rs, :] += gcol * _dot(hid.astype(BF16), w2_ref[...])

    @pl.when(e % EXP_PER_GROUP == EXP_PER_GROUP - 1)
    def _():
        for c, rs in enumerate(slots):
            @pl.when(c * ch < cnt)
            def _():
                p = one_hot(c)
                hi, lo = _split_bf16(ys_scr[rs, :])
                acc_scr[...] += _dot_tn(p, hi) + _dot_tn(p, lo)

    @pl.when(e == pl.num_programs(1) - 1)
    def _():
        o_ref[...] = _layer_norm_rows(ALPHA * x_ref[...] + acc_scr[...], g_ref[...], b_ref[...])


def _moe_layer_grouped(x, M, g, b):
    n, d = x.shape
    tm = min(1024, n)
    ch = min(256, tm)
    nt = n // tm
    gw, cnt = pl.pallas_call(
        _moe_route_kernel,
        out_shape=[jax.ShapeDtypeStruct((n, LANES), F32), jax.ShapeDtypeStruct((nt, 8, LANES), F32)],
        grid=(nt,),
        in_specs=[pl.BlockSpec((tm, d), lambda i: (i, 0)), pl.BlockSpec((d, LANES), lambda i: (0, 0)),
                  pl.BlockSpec((1, LANES), lambda i: (0, 0))],
        out_specs=[pl.BlockSpec((tm, LANES), lambda i: (i, 0)), pl.BlockSpec((None, 8, LANES), lambda i: (i, 0, 0))],
        compiler_params=_cp("parallel"),
    )(x, M['wr'], M['br'])
    counts = cnt[:, 0, :N_GROUPS].astype(jnp.int32).reshape(-1)
    full2 = lambda shape: pl.BlockSpec(shape, lambda i, e, c: (0, 0))
    return pl.pallas_call(
        functools.partial(_moe_group_kernel, tm=tm, ch=ch),
        out_shape=jax.ShapeDtypeStruct((n, d), F32),
        grid_spec=pltpu.PrefetchScalarGridSpec(
            num_scalar_prefetch=1, grid=(nt, N_EXPERTS),
            in_specs=[pl.BlockSpec((tm, d), lambda i, e, c: (i, 0)), pl.BlockSpec((tm, LANES), lambda i, e, c: (i, 0)),
                      pl.BlockSpec((None, d, D_EXPERT), lambda i, e, c: (e, 0, 0)),
                      pl.BlockSpec((None, d, D_EXPERT), lambda i, e, c: (e, 0, 0)),
                      pl.BlockSpec((None, D_EXPERT, d), lambda i, e, c: (e, 0, 0)),
                      full2((1, d)), full2((1, d))],
            out_specs=pl.BlockSpec((tm, d), lambda i, e, c: (i, 0)),
            scratch_shapes=[pltpu.VMEM((tm, d), BF16), pltpu.VMEM((8, tm), F32), pltpu.VMEM((8, tm), F32),
                            pltpu.VMEM((tm, d), BF16), pltpu.VMEM((tm, LANES), F32), pltpu.VMEM((tm, d), F32),
                            pltpu.VMEM((tm, d), F32)]),
        compiler_params=_cp("parallel", "arbitrary"),
    )(counts, x, gw, M['w1'], M['w3'], M['w2'], g.reshape(1, d), b.reshape(1, d))


EVEN_SEGS = ([(MIX, False)] * 4 + [(MIX, True), (LANES, False), (LANES, False), (LANES, True), (LANES, False),
                                   (LANES, True), (LANES, False), (LANES, False)])
ODD_SEGS = [(MIX, False)] * 3 + [(RWKV_SHIFT, False), (LANES, False)]
RW_PERM = np.concatenate([np.arange(0, 512), np.arange(576, 1088), np.arange(1088, 1600), np.arange(512, 576),
                          np.arange(1600, 1664), np.arange(1664, 1792)])
RW_INV = np.argsort(RW_PERM)


def _prep_even(w_in, w_out, b_i, b_f):
    zeros = jnp.zeros((D_MODEL, LANES - 2 * M_HEADS - 3 * NSA_HEADS), F32)
    w = jnp.concatenate([w_in[:, 0:1536], w_in[:, 1544:2056], w_in[:, 2056:3336], w_in[:, 1536:1544],
                         w_in[:, 3336:3360], zeros], axis=1).astype(BF16)
    bias = jnp.concatenate([b_i, b_f, jnp.zeros((LANES - 2 * M_HEADS,), F32)]).reshape(1, LANES)
    return dict(w_in=w, w_out_a=w_out[:MIX].astype(BF16), w_out_b=w_out[MIX:].astype(BF16), gate_bias=bias)


def _prep_odd(w_in, w_out, fox_b_f, mu, w0, w2, a0, a2, g2, k_k, k_a, r_k, ln_g, ln_b):
    rw_cols = w_in[:, 1544:3336][:, RW_PERM]
    zeros = jnp.zeros((D_MODEL, LANES - FOX_HEADS), F32)
    w = jnp.concatenate([w_in[:, 0:1536], rw_cols, w_in[:, 1536:1544], zeros], axis=1).astype(BF16)
    bias = jnp.concatenate([fox_b_f, jnp.zeros((LANES - FOX_HEADS,), F32)]).reshape(1, LANES)
    z64 = jnp.zeros((64, MIX), F32)
    hd = np.arange(MIX) // RWKV_N
    bd64 = jnp.asarray((hd[:, None] == hd[None, :]).astype(np.float32), BF16)
    P = dict(mu=mu[RW_PERM].reshape(1, -1), w0=w0.reshape(1, -1), a0=a0.reshape(1, -1),
             w2=jnp.concatenate([w2, z64], 0).astype(BF16), a2=jnp.concatenate([z64, a2], 0).astype(BF16),
             g2=g2.astype(BF16), k_k=k_k.reshape(1, -1), k_a=k_a.reshape(1, -1), r_k=r_k.reshape(1, -1),
             ln_g=ln_g.reshape(1, -1), ln_b=ln_b.reshape(1, -1), bd64=bd64)
    return dict(w_in=w, w_out_a=w_out[:MIX].astype(BF16), w_out_b=w_out[MIX:].astype(BF16), fox_bias=bias, P=P)


def _prep_moe(w_group, b_group, w_expert, b_expert, w1, w3, w2):
    padw = LANES - N_GROUPS - N_EXPERTS
    wr = jnp.concatenate([w_group, w_expert, jnp.zeros((D_MODEL, padw), F32)], axis=1)
    br = jnp.concatenate([b_group, b_expert, jnp.zeros((padw,), F32)]).reshape(1, LANES)
    return dict(wr=wr, br=br, w1=w1.astype(BF16), w3=w3.astype(BF16), w2=w2.astype(BF16))


def _even_odd_blocks(kcmp):
    return jnp.concatenate([kcmp[:, 0::2], kcmp[:, 1::2]], axis=1)


def _pages_t(cache):
    pool, page, h, dh = cache.shape
    return jnp.transpose(cache, (0, 2, 3, 1)).reshape(pool, h * dh, page)


def _tail_t(rows3):
    B, T, _ = rows3.shape
    return jnp.swapaxes(jnp.pad(rows3, ((0, 0), (0, PAGE - T), (0, 0))), 1, 2)


def _even_mixer(x3, q0, W, nsa_w, norm_g, c0, n0, m0, past, ln_g, ln_b):
    B, T, D = x3.shape
    N = B * T
    x2 = x3.reshape(N, D)
    pos = q0 + jnp.arange(T)
    cos, sin = _rope_tables(pos)
    rope = (jnp.tile(cos, (B, 1)), jnp.tile(sin, (B, 1)))
    mq, mk, mv, mo, nq, kc, vc, ks, vs, kw, vw, small = _proj_split(x2, W['w_in'], EVEN_SEGS, rope)
    h, c1, n1, m1 = _mlstm(mq, mk, mv, mo, small, W['gate_bias'], norm_g, c0, n0, m0, B, T)
    pe_k, pe_v, ck1, ck2, cv1, cv2 = nsa_w
    blk_w = CMP_BLOCK * LANES
    q3 = nq.reshape(B, T, MIX)
    if past is None:
        L = T
        nc = L // CMP_BLOCK
        kcmp = _compress(kc.reshape(B * nc, blk_w), pe_k, ck1, ck2).reshape(B, nc, LANES)
        vcmp = _compress(vc.reshape(B * nc, blk_w), pe_v, cv1, cv2).reshape(B, nc, LANES)
    else:
        pt = past['page_table']
        npages = pt.shape[1]
        L = npages * PAGE + T
        nc = L // CMP_BLOCK
        per_page = PAGE // CMP_BLOCK
        pool = past['kc'].shape[0]
        kcp = _compress_pool(_pages_t(past['kc']), pe_k, ck1, ck2).reshape(pool, per_page * LANES)
        vcp = _compress_pool(_pages_t(past['vc']), pe_v, cv1, cv2).reshape(pool, per_page * LANES)
        kcmp = jnp.take(kcp, pt, axis=0).reshape(B, npages * per_page, LANES)[:, :nc]
        vcmp = jnp.take(vcp, pt, axis=0).reshape(B, npages * per_page, LANES)[:, :nc]
    cpos = jnp.arange(nc) * CMP_BLOCK + CMP_BLOCK - 1
    ccos, csin = _rope_tables(cpos)
    kcmp = _rope_rows(kcmp.reshape(B * nc, LANES), jnp.tile(ccos, (B, 1)), jnp.tile(csin, (B, 1))).reshape(B, nc, LANES)
    ns = -(-L // SEL_BLOCK)
    ns_pad = -(-ns // LANES) * LANES
    cend = jnp.concatenate([cpos[0::2], cpos[1::2]]).astype(jnp.int32).reshape(1, nc)
    o_cmp, sel = _cmp_sel(q3, _even_odd_blocks(kcmp), _even_odd_blocks(vcmp), cend, q0, ns, ns_pad)
    ks3, vs3 = ks.reshape(B, T, LANES), vs.reshape(B, T, LANES)
    kw3, vw3 = kw.reshape(B, T, LANES), vw.reshape(B, T, LANES)
    if past is None:
        o_slc = _flash(q3, ks3, vs3, mode='sel', layout='nsa', tq=min(TQ_SLC, T), tk=min(TK_SLC, T), q0=0, sel=sel,
                       ns_pad=ns_pad)
        o_win = _flash(q3, kw3, vw3, mode='band', layout='nsa', tq=min(TQ_WIN, T), tk=min(TK_WIN, T), q0=0)
        kw_new, vw_new = kw3[:, -min(WINDOW, T):], vw3[:, -min(WINDOW, T):]
    else:
        o_slc = _flash(q3, _pages_t(past['ks']), _pages_t(past['vs']), mode='sel', layout='nsa', tq=T, tk=0, q0=q0,
                       page_table=pt, pps=_largest_divisor(npages, (8, 4, 2, 1)), tail=(_tail_t(ks3), _tail_t(vs3)),
                       tail_pos0=q0, sel=sel, ns_pad=ns_pad)
        kw_ext = jnp.concatenate([past['kw_buf'], kw3], axis=1)
        vw_ext = jnp.concatenate([past['vw_buf'], vw3], axis=1)
        wb = past['kw_buf'].shape[1]
        o_win = _flash(q3, kw_ext, vw_ext, mode='band', layout='nsa', tq=T, tk=wb + T, q0=q0, kbase=q0 - wb)
        keep = min(WINDOW, wb + T)
        kw_new, vw_new = kw_ext[:, -keep:], vw_ext[:, -keep:]
    o = _nsa_combine(o_cmp.reshape(N, MIX), o_slc.reshape(N, MIX), o_win.reshape(N, MIX), small)
    x_new = _mm_res_ln([h, o], [W['w_out_a'], W['w_out_b']], x2, ln_g, ln_b).reshape(B, T, D)
    sh = (B, T, NSA_GROUPS, NSA_DH)
    wsh = (B, -1, NSA_GROUPS, NSA_DH)
    state = (c1, n1, m1, kc.reshape(sh), vc.reshape(sh), ks.reshape(sh), vs.reshape(sh),
             kw_new.reshape(wsh), vw_new.reshape(wsh))
    return x_new, state


def _odd_mixer(x3, q0, W, s0, shift0, past, ln_g, ln_b):
    B, T, D = x3.shape
    N = B * T
    x2 = x3.reshape(N, D)
    fq, fk, fv, rw, small = _proj_split(x2, W['w_in'], ODD_SEGS)
    q3, k3, v3 = (t.reshape(B, T, MIX) for t in (fq, fk, fv))
    if past is None:
        lf_full, caug = _fox_cumsum(small, W['fox_bias'], B, T)
        lf = lf_full[:, :FOX_HEADS].reshape(B, T, FOX_HEADS)
        o_c = _flash(q3, k3, v3, mode='causal', layout='fox', tq=min(TQ_FOX, T), tk=min(TK_FOX, T), q0=0,
                     caug=caug.reshape(B, T, LANES))
    else:
        pt = past['page_table']
        npages = pt.shape[1]
        pps = _largest_divisor(npages, (8, 4, 2, 1))
        padrows = ((0, 0), (0, PAGE - T), (0, 0))
        lf_pool_t = jnp.swapaxes(past['lf'], 1, 2)
        c_all, lf_new = _fox_cumsum_paged(lf_pool_t, pt, jnp.pad(small.reshape(B, T, LANES), padrows), W['fox_bias'], pps)
        lf = lf_new[:, :T, :FOX_HEADS]
        n_main = npages * PAGE
        o_c = _flash(q3, _pages_t(past['k']), _pages_t(past['v']), mode='causal', layout='fox_wide', tq=T, tk=0, q0=q0,
                     page_table=pt, pps=pps, tail=(_tail_t(k3), _tail_t(v3)), tail_pos0=q0,
                     crow=c_all, crow_tail=c_all[:, :, n_main:n_main + PAGE])
    rw3 = rw.reshape(B, T, RWKV_SHIFT)
    r, lw, k, v, kk, a, g = _rwkv_prep(rw3, shift0[:, RW_PERM], W['P'])
    y, s1 = _rwkv_scan(r, lw, k, v, kk, a, g, s0, W['P'], B, T)
    x_new = _mm_res_ln([o_c.reshape(N, MIX), y.reshape(N, MIX)], [W['w_out_a'], W['w_out_b']], x2, ln_g, ln_b)
    hs = (B, T, FOX_HEADS, FOX_DH)
    state = (fk.reshape(hs), fv.reshape(hs), lf, s1, rw3[:, -1][:, RW_INV])
    return x_new.reshape(B, T, D), state


def kernel(x_prompt, x_sample, state_mlstm_C, state_mlstm_n, state_mlstm_m, cache_nsa_kc, cache_nsa_vc, cache_nsa_ks, cache_nsa_vs, cache_nsa_kw, cache_nsa_vw, cache_fox_k, cache_fox_v, cache_fox_logf, state_rwkv_S, state_rwkv_shift, cache_mem_k, cache_mem_v, page_table, mem_prompt, even_w_in, even_w_out, mlstm_b_i, mlstm_b_f, mlstm_norm_g, nsa_pe_k, nsa_pe_v, nsa_cmp_k_w1, nsa_cmp_k_w2, nsa_cmp_v_w1, nsa_cmp_v_w2, odd_w_in, odd_w_out, fox_b_f, rwkv_mu, rwkv_w0, rwkv_w2, rwkv_a0, rwkv_a2, rwkv_g2, rwkv_k_k, rwkv_k_a, rwkv_r_k, rwkv_ln_g, rwkv_ln_b, mem_wq, mem_wk, mem_wv, mem_wo, moe_w_group, moe_b_group, moe_w_expert, moe_b_expert, moe_w1, moe_w3, moe_w2, ln_g, ln_b):
    bp, tp, d = x_prompt.shape
    bs, ts, _ = x_sample.shape
    depth = ln_g.shape[0]
    past = page_table.shape[1] * cache_nsa_kc.shape[2]
    xp, xs = x_prompt, x_sample
    acc = {}

    def push(prefix, names, vals):
        for nm, val in zip(names, vals):
            acc.setdefault(prefix + nm, []).append(val)

    even_names = ('mlstm_C', 'mlstm_n', 'mlstm_m', 'nsa_kc', 'nsa_vc', 'nsa_ks', 'nsa_vs', 'nsa_kw', 'nsa_vw')
    odd_names = ('fox_k', 'fox_v', 'fox_logf', 'rwkv_S', 'rwkv_shift')
    for layer in range(depth):
        e = layer // 2
        if layer % 2 == 0:
            W = _prep_even(even_w_in[e], even_w_out[e], mlstm_b_i[e], mlstm_b_f[e])
            nsa_w = (nsa_pe_k[e], nsa_pe_v[e], nsa_cmp_k_w1[e], nsa_cmp_k_w2[e], nsa_cmp_v_w1[e], nsa_cmp_v_w2[e])
            xp, sp = _even_mixer(xp, 0, W, nsa_w, mlstm_norm_g[e], jnp.zeros((bp, M_HEADS, M_DV, M_DK), F32),
                                 jnp.zeros((bp, M_HEADS, M_DK), F32), jnp.zeros((bp, M_HEADS), F32), None,
                                 ln_g[layer, 0], ln_b[layer, 0])
            wbuf = lambda c: c[e].reshape(bs, -1, LANES)
            past_d = dict(kc=cache_nsa_kc[e], vc=cache_nsa_vc[e], ks=cache_nsa_ks[e], vs=cache_nsa_vs[e],
                          page_table=page_table, kw_buf=wbuf(cache_nsa_kw), vw_buf=wbuf(cache_nsa_vw))
            xs, ss = _even_mixer(xs, past, W, nsa_w, mlstm_norm_g[e], state_mlstm_C[e], state_mlstm_n[e],
                                 state_mlstm_m[e], past_d, ln_g[layer, 0], ln_b[layer, 0])
            names = even_names
        else:
            W = _prep_odd(odd_w_in[e], odd_w_out[e], fox_b_f[e], rwkv_mu[e], rwkv_w0[e], rwkv_w2[e], rwkv_a0[e],
                          rwkv_a2[e], rwkv_g2[e], rwkv_k_k[e], rwkv_k_a[e], rwkv_r_k[e], rwkv_ln_g[e], rwkv_ln_b[e])
            xp, sp = _odd_mixer(xp, 0, W, jnp.zeros((bp, RWKV_HEADS, RWKV_N, RWKV_N), F32),
                                jnp.zeros((bp, RWKV_SHIFT), F32), None, ln_g[layer, 0], ln_b[layer, 0])
            past_d = dict(k=cache_fox_k[e], v=cache_fox_v[e], lf=cache_fox_logf[e], page_table=page_table)
            xs, ss = _odd_mixer(xs, past, W, state_rwkv_S[e], state_rwkv_shift[e], past_d,
                                ln_g[layer, 0], ln_b[layer, 0])
            names = odd_names
        push('p_', names, sp)
        push('s_', names, ss)
        wkv = jnp.concatenate([mem_wk[layer], mem_wv[layer]], axis=1).astype(BF16)
        mt = mem_prompt.shape[1]
        kmp, vmp = _proj_split(mem_prompt.reshape(bp * mt, d), wkv, [(d, False), (d, False)])
        push('p_', ('mem_k', 'mem_v'), (kmp.reshape(bp, mt, MEM_HEADS, MEM_DH), vmp.reshape(bp, mt, MEM_HEADS, MEM_DH)))
        wq, wo = mem_wq[layer].astype(BF16), mem_wo[layer].astype(BF16)
        xp = _mem_layer(xp, kmp.reshape(bp, mt, d), vmp.reshape(bp, mt, d), wq, wo, ln_g[layer, 1], ln_b[layer, 1])
        xs = _mem_layer(xs, cache_mem_k[layer].reshape(bs, -1, d), cache_mem_v[layer].reshape(bs, -1, d), wq, wo,
                        ln_g[layer, 1], ln_b[layer, 1])
        M = _prep_moe(moe_w_group[layer], moe_b_group[layer], moe_w_expert[layer], moe_b_expert[layer],
                      moe_w1[layer], moe_w3[layer], moe_w2[layer])
        xp = _moe_layer_grouped(xp.reshape(bp * tp, d), M, ln_g[layer, 2], ln_b[layer, 2]).reshape(bp, tp, d)
        xs = _moe_layer(xs.reshape(bs * ts, d), M, ln_g[layer, 2], ln_b[layer, 2]).reshape(bs, ts, d)

    st = {name: jnp.stack(vals) for name, vals in acc.items()}
    return (xp, xs,
            st['p_mlstm_C'], st['p_mlstm_n'], st['p_mlstm_m'],
            st['p_nsa_kc'], st['p_nsa_vc'], st['p_nsa_ks'], st['p_nsa_vs'], st['p_nsa_kw'], st['p_nsa_vw'],
            st['p_fox_k'], st['p_fox_v'], st['p_fox_logf'], st['p_rwkv_S'], st['p_rwkv_shift'],
            st['p_mem_k'], st['p_mem_v'],
            st['s_mlstm_C'], st['s_mlstm_n'], st['s_mlstm_m'],
            st['s_nsa_kc'], st['s_nsa_vc'], st['s_nsa_ks'], st['s_nsa_vs'], st['s_nsa_kw'], st['s_nsa_vw'],
            st['s_fox_k'], st['s_fox_v'], st['s_fox_logf'], st['s_rwkv_S'], st['s_rwkv_shift'])
```

```python
import functools

import numpy as np
import jax
import jax.numpy as jnp
from jax import lax
from jax.experimental import pallas as pl
from jax.experimental.pallas import tpu as pltpu

F32 = jnp.float32
BF16 = jnp.bfloat16
HI = lax.Precision.HIGHEST

D_MODEL = 1024
DEPTH = 2
PAGE = 128
M_HEADS, M_DK, M_DV, M_CHUNK = 4, 128, 128, 64
NSA_HEADS, NSA_GROUPS, NSA_DH = 8, 2, 64
CMP_BLOCK, SEL_BLOCK, N_SEL, WINDOW = 32, 64, 16, 512
FOX_HEADS, FOX_DH = 8, 64
RWKV_HEADS, RWKV_N, W_LORA, A_LORA, G_LORA = 8, 64, 64, 64, 128
RWKV_CHUNK = 64
MEM_HEADS = 4
MEM_DH = D_MODEL // MEM_HEADS
N_GROUPS, EXP_PER_GROUP, TOP_K, D_EXPERT = 4, 8, 2, 256
N_EXPERTS = N_GROUPS * EXP_PER_GROUP
ROPE_THETA = 10000.0
ALPHA = (2.0 * DEPTH) ** 0.25
LN_EPS = 1e-5
NEG = -1e30
FORCE = 1e4
MIX = 512
RWKV_SHIFT = 1792
SEL_OFF = 2.0 ** 30
LOG2E = 1.4426950408889634

LANES = 128
VMEM_LIMIT = 48 * 1024 * 1024
TQ_FOX, TK_FOX = 512, 512
TQ_SLC, TK_SLC = 256, 512
TQ_WIN, TK_WIN = 128, 128
TQ_CMP = 256


def _cp(*sem):
    return pltpu.CompilerParams(dimension_semantics=sem, vmem_limit_bytes=VMEM_LIMIT)


def _dot(a, b, prec=None):
    return jnp.dot(a, b, preferred_element_type=F32, precision=prec)


def _dot_nt(a, b, prec=None):
    return lax.dot_general(a, b, (((1,), (1,)), ((), ())), preferred_element_type=F32, precision=prec)


def _dot_tn(a, b, prec=None):
    return lax.dot_general(a, b, (((0,), (0,)), ((), ())), preferred_element_type=F32, precision=prec)


def _bdot(a, b):
    return _dot(a.astype(BF16), b.astype(BF16))


def _bdot_nt(a, b):
    return _dot_nt(a.astype(BF16), b.astype(BF16))


def _bdot_tn(a, b):
    return _dot_tn(a.astype(BF16), b.astype(BF16))


def _split_bf16(a):
    hi = a.astype(BF16)
    return hi, (a - hi.astype(F32)).astype(BF16)


def _dot3(a, b):
    ah, al = _split_bf16(a)
    bh, bl = _split_bf16(b)
    return _dot(ah, bh) + (_dot(ah, bl) + _dot(al, bh))


def _split3_bf16(a):
    hi = a.astype(BF16)
    r1 = a - hi.astype(F32)
    mid = r1.astype(BF16)
    return hi, mid, (r1 - mid.astype(F32)).astype(BF16)


def _dot_r01(a, b01):
    b = b01.astype(BF16)
    hi, mid, lo = _split3_bf16(a)
    return _dot(hi, b) + (_dot(mid, b) + _dot(lo, b))


def _dot_l01(a01, b):
    a = a01.astype(BF16)
    hi, mid, lo = _split3_bf16(b)
    return _dot(a, hi) + (_dot(a, mid) + _dot(a, lo))


def _dot_nt_l01(a01, b):
    a = a01.astype(BF16)
    hi, mid, lo = _split3_bf16(b)
    return _dot_nt(a, hi) + (_dot_nt(a, mid) + _dot_nt(a, lo))


def _iota(shape, dim):
    return lax.broadcasted_iota(jnp.int32, shape, dim)


def _log_sigmoid(x):
    return jnp.minimum(x, 0.0) - jnp.log1p(jnp.exp(-jnp.abs(x)))


def _sigmoid(x):
    return 1.0 / (1.0 + jnp.exp(-x))


def _softplus(x):
    return jnp.maximum(x, 0.0) + jnp.log1p(jnp.exp(-jnp.abs(x)))


def _layer_norm_rows(z, g, b):
    mu = jnp.mean(z, axis=-1, keepdims=True)
    zc = z - mu
    var = jnp.mean(zc * zc, axis=-1, keepdims=True)
    return zc * lax.rsqrt(var + LN_EPS) * g + b


def _rope128(y, cos, sin_signed):
    lane = _iota(y.shape, 1)
    lo = (lane & 63) < 32
    sw = jnp.where(lo, pltpu.roll(y, 96, 1), pltpu.roll(y, 32, 1))
    return y * cos + sw * sin_signed


def _rope_tables(pos):
    half = NSA_DH // 2
    freq = ROPE_THETA ** (-jnp.arange(half, dtype=F32) / half)
    ang = pos.astype(F32)[:, None] * freq[None, :]
    cos, sin = jnp.cos(ang), jnp.sin(ang)
    return jnp.tile(cos, (1, 4)), jnp.tile(jnp.concatenate([-sin, sin], -1), (1, 2))


def _largest_divisor(n, options):
    for p in options:
        if n % p == 0:
            return p


def _proj_kernel(*refs, segs, use_rope, t_segs):
    if use_rope:
        x_ref, w_ref, cos_ref, sin_ref = refs[:4]
        outs = refs[4:]
    else:
        x_ref, w_ref = refs[:2]
        outs = refs[2:]
    t_outs = dict(zip(t_segs, outs[len(segs):]))
    xb = x_ref[...].astype(BF16)
    off = 0
    for si, ((wd, rp), o_ref) in enumerate(zip(segs, outs)):
        y = _dot(xb, w_ref[:, off:off + wd])
        if rp:
            y = jnp.concatenate([_rope128(y[:, c * LANES:(c + 1) * LANES], cos_ref[...], sin_ref[...])
                                 for c in range(wd // LANES)], axis=1) if wd > LANES else \
                _rope128(y, cos_ref[...], sin_ref[...])
        o_ref[...] = y
        if si in t_outs:
            t_outs[si][...] = y.T
        off += wd


def _proj_split(x, w, segs, rope=None, t_segs=(), bt=None):
    n, k = x.shape
    tm = min(256, n)
    wtot = sum(s[0] for s in segs)
    use_rope = rope is not None
    in_specs = [pl.BlockSpec((tm, k), lambda i: (i, 0)), pl.BlockSpec((k, wtot), lambda i: (0, 0))]
    args = [x, w]
    if use_rope:
        in_specs += [pl.BlockSpec((tm, LANES), lambda i: (i, 0))] * 2
        args += list(rope)
    out_shape = [jax.ShapeDtypeStruct((n, wd), F32) for wd, _ in segs]
    out_specs = [pl.BlockSpec((tm, wd), lambda i: (i, 0)) for wd, _ in segs]
    if t_segs:
        B, T = bt
        npt = T // tm
        for si in t_segs:
            wd = segs[si][0]
            out_shape.append(jax.ShapeDtypeStruct((B, wd, T), F32))
            out_specs.append(pl.BlockSpec((None, wd, tm), lambda i: (i // npt, 0, i % npt)))
    outs = pl.pallas_call(
        functools.partial(_proj_kernel, segs=tuple(segs), use_rope=use_rope, t_segs=tuple(t_segs)),
        out_shape=out_shape,
        grid=(n // tm,),
        in_specs=in_specs,
        out_specs=out_specs,
        compiler_params=_cp("parallel"),
    )(*args)
    return (outs[:len(segs)], outs[len(segs):]) if t_segs else outs


def _mm_res_ln_kernel(*refs, n_in):
    a_refs = refs[:n_in]
    w_refs = refs[n_in:2 * n_in]
    x_ref, g_ref, b_ref, o_ref = refs[2 * n_in:]
    y = _dot(a_refs[0][...].astype(BF16), w_refs[0][...])
    for a_ref, w_ref in zip(a_refs[1:], w_refs[1:]):
        y = y + _dot(a_ref[...].astype(BF16), w_ref[...])
    o_ref[...] = _layer_norm_rows(ALPHA * x_ref[...] + y, g_ref[...], b_ref[...])


def _mm_res_ln(a_list, w_list, x, g, b):
    n, d = x.shape
    tm = min(512, n)
    n_in = len(a_list)
    in_specs = ([pl.BlockSpec((tm, a.shape[1]), lambda i: (i, 0)) for a in a_list]
                + [pl.BlockSpec(w.shape, lambda i: (0, 0)) for w in w_list]
                + [pl.BlockSpec((tm, d), lambda i: (i, 0)), pl.BlockSpec((1, d), lambda i: (0, 0)),
                   pl.BlockSpec((1, d), lambda i: (0, 0))])
    return pl.pallas_call(
        functools.partial(_mm_res_ln_kernel, n_in=n_in),
        out_shape=jax.ShapeDtypeStruct((n, d), F32),
        grid=(n // tm,),
        in_specs=in_specs,
        out_specs=pl.BlockSpec((tm, d), lambda i: (i, 0)),
        compiler_params=_cp("parallel"),
    )(*a_list, *w_list, x, g.reshape(1, d), b.reshape(1, d))


def _mlstm_kernel(q_ref, k_ref, v_ref, o_ref, sm_ref, bias_ref, g_ref, c0_ref, n0_ref, m0_ref,
                  h_ref, c1_ref, n1_ref, m1_ref, c_scr, n_scr, m_scr, *, L):
    ci = pl.program_id(1)

    @pl.when(ci == 0)
    def _():
        c_scr[...] = c0_ref[...]
        n_scr[...] = n0_ref[...]
        m_scr[...] = m0_ref[...]

    sm = sm_ref[...] + bias_ref[...]
    lane = _iota(sm.shape, 1)
    gates = jnp.where(lane < M_HEADS, sm, _log_sigmoid(sm))
    eye8 = (_iota((8, LANES), 0) == _iota((8, LANES), 1)).astype(F32)
    gates_row = _dot_nt_l01(eye8, gates)
    r_i = _iota((L, L), 0)
    c_i = _iota((L, L), 1)
    tri = c_i <= r_i
    scale = M_DK ** -0.5
    hs = []
    for h in range(M_HEADS):
        ic_col = gates[:, h:h + 1]
        f_col = gates[:, M_HEADS + h:M_HEADS + h + 1]
        ic_row = gates_row[h:h + 1, :]
        f_row = gates_row[M_HEADS + h:M_HEADS + h + 1, :]
        b_col = jnp.sum(jnp.where(tri, jnp.broadcast_to(f_row, (L, L)), 0.0), axis=1, keepdims=True)
        b_row = jnp.sum(jnp.where(r_i <= c_i, jnp.broadcast_to(f_col, (L, L)), 0.0), axis=0, keepdims=True)
        m_prev = m_scr[h:h + 1, 0:1]
        qh = q_ref[:, h * M_DK:(h + 1) * M_DK]
        kh = k_ref[:, h * M_DK:(h + 1) * M_DK] * scale
        vh = v_ref[:, h * M_DV:(h + 1) * M_DV]
        hs.append(dict(ic_col=ic_col, ic_row=ic_row, b_col=b_col, b_row=b_row, m_prev=m_prev, qh=qh, kh=kh, vh=vh,
                       qb=qh.astype(BF16), kb=kh.astype(BF16), vb=vh.astype(BF16),
                       c_old=c_scr[h], n_old=n_scr[h:h + 1, :]))
    for d in hs:
        dmat = jnp.where(tri, d['b_col'] - d['b_row'] + d['ic_row'], NEG)
        inter = d['b_col'] + d['m_prev']
        d['mt'] = jnp.maximum(inter, jnp.max(dmat, axis=1, keepdims=True))
        d['wmat'] = jnp.where(tri, jnp.exp(dmat - d['mt']), 0.0)
        d['a'] = jnp.exp(inter - d['mt'])
    for d in hs:
        d['s'] = _dot_nt(d['qb'], d['kb']) * d['wmat']
        d['cq'] = _dot_nt(d['qb'], d['c_old'].astype(BF16))
    for d in hs:
        num = d['a'] * d['cq'] + _dot(d['s'].astype(BF16), d['vb'])
        den = d['a'] * jnp.sum(d['qh'] * d['n_old'], axis=1, keepdims=True) + jnp.sum(d['s'], axis=1, keepdims=True)
        d['hh'] = num / jnp.maximum(jnp.abs(den), jnp.exp(-d['mt']))
    for h, d in enumerate(hs):
        m_l = d['mt'][L - 1:L, :]
        b_l = d['b_col'][L - 1:L, :]
        g_l = jnp.exp(b_l - d['b_col'] + d['ic_col'] - m_l)
        a_l = jnp.exp(b_l + d['m_prev'] - m_l)
        c_scr[h] = a_l * d['c_old'] + _dot_tn((g_l * d['vh']).astype(BF16), d['kb'])
        n_scr[h:h + 1, :] = a_l * d['n_old'] + jnp.sum(g_l * d['kh'], axis=0, keepdims=True)
        m_scr[h:h + 1, :] = jnp.broadcast_to(m_l, (1, LANES))
    for h, d in enumerate(hs):
        hh = d['hh']
        mu = jnp.mean(hh, axis=1, keepdims=True)
        hc = hh - mu
        var = jnp.mean(hc * hc, axis=1, keepdims=True)
        hn = hc * lax.rsqrt(var + 1e-6) * g_ref[:, h * M_DV:(h + 1) * M_DV]
        h_ref[:, h * M_DV:(h + 1) * M_DV] = hn * _sigmoid(o_ref[:, h * M_DV:(h + 1) * M_DV])

    @pl.when(ci == pl.num_programs(1) - 1)
    def _():
        c1_ref[...] = c_scr[...]
        n1_ref[...] = n_scr[...]
        m1_ref[...] = m_scr[...]


def _mlstm(mq, mk, mv, mo, small, bias_row, norm_g, c0, n0, m0, B, T):
    L = M_CHUNK if T % M_CHUNK == 0 else T
    nch = T // L
    row = lambda b, c: (b * nch + c, 0)
    st4 = lambda b, c: (b, 0, 0, 0)
    st3 = lambda b, c: (b, 0, 0)
    m0b = jnp.broadcast_to(m0[:, :, None], (B, M_HEADS, LANES)).astype(F32)
    h, c1, n1, m1 = pl.pallas_call(
        functools.partial(_mlstm_kernel, L=L),
        out_shape=[jax.ShapeDtypeStruct((B * T, MIX), F32),
                   jax.ShapeDtypeStruct((B, M_HEADS, M_DV, M_DK), F32),
                   jax.ShapeDtypeStruct((B, M_HEADS, M_DK), F32),
                   jax.ShapeDtypeStruct((B, M_HEADS, LANES), F32)],
        grid=(B, nch),
        in_specs=[pl.BlockSpec((L, MIX), row)] * 4
        + [pl.BlockSpec((L, LANES), row), pl.BlockSpec((1, LANES), lambda b, c: (0, 0)),
           pl.BlockSpec((1, MIX), lambda b, c: (0, 0)),
           pl.BlockSpec((None, M_HEADS, M_DV, M_DK), st4), pl.BlockSpec((None, M_HEADS, M_DK), st3),
           pl.BlockSpec((None, M_HEADS, LANES), st3)],
        out_specs=[pl.BlockSpec((L, MIX), row), pl.BlockSpec((None, M_HEADS, M_DV, M_DK), st4),
                   pl.BlockSpec((None, M_HEADS, M_DK), st3), pl.BlockSpec((None, M_HEADS, LANES), st3)],
        scratch_shapes=[pltpu.VMEM((M_HEADS, M_DV, M_DK), F32), pltpu.VMEM((M_HEADS, M_DK), F32),
                        pltpu.VMEM((M_HEADS, LANES), F32)],
        compiler_params=_cp("parallel", "arbitrary"),
    )(mq, mk, mv, mo, small, bias_row, norm_g.reshape(1, MIX), c0.astype(F32), n0.astype(F32), m0b)
    return h, c1, n1, m1[:, :, 0]


def _compress_weights(pe, w1, w2):
    eye = jnp.eye(NSA_GROUPS, dtype=F32)
    w1j = jnp.einsum('jdo,gh->jgdho', w1.reshape(CMP_BLOCK, NSA_DH, NSA_DH), eye)
    w1j = w1j.reshape(CMP_BLOCK, LANES, LANES).astype(BF16)
    w2b = jnp.einsum('do,gh->gdho', w2, eye).reshape(LANES, LANES).astype(BF16)
    pe2 = jnp.tile(pe, (1, NSA_GROUPS))
    return pe2, w1j, w2b


def _compress_kernel(x_ref, pe_ref, w1_ref, w2_ref, o_ref):
    xb = (x_ref[...] + pe_ref[...]).astype(BF16)
    hid = jax.nn.gelu(_dot(xb, w1_ref[...]))
    o_ref[...] = _dot(hid.astype(BF16), w2_ref[...])


def _compress(blocks, pe, w1, w2):
    r = blocks.shape[0]
    tb = min(256, r)
    width = CMP_BLOCK * LANES
    pe2, w1j, w2b = _compress_weights(pe, w1, w2)
    return pl.pallas_call(
        _compress_kernel,
        out_shape=jax.ShapeDtypeStruct((r, LANES), F32),
        grid=(pl.cdiv(r, tb),),
        in_specs=[pl.BlockSpec((tb, width), lambda i: (i, 0)), pl.BlockSpec((1, width), lambda i: (0, 0)),
                  pl.BlockSpec((width, LANES), lambda i: (0, 0)), pl.BlockSpec((LANES, LANES), lambda i: (0, 0))],
        out_specs=pl.BlockSpec((tb, LANES), lambda i: (i, 0)),
        compiler_params=_cp("parallel"),
    )(blocks, pe2.reshape(1, width), w1j.reshape(width, LANES), w2b)


def _compress_pool_kernel(xt_ref, pe_ref, w1_ref, w2_ref, o_ref, xs_scr, *, pb):
    per_page = PAGE // CMP_BLOCK
    for p in range(pb):
        xs_scr[p * PAGE:(p + 1) * PAGE, :] = xt_ref[p].T + pe_ref[...]
    acc = None
    for j in range(CMP_BLOCK):
        rows = xs_scr[pl.ds(j, per_page * pb, stride=CMP_BLOCK), :]
        t = _dot(rows.astype(BF16), w1_ref[j])
        acc = t if acc is None else acc + t
    o_ref[...] = _dot(jax.nn.gelu(acc).astype(BF16), w2_ref[...])


def _compress_pool(pool_t, pe, w1, w2):
    npool = pool_t.shape[0]
    per_page = PAGE // CMP_BLOCK
    pb = _largest_divisor(npool, (16, 8, 4, 2))
    pe2, w1j, w2b = _compress_weights(pe, w1, w2)
    pe_tile = jnp.tile(pe2, (per_page, 1))
    return pl.pallas_call(
        functools.partial(_compress_pool_kernel, pb=pb),
        out_shape=jax.ShapeDtypeStruct((npool * per_page, LANES), F32),
        grid=(npool // pb,),
        in_specs=[pl.BlockSpec((pb, LANES, PAGE), lambda i: (i, 0, 0)), pl.BlockSpec((PAGE, LANES), lambda i: (0, 0)),
                  pl.BlockSpec((CMP_BLOCK, LANES, LANES), lambda i: (0, 0, 0)),
                  pl.BlockSpec((LANES, LANES), lambda i: (0, 0))],
        out_specs=pl.BlockSpec((per_page * pb, LANES), lambda i: (i, 0)),
        scratch_shapes=[pltpu.VMEM((pb * PAGE, LANES), F32)],
        compiler_params=_cp("parallel"),
    )(pool_t, pe_tile, w1j, w2b)


def _rope_rows_kernel(x_ref, cos_ref, sin_ref, o_ref):
    o_ref[...] = _rope128(x_ref[...], cos_ref[...], sin_ref[...])


def _rope_rows(x, cos, sin):
    r = x.shape[0]
    tb = min(512, r)
    spec = pl.BlockSpec((tb, LANES), lambda i: (i, 0))
    return pl.pallas_call(_rope_rows_kernel, out_shape=jax.ShapeDtypeStruct((r, LANES), F32), grid=(r // tb,),
                          in_specs=[spec] * 3, out_specs=spec, compiler_params=_cp("parallel"))(x, cos, sin)


def _stack_heads(qblk, heads, scale):
    rows = []
    for src_blk, src_half, dst_half in heads:
        x = qblk[:, src_blk * LANES:(src_blk + 1) * LANES] * scale
        if src_half != dst_half:
            x = pltpu.roll(x, 64, 1)
        lane = _iota(x.shape, 1)
        keep = (lane >= 64) if dst_half else (lane < 64)
        rows.append(jnp.where(keep, x, 0.0))
    return jnp.concatenate(rows, axis=0)


def _unstack_heads(o_full_list, heads_per_kb, tq, n_out_blocks):
    outs = [None] * n_out_blocks
    for o_full, heads in zip(o_full_list, heads_per_kb):
        for idx, (src_blk, src_half, dst_half) in enumerate(heads):
            piece = o_full[idx * tq:(idx + 1) * tq]
            if src_half != dst_half:
                piece = pltpu.roll(piece, 64, 1)
            lane = _iota(piece.shape, 1)
            keep = (lane >= 64) if src_half else (lane < 64)
            prev = outs[src_blk]
            outs[src_blk] = jnp.where(keep, piece, 0.0 if prev is None else prev)
    return outs


def _lane_repeat(x, width):
    if width == LANES:
        return x
    if width % LANES == 0:
        return pltpu.repeat(x, width // LANES, axis=1)
    return jnp.broadcast_to(x[:, 0:1], (x.shape[0], width))


def _stack_wide(qblk, n_heads, scale):
    head_of_lane = _iota(qblk.shape, 1) >> 6
    return jnp.concatenate([jnp.where(head_of_lane == h, qblk * scale, 0.0) for h in range(n_heads)], axis=0)


def _unstack_wide(o_full, n_heads, tq):
    head_of_lane = _iota((tq, o_full.shape[1]), 1) >> 6
    out = jnp.zeros((tq, o_full.shape[1]), F32)
    for h in range(n_heads):
        out = jnp.where(head_of_lane == h, o_full[h * tq:(h + 1) * tq], out)
    return out


NSA_HEAD_MAP = [[(h // 2, h % 2, h // (NSA_HEADS // NSA_GROUPS)) for h in range(NSA_HEADS)]]
FOX_HEAD_MAP = [[(kb, 0, 0), (kb, 1, 1)] for kb in range(FOX_HEADS // 2)]


def _cmp_sel_kernel(q_ref, kc_ref, vc_ref, cend_ref, o_ref, sel_ref, *, tq, q0, nc, ns, ns_pad):
    qi = pl.program_id(1)
    heads = NSA_HEAD_MAP[0]
    hpg = NSA_HEADS // NSA_GROUPS
    qbd = _stack_heads(q_ref[...], heads, NSA_DH ** -0.5).astype(BF16)
    s = _dot_nt(qbd, kc_ref[...].astype(BF16))
    rows = NSA_HEADS * tq
    qpos = q0 + qi * tq + (_iota((rows, 1), 0) & (tq - 1))
    mask = cend_ref[...] <= qpos
    s = jnp.where(mask, s, NEG)
    e = jnp.where(mask, jnp.exp(s - jnp.max(s, axis=1, keepdims=True)), 0.0)
    p = e / jnp.maximum(jnp.sum(e, axis=1, keepdims=True), 1e-30)
    o_full = _dot(p.astype(BF16), vc_ref[...].astype(BF16))
    outs = _unstack_heads([o_full], NSA_HEAD_MAP, tq, MIX // LANES)
    for c, blk in enumerate(outs):
        o_ref[:, c * LANES:(c + 1) * LANES] = blk
    half = nc // 2
    qp = q0 + qi * tq + _iota((tq, 1), 0)
    blk_i = _iota((tq, ns_pad), 1)
    blk_f = blk_i.astype(F32)
    cur = qp >> 6
    forced = (blk_i == 0) | (blk_i == cur) | (blk_i == cur - 1)
    valid = (blk_i * SEL_BLOCK <= qp) & (blk_i < ns)
    n_sel = min(N_SEL, ns)
    scores, chosen = [], []
    for g in range(NSA_GROUPS):
        pg = p[g * hpg * tq:(g * hpg + 1) * tq]
        for j in range(1, hpg):
            pg = pg + p[(g * hpg + j) * tq:(g * hpg + j + 1) * tq]
        imp = pg[:, :half] + pg[:, half:]
        if ns_pad > half:
            imp = jnp.concatenate([imp, jnp.zeros((tq, ns_pad - half), F32)], axis=1)
        score = jnp.where(valid, imp + FORCE * forced.astype(F32), NEG)
        scores.append(jnp.where(blk_i < ns, score, -jnp.inf))
        chosen.append(jnp.zeros((tq, ns_pad), F32))
    for _ in range(n_sel):
        for g in range(NSA_GROUPS):
            mx = jnp.max(scores[g], axis=1, keepdims=True)
            first = jnp.min(jnp.where(scores[g] == mx, blk_f, 1e9), axis=1, keepdims=True)
            hit = blk_f == first
            chosen[g] = jnp.where(hit, 1.0, chosen[g])
            scores[g] = jnp.where(hit, -jnp.inf, scores[g])
    for g in range(NSA_GROUPS):
        sel_ref[:, g * ns_pad:(g + 1) * ns_pad] = jnp.where(valid, chosen[g], 0.0)


def _cmp_sel(q3, kcmp, vcmp, cend, q0, ns, ns_pad):
    B, T, _ = q3.shape
    nc = kcmp.shape[1]
    tq = min(TQ_CMP, T)
    return pl.pallas_call(
        functools.partial(_cmp_sel_kernel, tq=tq, q0=q0, nc=nc, ns=ns, ns_pad=ns_pad),
        out_shape=[jax.ShapeDtypeStruct((B, T, MIX), F32), jax.ShapeDtypeStruct((B, T, NSA_GROUPS * ns_pad), F32)],
        grid=(B, T // tq),
        in_specs=[pl.BlockSpec((None, tq, MIX), lambda b, i: (b, i, 0)),
                  pl.BlockSpec((None, nc, LANES), lambda b, i: (b, 0, 0)),
                  pl.BlockSpec((None, nc, LANES), lambda b, i: (b, 0, 0)),
                  pl.BlockSpec((1, nc), lambda b, i: (0, 0))],
        out_specs=[pl.BlockSpec((None, tq, MIX), lambda b, i: (b, i, 0)),
                   pl.BlockSpec((None, tq, NSA_GROUPS * ns_pad), lambda b, i: (b, i, 0))],
        compiler_params=_cp("parallel", "arbitrary"),
    )(q3, kcmp, vcmp, cend)


def _flash_kernel(*refs, cfg):
    mode, layout, tq, tk = cfg['mode'], cfg['layout'], cfg['tq'], cfg['tk']
    q0, kbase, pps, has_tail, tail_pos0 = cfg['q0'], cfg['kbase'], cfg['pps'], cfg['has_tail'], cfg['tail_pos0']
    ns_pad, n_pref, qw, kv_t = cfg['ns_pad'], cfg['n_pref'], cfg['qw'], cfg['kv_t']
    head_map = NSA_HEAD_MAP if layout == 'nsa' else FOX_HEAD_MAP
    nkb = 1 if layout in ('nsa', 'fox_wide') else FOX_HEADS // 2
    n_heads_kb = FOX_HEADS if layout == 'fox_wide' else len(head_map[0])
    rows = n_heads_kb * tq
    qi_ref, kj_ref, fl_ref = refs[:3]
    pos = n_pref
    q_ref = refs[pos]; pos += 1
    npg = max(pps, 1)
    k_refs = refs[pos:pos + npg]; pos += npg
    v_refs = refs[pos:pos + npg]; pos += npg
    if has_tail:
        tk_ref, tv_ref = refs[pos:pos + 2]; pos += 2
    c_ref = ct_ref = sel_ref = None
    if cfg['has_caug'] or cfg['has_crow']:
        c_ref = refs[pos]; pos += 1
        if has_tail:
            ct_ref = refs[pos]; pos += 1
    if mode == 'sel':
        sel_ref = refs[pos]; pos += 1
    o_ref = refs[pos]; pos += 1
    qst_scr, m_scr, l_scr, acc_scr = refs[pos:pos + 4]

    s_id = pl.program_id(1)
    qi = qi_ref[s_id]
    kj = kj_ref[s_id]
    fl = fl_ref[s_id]
    scale = NSA_DH ** -0.5 * LOG2E

    @pl.when((fl & 1) != 0)
    def _init():
        qblk = q_ref[...]
        if layout == 'fox_wide':
            qst_scr[0] = _stack_wide(qblk, FOX_HEADS, scale).astype(BF16)
        elif layout == 'nsa':
            qbd = _stack_heads(qblk, head_map[0], scale).astype(BF16)
            if mode == 'sel':
                selblk = sel_ref[...]
                hpg = NSA_HEADS // NSA_GROUPS
                parts = []
                for g in range(NSA_GROUPS):
                    off = ((selblk[:, g * ns_pad:(g + 1) * ns_pad] - 1.0) * SEL_OFF).astype(BF16)
                    parts += [off] * hpg
                qbd = jnp.concatenate([qbd, jnp.concatenate(parts, axis=0)], axis=1)
            qst_scr[0] = qbd
        else:
            for kb in range(nkb):
                qbd = _stack_heads(qblk, head_map[kb], scale).astype(BF16)
                if cfg['has_caug']:
                    lane = _iota((rows, LANES), 1)
                    head = 2 * kb + (_iota((rows, LANES), 0) >= tq).astype(jnp.int32)
                    aug = jnp.where((lane >= 3 * head) & (lane < 3 * head + 3), -1.0, 0.0).astype(BF16)
                    qbd = jnp.concatenate([qbd, aug], axis=1)
                qst_scr[kb] = qbd
        m_scr[...] = jnp.full(m_scr.shape, NEG, F32)
        l_scr[...] = jnp.zeros(l_scr.shape, F32)
        acc_scr[...] = jnp.zeros(acc_scr.shape, F32)

    def update(k_tile, v_tile, kpos0, c_tile, width, masked):
        if masked:
            qpos = q0 + qi * tq + (_iota((rows, 1), 0) & (tq - 1))
            kpos = kpos0 + _iota((1, width), 1)
            mask = kpos <= qpos
            if mode == 'band':
                mask = mask & (kpos > qpos - WINDOW)
        for kb in range(nkb):
            if kv_t:
                kt = k_tile.astype(BF16)
                if mode == 'sel':
                    onehot = (_iota((ns_pad, width), 0) == (kpos0 >> 6) + (_iota((ns_pad, width), 1) >> 6))
                    kt = jnp.concatenate([kt, onehot.astype(BF16)], axis=0)
                sc = _dot(qst_scr[kb], kt)
                if c_tile is not None:
                    sc = sc - LOG2E * jnp.concatenate([jnp.broadcast_to(c_tile[h:h + 1, :], (tq, width))
                                                       for h in range(n_heads_kb)], axis=0)
            else:
                kt = k_tile[:, kb * LANES:(kb + 1) * LANES].astype(BF16)
                if mode == 'sel':
                    onehot = (_iota((width, ns_pad), 1) == (kpos0 >> 6) + (_iota((width, ns_pad), 0) >> 6))
                    kt = jnp.concatenate([kt, onehot.astype(BF16)], axis=1)
                if c_tile is not None:
                    kt = jnp.concatenate([kt, c_tile], axis=1)
                sc = _dot_nt(qst_scr[kb], kt)
            if masked:
                sc = jnp.where(mask, sc, NEG)
            m_old = m_scr[kb]
            m_new = jnp.maximum(m_old, jnp.max(sc, axis=1, keepdims=True))
            alpha = jnp.exp2(m_old - m_new)
            p = jnp.exp2(sc - _lane_repeat(m_new, width))
            l_scr[kb] = alpha * l_scr[kb] + jnp.sum(p, axis=1, keepdims=True)
            if kv_t:
                pv = _dot_nt(p.astype(BF16), v_tile.astype(BF16))
            else:
                pv = _dot(p.astype(BF16), v_tile[:, kb * LANES:(kb + 1) * LANES].astype(BF16))
            acc_scr[kb] = _lane_repeat(alpha, pv.shape[1]) * acc_scr[kb] + pv
            m_scr[kb] = m_new

    def main_update(masked):
        if pps > 1:
            k_tile = jnp.concatenate([r[...] for r in k_refs], axis=1)
            v_tile = jnp.concatenate([r[...] for r in v_refs], axis=1)
        else:
            k_tile, v_tile = k_refs[0][...], v_refs[0][...]
        update(k_tile, v_tile, kbase + kj * tk, c_ref[...] if c_ref is not None else None, tk, masked)

    is_tail = (fl & 4) != 0
    need_mask = (fl & 8) != 0
    if has_tail:
        @pl.when(is_tail)
        def _():
            update(tk_ref[...], tv_ref[...], tail_pos0, ct_ref[...] if ct_ref is not None else None, PAGE, True)

    @pl.when(jnp.logical_and(jnp.logical_not(is_tail), need_mask))
    def _():
        main_update(True)

    @pl.when(jnp.logical_and(jnp.logical_not(is_tail), jnp.logical_not(need_mask)))
    def _():
        main_update(False)

    @pl.when((fl & 2) != 0)
    def _fin():
        fulls = [acc_scr[kb] / _lane_repeat(jnp.maximum(l_scr[kb], 1e-30), acc_scr.shape[2]) for kb in range(nkb)]
        if layout == 'fox_wide':
            o_ref[...] = _unstack_wide(fulls[0], FOX_HEADS, tq)
        else:
            outs = _unstack_heads(fulls, head_map, tq, qw // LANES)
            for c, blk in enumerate(outs):
                o_ref[:, c * LANES:(c + 1) * LANES] = blk


def _schedule(nq, tq, tk, q0, kbase, n_ktiles, mode, has_tail):
    qi_l, kj_l, fl_l = [], [], []
    for qi in range(nq):
        q_lo, q_hi = q0 + qi * tq, q0 + (qi + 1) * tq - 1
        entries = []
        for kj in range(n_ktiles):
            k_lo, k_hi = kbase + kj * tk, kbase + (kj + 1) * tk - 1
            if k_lo > q_hi:
                continue
            if mode == 'band' and k_hi <= q_lo - WINDOW:
                continue
            all_visible = k_hi <= q_lo and (mode != 'band' or k_lo > q_hi - WINDOW)
            entries.append((kj, 0 if all_visible else 8))
        if has_tail:
            entries.append((entries[-1][0] if entries else 0, 4 | 8))
        for n, (kj, f) in enumerate(entries):
            f |= (1 if n == 0 else 0) | (2 if n == len(entries) - 1 else 0)
            qi_l.append(qi); kj_l.append(kj); fl_l.append(f)
    return (np.asarray(qi_l, np.int32), np.asarray(kj_l, np.int32), np.asarray(fl_l, np.int32))


def _flash(q3, k3, v3, *, mode, layout, tq, tk, q0, kbase=0, page_table=None, pps=0, tail=None, tail_pos0=0,
           caug=None, crow=None, crow_tail=None, sel=None, ns_pad=0):
    B, Tq, qw = q3.shape
    nq = Tq // tq
    paged = page_table is not None
    has_tail = tail is not None
    if paged:
        kw = k3.shape[1]
        npages = page_table.shape[1]
        tk = pps * PAGE
        n_ktiles = npages // pps
    else:
        kw = k3.shape[-1]
        n_ktiles = k3.shape[1] // tk
    qi_t, kj_t, fl_t = _schedule(nq, tq, tk, q0, kbase, n_ktiles, mode, has_tail)
    nsteps = len(qi_t)
    prefetch = [jnp.asarray(qi_t), jnp.asarray(kj_t), jnp.asarray(fl_t)]
    if paged:
        prefetch.append(page_table.reshape(-1).astype(jnp.int32))
    n_pref = len(prefetch)
    nkb = 1 if layout in ('nsa', 'fox_wide') else FOX_HEADS // 2
    n_heads_kb = FOX_HEADS if layout == 'fox_wide' else (NSA_HEADS if layout == 'nsa' else 2)
    rows = n_heads_kb * tq
    cq = kw if layout == 'fox_wide' else LANES
    if mode == 'sel':
        cq += ns_pad
    if caug is not None:
        cq += LANES
    cv = kw if layout == 'fox_wide' else LANES

    in_specs = [pl.BlockSpec((None, tq, qw), lambda b, s, qi, kj, fl, *_: (b, qi[s], 0))]
    args = [q3]
    if paged:
        def page_map(i):
            return lambda b, s, qi, kj, fl, pt: (pt[b * npages + kj[s] * pps + i], 0, 0)
        for arr in (k3, v3):
            for i in range(pps):
                in_specs.append(pl.BlockSpec((None, kw, PAGE), page_map(i)))
                args.append(arr)
    else:
        for arr in (k3, v3):
            in_specs.append(pl.BlockSpec((None, tk, kw), lambda b, s, qi, kj, fl, *_: (b, kj[s], 0)))
            args.append(arr)
    if has_tail:
        for arr in tail:
            in_specs.append(pl.BlockSpec((None, kw, PAGE), lambda b, s, qi, kj, fl, *_: (b, 0, 0)))
            args.append(arr)
    if caug is not None:
        in_specs.append(pl.BlockSpec((None, tk, LANES), lambda b, s, qi, kj, fl, *_: (b, kj[s], 0)))
        args.append(caug)
    if crow is not None:
        in_specs.append(pl.BlockSpec((None, 8, tk), lambda b, s, qi, kj, fl, *_: (b, 0, kj[s])))
        args.append(crow)
        if has_tail:
            in_specs.append(pl.BlockSpec((None, 8, PAGE), lambda b, s, qi, kj, fl, *_: (b, 0, 0)))
            args.append(crow_tail)
    if mode == 'sel':
        in_specs.append(pl.BlockSpec((None, tq, NSA_GROUPS * ns_pad), lambda b, s, qi, kj, fl, *_: (b, qi[s], 0)))
        args.append(sel)
    cfg = dict(mode=mode, layout=layout, tq=tq, tk=tk, q0=q0, kbase=kbase, pps=pps if paged else 0,
               has_tail=has_tail, tail_pos0=tail_pos0, ns_pad=ns_pad, n_pref=n_pref, qw=qw, kv_t=paged,
               has_caug=caug is not None, has_crow=crow is not None)
    return pl.pallas_call(
        functools.partial(_flash_kernel, cfg=cfg),
        out_shape=jax.ShapeDtypeStruct((B, Tq, qw), F32),
        grid_spec=pltpu.PrefetchScalarGridSpec(
            num_scalar_prefetch=n_pref,
            grid=(B, nsteps),
            in_specs=in_specs,
            out_specs=pl.BlockSpec((None, tq, qw), lambda b, s, qi, kj, fl, *_: (b, qi[s], 0)),
            scratch_shapes=[pltpu.VMEM((nkb, rows, cq), BF16), pltpu.VMEM((nkb, rows, LANES), F32),
                            pltpu.VMEM((nkb, rows, LANES), F32), pltpu.VMEM((nkb, rows, cv), F32)]),
        compiler_params=_cp("parallel", "arbitrary"),
    )(*prefetch, *args)


def _nsa_combine_kernel(oc_ref, os_ref, ow_ref, sm_ref, e_ref, o_ref):
    gate = _sigmoid(sm_ref[...])
    acc = None
    for j, br in enumerate((oc_ref, os_ref, ow_ref)):
        gexp = _dot_r01(gate, e_ref[j])
        acc = gexp * br[...] if acc is None else acc + gexp * br[...]
    o_ref[...] = acc


def _nsa_combine(o_cmp, o_slc, o_win, small):
    n = o_cmp.shape[0]
    tm = min(512, n)
    src = np.zeros((3, LANES, MIX), np.float32)
    for h in range(NSA_HEADS):
        for j in range(3):
            src[j, 2 * M_HEADS + 3 * h + j, h * NSA_DH:(h + 1) * NSA_DH] = 1.0
    spec = pl.BlockSpec((tm, MIX), lambda i: (i, 0))
    return pl.pallas_call(
        _nsa_combine_kernel, out_shape=jax.ShapeDtypeStruct((n, MIX), F32), grid=(n // tm,),
        in_specs=[spec, spec, spec, pl.BlockSpec((tm, LANES), lambda i: (i, 0)),
                  pl.BlockSpec((3, LANES, MIX), lambda i: (0, 0, 0))],
        out_specs=spec, compiler_params=_cp("parallel"))(o_cmp, o_slc, o_win, small, jnp.asarray(src, BF16))


def _fox_cumsum_kernel(sm_ref, bias_ref, place_ref, lf_ref, caug_ref, carry, *, tt):
    @pl.when(pl.program_id(1) == 0)
    def _():
        carry[...] = jnp.zeros(carry.shape, F32)

    lf = _log_sigmoid(sm_ref[...] + bias_ref[...])
    lf_ref[...] = lf
    lower = (_iota((tt, tt), 1) <= _iota((tt, tt), 0)).astype(F32)
    c = _dot_l01(lower, lf) + carry[0:1, :]
    carry[...] = jnp.broadcast_to(c[tt - 1:tt, :], carry.shape)
    c = c * LOG2E
    c_hi = c.astype(BF16)
    r1 = c - c_hi.astype(F32)
    c_mid = r1.astype(BF16)
    c_lo = (r1 - c_mid.astype(F32)).astype(BF16)
    caug = _dot(c_hi, place_ref[0]) + _dot(c_mid, place_ref[1]) + _dot(c_lo, place_ref[2])
    caug_ref[...] = caug.astype(BF16)


def _fox_cumsum(small, bias_row, B, T):
    tt = min(256, T)
    row = lambda b, i: (b * (T // tt) + i, 0)
    place = np.zeros((3, LANES, LANES), np.float32)
    for h in range(FOX_HEADS):
        for j in range(3):
            place[j, h, 3 * h + j] = 1.0
    return pl.pallas_call(
        functools.partial(_fox_cumsum_kernel, tt=tt),
        out_shape=[jax.ShapeDtypeStruct((B * T, LANES), F32), jax.ShapeDtypeStruct((B * T, LANES), BF16)],
        grid=(B, T // tt),
        in_specs=[pl.BlockSpec((tt, LANES), row), pl.BlockSpec((1, LANES), lambda b, i: (0, 0)),
                  pl.BlockSpec((3, LANES, LANES), lambda b, i: (0, 0, 0))],
        out_specs=[pl.BlockSpec((tt, LANES), row), pl.BlockSpec((tt, LANES), row)],
        scratch_shapes=[pltpu.VMEM((8, LANES), F32)],
        compiler_params=_cp("parallel", "arbitrary"),
    )(small, bias_row, jnp.asarray(place, BF16))


def _fox_cumsum_paged_kernel(pt_ref, *refs, pps, n_main):
    lf_refs = refs[:pps]
    sm_ref, bias_ref, c_ref, lfn_ref, carry = refs[pps:]
    j = pl.program_id(1)

    @pl.when(j == 0)
    def _():
        carry[...] = jnp.zeros(carry.shape, F32)

    upper = (_iota((PAGE, PAGE), 0) <= _iota((PAGE, PAGE), 1)).astype(F32)

    @pl.when(j < n_main)
    def _():
        run = carry[:, 0:1]
        for i in range(pps):
            c = _dot_r01(lf_refs[i][...], upper) + run
            c_ref[:, i * PAGE:(i + 1) * PAGE] = c
            run = c[:, PAGE - 1:PAGE]
        carry[...] = jnp.broadcast_to(run, carry.shape)

    @pl.when(j == n_main)
    def _():
        lf = _log_sigmoid(sm_ref[...] + bias_ref[...])
        lfn_ref[...] = lf
        eye8 = (_iota((8, LANES), 0) == _iota((8, LANES), 1)).astype(F32)
        c_new = _dot_r01(_dot_nt_l01(eye8, lf), upper) + carry[:, 0:1]
        for i in range(pps):
            c_ref[:, i * PAGE:(i + 1) * PAGE] = c_new


def _fox_cumsum_paged(lf_pool_t, page_table, small_pad, bias_row, pps):
    B, npages = page_table.shape
    n_main = npages // pps

    def page_map(i):
        return lambda b, j, pt: (pt[b * npages + jnp.minimum(j, n_main - 1) * pps + i], 0, 0)

    return pl.pallas_call(
        functools.partial(_fox_cumsum_paged_kernel, pps=pps, n_main=n_main),
        out_shape=[jax.ShapeDtypeStruct((B, 8, (n_main + 1) * pps * PAGE), F32),
                   jax.ShapeDtypeStruct((B, PAGE, LANES), F32)],
        grid_spec=pltpu.PrefetchScalarGridSpec(
            num_scalar_prefetch=1, grid=(B, n_main + 1),
            in_specs=[pl.BlockSpec((None, 8, PAGE), page_map(i)) for i in range(pps)]
            + [pl.BlockSpec((None, PAGE, LANES), lambda b, j, pt: (b, 0, 0)),
               pl.BlockSpec((1, LANES), lambda b, j, pt: (0, 0))],
            out_specs=[pl.BlockSpec((None, 8, pps * PAGE), lambda b, j, pt: (b, 0, j)),
                       pl.BlockSpec((None, PAGE, LANES), lambda b, j, pt: (b, 0, 0))],
            scratch_shapes=[pltpu.VMEM((8, LANES), F32)]),
        compiler_params=_cp("parallel", "arbitrary"),
    )(page_table.reshape(-1).astype(jnp.int32), *([lf_pool_t] * pps), small_pad, bias_row)


def _rwkv_prep_kernel(rw_ref, prev_ref, sh_ref, mu_ref, w0_ref, w2_ref, a0_ref, a2_ref, g2_ref, kk_ref, ka_ref,
                      bd_ref, r_o, lw_o, k_o, v_o, kk_o, a_o, g_o, *, tt):
    i = pl.program_id(1)
    rw = rw_ref[...]
    first_prev = jnp.where(i == 0, sh_ref[...], prev_ref[7:8, :])
    prev = jnp.where(_iota(rw.shape, 0) == 0, first_prev, pltpu.roll(rw, 1, 0))
    m = rw + mu_ref[...] * (prev - rw)
    r, kr, vr = m[:, 0:MIX], m[:, MIX:2 * MIX], m[:, 2 * MIX:3 * MIX]
    lora = m[:, 3 * MIX:3 * MIX + LANES]
    gl = m[:, 3 * MIX + LANES:3 * MIX + 2 * LANES]
    w = -_softplus(-(w0_ref[...] + _dot(jnp.tanh(lora).astype(BF16), w2_ref[...]))) - 0.5
    a = _sigmoid(a0_ref[...] + _dot(lora.astype(BF16), a2_ref[...]))
    kk = kr * kk_ref[...]
    ss = _dot_r01(kk * kk, bd_ref[...])
    r_o[...] = r
    lw_o[...] = -jnp.exp(w)
    k_o[...] = kr * (1.0 + (a - 1.0) * ka_ref[...])
    v_o[...] = vr
    kk_o[...] = kk / jnp.maximum(jnp.sqrt(ss), 1e-12)
    a_o[...] = a
    g_o[...] = _dot(_sigmoid(gl).astype(BF16), g2_ref[...])


def _rwkv_prep(rw3, shift0, P):
    B, T, wdt = rw3.shape
    tt = min(256, T)
    nblk8 = tt // 8
    row3 = lambda b, i: (b, i, 0)
    vec = lambda n: pl.BlockSpec((1, n), lambda b, i: (0, 0))
    outs = pl.pallas_call(
        functools.partial(_rwkv_prep_kernel, tt=tt),
        out_shape=[jax.ShapeDtypeStruct((B, T, MIX), F32)] * 7,
        grid=(B, T // tt),
        in_specs=[pl.BlockSpec((None, tt, wdt), row3),
                  pl.BlockSpec((None, 8, wdt), lambda b, i: (b, jnp.maximum(i * nblk8 - 1, 0), 0)),
                  pl.BlockSpec((None, 1, wdt), lambda b, i: (b, 0, 0)),
                  vec(wdt), vec(MIX), pl.BlockSpec((LANES, MIX), lambda b, i: (0, 0)),
                  vec(MIX), pl.BlockSpec((LANES, MIX), lambda b, i: (0, 0)),
                  pl.BlockSpec((LANES, MIX), lambda b, i: (0, 0)), vec(MIX), vec(MIX),
                  pl.BlockSpec((MIX, MIX), lambda b, i: (0, 0))],
        out_specs=[pl.BlockSpec((None, tt, MIX), row3)] * 7,
        compiler_params=_cp("parallel", "arbitrary"),
    )(rw3, rw3, shift0[:, None, :], P['mu'], P['w0'], P['w2'], P['a0'], P['a2'], P['g2'], P['k_k'], P['k_a'], P['bd64'])
    return outs


def _rwkv_scan_kernel(r_ref, lw_ref, k_ref, v_ref, kk_ref, a_ref, g_ref, s0_ref, rk_ref, lng_ref, lnb_ref,
                      y_ref, s1_ref, s_scr, *, L):
    ci = pl.program_id(1)

    @pl.when(ci == 0)
    def _():
        s_scr[...] = s0_ref[...]

    r_i = _iota((L, L), 0)
    c_i = _iota((L, L), 1)
    lower = (c_i <= r_i).astype(F32)
    strict = c_i < r_i
    incl = c_i <= r_i
    eye_l = (c_i == r_i).astype(F32)
    lane = _iota((L, LANES), 1)
    half_masks = (lane < 64, lane >= 64)
    rr = _iota((LANES, LANES), 0)
    cc = _iota((LANES, LANES), 1)
    bdmask = (rr >= 64) == (cc >= 64)
    eye128 = rr == cc
    bd_f = bdmask.astype(F32)
    pairs = range(RWKV_HEADS // 2)
    sls = [slice(pr * LANES, (pr + 1) * LANES) for pr in pairs]
    P = []
    for sl in sls:
        lw = lw_ref[:, sl]
        cl = _dot_l01(lower, lw)
        w_in = jnp.exp(cl)
        w_out = jnp.exp(-cl)
        kk = kk_ref[:, sl]
        P.append(dict(w_in=w_in, kap=kk * jnp.exp(cl - lw), bet=kk * a_ref[:, sl] * w_out,
                      kt=k_ref[:, sl] * w_out, rt=r_ref[:, sl] * w_in, v=v_ref[:, sl]))
    for pr, d in zip(pairs, P):
        d['sbd'] = s_scr[pr]
        d['rhs'] = _bdot(d['kap'], d['sbd'])
        d['ys'] = _bdot(d['rt'], d['sbd'])
    H = []
    for d in P:
        for hm in half_masks:
            kap_h = jnp.where(hm, d['kap'], 0.0)
            rt_h = jnp.where(hm, d['rt'], 0.0)
            amat = jnp.where(strict, _bdot_nt(kap_h, d['bet']), 0.0)
            H.append(dict(d=d, hm=hm, pw=amat, tinv=eye_l - amat,
                          cmat=jnp.where(strict, _bdot_nt(kap_h, d['kt']), 0.0),
                          rb=jnp.where(incl, _bdot_nt(rt_h, d['bet']), 0.0),
                          rk=jnp.where(incl, _bdot_nt(rt_h, d['kt']), 0.0)))
    def same_block(shift):
        return (r_i >> shift) == (c_i >> shift)

    for h in H:
        h['amat'] = h['pw']
        h['pw'] = jnp.where(same_block(3), h['amat'], 0.0)
        h['tinv'] = eye_l - h['pw']
    for _ in range(2):
        for h in H:
            h['pw'] = _dot3(h['pw'], h['pw'])
        for h in H:
            h['tinv'] = h['tinv'] + _dot3(h['tinv'], h['pw'])
    shift = 3
    while (1 << shift) < L:
        off_diag = jnp.logical_and(same_block(shift + 1), jnp.logical_not(same_block(shift)))
        for h in H:
            h['x'] = _dot3(jnp.where(off_diag, h['amat'], 0.0), h['tinv'])
        for h in H:
            h['tinv'] = h['tinv'] - _dot3(h['tinv'], h['x'])
        shift += 1
    for h in H:
        d, hm = h['d'], h['hm']
        h['v_h'] = jnp.where(hm, d['v'], 0.0)
        h['rhs_h'] = jnp.where(hm, d['rhs'], 0.0) + _bdot(h['cmat'], h['v_h'])
    for h in H:
        h['u_h'] = -_dot3(h['tinv'], h['rhs_h'])
    for h in H:
        h['y_h'] = _bdot(h['rb'], h['u_h']) + _bdot(h['rk'], h['v_h'])
    for pr, (d, sl) in enumerate(zip(P, sls)):
        h0, h1 = H[2 * pr], H[2 * pr + 1]
        u = h0['u_h'] + h1['u_h']
        y = d['ys'] + h0['y_h'] + h1['y_h']
        upd = jnp.where(bdmask, _bdot_tn(d['bet'], u) + _bdot_tn(d['kt'], d['v']), 0.0)
        w_end = jnp.broadcast_to(d['w_in'][L - 1:L, :], (LANES, LANES))
        w_end_col = jnp.sum(jnp.where(eye128, w_end, 0.0), axis=1, keepdims=True)
        s_scr[pr] = (d['sbd'] + upd) * w_end_col
        d['y'] = y
    for d, sl in zip(P, sls):
        y = d['y']
        mean = _dot_r01(y, bd_f) * (1.0 / RWKV_N)
        yc = y - mean
        var = _dot_r01(yc * yc, bd_f) * (1.0 / RWKV_N)
        yn = yc * lax.rsqrt(var + 64e-5) * lng_ref[:, sl] + lnb_ref[:, sl]
        bonus = _dot_r01(r_ref[:, sl] * k_ref[:, sl] * rk_ref[:, sl], bd_f)
        y_ref[:, sl] = (yn + bonus * d['v']) * g_ref[:, sl]

    @pl.when(ci == pl.num_programs(1) - 1)
    def _():
        s1_ref[...] = s_scr[...]


def _rwkv_scan(r, lw, k, v, kk, a, g, s0, P, B, T):
    L = RWKV_CHUNK
    tp = -(-T // L) * L
    if tp != T:
        pad = lambda t: jnp.pad(t, ((0, 0), (0, tp - T), (0, 0)))
        r, lw, k, v, kk, a, g = (pad(t) for t in (r, lw, k, v, kk, a, g))
    nch = tp // L
    st = jnp.swapaxes(s0.astype(F32), -1, -2).reshape(B, RWKV_HEADS // 2, 2, RWKV_N, RWKV_N)
    z = jnp.zeros_like(st[:, :, 0])
    sbd0 = jnp.concatenate([jnp.concatenate([st[:, :, 0], z], -1), jnp.concatenate([z, st[:, :, 1]], -1)], -2)
    row3 = lambda b, c: (b, c, 0)
    st_spec = pl.BlockSpec((None, RWKV_HEADS // 2, LANES, LANES), lambda b, c: (b, 0, 0, 0))
    vec = pl.BlockSpec((1, MIX), lambda b, c: (0, 0))
    y, sbd1 = pl.pallas_call(
        functools.partial(_rwkv_scan_kernel, L=L),
        out_shape=[jax.ShapeDtypeStruct((B, tp, MIX), F32),
                   jax.ShapeDtypeStruct((B, RWKV_HEADS // 2, LANES, LANES), F32)],
        grid=(B, nch),
        in_specs=[pl.BlockSpec((None, L, MIX), row3)] * 7 + [st_spec, vec, vec, vec],
        out_specs=[pl.BlockSpec((None, L, MIX), row3), st_spec],
        scratch_shapes=[pltpu.VMEM((RWKV_HEADS // 2, LANES, LANES), F32)],
        compiler_params=_cp("parallel", "arbitrary"),
    )(r, lw, k, v, kk, a, g, sbd0, P['r_k'], P['ln_g'], P['ln_b'])
    s_e = sbd1[:, :, :RWKV_N, :RWKV_N]
    s_o = sbd1[:, :, RWKV_N:, RWKV_N:]
    s1 = jnp.swapaxes(jnp.stack([s_e, s_o], 2).reshape(B, RWKV_HEADS, RWKV_N, RWKV_N), -1, -2)
    return y[:, :T], s1


def _mem_kernel(x_ref, km_ref, vm_ref, wq_ref, wo_ref, g_ref, b_ref, o_ref):
    x = x_ref[...]
    q = _dot(x.astype(BF16), wq_ref[...])
    heads = []
    for h in range(MEM_HEADS):
        sl = slice(h * MEM_DH, (h + 1) * MEM_DH)
        s = _dot_nt((q[:, sl] * (MEM_DH ** -0.5)).astype(BF16), km_ref[:, sl].astype(BF16))
        e = jnp.exp(s - jnp.max(s, axis=1, keepdims=True))
        p = e / jnp.sum(e, axis=1, keepdims=True)
        heads.append(_dot(p.astype(BF16), vm_ref[:, sl].astype(BF16)))
    o = jnp.concatenate(heads, axis=1)
    y = _dot(o.astype(BF16), wo_ref[...])
    o_ref[...] = _layer_norm_rows(ALPHA * x + y, g_ref[...], b_ref[...])


def _mem_layer(x3, km, vm, wq, wo, g, b):
    B, T, d = x3.shape
    tq = min(512, T)
    mt = km.shape[1]
    full = lambda shape: pl.BlockSpec(shape, lambda bb, i: (0,) * len(shape))
    return pl.pallas_call(
        _mem_kernel, out_shape=jax.ShapeDtypeStruct((B, T, d), F32), grid=(B, T // tq),
        in_specs=[pl.BlockSpec((None, tq, d), lambda bb, i: (bb, i, 0)),
                  pl.BlockSpec((None, mt, d), lambda bb, i: (bb, 0, 0)),
                  pl.BlockSpec((None, mt, d), lambda bb, i: (bb, 0, 0)),
                  full((d, d)), full((d, d)), full((1, d)), full((1, d))],
        out_specs=pl.BlockSpec((None, tq, d), lambda bb, i: (bb, i, 0)),
        compiler_params=_cp("parallel", "arbitrary"),
    )(x3, km, vm, wq, wo, g.reshape(1, d), b.reshape(1, d))


def _moe_kernel(x_ref, wr_ref, br_ref, w1_ref, w3_ref, w2_ref, g_ref, b_ref, o_ref, xb_scr, gw_scr, acc_scr):
    e = pl.program_id(1)

    @pl.when(e == 0)
    def _route():
        x = x_ref[...]
        xb_scr[...] = x.astype(BF16)
        logit = _dot(x, wr_ref[...], HI) + br_ref[...]
        lane = _iota(logit.shape, 1)
        lane_f = lane.astype(F32)
        is_g = lane < N_GROUPS
        gl = jnp.where(is_g, logit, -jnp.inf)
        gmax = jnp.max(gl, axis=1, keepdims=True)
        grp = jnp.min(jnp.where(gl == gmax, lane_f, 1e9), axis=1, keepdims=True)
        g_w = 1.0 / jnp.sum(jnp.where(is_g, jnp.exp(gl - gmax), 0.0), axis=1, keepdims=True)
        lo = N_GROUPS + EXP_PER_GROUP * grp
        in_grp = (lane_f >= lo) & (lane_f < lo + EXP_PER_GROUP)
        el = jnp.where(in_grp, logit, -jnp.inf)
        emax = jnp.max(el, axis=1, keepdims=True)
        ee = jnp.where(in_grp, jnp.exp(el - emax), 0.0)
        p = jnp.where(in_grp, ee / jnp.sum(ee, axis=1, keepdims=True), -1.0)
        p1 = jnp.max(p, axis=1, keepdims=True)
        i1 = jnp.min(jnp.where(p == p1, lane_f, 1e9), axis=1, keepdims=True)
        pr = jnp.where(lane_f == i1, -1.0, p)
        p2 = jnp.max(pr, axis=1, keepdims=True)
        i2 = jnp.min(jnp.where(pr == p2, lane_f, 1e9), axis=1, keepdims=True)
        den = p1 + p2
        gw_scr[...] = jnp.where(lane_f == i1, g_w * p1 / den, 0.0) + jnp.where(lane_f == i2, g_w * p2 / den, 0.0)
        acc_scr[...] = jnp.zeros(acc_scr.shape, F32)

    gw = gw_scr[...]
    gcol = jnp.sum(jnp.where(_iota(gw.shape, 1) == e + N_GROUPS, gw, 0.0), axis=1, keepdims=True)
    xb = xb_scr[...]
    h1 = _dot(xb, w1_ref[...])
    h3 = _dot(xb, w3_ref[...])
    hid = h1 * _sigmoid(h1) * h3
    acc_scr[...] += gcol * _dot(hid.astype(BF16), w2_ref[...])

    @pl.when(e == pl.num_programs(1) - 1)
    def _():
        o_ref[...] = _layer_norm_rows(ALPHA * x_ref[...] + acc_scr[...], g_ref[...], b_ref[...])


def _moe_layer(x, M, g, b):
    n, d = x.shape
    tm = min(1024, n)
    full2 = lambda shape: pl.BlockSpec(shape, lambda i, e: (0, 0))
    return pl.pallas_call(
        _moe_kernel, out_shape=jax.ShapeDtypeStruct((n, d), F32), grid=(n // tm, N_EXPERTS),
        in_specs=[pl.BlockSpec((tm, d), lambda i, e: (i, 0)), full2((d, LANES)), full2((1, LANES)),
                  pl.BlockSpec((None, d, D_EXPERT), lambda i, e: (e, 0, 0)),
                  pl.BlockSpec((None, d, D_EXPERT), lambda i, e: (e, 0, 0)),
                  pl.BlockSpec((None, D_EXPERT, d), lambda i, e: (e, 0, 0)),
                  full2((1, d)), full2((1, d))],
        out_specs=pl.BlockSpec((tm, d), lambda i, e: (i, 0)),
        scratch_shapes=[pltpu.VMEM((tm, d), BF16), pltpu.VMEM((tm, LANES), F32), pltpu.VMEM((tm, d), F32)],
        compiler_params=_cp("parallel", "arbitrary"),
    )(x, M['wr'], M['br'], M['w1'], M['w3'], M['w2'], g.reshape(1, d), b.reshape(1, d))


def _moe_route_kernel(x_ref, wr_ref, br_ref, gw_ref, cnt_ref):
    x = x_ref[...]
    logit = _dot(x, wr_ref[...], HI) + br_ref[...]
    lane_f = _iota(logit.shape, 1).astype(F32)
    is_g = lane_f < N_GROUPS
    gl = jnp.where(is_g, logit, -jnp.inf)
    gmax = jnp.max(gl, axis=1, keepdims=True)
    grp = jnp.min(jnp.where(gl == gmax, lane_f, 1e9), axis=1, keepdims=True)
    g_w = 1.0 / jnp.sum(jnp.where(is_g, jnp.exp(gl - gmax), 0.0), axis=1, keepdims=True)
    lo = N_GROUPS + EXP_PER_GROUP * grp
    in_grp = (lane_f >= lo) & (lane_f < lo + EXP_PER_GROUP)
    el = jnp.where(in_grp, logit, -jnp.inf)
    emax = jnp.max(el, axis=1, keepdims=True)
    ee = jnp.where(in_grp, jnp.exp(el - emax), 0.0)
    p = jnp.where(in_grp, ee / jnp.sum(ee, axis=1, keepdims=True), -1.0)
    p1 = jnp.max(p, axis=1, keepdims=True)
    i1 = jnp.min(jnp.where(p == p1, lane_f, 1e9), axis=1, keepdims=True)
    pr = jnp.where(lane_f == i1, -1.0, p)
    p2 = jnp.max(pr, axis=1, keepdims=True)
    i2 = jnp.min(jnp.where(pr == p2, lane_f, 1e9), axis=1, keepdims=True)
    den = p1 + p2
    member = jnp.where(lane_f == grp, 1.0, 0.0)
    gw_ref[...] = (member + jnp.where(lane_f == i1, g_w * p1 / den, 0.0)
                   + jnp.where(lane_f == i2, g_w * p2 / den, 0.0))
    cnt_ref[...] = jnp.broadcast_to(jnp.sum(member, axis=0, keepdims=True), cnt_ref.shape)


def _moe_group_kernel(cnt_ref, x_ref, gw_ref, w1_ref, w3_ref, w2_ref, g_ref, b_ref, o_ref,
                      xb_scr, gt_scr, rk_scr, xs_scr, gws_scr, ys_scr, acc_scr, *, tm, ch):
    i = pl.program_id(0)
    e = pl.program_id(1)
    grp = e // EXP_PER_GROUP
    cnt = cnt_ref[i * N_GROUPS + grp]
    slots = [slice(c * ch, (c + 1) * ch) for c in range(tm // ch)]

    @pl.when(e == 0)
    def _():
        xb_scr[...] = x_ref[...].astype(BF16)
        acc_scr[...] = jnp.zeros(acc_scr.shape, F32)
        member = gw_ref[...].T[0:8, :]
        gt_scr[...] = member
        before = (_iota((tm, tm), 0) < _iota((tm, tm), 1)).astype(BF16)
        rk_scr[...] = _dot(member.astype(BF16), before)

    def one_hot(c):
        member = gt_scr[pl.ds(grp, 1), :]
        rank = rk_scr[pl.ds(grp, 1), :]
        want = (c * ch + _iota((ch, 1), 0)).astype(F32)
        return jnp.where((member > 0.5) & (rank == want), 1.0, 0.0).astype(BF16)

    @pl.when(e % EXP_PER_GROUP == 0)
    def _():
        for c, rs in enumerate(slots):
            @pl.when(c * ch < cnt)
            def _():
                p = one_hot(c)
                xs_scr[rs, :] = _dot(p, xb_scr[...]).astype(BF16)
                gws_scr[rs, :] = _dot_l01(p, gw_ref[...])
                ys_scr[rs, :] = jnp.zeros((ch, ys_scr.shape[1]), F32)

    for c, rs in enumerate(slots):
        @pl.when(c * ch < cnt)
        def _():
            xs = xs_scr[rs, :]
            h1 = _dot(xs, w1_ref[...])
            h3 = _dot(xs, w3_ref[...])
            hid = h1 * _sigmoid(h1) * h3
            gws = gws_scr[rs, :]
            gcol = jnp.sum(jnp.where(_iota(gws.shape, 1) == e + N_GROUPS, gws, 0.0), axis=1, keepdims=True)
            ys_scr[rs, :] += gcol * _dot(hid.astype(BF16), w2_ref[...])

    @pl.when(e % EXP_PER_GROUP == EXP_PER_GROUP - 1)
    def _():
        for c, rs in enumerate(slots):
            @pl.when(c * ch < cnt)
            def _():
                p = one_hot(c)
                hi, lo = _split_bf16(ys_scr[rs, :])
                acc_scr[...] += _dot_tn(p, hi) + _dot_tn(p, lo)

    @pl.when(e == pl.num_programs(1) - 1)
    def _():
        o_ref[...] = _layer_norm_rows(ALPHA * x_ref[...] + acc_scr[...], g_ref[...], b_ref[...])


def _moe_layer_grouped(x, M, g, b):
    n, d = x.shape
    tm = min(1024, n)
    ch = min(256, tm)
    nt = n // tm
    gw, cnt = pl.pallas_call(
        _moe_route_kernel,
        out_shape=[jax.ShapeDtypeStruct((n, LANES), F32), jax.ShapeDtypeStruct((nt, 8, LANES), F32)],
        grid=(nt,),
        in_specs=[pl.BlockSpec((tm, d), lambda i: (i, 0)), pl.BlockSpec((d, LANES), lambda i: (0, 0)),
                  pl.BlockSpec((1, LANES), lambda i: (0, 0))],
        out_specs=[pl.BlockSpec((tm, LANES), lambda i: (i, 0)), pl.BlockSpec((None, 8, LANES), lambda i: (i, 0, 0))],
        compiler_params=_cp("parallel"),
    )(x, M['wr'], M['br'])
    counts = cnt[:, 0, :N_GROUPS].astype(jnp.int32).reshape(-1)
    full2 = lambda shape: pl.BlockSpec(shape, lambda i, e, c: (0, 0))
    return pl.pallas_call(
        functools.partial(_moe_group_kernel, tm=tm, ch=ch),
        out_shape=jax.ShapeDtypeStruct((n, d), F32),
        grid_spec=pltpu.PrefetchScalarGridSpec(
            num_scalar_prefetch=1, grid=(nt, N_EXPERTS),
            in_specs=[pl.BlockSpec((tm, d), lambda i, e, c: (i, 0)), pl.BlockSpec((tm, LANES), lambda i, e, c: (i, 0)),
                      pl.BlockSpec((None, d, D_EXPERT), lambda i, e, c: (e, 0, 0)),
                      pl.BlockSpec((None, d, D_EXPERT), lambda i, e, c: (e, 0, 0)),
                      pl.BlockSpec((None, D_EXPERT, d), lambda i, e, c: (e, 0, 0)),
                      full2((1, d)), full2((1, d))],
            out_specs=pl.BlockSpec((tm, d), lambda i, e, c: (i, 0)),
            scratch_shapes=[pltpu.VMEM((tm, d), BF16), pltpu.VMEM((8, tm), F32), pltpu.VMEM((8, tm), F32),
                            pltpu.VMEM((tm, d), BF16), pltpu.VMEM((tm, LANES), F32), pltpu.VMEM((tm, d), F32),
                            pltpu.VMEM((tm, d), F32)]),
        compiler_params=_cp("parallel", "arbitrary"),
    )(counts, x, gw, M['w1'], M['w3'], M['w2'], g.reshape(1, d), b.reshape(1, d))


EVEN_SEGS = ([(MIX, False)] * 4 + [(MIX, True), (LANES, False), (LANES, False), (LANES, True), (LANES, False),
                                   (LANES, True), (LANES, False), (LANES, False)])
ODD_SEGS = [(MIX, False)] * 3 + [(RWKV_SHIFT, False), (LANES, False)]
RW_PERM = np.concatenate([np.arange(0, 512), np.arange(576, 1088), np.arange(1088, 1600), np.arange(512, 576),
                          np.arange(1600, 1664), np.arange(1664, 1792)])
RW_INV = np.argsort(RW_PERM)


def _prep_even(w_in, w_out, b_i, b_f):
    zeros = jnp.zeros((D_MODEL, LANES - 2 * M_HEADS - 3 * NSA_HEADS), F32)
    w = jnp.concatenate([w_in[:, 0:1536], w_in[:, 1544:2056], w_in[:, 2056:3336], w_in[:, 1536:1544],
                         w_in[:, 3336:3360], zeros], axis=1).astype(BF16)
    bias = jnp.concatenate([b_i, b_f, jnp.zeros((LANES - 2 * M_HEADS,), F32)]).reshape(1, LANES)
    return dict(w_in=w, w_out_a=w_out[:MIX].astype(BF16), w_out_b=w_out[MIX:].astype(BF16), gate_bias=bias)


def _prep_odd(w_in, w_out, fox_b_f, mu, w0, w2, a0, a2, g2, k_k, k_a, r_k, ln_g, ln_b):
    rw_cols = w_in[:, 1544:3336][:, RW_PERM]
    zeros = jnp.zeros((D_MODEL, LANES - FOX_HEADS), F32)
    w = jnp.concatenate([w_in[:, 0:1536], rw_cols, w_in[:, 1536:1544], zeros], axis=1).astype(BF16)
    bias = jnp.concatenate([fox_b_f, jnp.zeros((LANES - FOX_HEADS,), F32)]).reshape(1, LANES)
    z64 = jnp.zeros((64, MIX), F32)
    hd = np.arange(MIX) // RWKV_N
    bd64 = jnp.asarray((hd[:, None] == hd[None, :]).astype(np.float32), BF16)
    P = dict(mu=mu[RW_PERM].reshape(1, -1), w0=w0.reshape(1, -1), a0=a0.reshape(1, -1),
             w2=jnp.concatenate([w2, z64], 0).astype(BF16), a2=jnp.concatenate([z64, a2], 0).astype(BF16),
             g2=g2.astype(BF16), k_k=k_k.reshape(1, -1), k_a=k_a.reshape(1, -1), r_k=r_k.reshape(1, -1),
             ln_g=ln_g.reshape(1, -1), ln_b=ln_b.reshape(1, -1), bd64=bd64)
    return dict(w_in=w, w_out_a=w_out[:MIX].astype(BF16), w_out_b=w_out[MIX:].astype(BF16), fox_bias=bias, P=P)


def _prep_moe(w_group, b_group, w_expert, b_expert, w1, w3, w2):
    padw = LANES - N_GROUPS - N_EXPERTS
    wr = jnp.concatenate([w_group, w_expert, jnp.zeros((D_MODEL, padw), F32)], axis=1)
    br = jnp.concatenate([b_group, b_expert, jnp.zeros((padw,), F32)]).reshape(1, LANES)
    return dict(wr=wr, br=br, w1=w1.astype(BF16), w3=w3.astype(BF16), w2=w2.astype(BF16))


def _even_odd_blocks(kcmp):
    return jnp.concatenate([kcmp[:, 0::2], kcmp[:, 1::2]], axis=1)


def _pages_t(cache):
    pool, page, h, dh = cache.shape
    return jnp.transpose(cache, (0, 2, 3, 1)).reshape(pool, h * dh, page)


def _tail_t(rows3):
    B, T, _ = rows3.shape
    return jnp.swapaxes(jnp.pad(rows3, ((0, 0), (0, PAGE - T), (0, 0))), 1, 2)


def _even_mixer(x3, q0, W, nsa_w, norm_g, c0, n0, m0, past, ln_g, ln_b):
    B, T, D = x3.shape
    N = B * T
    x2 = x3.reshape(N, D)
    pos = q0 + jnp.arange(T)
    cos, sin = _rope_tables(pos)
    rope = (jnp.tile(cos, (B, 1)), jnp.tile(sin, (B, 1)))
    sh = (B, T, NSA_GROUPS, NSA_DH)
    if past is None and T % 256 == 0:
        outs, outs_t = _proj_split(x2, W['w_in'], EVEN_SEGS, rope, t_segs=(5, 6, 7, 8), bt=(B, T))
        new_rows = [jnp.transpose(t.reshape(B, NSA_GROUPS, NSA_DH, T), (0, 3, 1, 2)) for t in outs_t]
    else:
        outs = _proj_split(x2, W['w_in'], EVEN_SEGS, rope)
        new_rows = [t.reshape(sh) for t in outs[5:9]]
    mq, mk, mv, mo, nq, kc, vc, ks, vs, kw, vw, small = outs
    h, c1, n1, m1 = _mlstm(mq, mk, mv, mo, small, W['gate_bias'], norm_g, c0, n0, m0, B, T)
    pe_k, pe_v, ck1, ck2, cv1, cv2 = nsa_w
    blk_w = CMP_BLOCK * LANES
    q3 = nq.reshape(B, T, MIX)
    if past is None:
        L = T
        nc = L // CMP_BLOCK
        kcmp = _compress(kc.reshape(B * nc, blk_w), pe_k, ck1, ck2).reshape(B, nc, LANES)
        vcmp = _compress(vc.reshape(B * nc, blk_w), pe_v, cv1, cv2).reshape(B, nc, LANES)
    else:
        pt = past['page_table']
        npages = pt.shape[1]
        L = npages * PAGE + T
        nc = L // CMP_BLOCK
        per_page = PAGE // CMP_BLOCK
        pool = past['kc'].shape[0]
        kcp = _compress_pool(_pages_t(past['kc']), pe_k, ck1, ck2).reshape(pool, per_page * LANES)
        vcp = _compress_pool(_pages_t(past['vc']), pe_v, cv1, cv2).reshape(pool, per_page * LANES)
        kcmp = jnp.take(kcp, pt, axis=0).reshape(B, npages * per_page, LANES)[:, :nc]
        vcmp = jnp.take(vcp, pt, axis=0).reshape(B, npages * per_page, LANES)[:, :nc]
    cpos = jnp.arange(nc) * CMP_BLOCK + CMP_BLOCK - 1
    ccos, csin = _rope_tables(cpos)
    kcmp = _rope_rows(kcmp.reshape(B * nc, LANES), jnp.tile(ccos, (B, 1)), jnp.tile(csin, (B, 1))).reshape(B, nc, LANES)
    ns = -(-L // SEL_BLOCK)
    ns_pad = -(-ns // LANES) * LANES
    cend = jnp.concatenate([cpos[0::2], cpos[1::2]]).astype(jnp.int32).reshape(1, nc)
    o_cmp, sel = _cmp_sel(q3, _even_odd_blocks(kcmp), _even_odd_blocks(vcmp), cend, q0, ns, ns_pad)
    ks3, vs3 = ks.reshape(B, T, LANES), vs.reshape(B, T, LANES)
    kw3, vw3 = kw.reshape(B, T, LANES), vw.reshape(B, T, LANES)
    if past is None:
        o_slc = _flash(q3, ks3, vs3, mode='sel', layout='nsa', tq=min(TQ_SLC, T), tk=min(TK_SLC, T), q0=0, sel=sel,
                       ns_pad=ns_pad)
        o_win = _flash(q3, kw3, vw3, mode='band', layout='nsa', tq=min(TQ_WIN, T), tk=min(TK_WIN, T), q0=0)
        kw_new, vw_new = kw3[:, -min(WINDOW, T):], vw3[:, -min(WINDOW, T):]
    else:
        o_slc = _flash(q3, _pages_t(past['ks']), _pages_t(past['vs']), mode='sel', layout='nsa', tq=T, tk=0, q0=q0,
                       page_table=pt, pps=_largest_divisor(npages, (16, 8, 4, 2, 1)), tail=(_tail_t(ks3), _tail_t(vs3)),
                       tail_pos0=q0, sel=sel, ns_pad=ns_pad)
        kw_ext = jnp.concatenate([past['kw_buf'], kw3], axis=1)
        vw_ext = jnp.concatenate([past['vw_buf'], vw3], axis=1)
        wb = past['kw_buf'].shape[1]
        o_win = _flash(q3, kw_ext, vw_ext, mode='band', layout='nsa', tq=T, tk=wb + T, q0=q0, kbase=q0 - wb)
        keep = min(WINDOW, wb + T)
        kw_new, vw_new = kw_ext[:, -keep:], vw_ext[:, -keep:]
    o = _nsa_combine(o_cmp.reshape(N, MIX), o_slc.reshape(N, MIX), o_win.reshape(N, MIX), small)
    x_new = _mm_res_ln([h, o], [W['w_out_a'], W['w_out_b']], x2, ln_g, ln_b).reshape(B, T, D)
    wsh = (B, -1, NSA_GROUPS, NSA_DH)
    state = (c1, n1, m1, *new_rows, kw_new.reshape(wsh), vw_new.reshape(wsh))
    return x_new, state


def _odd_mixer(x3, q0, W, s0, shift0, past, ln_g, ln_b):
    B, T, D = x3.shape
    N = B * T
    x2 = x3.reshape(N, D)
    hs = (B, T, FOX_HEADS, FOX_DH)
    if past is None and T % 256 == 0:
        outs, outs_t = _proj_split(x2, W['w_in'], ODD_SEGS, t_segs=(1, 2), bt=(B, T))
        new_kv = [jnp.transpose(t.reshape(B, FOX_HEADS, FOX_DH, T), (0, 3, 1, 2)) for t in outs_t]
    else:
        outs = _proj_split(x2, W['w_in'], ODD_SEGS)
        new_kv = [t.reshape(hs) for t in outs[1:3]]
    fq, fk, fv, rw, small = outs
    q3, k3, v3 = (t.reshape(B, T, MIX) for t in (fq, fk, fv))
    if past is None:
        lf_full, caug = _fox_cumsum(small, W['fox_bias'], B, T)
        lf = lf_full[:, :FOX_HEADS].reshape(B, T, FOX_HEADS)
        o_c = _flash(q3, k3, v3, mode='causal', layout='fox', tq=min(TQ_FOX, T), tk=min(TK_FOX, T), q0=0,
                     caug=caug.reshape(B, T, LANES))
    else:
        pt = past['page_table']
        npages = pt.shape[1]
        pps = _largest_divisor(npages, (16, 8, 4, 2, 1))
        padrows = ((0, 0), (0, PAGE - T), (0, 0))
        lf_pool_t = jnp.swapaxes(past['lf'], 1, 2)
        c_all, lf_new = _fox_cumsum_paged(lf_pool_t, pt, jnp.pad(small.reshape(B, T, LANES), padrows), W['fox_bias'], pps)
        lf = lf_new[:, :T, :FOX_HEADS]
        n_main = npages * PAGE
        o_c = _flash(q3, _pages_t(past['k']), _pages_t(past['v']), mode='causal', layout='fox_wide', tq=T, tk=0, q0=q0,
                     page_table=pt, pps=pps, tail=(_tail_t(k3), _tail_t(v3)), tail_pos0=q0,
                     crow=c_all, crow_tail=c_all[:, :, n_main:n_main + PAGE])
    rw3 = rw.reshape(B, T, RWKV_SHIFT)
    r, lw, k, v, kk, a, g = _rwkv_prep(rw3, shift0[:, RW_PERM], W['P'])
    y, s1 = _rwkv_scan(r, lw, k, v, kk, a, g, s0, W['P'], B, T)
    x_new = _mm_res_ln([o_c.reshape(N, MIX), y.reshape(N, MIX)], [W['w_out_a'], W['w_out_b']], x2, ln_g, ln_b)
    state = (*new_kv, lf, s1, rw3[:, -1][:, RW_INV])
    return x_new.reshape(B, T, D), state


def kernel(x_prompt, x_sample, state_mlstm_C, state_mlstm_n, state_mlstm_m, cache_nsa_kc, cache_nsa_vc, cache_nsa_ks, cache_nsa_vs, cache_nsa_kw, cache_nsa_vw, cache_fox_k, cache_fox_v, cache_fox_logf, state_rwkv_S, state_rwkv_shift, cache_mem_k, cache_mem_v, page_table, mem_prompt, even_w_in, even_w_out, mlstm_b_i, mlstm_b_f, mlstm_norm_g, nsa_pe_k, nsa_pe_v, nsa_cmp_k_w1, nsa_cmp_k_w2, nsa_cmp_v_w1, nsa_cmp_v_w2, odd_w_in, odd_w_out, fox_b_f, rwkv_mu, rwkv_w0, rwkv_w2, rwkv_a0, rwkv_a2, rwkv_g2, rwkv_k_k, rwkv_k_a, rwkv_r_k, rwkv_ln_g, rwkv_ln_b, mem_wq, mem_wk, mem_wv, mem_wo, moe_w_group, moe_b_group, moe_w_expert, moe_b_expert, moe_w1, moe_w3, moe_w2, ln_g, ln_b):
    bp, tp, d = x_prompt.shape
    bs, ts, _ = x_sample.shape
    depth = ln_g.shape[0]
    past = page_table.shape[1] * cache_nsa_kc.shape[2]
    xp, xs = x_prompt, x_sample
    acc = {}

    def push(prefix, names, vals):
        for nm, val in zip(names, vals):
            acc.setdefault(prefix + nm, []).append(val)

    even_names = ('mlstm_C', 'mlstm_n', 'mlstm_m', 'nsa_kc', 'nsa_vc', 'nsa_ks', 'nsa_vs', 'nsa_kw', 'nsa_vw')
    odd_names = ('fox_k', 'fox_v', 'fox_logf', 'rwkv_S', 'rwkv_shift')
    for layer in range(depth):
        e = layer // 2
        if layer % 2 == 0:
            W = _prep_even(even_w_in[e], even_w_out[e], mlstm_b_i[e], mlstm_b_f[e])
            nsa_w = (nsa_pe_k[e], nsa_pe_v[e], nsa_cmp_k_w1[e], nsa_cmp_k_w2[e], nsa_cmp_v_w1[e], nsa_cmp_v_w2[e])
            xp, sp = _even_mixer(xp, 0, W, nsa_w, mlstm_norm_g[e], jnp.zeros((bp, M_HEADS, M_DV, M_DK), F32),
                                 jnp.zeros((bp, M_HEADS, M_DK), F32), jnp.zeros((bp, M_HEADS), F32), None,
                                 ln_g[layer, 0], ln_b[layer, 0])
            wbuf = lambda c: c[e].reshape(bs, -1, LANES)
            past_d = dict(kc=cache_nsa_kc[e], vc=cache_nsa_vc[e], ks=cache_nsa_ks[e], vs=cache_nsa_vs[e],
                          page_table=page_table, kw_buf=wbuf(cache_nsa_kw), vw_buf=wbuf(cache_nsa_vw))
            xs, ss = _even_mixer(xs, past, W, nsa_w, mlstm_norm_g[e], state_mlstm_C[e], state_mlstm_n[e],
                                 state_mlstm_m[e], past_d, ln_g[layer, 0], ln_b[layer, 0])
            names = even_names
        else:
            W = _prep_odd(odd_w_in[e], odd_w_out[e], fox_b_f[e], rwkv_mu[e], rwkv_w0[e], rwkv_w2[e], rwkv_a0[e],
                          rwkv_a2[e], rwkv_g2[e], rwkv_k_k[e], rwkv_k_a[e], rwkv_r_k[e], rwkv_ln_g[e], rwkv_ln_b[e])
            xp, sp = _odd_mixer(xp, 0, W, jnp.zeros((bp, RWKV_HEADS, RWKV_N, RWKV_N), F32),
                                jnp.zeros((bp, RWKV_SHIFT), F32), None, ln_g[layer, 0], ln_b[layer, 0])
            past_d = dict(k=cache_fox_k[e], v=cache_fox_v[e], lf=cache_fox_logf[e], page_table=page_table)
            xs, ss = _odd_mixer(xs, past, W, state_rwkv_S[e], state_rwkv_shift[e], past_d,
                                ln_g[layer, 0], ln_b[layer, 0])
            names = odd_names
        push('p_', names, sp)
        push('s_', names, ss)
        wkv = jnp.concatenate([mem_wk[layer], mem_wv[layer]], axis=1).astype(BF16)
        mt = mem_prompt.shape[1]
        kmp, vmp = _proj_split(mem_prompt.reshape(bp * mt, d), wkv, [(d, False), (d, False)])
        push('p_', ('mem_k', 'mem_v'), (kmp.reshape(bp, mt, MEM_HEADS, MEM_DH), vmp.reshape(bp, mt, MEM_HEADS, MEM_DH)))
        wq, wo = mem_wq[layer].astype(BF16), mem_wo[layer].astype(BF16)
        xp = _mem_layer(xp, kmp.reshape(bp, mt, d), vmp.reshape(bp, mt, d), wq, wo, ln_g[layer, 1], ln_b[layer, 1])
        xs = _mem_layer(xs, cache_mem_k[layer].reshape(bs, -1, d), cache_mem_v[layer].reshape(bs, -1, d), wq, wo,
                        ln_g[layer, 1], ln_b[layer, 1])
        M = _prep_moe(moe_w_group[layer], moe_b_group[layer], moe_w_expert[layer], moe_b_expert[layer],
                      moe_w1[layer], moe_w3[layer], moe_w2[layer])
        xp = _moe_layer_grouped(xp.reshape(bp * tp, d), M, ln_g[layer, 2], ln_b[layer, 2]).reshape(bp, tp, d)
        xs = _moe_layer_grouped(xs.reshape(bs * ts, d), M, ln_g[layer, 2], ln_b[layer, 2]).reshape(bs, ts, d)

    st = {name: jnp.stack(vals) for name, vals in acc.items()}
    return (xp, xs,
            st['p_mlstm_C'], st['p_mlstm_n'], st['p_mlstm_m'],
            st['p_nsa_kc'], st['p_nsa_vc'], st['p_nsa_ks'], st['p_nsa_vs'], st['p_nsa_kw'], st['p_nsa_vw'],
            st['p_fox_k'], st['p_fox_v'], st['p_fox_logf'], st['p_rwkv_S'], st['p_rwkv_shift'],
            st['p_mem_k'], st['p_mem_v'],
            st['s_mlstm_C'], st['s_mlstm_n'], st['s_mlstm_m'],
            st['s_nsa_kc'], st['s_nsa_vc'], st['s_nsa_ks'], st['s_nsa_vs'], st['s_nsa_kw'], st['s_nsa_vw'],
            st['s_fox_k'], st['s_fox_v'], st['s_fox_logf'], st['s_rwkv_S'], st['s_rwkv_shift'])
```

```python
import functools

import numpy as np
import jax
import jax.numpy as jnp
from jax import lax
from jax.experimental import pallas as pl
from jax.experimental.pallas import tpu as pltpu

F32 = jnp.float32
BF16 = jnp.bfloat16
HI = lax.Precision.HIGHEST

D_MODEL = 1024
DEPTH = 2
PAGE = 128
M_HEADS, M_DK, M_DV, M_CHUNK = 4, 128, 128, 64
NSA_HEADS, NSA_GROUPS, NSA_DH = 8, 2, 64
CMP_BLOCK, SEL_BLOCK, N_SEL, WINDOW = 32, 64, 16, 512
FOX_HEADS, FOX_DH = 8, 64
RWKV_HEADS, RWKV_N, W_LORA, A_LORA, G_LORA = 8, 64, 64, 64, 128
RWKV_CHUNK = 64
MEM_HEADS = 4
MEM_DH = D_MODEL // MEM_HEADS
N_GROUPS, EXP_PER_GROUP, TOP_K, D_EXPERT = 4, 8, 2, 256
N_EXPERTS = N_GROUPS * EXP_PER_GROUP
ROPE_THETA = 10000.0
ALPHA = (2.0 * DEPTH) ** 0.25
LN_EPS = 1e-5
NEG = -1e30
FORCE = 1e4
MIX = 512
RWKV_SHIFT = 1792
SEL_OFF = 2.0 ** 30
LOG2E = 1.4426950408889634

LANES = 128
VMEM_LIMIT = 48 * 1024 * 1024
TQ_FOX, TK_FOX = 512, 512
TQ_SLC, TK_SLC = 256, 512
TQ_WIN, TK_WIN = 128, 128
TQ_CMP = 256


def _cp(*sem):
    return pltpu.CompilerParams(dimension_semantics=sem, vmem_limit_bytes=VMEM_LIMIT)


def _dot(a, b, prec=None):
    return jnp.dot(a, b, preferred_element_type=F32, precision=prec)


def _dot_nt(a, b, prec=None):
    return lax.dot_general(a, b, (((1,), (1,)), ((), ())), preferred_element_type=F32, precision=prec)


def _dot_tn(a, b, prec=None):
    return lax.dot_general(a, b, (((0,), (0,)), ((), ())), preferred_element_type=F32, precision=prec)


def _bdot(a, b):
    return _dot(a.astype(BF16), b.astype(BF16))


def _bdot_nt(a, b):
    return _dot_nt(a.astype(BF16), b.astype(BF16))


def _bdot_tn(a, b):
    return _dot_tn(a.astype(BF16), b.astype(BF16))


def _split_bf16(a):
    hi = a.astype(BF16)
    return hi, (a - hi.astype(F32)).astype(BF16)


def _dot3(a, b):
    ah, al = _split_bf16(a)
    bh, bl = _split_bf16(b)
    return _dot(ah, bh) + (_dot(ah, bl) + _dot(al, bh))


def _split3_bf16(a):
    hi = a.astype(BF16)
    r1 = a - hi.astype(F32)
    mid = r1.astype(BF16)
    return hi, mid, (r1 - mid.astype(F32)).astype(BF16)


def _dot_r01(a, b01):
    b = b01.astype(BF16)
    hi, mid, lo = _split3_bf16(a)
    return _dot(hi, b) + (_dot(mid, b) + _dot(lo, b))


def _dot_l01(a01, b):
    a = a01.astype(BF16)
    hi, mid, lo = _split3_bf16(b)
    return _dot(a, hi) + (_dot(a, mid) + _dot(a, lo))


def _dot_nt_l01(a01, b):
    a = a01.astype(BF16)
    hi, mid, lo = _split3_bf16(b)
    return _dot_nt(a, hi) + (_dot_nt(a, mid) + _dot_nt(a, lo))


def _iota(shape, dim):
    return lax.broadcasted_iota(jnp.int32, shape, dim)


def _log_sigmoid(x):
    return jnp.minimum(x, 0.0) - jnp.log1p(jnp.exp(-jnp.abs(x)))


def _sigmoid(x):
    return 1.0 / (1.0 + jnp.exp(-x))


def _softplus(x):
    return jnp.maximum(x, 0.0) + jnp.log1p(jnp.exp(-jnp.abs(x)))


def _layer_norm_rows(z, g, b):
    mu = jnp.mean(z, axis=-1, keepdims=True)
    zc = z - mu
    var = jnp.mean(zc * zc, axis=-1, keepdims=True)
    return zc * lax.rsqrt(var + LN_EPS) * g + b


def _rope128(y, cos, sin_signed):
    lane = _iota(y.shape, 1)
    lo = (lane & 63) < 32
    sw = jnp.where(lo, pltpu.roll(y, 96, 1), pltpu.roll(y, 32, 1))
    return y * cos + sw * sin_signed


def _rope_tables(pos):
    half = NSA_DH // 2
    freq = ROPE_THETA ** (-jnp.arange(half, dtype=F32) / half)
    ang = pos.astype(F32)[:, None] * freq[None, :]
    cos, sin = jnp.cos(ang), jnp.sin(ang)
    return jnp.tile(cos, (1, 4)), jnp.tile(jnp.concatenate([-sin, sin], -1), (1, 2))


def _largest_divisor(n, options):
    for p in options:
        if n % p == 0:
            return p


def _proj_kernel(*refs, segs, use_rope, t_segs):
    if use_rope:
        x_ref, w_ref, cos_ref, sin_ref = refs[:4]
        outs = refs[4:]
    else:
        x_ref, w_ref = refs[:2]
        outs = refs[2:]
    t_outs = dict(zip(t_segs, outs[len(segs):]))
    xb = x_ref[...].astype(BF16)
    off = 0
    for si, ((wd, rp), o_ref) in enumerate(zip(segs, outs)):
        y = _dot(xb, w_ref[:, off:off + wd])
        if rp:
            y = jnp.concatenate([_rope128(y[:, c * LANES:(c + 1) * LANES], cos_ref[...], sin_ref[...])
                                 for c in range(wd // LANES)], axis=1) if wd > LANES else \
                _rope128(y, cos_ref[...], sin_ref[...])
        o_ref[...] = y
        if si in t_outs:
            t_outs[si][...] = y.T
        off += wd


def _proj_split(x, w, segs, rope=None, t_segs=(), bt=None):
    n, k = x.shape
    tm = min(256, n)
    wtot = sum(s[0] for s in segs)
    use_rope = rope is not None
    in_specs = [pl.BlockSpec((tm, k), lambda i: (i, 0)), pl.BlockSpec((k, wtot), lambda i: (0, 0))]
    args = [x, w]
    if use_rope:
        in_specs += [pl.BlockSpec((tm, LANES), lambda i: (i, 0))] * 2
        args += list(rope)
    out_shape = [jax.ShapeDtypeStruct((n, wd), F32) for wd, _ in segs]
    out_specs = [pl.BlockSpec((tm, wd), lambda i: (i, 0)) for wd, _ in segs]
    if t_segs:
        B, T = bt
        npt = T // tm
        for si in t_segs:
            wd = segs[si][0]
            out_shape.append(jax.ShapeDtypeStruct((B, wd, T), F32))
            out_specs.append(pl.BlockSpec((None, wd, tm), lambda i: (i // npt, 0, i % npt)))
    outs = pl.pallas_call(
        functools.partial(_proj_kernel, segs=tuple(segs), use_rope=use_rope, t_segs=tuple(t_segs)),
        out_shape=out_shape,
        grid=(n // tm,),
        in_specs=in_specs,
        out_specs=out_specs,
        compiler_params=_cp("parallel"),
    )(*args)
    return (outs[:len(segs)], outs[len(segs):]) if t_segs else outs


def _mm_res_ln_kernel(*refs, n_in):
    a_refs = refs[:n_in]
    w_refs = refs[n_in:2 * n_in]
    x_ref, g_ref, b_ref, o_ref = refs[2 * n_in:]
    y = _dot(a_refs[0][...].astype(BF16), w_refs[0][...])
    for a_ref, w_ref in zip(a_refs[1:], w_refs[1:]):
        y = y + _dot(a_ref[...].astype(BF16), w_ref[...])
    o_ref[...] = _layer_norm_rows(ALPHA * x_ref[...] + y, g_ref[...], b_ref[...])


def _mm_res_ln(a_list, w_list, x, g, b):
    n, d = x.shape
    tm = min(512, n)
    n_in = len(a_list)
    in_specs = ([pl.BlockSpec((tm, a.shape[1]), lambda i: (i, 0)) for a in a_list]
                + [pl.BlockSpec(w.shape, lambda i: (0, 0)) for w in w_list]
                + [pl.BlockSpec((tm, d), lambda i: (i, 0)), pl.BlockSpec((1, d), lambda i: (0, 0)),
                   pl.BlockSpec((1, d), lambda i: (0, 0))])
    return pl.pallas_call(
        functools.partial(_mm_res_ln_kernel, n_in=n_in),
        out_shape=jax.ShapeDtypeStruct((n, d), F32),
        grid=(n // tm,),
        in_specs=in_specs,
        out_specs=pl.BlockSpec((tm, d), lambda i: (i, 0)),
        compiler_params=_cp("parallel"),
    )(*a_list, *w_list, x, g.reshape(1, d), b.reshape(1, d))


def _mlstm_kernel(q_ref, k_ref, v_ref, o_ref, sm_ref, bias_ref, g_ref, c0_ref, n0_ref, m0_ref,
                  h_ref, c1_ref, n1_ref, m1_ref, c_scr, n_scr, m_scr, *, L):
    ci = pl.program_id(1)

    @pl.when(ci == 0)
    def _():
        c_scr[...] = c0_ref[...]
        n_scr[...] = n0_ref[...]
        m_scr[...] = m0_ref[...]

    sm = sm_ref[...] + bias_ref[...]
    lane = _iota(sm.shape, 1)
    gates = jnp.where(lane < M_HEADS, sm, _log_sigmoid(sm))
    eye8 = (_iota((8, LANES), 0) == _iota((8, LANES), 1)).astype(F32)
    gates_row = _dot_nt_l01(eye8, gates)
    r_i = _iota((L, L), 0)
    c_i = _iota((L, L), 1)
    tri = c_i <= r_i
    scale = M_DK ** -0.5
    hs = []
    for h in range(M_HEADS):
        ic_col = gates[:, h:h + 1]
        f_col = gates[:, M_HEADS + h:M_HEADS + h + 1]
        ic_row = gates_row[h:h + 1, :]
        f_row = gates_row[M_HEADS + h:M_HEADS + h + 1, :]
        b_col = jnp.sum(jnp.where(tri, jnp.broadcast_to(f_row, (L, L)), 0.0), axis=1, keepdims=True)
        b_row = jnp.sum(jnp.where(r_i <= c_i, jnp.broadcast_to(f_col, (L, L)), 0.0), axis=0, keepdims=True)
        m_prev = m_scr[h:h + 1, 0:1]
        qh = q_ref[:, h * M_DK:(h + 1) * M_DK]
        kh = k_ref[:, h * M_DK:(h + 1) * M_DK] * scale
        vh = v_ref[:, h * M_DV:(h + 1) * M_DV]
        hs.append(dict(ic_col=ic_col, ic_row=ic_row, b_col=b_col, b_row=b_row, m_prev=m_prev, qh=qh, kh=kh, vh=vh,
                       qb=qh.astype(BF16), kb=kh.astype(BF16), vb=vh.astype(BF16),
                       c_old=c_scr[h], n_old=n_scr[h:h + 1, :]))
    for d in hs:
        dmat = jnp.where(tri, d['b_col'] - d['b_row'] + d['ic_row'], NEG)
        inter = d['b_col'] + d['m_prev']
        d['mt'] = jnp.maximum(inter, jnp.max(dmat, axis=1, keepdims=True))
        d['wmat'] = jnp.where(tri, jnp.exp(dmat - d['mt']), 0.0)
        d['a'] = jnp.exp(inter - d['mt'])
    for d in hs:
        d['s'] = _dot_nt(d['qb'], d['kb']) * d['wmat']
        d['cq'] = _dot_nt(d['qb'], d['c_old'].astype(BF16))
    for d in hs:
        num = d['a'] * d['cq'] + _dot(d['s'].astype(BF16), d['vb'])
        den = d['a'] * jnp.sum(d['qh'] * d['n_old'], axis=1, keepdims=True) + jnp.sum(d['s'], axis=1, keepdims=True)
        d['hh'] = num / jnp.maximum(jnp.abs(den), jnp.exp(-d['mt']))
    for h, d in enumerate(hs):
        m_l = d['mt'][L - 1:L, :]
        b_l = d['b_col'][L - 1:L, :]
        g_l = jnp.exp(b_l - d['b_col'] + d['ic_col'] - m_l)
        a_l = jnp.exp(b_l + d['m_prev'] - m_l)
        c_scr[h] = a_l * d['c_old'] + _dot_tn((g_l * d['vh']).astype(BF16), d['kb'])
        n_scr[h:h + 1, :] = a_l * d['n_old'] + jnp.sum(g_l * d['kh'], axis=0, keepdims=True)
        m_scr[h:h + 1, :] = jnp.broadcast_to(m_l, (1, LANES))
    for h, d in enumerate(hs):
        hh = d['hh']
        mu = jnp.mean(hh, axis=1, keepdims=True)
        hc = hh - mu
        var = jnp.mean(hc * hc, axis=1, keepdims=True)
        hn = hc * lax.rsqrt(var + 1e-6) * g_ref[:, h * M_DV:(h + 1) * M_DV]
        h_ref[:, h * M_DV:(h + 1) * M_DV] = hn * _sigmoid(o_ref[:, h * M_DV:(h + 1) * M_DV])

    @pl.when(ci == pl.num_programs(1) - 1)
    def _():
        c1_ref[...] = c_scr[...]
        n1_ref[...] = n_scr[...]
        m1_ref[...] = m_scr[...]


def _mlstm(mq, mk, mv, mo, small, bias_row, norm_g, c0, n0, m0, B, T):
    L = M_CHUNK if T % M_CHUNK == 0 else T
    nch = T // L
    row = lambda b, c: (b * nch + c, 0)
    st4 = lambda b, c: (b, 0, 0, 0)
    st3 = lambda b, c: (b, 0, 0)
    m0b = jnp.broadcast_to(m0[:, :, None], (B, M_HEADS, LANES)).astype(F32)
    h, c1, n1, m1 = pl.pallas_call(
        functools.partial(_mlstm_kernel, L=L),
        out_shape=[jax.ShapeDtypeStruct((B * T, MIX), F32),
                   jax.ShapeDtypeStruct((B, M_HEADS, M_DV, M_DK), F32),
                   jax.ShapeDtypeStruct((B, M_HEADS, M_DK), F32),
                   jax.ShapeDtypeStruct((B, M_HEADS, LANES), F32)],
        grid=(B, nch),
        in_specs=[pl.BlockSpec((L, MIX), row)] * 4
        + [pl.BlockSpec((L, LANES), row), pl.BlockSpec((1, LANES), lambda b, c: (0, 0)),
           pl.BlockSpec((1, MIX), lambda b, c: (0, 0)),
           pl.BlockSpec((None, M_HEADS, M_DV, M_DK), st4), pl.BlockSpec((None, M_HEADS, M_DK), st3),
           pl.BlockSpec((None, M_HEADS, LANES), st3)],
        out_specs=[pl.BlockSpec((L, MIX), row), pl.BlockSpec((None, M_HEADS, M_DV, M_DK), st4),
                   pl.BlockSpec((None, M_HEADS, M_DK), st3), pl.BlockSpec((None, M_HEADS, LANES), st3)],
        scratch_shapes=[pltpu.VMEM((M_HEADS, M_DV, M_DK), F32), pltpu.VMEM((M_HEADS, M_DK), F32),
                        pltpu.VMEM((M_HEADS, LANES), F32)],
        compiler_params=_cp("parallel", "arbitrary"),
    )(mq, mk, mv, mo, small, bias_row, norm_g.reshape(1, MIX), c0.astype(F32), n0.astype(F32), m0b)
    return h, c1, n1, m1[:, :, 0]


def _compress_weights(pe, w1, w2):
    eye = jnp.eye(NSA_GROUPS, dtype=F32)
    w1j = jnp.einsum('jdo,gh->jgdho', w1.reshape(CMP_BLOCK, NSA_DH, NSA_DH), eye)
    w1j = w1j.reshape(CMP_BLOCK, LANES, LANES).astype(BF16)
    w2b = jnp.einsum('do,gh->gdho', w2, eye).reshape(LANES, LANES).astype(BF16)
    pe2 = jnp.tile(pe, (1, NSA_GROUPS))
    return pe2, w1j, w2b


def _compress_kernel(x_ref, pe_ref, w1_ref, w2_ref, o_ref):
    xb = (x_ref[...] + pe_ref[...]).astype(BF16)
    hid = jax.nn.gelu(_dot(xb, w1_ref[...]))
    o_ref[...] = _dot(hid.astype(BF16), w2_ref[...])


def _compress(blocks, pe, w1, w2):
    r = blocks.shape[0]
    tb = min(256, r)
    width = CMP_BLOCK * LANES
    pe2, w1j, w2b = _compress_weights(pe, w1, w2)
    return pl.pallas_call(
        _compress_kernel,
        out_shape=jax.ShapeDtypeStruct((r, LANES), F32),
        grid=(pl.cdiv(r, tb),),
        in_specs=[pl.BlockSpec((tb, width), lambda i: (i, 0)), pl.BlockSpec((1, width), lambda i: (0, 0)),
                  pl.BlockSpec((width, LANES), lambda i: (0, 0)), pl.BlockSpec((LANES, LANES), lambda i: (0, 0))],
        out_specs=pl.BlockSpec((tb, LANES), lambda i: (i, 0)),
        compiler_params=_cp("parallel"),
    )(blocks, pe2.reshape(1, width), w1j.reshape(width, LANES), w2b)


def _compress_pool_kernel(xt_ref, pe_ref, w1_ref, w2_ref, o_ref, xs_scr, *, pb):
    per_page = PAGE // CMP_BLOCK
    for p in range(pb):
        xs_scr[p * PAGE:(p + 1) * PAGE, :] = xt_ref[p].T + pe_ref[...]
    acc = None
    for j in range(CMP_BLOCK):
        rows = xs_scr[pl.ds(j, per_page * pb, stride=CMP_BLOCK), :]
        t = _dot(rows.astype(BF16), w1_ref[j])
        acc = t if acc is None else acc + t
    o_ref[...] = _dot(jax.nn.gelu(acc).astype(BF16), w2_ref[...])


def _compress_pool(pool_t, pe, w1, w2):
    npool = pool_t.shape[0]
    per_page = PAGE // CMP_BLOCK
    pb = _largest_divisor(npool, (32, 16, 8, 4, 2))
    pe2, w1j, w2b = _compress_weights(pe, w1, w2)
    pe_tile = jnp.tile(pe2, (per_page, 1))
    return pl.pallas_call(
        functools.partial(_compress_pool_kernel, pb=pb),
        out_shape=jax.ShapeDtypeStruct((npool * per_page, LANES), F32),
        grid=(npool // pb,),
        in_specs=[pl.BlockSpec((pb, LANES, PAGE), lambda i: (i, 0, 0)), pl.BlockSpec((PAGE, LANES), lambda i: (0, 0)),
                  pl.BlockSpec((CMP_BLOCK, LANES, LANES), lambda i: (0, 0, 0)),
                  pl.BlockSpec((LANES, LANES), lambda i: (0, 0))],
        out_specs=pl.BlockSpec((per_page * pb, LANES), lambda i: (i, 0)),
        scratch_shapes=[pltpu.VMEM((pb * PAGE, LANES), F32)],
        compiler_params=_cp("parallel"),
    )(pool_t, pe_tile, w1j, w2b)


def _rope_rows_kernel(x_ref, cos_ref, sin_ref, o_ref):
    o_ref[...] = _rope128(x_ref[...], cos_ref[...], sin_ref[...])


def _rope_rows(x, cos, sin):
    r = x.shape[0]
    tb = min(512, r)
    spec = pl.BlockSpec((tb, LANES), lambda i: (i, 0))
    return pl.pallas_call(_rope_rows_kernel, out_shape=jax.ShapeDtypeStruct((r, LANES), F32), grid=(r // tb,),
                          in_specs=[spec] * 3, out_specs=spec, compiler_params=_cp("parallel"))(x, cos, sin)


def _stack_heads(qblk, heads, scale):
    rows = []
    for src_blk, src_half, dst_half in heads:
        x = qblk[:, src_blk * LANES:(src_blk + 1) * LANES] * scale
        if src_half != dst_half:
            x = pltpu.roll(x, 64, 1)
        lane = _iota(x.shape, 1)
        keep = (lane >= 64) if dst_half else (lane < 64)
        rows.append(jnp.where(keep, x, 0.0))
    return jnp.concatenate(rows, axis=0)


def _unstack_heads(o_full_list, heads_per_kb, tq, n_out_blocks):
    outs = [None] * n_out_blocks
    for o_full, heads in zip(o_full_list, heads_per_kb):
        for idx, (src_blk, src_half, dst_half) in enumerate(heads):
            piece = o_full[idx * tq:(idx + 1) * tq]
            if src_half != dst_half:
                piece = pltpu.roll(piece, 64, 1)
            lane = _iota(piece.shape, 1)
            keep = (lane >= 64) if src_half else (lane < 64)
            prev = outs[src_blk]
            outs[src_blk] = jnp.where(keep, piece, 0.0 if prev is None else prev)
    return outs


def _lane_repeat(x, width):
    if width == LANES:
        return x
    if width % LANES == 0:
        return pltpu.repeat(x, width // LANES, axis=1)
    return jnp.broadcast_to(x[:, 0:1], (x.shape[0], width))


def _stack_wide(qblk, n_heads, scale):
    head_of_lane = _iota(qblk.shape, 1) >> 6
    return jnp.concatenate([jnp.where(head_of_lane == h, qblk * scale, 0.0) for h in range(n_heads)], axis=0)


def _unstack_wide(o_full, n_heads, tq):
    head_of_lane = _iota((tq, o_full.shape[1]), 1) >> 6
    out = jnp.zeros((tq, o_full.shape[1]), F32)
    for h in range(n_heads):
        out = jnp.where(head_of_lane == h, o_full[h * tq:(h + 1) * tq], out)
    return out


NSA_HEAD_MAP = [[(h // 2, h % 2, h // (NSA_HEADS // NSA_GROUPS)) for h in range(NSA_HEADS)]]
FOX_HEAD_MAP = [[(kb, 0, 0), (kb, 1, 1)] for kb in range(FOX_HEADS // 2)]


def _cmp_sel_kernel(q_ref, kc_ref, vc_ref, cend_ref, o_ref, sel_ref, *, tq, q0, nc, ns, ns_pad):
    qi = pl.program_id(1)
    heads = NSA_HEAD_MAP[0]
    hpg = NSA_HEADS // NSA_GROUPS
    qbd = _stack_heads(q_ref[...], heads, NSA_DH ** -0.5).astype(BF16)
    s = _dot_nt(qbd, kc_ref[...].astype(BF16))
    rows = NSA_HEADS * tq
    qpos = q0 + qi * tq + (_iota((rows, 1), 0) & (tq - 1))
    mask = cend_ref[...] <= qpos
    s = jnp.where(mask, s, NEG)
    e = jnp.where(mask, jnp.exp(s - jnp.max(s, axis=1, keepdims=True)), 0.0)
    p = e / jnp.maximum(jnp.sum(e, axis=1, keepdims=True), 1e-30)
    o_full = _dot(p.astype(BF16), vc_ref[...].astype(BF16))
    outs = _unstack_heads([o_full], NSA_HEAD_MAP, tq, MIX // LANES)
    for c, blk in enumerate(outs):
        o_ref[:, c * LANES:(c + 1) * LANES] = blk
    half = nc // 2
    qp = q0 + qi * tq + _iota((tq, 1), 0)
    blk_i = _iota((tq, ns_pad), 1)
    blk_f = blk_i.astype(F32)
    cur = qp >> 6
    forced = (blk_i == 0) | (blk_i == cur) | (blk_i == cur - 1)
    valid = (blk_i * SEL_BLOCK <= qp) & (blk_i < ns)
    n_sel = min(N_SEL, ns)
    scores, chosen = [], []
    for g in range(NSA_GROUPS):
        pg = p[g * hpg * tq:(g * hpg + 1) * tq]
        for j in range(1, hpg):
            pg = pg + p[(g * hpg + j) * tq:(g * hpg + j + 1) * tq]
        imp = pg[:, :half] + pg[:, half:]
        if ns_pad > half:
            imp = jnp.concatenate([imp, jnp.zeros((tq, ns_pad - half), F32)], axis=1)
        score = jnp.where(valid, imp + FORCE * forced.astype(F32), NEG)
        scores.append(jnp.where(blk_i < ns, score, -jnp.inf))
        chosen.append(jnp.zeros((tq, ns_pad), F32))
    for _ in range(n_sel):
        for g in range(NSA_GROUPS):
            mx = jnp.max(scores[g], axis=1, keepdims=True)
            first = jnp.min(jnp.where(scores[g] == mx, blk_f, 1e9), axis=1, keepdims=True)
            hit = blk_f == first
            chosen[g] = jnp.where(hit, 1.0, chosen[g])
            scores[g] = jnp.where(hit, -jnp.inf, scores[g])
    for g in range(NSA_GROUPS):
        sel_ref[:, g * ns_pad:(g + 1) * ns_pad] = jnp.where(valid, chosen[g], 0.0)


def _cmp_sel(q3, kcmp, vcmp, cend, q0, ns, ns_pad):
    B, T, _ = q3.shape
    nc = kcmp.shape[1]
    tq = min(TQ_CMP, T)
    return pl.pallas_call(
        functools.partial(_cmp_sel_kernel, tq=tq, q0=q0, nc=nc, ns=ns, ns_pad=ns_pad),
        out_shape=[jax.ShapeDtypeStruct((B, T, MIX), F32), jax.ShapeDtypeStruct((B, T, NSA_GROUPS * ns_pad), F32)],
        grid=(B, T // tq),
        in_specs=[pl.BlockSpec((None, tq, MIX), lambda b, i: (b, i, 0)),
                  pl.BlockSpec((None, nc, LANES), lambda b, i: (b, 0, 0)),
                  pl.BlockSpec((None, nc, LANES), lambda b, i: (b, 0, 0)),
                  pl.BlockSpec((1, nc), lambda b, i: (0, 0))],
        out_specs=[pl.BlockSpec((None, tq, MIX), lambda b, i: (b, i, 0)),
                   pl.BlockSpec((None, tq, NSA_GROUPS * ns_pad), lambda b, i: (b, i, 0))],
        compiler_params=_cp("parallel", "arbitrary"),
    )(q3, kcmp, vcmp, cend)


def _flash_kernel(*refs, cfg):
    mode, layout, tq, tk = cfg['mode'], cfg['layout'], cfg['tq'], cfg['tk']
    q0, kbase, pps, has_tail, tail_pos0 = cfg['q0'], cfg['kbase'], cfg['pps'], cfg['has_tail'], cfg['tail_pos0']
    ns_pad, n_pref, qw, kv_t = cfg['ns_pad'], cfg['n_pref'], cfg['qw'], cfg['kv_t']
    head_map = NSA_HEAD_MAP if layout == 'nsa' else FOX_HEAD_MAP
    nkb = 1 if layout in ('nsa', 'fox_wide') else FOX_HEADS // 2
    n_heads_kb = FOX_HEADS if layout == 'fox_wide' else len(head_map[0])
    rows = n_heads_kb * tq
    qi_ref, kj_ref, fl_ref = refs[:3]
    pos = n_pref
    q_ref = refs[pos]; pos += 1
    npg = max(pps, 1)
    k_refs = refs[pos:pos + npg]; pos += npg
    v_refs = refs[pos:pos + npg]; pos += npg
    if has_tail:
        tk_ref, tv_ref = refs[pos:pos + 2]; pos += 2
    c_ref = ct_ref = sel_ref = None
    if cfg['has_caug'] or cfg['has_crow']:
        c_ref = refs[pos]; pos += 1
        if has_tail:
            ct_ref = refs[pos]; pos += 1
    if mode == 'sel':
        sel_ref = refs[pos]; pos += 1
    o_ref = refs[pos]; pos += 1
    qst_scr, m_scr, l_scr, acc_scr = refs[pos:pos + 4]

    s_id = pl.program_id(1)
    qi = qi_ref[s_id]
    kj = kj_ref[s_id]
    fl = fl_ref[s_id]
    scale = NSA_DH ** -0.5 * LOG2E

    @pl.when((fl & 1) != 0)
    def _init():
        qblk = q_ref[...]
        if layout == 'fox_wide':
            qst_scr[0] = _stack_wide(qblk, FOX_HEADS, scale).astype(BF16)
        elif layout == 'nsa':
            qbd = _stack_heads(qblk, head_map[0], scale).astype(BF16)
            if mode == 'sel':
                selblk = sel_ref[...]
                hpg = NSA_HEADS // NSA_GROUPS
                parts = []
                for g in range(NSA_GROUPS):
                    off = ((selblk[:, g * ns_pad:(g + 1) * ns_pad] - 1.0) * SEL_OFF).astype(BF16)
                    parts += [off] * hpg
                qbd = jnp.concatenate([qbd, jnp.concatenate(parts, axis=0)], axis=1)
            qst_scr[0] = qbd
        else:
            for kb in range(nkb):
                qbd = _stack_heads(qblk, head_map[kb], scale).astype(BF16)
                if cfg['has_caug']:
                    lane = _iota((rows, LANES), 1)
                    head = 2 * kb + (_iota((rows, LANES), 0) >= tq).astype(jnp.int32)
                    aug = jnp.where((lane >= 3 * head) & (lane < 3 * head + 3), -1.0, 0.0).astype(BF16)
                    qbd = jnp.concatenate([qbd, aug], axis=1)
                qst_scr[kb] = qbd
        m_scr[...] = jnp.full(m_scr.shape, NEG, F32)
        l_scr[...] = jnp.zeros(l_scr.shape, F32)
        acc_scr[...] = jnp.zeros(acc_scr.shape, F32)

    def update(k_tile, v_tile, kpos0, c_tile, width, masked):
        if masked:
            qpos = q0 + qi * tq + (_iota((rows, 1), 0) & (tq - 1))
            kpos = kpos0 + _iota((1, width), 1)
            mask = kpos <= qpos
            if mode == 'band':
                mask = mask & (kpos > qpos - WINDOW)
        for kb in range(nkb):
            if kv_t:
                kt = k_tile.astype(BF16)
                if mode == 'sel':
                    onehot = (_iota((ns_pad, width), 0) == (kpos0 >> 6) + (_iota((ns_pad, width), 1) >> 6))
                    kt = jnp.concatenate([kt, onehot.astype(BF16)], axis=0)
                sc = _dot(qst_scr[kb], kt)
                if c_tile is not None:
                    sc = sc - LOG2E * jnp.concatenate([jnp.broadcast_to(c_tile[h:h + 1, :], (tq, width))
                                                       for h in range(n_heads_kb)], axis=0)
            else:
                kt = k_tile[:, kb * LANES:(kb + 1) * LANES].astype(BF16)
                if mode == 'sel':
                    onehot = (_iota((width, ns_pad), 1) == (kpos0 >> 6) + (_iota((width, ns_pad), 0) >> 6))
                    kt = jnp.concatenate([kt, onehot.astype(BF16)], axis=1)
                if c_tile is not None:
                    kt = jnp.concatenate([kt, c_tile], axis=1)
                sc = _dot_nt(qst_scr[kb], kt)
            if masked:
                sc = jnp.where(mask, sc, NEG)
            m_old = m_scr[kb]
            m_new = jnp.maximum(m_old, jnp.max(sc, axis=1, keepdims=True))
            alpha = jnp.exp2(m_old - m_new)
            p = jnp.exp2(sc - _lane_repeat(m_new, width))
            l_scr[kb] = alpha * l_scr[kb] + jnp.sum(p, axis=1, keepdims=True)
            if kv_t:
                pv = _dot_nt(p.astype(BF16), v_tile.astype(BF16))
            else:
                pv = _dot(p.astype(BF16), v_tile[:, kb * LANES:(kb + 1) * LANES].astype(BF16))
            acc_scr[kb] = _lane_repeat(alpha, pv.shape[1]) * acc_scr[kb] + pv
            m_scr[kb] = m_new

    def main_update(masked):
        if pps > 1:
            k_tile = jnp.concatenate([r[...] for r in k_refs], axis=1)
            v_tile = jnp.concatenate([r[...] for r in v_refs], axis=1)
        else:
            k_tile, v_tile = k_refs[0][...], v_refs[0][...]
        update(k_tile, v_tile, kbase + kj * tk, c_ref[...] if c_ref is not None else None, tk, masked)

    is_tail = (fl & 4) != 0
    need_mask = (fl & 8) != 0
    if has_tail:
        @pl.when(is_tail)
        def _():
            update(tk_ref[...], tv_ref[...], tail_pos0, ct_ref[...] if ct_ref is not None else None, PAGE, True)

    @pl.when(jnp.logical_and(jnp.logical_not(is_tail), need_mask))
    def _():
        main_update(True)

    @pl.when(jnp.logical_and(jnp.logical_not(is_tail), jnp.logical_not(need_mask)))
    def _():
        main_update(False)

    @pl.when((fl & 2) != 0)
    def _fin():
        fulls = [acc_scr[kb] / _lane_repeat(jnp.maximum(l_scr[kb], 1e-30), acc_scr.shape[2]) for kb in range(nkb)]
        if layout == 'fox_wide':
            o_ref[...] = _unstack_wide(fulls[0], FOX_HEADS, tq)
        else:
            outs = _unstack_heads(fulls, head_map, tq, qw // LANES)
            for c, blk in enumerate(outs):
                o_ref[:, c * LANES:(c + 1) * LANES] = blk


def _schedule(nq, tq, tk, q0, kbase, n_ktiles, mode, has_tail):
    qi_l, kj_l, fl_l = [], [], []
    for qi in range(nq):
        q_lo, q_hi = q0 + qi * tq, q0 + (qi + 1) * tq - 1
        entries = []
        for kj in range(n_ktiles):
            k_lo, k_hi = kbase + kj * tk, kbase + (kj + 1) * tk - 1
            if k_lo > q_hi:
                continue
            if mode == 'band' and k_hi <= q_lo - WINDOW:
                continue
            all_visible = k_hi <= q_lo and (mode != 'band' or k_lo > q_hi - WINDOW)
            entries.append((kj, 0 if all_visible else 8))
        if has_tail:
            entries.append((entries[-1][0] if entries else 0, 4 | 8))
        for n, (kj, f) in enumerate(entries):
            f |= (1 if n == 0 else 0) | (2 if n == len(entries) - 1 else 0)
            qi_l.append(qi); kj_l.append(kj); fl_l.append(f)
    return (np.asarray(qi_l, np.int32), np.asarray(kj_l, np.int32), np.asarray(fl_l, np.int32))


def _flash(q3, k3, v3, *, mode, layout, tq, tk, q0, kbase=0, page_table=None, pps=0, tail=None, tail_pos0=0,
           caug=None, crow=None, crow_tail=None, sel=None, ns_pad=0):
    B, Tq, qw = q3.shape
    nq = Tq // tq
    paged = page_table is not None
    has_tail = tail is not None
    if paged:
        kw = k3.shape[1]
        npages = page_table.shape[1]
        tk = pps * PAGE
        n_ktiles = npages // pps
    else:
        kw = k3.shape[-1]
        n_ktiles = k3.shape[1] // tk
    qi_t, kj_t, fl_t = _schedule(nq, tq, tk, q0, kbase, n_ktiles, mode, has_tail)
    nsteps = len(qi_t)
    prefetch = [jnp.asarray(qi_t), jnp.asarray(kj_t), jnp.asarray(fl_t)]
    if paged:
        prefetch.append(page_table.reshape(-1).astype(jnp.int32))
    n_pref = len(prefetch)
    nkb = 1 if layout in ('nsa', 'fox_wide') else FOX_HEADS // 2
    n_heads_kb = FOX_HEADS if layout == 'fox_wide' else (NSA_HEADS if layout == 'nsa' else 2)
    rows = n_heads_kb * tq
    cq = kw if layout == 'fox_wide' else LANES
    if mode == 'sel':
        cq += ns_pad
    if caug is not None:
        cq += LANES
    cv = kw if layout == 'fox_wide' else LANES

    in_specs = [pl.BlockSpec((None, tq, qw), lambda b, s, qi, kj, fl, *_: (b, qi[s], 0))]
    args = [q3]
    if paged:
        def page_map(i):
            return lambda b, s, qi, kj, fl, pt: (pt[b * npages + kj[s] * pps + i], 0, 0)
        for arr in (k3, v3):
            for i in range(pps):
                in_specs.append(pl.BlockSpec((None, kw, PAGE), page_map(i)))
                args.append(arr)
    else:
        for arr in (k3, v3):
            in_specs.append(pl.BlockSpec((None, tk, kw), lambda b, s, qi, kj, fl, *_: (b, kj[s], 0)))
            args.append(arr)
    if has_tail:
        for arr in tail:
            in_specs.append(pl.BlockSpec((None, kw, PAGE), lambda b, s, qi, kj, fl, *_: (b, 0, 0)))
            args.append(arr)
    if caug is not None:
        in_specs.append(pl.BlockSpec((None, tk, LANES), lambda b, s, qi, kj, fl, *_: (b, kj[s], 0)))
        args.append(caug)
    if crow is not None:
        in_specs.append(pl.BlockSpec((None, 8, tk), lambda b, s, qi, kj, fl, *_: (b, 0, kj[s])))
        args.append(crow)
        if has_tail:
            in_specs.append(pl.BlockSpec((None, 8, PAGE), lambda b, s, qi, kj, fl, *_: (b, 0, 0)))
            args.append(crow_tail)
    if mode == 'sel':
        in_specs.append(pl.BlockSpec((None, tq, NSA_GROUPS * ns_pad), lambda b, s, qi, kj, fl, *_: (b, qi[s], 0)))
        args.append(sel)
    cfg = dict(mode=mode, layout=layout, tq=tq, tk=tk, q0=q0, kbase=kbase, pps=pps if paged else 0,
               has_tail=has_tail, tail_pos0=tail_pos0, ns_pad=ns_pad, n_pref=n_pref, qw=qw, kv_t=paged,
               has_caug=caug is not None, has_crow=crow is not None)
    return pl.pallas_call(
        functools.partial(_flash_kernel, cfg=cfg),
        out_shape=jax.ShapeDtypeStruct((B, Tq, qw), F32),
        grid_spec=pltpu.PrefetchScalarGridSpec(
            num_scalar_prefetch=n_pref,
            grid=(B, nsteps),
            in_specs=in_specs,
            out_specs=pl.BlockSpec((None, tq, qw), lambda b, s, qi, kj, fl, *_: (b, qi[s], 0)),
            scratch_shapes=[pltpu.VMEM((nkb, rows, cq), BF16), pltpu.VMEM((nkb, rows, LANES), F32),
                            pltpu.VMEM((nkb, rows, LANES), F32), pltpu.VMEM((nkb, rows, cv), F32)]),
        compiler_params=_cp("parallel", "arbitrary"),
    )(*prefetch, *args)


def _nsa_combine_kernel(oc_ref, os_ref, ow_ref, sm_ref, e_ref, o_ref):
    gate = _sigmoid(sm_ref[...])
    acc = None
    for j, br in enumerate((oc_ref, os_ref, ow_ref)):
        gexp = _dot_r01(gate, e_ref[j])
        acc = gexp * br[...] if acc is None else acc + gexp * br[...]
    o_ref[...] = acc


def _nsa_combine(o_cmp, o_slc, o_win, small):
    n = o_cmp.shape[0]
    tm = min(512, n)
    src = np.zeros((3, LANES, MIX), np.float32)
    for h in range(NSA_HEADS):
        for j in range(3):
            src[j, 2 * M_HEADS + 3 * h + j, h * NSA_DH:(h + 1) * NSA_DH] = 1.0
    spec = pl.BlockSpec((tm, MIX), lambda i: (i, 0))
    return pl.pallas_call(
        _nsa_combine_kernel, out_shape=jax.ShapeDtypeStruct((n, MIX), F32), grid=(n // tm,),
        in_specs=[spec, spec, spec, pl.BlockSpec((tm, LANES), lambda i: (i, 0)),
                  pl.BlockSpec((3, LANES, MIX), lambda i: (0, 0, 0))],
        out_specs=spec, compiler_params=_cp("parallel"))(o_cmp, o_slc, o_win, small, jnp.asarray(src, BF16))


def _fox_cumsum_kernel(sm_ref, bias_ref, place_ref, lf_ref, caug_ref, carry, *, tt):
    @pl.when(pl.program_id(1) == 0)
    def _():
        carry[...] = jnp.zeros(carry.shape, F32)

    lf = _log_sigmoid(sm_ref[...] + bias_ref[...])
    lf_ref[...] = lf
    lower = (_iota((tt, tt), 1) <= _iota((tt, tt), 0)).astype(F32)
    c = _dot_l01(lower, lf) + carry[0:1, :]
    carry[...] = jnp.broadcast_to(c[tt - 1:tt, :], carry.shape)
    c = c * LOG2E
    c_hi = c.astype(BF16)
    r1 = c - c_hi.astype(F32)
    c_mid = r1.astype(BF16)
    c_lo = (r1 - c_mid.astype(F32)).astype(BF16)
    caug = _dot(c_hi, place_ref[0]) + _dot(c_mid, place_ref[1]) + _dot(c_lo, place_ref[2])
    caug_ref[...] = caug.astype(BF16)


def _fox_cumsum(small, bias_row, B, T):
    tt = min(256, T)
    row = lambda b, i: (b * (T // tt) + i, 0)
    place = np.zeros((3, LANES, LANES), np.float32)
    for h in range(FOX_HEADS):
        for j in range(3):
            place[j, h, 3 * h + j] = 1.0
    return pl.pallas_call(
        functools.partial(_fox_cumsum_kernel, tt=tt),
        out_shape=[jax.ShapeDtypeStruct((B * T, LANES), F32), jax.ShapeDtypeStruct((B * T, LANES), BF16)],
        grid=(B, T // tt),
        in_specs=[pl.BlockSpec((tt, LANES), row), pl.BlockSpec((1, LANES), lambda b, i: (0, 0)),
                  pl.BlockSpec((3, LANES, LANES), lambda b, i: (0, 0, 0))],
        out_specs=[pl.BlockSpec((tt, LANES), row), pl.BlockSpec((tt, LANES), row)],
        scratch_shapes=[pltpu.VMEM((8, LANES), F32)],
        compiler_params=_cp("parallel", "arbitrary"),
    )(small, bias_row, jnp.asarray(place, BF16))


def _fox_cumsum_paged_kernel(pt_ref, *refs, pps, n_main):
    lf_refs = refs[:pps]
    sm_ref, bias_ref, c_ref, lfn_ref, carry = refs[pps:]
    j = pl.program_id(1)

    @pl.when(j == 0)
    def _():
        carry[...] = jnp.zeros(carry.shape, F32)

    upper = (_iota((PAGE, PAGE), 0) <= _iota((PAGE, PAGE), 1)).astype(F32)

    @pl.when(j < n_main)
    def _():
        run = carry[:, 0:1]
        for i in range(pps):
            c = _dot_r01(lf_refs[i][...], upper) + run
            c_ref[:, i * PAGE:(i + 1) * PAGE] = c
            run = c[:, PAGE - 1:PAGE]
        carry[...] = jnp.broadcast_to(run, carry.shape)

    @pl.when(j == n_main)
    def _():
        lf = _log_sigmoid(sm_ref[...] + bias_ref[...])
        lfn_ref[...] = lf
        eye8 = (_iota((8, LANES), 0) == _iota((8, LANES), 1)).astype(F32)
        c_new = _dot_r01(_dot_nt_l01(eye8, lf), upper) + carry[:, 0:1]
        for i in range(pps):
            c_ref[:, i * PAGE:(i + 1) * PAGE] = c_new


def _fox_cumsum_paged(lf_pool_t, page_table, small_pad, bias_row, pps):
    B, npages = page_table.shape
    n_main = npages // pps

    def page_map(i):
        return lambda b, j, pt: (pt[b * npages + jnp.minimum(j, n_main - 1) * pps + i], 0, 0)

    return pl.pallas_call(
        functools.partial(_fox_cumsum_paged_kernel, pps=pps, n_main=n_main),
        out_shape=[jax.ShapeDtypeStruct((B, 8, (n_main + 1) * pps * PAGE), F32),
                   jax.ShapeDtypeStruct((B, PAGE, LANES), F32)],
        grid_spec=pltpu.PrefetchScalarGridSpec(
            num_scalar_prefetch=1, grid=(B, n_main + 1),
            in_specs=[pl.BlockSpec((None, 8, PAGE), page_map(i)) for i in range(pps)]
            + [pl.BlockSpec((None, PAGE, LANES), lambda b, j, pt: (b, 0, 0)),
               pl.BlockSpec((1, LANES), lambda b, j, pt: (0, 0))],
            out_specs=[pl.BlockSpec((None, 8, pps * PAGE), lambda b, j, pt: (b, 0, j)),
                       pl.BlockSpec((None, PAGE, LANES), lambda b, j, pt: (b, 0, 0))],
            scratch_shapes=[pltpu.VMEM((8, LANES), F32)]),
        compiler_params=_cp("parallel", "arbitrary"),
    )(page_table.reshape(-1).astype(jnp.int32), *([lf_pool_t] * pps), small_pad, bias_row)


def _rwkv_prep_kernel(rw_ref, prev_ref, sh_ref, mu_ref, w0_ref, w2_ref, a0_ref, a2_ref, g2_ref, kk_ref, ka_ref,
                      bd_ref, r_o, lw_o, k_o, v_o, kk_o, a_o, g_o, *, tt):
    i = pl.program_id(1)
    rw = rw_ref[...]
    first_prev = jnp.where(i == 0, sh_ref[...], prev_ref[7:8, :])
    prev = jnp.where(_iota(rw.shape, 0) == 0, first_prev, pltpu.roll(rw, 1, 0))
    m = rw + mu_ref[...] * (prev - rw)
    r, kr, vr = m[:, 0:MIX], m[:, MIX:2 * MIX], m[:, 2 * MIX:3 * MIX]
    lora = m[:, 3 * MIX:3 * MIX + LANES]
    gl = m[:, 3 * MIX + LANES:3 * MIX + 2 * LANES]
    w = -_softplus(-(w0_ref[...] + _dot(jnp.tanh(lora).astype(BF16), w2_ref[...]))) - 0.5
    a = _sigmoid(a0_ref[...] + _dot(lora.astype(BF16), a2_ref[...]))
    kk = kr * kk_ref[...]
    ss = _dot_r01(kk * kk, bd_ref[...])
    r_o[...] = r
    lw_o[...] = -jnp.exp(w)
    k_o[...] = kr * (1.0 + (a - 1.0) * ka_ref[...])
    v_o[...] = vr
    kk_o[...] = kk / jnp.maximum(jnp.sqrt(ss), 1e-12)
    a_o[...] = a
    g_o[...] = _dot(_sigmoid(gl).astype(BF16), g2_ref[...])


def _rwkv_prep(rw3, shift0, P):
    B, T, wdt = rw3.shape
    tt = min(256, T)
    nblk8 = tt // 8
    row3 = lambda b, i: (b, i, 0)
    vec = lambda n: pl.BlockSpec((1, n), lambda b, i: (0, 0))
    outs = pl.pallas_call(
        functools.partial(_rwkv_prep_kernel, tt=tt),
        out_shape=[jax.ShapeDtypeStruct((B, T, MIX), F32)] * 7,
        grid=(B, T // tt),
        in_specs=[pl.BlockSpec((None, tt, wdt), row3),
                  pl.BlockSpec((None, 8, wdt), lambda b, i: (b, jnp.maximum(i * nblk8 - 1, 0), 0)),
                  pl.BlockSpec((None, 1, wdt), lambda b, i: (b, 0, 0)),
                  vec(wdt), vec(MIX), pl.BlockSpec((LANES, MIX), lambda b, i: (0, 0)),
                  vec(MIX), pl.BlockSpec((LANES, MIX), lambda b, i: (0, 0)),
                  pl.BlockSpec((LANES, MIX), lambda b, i: (0, 0)), vec(MIX), vec(MIX),
                  pl.BlockSpec((MIX, MIX), lambda b, i: (0, 0))],
        out_specs=[pl.BlockSpec((None, tt, MIX), row3)] * 7,
        compiler_params=_cp("parallel", "arbitrary"),
    )(rw3, rw3, shift0[:, None, :], P['mu'], P['w0'], P['w2'], P['a0'], P['a2'], P['g2'], P['k_k'], P['k_a'], P['bd64'])
    return outs


def _rwkv_scan_kernel(r_ref, lw_ref, k_ref, v_ref, kk_ref, a_ref, g_ref, s0_ref, rk_ref, lng_ref, lnb_ref,
                      y_ref, s1_ref, s_scr, *, L):
    ci = pl.program_id(1)

    @pl.when(ci == 0)
    def _():
        s_scr[...] = s0_ref[...]

    r_i = _iota((L, L), 0)
    c_i = _iota((L, L), 1)
    lower = (c_i <= r_i).astype(F32)
    strict = c_i < r_i
    incl = c_i <= r_i
    eye_l = (c_i == r_i).astype(F32)
    lane = _iota((L, LANES), 1)
    half_masks = (lane < 64, lane >= 64)
    rr = _iota((LANES, LANES), 0)
    cc = _iota((LANES, LANES), 1)
    bdmask = (rr >= 64) == (cc >= 64)
    eye128 = rr == cc
    bd_f = bdmask.astype(F32)
    pairs = range(RWKV_HEADS // 2)
    sls = [slice(pr * LANES, (pr + 1) * LANES) for pr in pairs]
    P = []
    for sl in sls:
        lw = lw_ref[:, sl]
        cl = _dot_l01(lower, lw)
        w_in = jnp.exp(cl)
        w_out = jnp.exp(-cl)
        kk = kk_ref[:, sl]
        P.append(dict(w_in=w_in, kap=kk * jnp.exp(cl - lw), bet=kk * a_ref[:, sl] * w_out,
                      kt=k_ref[:, sl] * w_out, rt=r_ref[:, sl] * w_in, v=v_ref[:, sl]))
    for pr, d in zip(pairs, P):
        d['sbd'] = s_scr[pr]
        d['rhs'] = _bdot(d['kap'], d['sbd'])
        d['ys'] = _bdot(d['rt'], d['sbd'])
    H = []
    for d in P:
        for hm in half_masks:
            kap_h = jnp.where(hm, d['kap'], 0.0)
            rt_h = jnp.where(hm, d['rt'], 0.0)
            amat = jnp.where(strict, _bdot_nt(kap_h, d['bet']), 0.0)
            H.append(dict(d=d, hm=hm, pw=amat, tinv=eye_l - amat,
                          cmat=jnp.where(strict, _bdot_nt(kap_h, d['kt']), 0.0),
                          rb=jnp.where(incl, _bdot_nt(rt_h, d['bet']), 0.0),
                          rk=jnp.where(incl, _bdot_nt(rt_h, d['kt']), 0.0)))
    def same_block(shift):
        return (r_i >> shift) == (c_i >> shift)

    for h in H:
        h['amat'] = h['pw']
        h['pw'] = jnp.where(same_block(3), h['amat'], 0.0)
        h['tinv'] = eye_l - h['pw']
    for _ in range(2):
        for h in H:
            h['pw'] = _dot3(h['pw'], h['pw'])
        for h in H:
            h['tinv'] = h['tinv'] + _dot3(h['tinv'], h['pw'])
    shift = 3
    while (1 << shift) < L:
        off_diag = jnp.logical_and(same_block(shift + 1), jnp.logical_not(same_block(shift)))
        for h in H:
            h['x'] = _dot3(jnp.where(off_diag, h['amat'], 0.0), h['tinv'])
        for h in H:
            h['tinv'] = h['tinv'] - _dot3(h['tinv'], h['x'])
        shift += 1
    for h in H:
        d, hm = h['d'], h['hm']
        h['v_h'] = jnp.where(hm, d['v'], 0.0)
        h['rhs_h'] = jnp.where(hm, d['rhs'], 0.0) + _bdot(h['cmat'], h['v_h'])
    for h in H:
        h['u_h'] = -_dot3(h['tinv'], h['rhs_h'])
    for h in H:
        h['y_h'] = _bdot(h['rb'], h['u_h']) + _bdot(h['rk'], h['v_h'])
    for pr, (d, sl) in enumerate(zip(P, sls)):
        h0, h1 = H[2 * pr], H[2 * pr + 1]
        u = h0['u_h'] + h1['u_h']
        y = d['ys'] + h0['y_h'] + h1['y_h']
        upd = jnp.where(bdmask, _bdot_tn(d['bet'], u) + _bdot_tn(d['kt'], d['v']), 0.0)
        w_end = jnp.broadcast_to(d['w_in'][L - 1:L, :], (LANES, LANES))
        w_end_col = jnp.sum(jnp.where(eye128, w_end, 0.0), axis=1, keepdims=True)
        s_scr[pr] = (d['sbd'] + upd) * w_end_col
        d['y'] = y
    for d, sl in zip(P, sls):
        y = d['y']
        mean = _dot_r01(y, bd_f) * (1.0 / RWKV_N)
        yc = y - mean
        var = _dot_r01(yc * yc, bd_f) * (1.0 / RWKV_N)
        yn = yc * lax.rsqrt(var + 64e-5) * lng_ref[:, sl] + lnb_ref[:, sl]
        bonus = _dot_r01(r_ref[:, sl] * k_ref[:, sl] * rk_ref[:, sl], bd_f)
        y_ref[:, sl] = (yn + bonus * d['v']) * g_ref[:, sl]

    @pl.when(ci == pl.num_programs(1) - 1)
    def _():
        s1_ref[...] = s_scr[...]


def _rwkv_scan(r, lw, k, v, kk, a, g, s0, P, B, T):
    L = RWKV_CHUNK
    tp = -(-T // L) * L
    if tp != T:
        pad = lambda t: jnp.pad(t, ((0, 0), (0, tp - T), (0, 0)))
        r, lw, k, v, kk, a, g = (pad(t) for t in (r, lw, k, v, kk, a, g))
    nch = tp // L
    st = jnp.swapaxes(s0.astype(F32), -1, -2).reshape(B, RWKV_HEADS // 2, 2, RWKV_N, RWKV_N)
    z = jnp.zeros_like(st[:, :, 0])
    sbd0 = jnp.concatenate([jnp.concatenate([st[:, :, 0], z], -1), jnp.concatenate([z, st[:, :, 1]], -1)], -2)
    row3 = lambda b, c: (b, c, 0)
    st_spec = pl.BlockSpec((None, RWKV_HEADS // 2, LANES, LANES), lambda b, c: (b, 0, 0, 0))
    vec = pl.BlockSpec((1, MIX), lambda b, c: (0, 0))
    y, sbd1 = pl.pallas_call(
        functools.partial(_rwkv_scan_kernel, L=L),
        out_shape=[jax.ShapeDtypeStruct((B, tp, MIX), F32),
                   jax.ShapeDtypeStruct((B, RWKV_HEADS // 2, LANES, LANES), F32)],
        grid=(B, nch),
        in_specs=[pl.BlockSpec((None, L, MIX), row3)] * 7 + [st_spec, vec, vec, vec],
        out_specs=[pl.BlockSpec((None, L, MIX), row3), st_spec],
        scratch_shapes=[pltpu.VMEM((RWKV_HEADS // 2, LANES, LANES), F32)],
        compiler_params=_cp("parallel", "arbitrary"),
    )(r, lw, k, v, kk, a, g, sbd0, P['r_k'], P['ln_g'], P['ln_b'])
    s_e = sbd1[:, :, :RWKV_N, :RWKV_N]
    s_o = sbd1[:, :, RWKV_N:, RWKV_N:]
    s1 = jnp.swapaxes(jnp.stack([s_e, s_o], 2).reshape(B, RWKV_HEADS, RWKV_N, RWKV_N), -1, -2)
    return y[:, :T], s1


def _mem_kernel(x_ref, km_ref, vm_ref, wq_ref, wo_ref, g_ref, b_ref, o_ref):
    x = x_ref[...]
    q = _dot(x.astype(BF16), wq_ref[...])
    heads = []
    for h in range(MEM_HEADS):
        sl = slice(h * MEM_DH, (h + 1) * MEM_DH)
        s = _dot_nt((q[:, sl] * (MEM_DH ** -0.5)).astype(BF16), km_ref[:, sl].astype(BF16))
        e = jnp.exp(s - jnp.max(s, axis=1, keepdims=True))
        p = e / jnp.sum(e, axis=1, keepdims=True)
        heads.append(_dot(p.astype(BF16), vm_ref[:, sl].astype(BF16)))
    o = jnp.concatenate(heads, axis=1)
    y = _dot(o.astype(BF16), wo_ref[...])
    o_ref[...] = _layer_norm_rows(ALPHA * x + y, g_ref[...], b_ref[...])


def _mem_layer(x3, km, vm, wq, wo, g, b):
    B, T, d = x3.shape
    tq = min(512, T)
    mt = km.shape[1]
    full = lambda shape: pl.BlockSpec(shape, lambda bb, i: (0,) * len(shape))
    return pl.pallas_call(
        _mem_kernel, out_shape=jax.ShapeDtypeStruct((B, T, d), F32), grid=(B, T // tq),
        in_specs=[pl.BlockSpec((None, tq, d), lambda bb, i: (bb, i, 0)),
                  pl.BlockSpec((None, mt, d), lambda bb, i: (bb, 0, 0)),
                  pl.BlockSpec((None, mt, d), lambda bb, i: (bb, 0, 0)),
                  full((d, d)), full((d, d)), full((1, d)), full((1, d))],
        out_specs=pl.BlockSpec((None, tq, d), lambda bb, i: (bb, i, 0)),
        compiler_params=_cp("parallel", "arbitrary"),
    )(x3, km, vm, wq, wo, g.reshape(1, d), b.reshape(1, d))


def _moe_route_kernel(x_ref, wr_ref, br_ref, gw_ref, cnt_ref):
    x = x_ref[...]
    logit = _dot(x, wr_ref[...], HI) + br_ref[...]
    lane_f = _iota(logit.shape, 1).astype(F32)
    is_g = lane_f < N_GROUPS
    gl = jnp.where(is_g, logit, -jnp.inf)
    gmax = jnp.max(gl, axis=1, keepdims=True)
    grp = jnp.min(jnp.where(gl == gmax, lane_f, 1e9), axis=1, keepdims=True)
    g_w = 1.0 / jnp.sum(jnp.where(is_g, jnp.exp(gl - gmax), 0.0), axis=1, keepdims=True)
    lo = N_GROUPS + EXP_PER_GROUP * grp
    in_grp = (lane_f >= lo) & (lane_f < lo + EXP_PER_GROUP)
    el = jnp.where(in_grp, logit, -jnp.inf)
    emax = jnp.max(el, axis=1, keepdims=True)
    ee = jnp.where(in_grp, jnp.exp(el - emax), 0.0)
    p = jnp.where(in_grp, ee / jnp.sum(ee, axis=1, keepdims=True), -1.0)
    p1 = jnp.max(p, axis=1, keepdims=True)
    i1 = jnp.min(jnp.where(p == p1, lane_f, 1e9), axis=1, keepdims=True)
    pr = jnp.where(lane_f == i1, -1.0, p)
    p2 = jnp.max(pr, axis=1, keepdims=True)
    i2 = jnp.min(jnp.where(pr == p2, lane_f, 1e9), axis=1, keepdims=True)
    den = p1 + p2
    member = jnp.where(lane_f == grp, 1.0, 0.0)
    gw_ref[...] = (member + jnp.where(lane_f == i1, g_w * p1 / den, 0.0)
                   + jnp.where(lane_f == i2, g_w * p2 / den, 0.0))
    cnt_ref[...] = jnp.broadcast_to(jnp.sum(member, axis=0, keepdims=True), cnt_ref.shape)


def _moe_group_kernel(cnt_ref, x_ref, gw_ref, w1_ref, w3_ref, w2_ref, g_ref, b_ref, o_ref,
                      xb_scr, gt_scr, rk_scr, xs_scr, gws_scr, ys_scr, acc_scr, *, tm, ch):
    i = pl.program_id(0)
    e = pl.program_id(1)
    grp = e // EXP_PER_GROUP
    cnt = cnt_ref[i * N_GROUPS + grp]
    slots = [slice(c * ch, (c + 1) * ch) for c in range(tm // ch)]

    @pl.when(e == 0)
    def _():
        xb_scr[...] = x_ref[...].astype(BF16)
        acc_scr[...] = jnp.zeros(acc_scr.shape, F32)
        member = gw_ref[...].T[0:8, :]
        gt_scr[...] = member
        before = (_iota((tm, tm), 0) < _iota((tm, tm), 1)).astype(BF16)
        rk_scr[...] = _dot(member.astype(BF16), before)

    def one_hot(c):
        member = gt_scr[pl.ds(grp, 1), :]
        rank = rk_scr[pl.ds(grp, 1), :]
        want = (c * ch + _iota((ch, 1), 0)).astype(F32)
        return jnp.where((member > 0.5) & (rank == want), 1.0, 0.0).astype(BF16)

    @pl.when(e % EXP_PER_GROUP == 0)
    def _():
        for c, rs in enumerate(slots):
            @pl.when(c * ch < cnt)
            def _():
                p = one_hot(c)
                xs_scr[rs, :] = _dot(p, xb_scr[...]).astype(BF16)
                gws_scr[rs, :] = _dot_l01(p, gw_ref[...])
                ys_scr[rs, :] = jnp.zeros((ch, ys_scr.shape[1]), F32)

    for c, rs in enumerate(slots):
        @pl.when(c * ch < cnt)
        def _():
            xs = xs_scr[rs, :]
            h1 = _dot(xs, w1_ref[...])
            h3 = _dot(xs, w3_ref[...])
            hid = h1 * _sigmoid(h1) * h3
            gws = gws_scr[rs, :]
            gcol = jnp.sum(jnp.where(_iota(gws.shape, 1) == e + N_GROUPS, gws, 0.0), axis=1, keepdims=True)
            ys_scr[rs, :] += gcol * _dot(hid.astype(BF16), w2_ref[...])

    @pl.when(e % EXP_PER_GROUP == EXP_PER_GROUP - 1)
    def _():
        for c, rs in enumerate(slots):
            @pl.when(c * ch < cnt)
            def _():
                p = one_hot(c)
                hi, lo = _split_bf16(ys_scr[rs, :])
                acc_scr[...] += _dot_tn(p, hi) + _dot_tn(p, lo)

    @pl.when(e == pl.num_programs(1) - 1)
    def _():
        o_ref[...] = _layer_norm_rows(ALPHA * x_ref[...] + acc_scr[...], g_ref[...], b_ref[...])


def _moe_layer_grouped(x, M, g, b):
    n, d = x.shape
    tm = min(1024, n)
    ch = min(256, tm)
    nt = n // tm
    gw, cnt = pl.pallas_call(
        _moe_route_kernel,
        out_shape=[jax.ShapeDtypeStruct((n, LANES), F32), jax.ShapeDtypeStruct((nt, 8, LANES), F32)],
        grid=(nt,),
        in_specs=[pl.BlockSpec((tm, d), lambda i: (i, 0)), pl.BlockSpec((d, LANES), lambda i: (0, 0)),
                  pl.BlockSpec((1, LANES), lambda i: (0, 0))],
        out_specs=[pl.BlockSpec((tm, LANES), lambda i: (i, 0)), pl.BlockSpec((None, 8, LANES), lambda i: (i, 0, 0))],
        compiler_params=_cp("parallel"),
    )(x, M['wr'], M['br'])
    counts = cnt[:, 0, :N_GROUPS].astype(jnp.int32).reshape(-1)
    full2 = lambda shape: pl.BlockSpec(shape, lambda i, e, c: (0, 0))
    return pl.pallas_call(
        functools.partial(_moe_group_kernel, tm=tm, ch=ch),
        out_shape=jax.ShapeDtypeStruct((n, d), F32),
        grid_spec=pltpu.PrefetchScalarGridSpec(
            num_scalar_prefetch=1, grid=(nt, N_EXPERTS),
            in_specs=[pl.BlockSpec((tm, d), lambda i, e, c: (i, 0)), pl.BlockSpec((tm, LANES), lambda i, e, c: (i, 0)),
                      pl.BlockSpec((None, d, D_EXPERT), lambda i, e, c: (e, 0, 0)),
                      pl.BlockSpec((None, d, D_EXPERT), lambda i, e, c: (e, 0, 0)),
                      pl.BlockSpec((None, D_EXPERT, d), lambda i, e, c: (e, 0, 0)),
                      full2((1, d)), full2((1, d))],
            out_specs=pl.BlockSpec((tm, d), lambda i, e, c: (i, 0)),
            scratch_shapes=[pltpu.VMEM((tm, d), BF16), pltpu.VMEM((8, tm), F32), pltpu.VMEM((8, tm), F32),
                            pltpu.VMEM((tm, d), BF16), pltpu.VMEM((tm, LANES), F32), pltpu.VMEM((tm, d), F32),
                            pltpu.VMEM((tm, d), F32)]),
        compiler_params=_cp("parallel", "arbitrary"),
    )(counts, x, gw, M['w1'], M['w3'], M['w2'], g.reshape(1, d), b.reshape(1, d))


EVEN_SEGS = ([(MIX, False)] * 4 + [(MIX, True), (LANES, False), (LANES, False), (LANES, True), (LANES, False),
                                   (LANES, True), (LANES, False), (LANES, False)])
ODD_SEGS = [(MIX, False)] * 3 + [(RWKV_SHIFT, False), (LANES, False)]
RW_PERM = np.concatenate([np.arange(0, 512), np.arange(576, 1088), np.arange(1088, 1600), np.arange(512, 576),
                          np.arange(1600, 1664), np.arange(1664, 1792)])
RW_INV = np.argsort(RW_PERM)


def _prep_even(w_in, w_out, b_i, b_f):
    zeros = jnp.zeros((D_MODEL, LANES - 2 * M_HEADS - 3 * NSA_HEADS), F32)
    w = jnp.concatenate([w_in[:, 0:1536], w_in[:, 1544:2056], w_in[:, 2056:3336], w_in[:, 1536:1544],
                         w_in[:, 3336:3360], zeros], axis=1).astype(BF16)
    bias = jnp.concatenate([b_i, b_f, jnp.zeros((LANES - 2 * M_HEADS,), F32)]).reshape(1, LANES)
    return dict(w_in=w, w_out_a=w_out[:MIX].astype(BF16), w_out_b=w_out[MIX:].astype(BF16), gate_bias=bias)


def _prep_odd(w_in, w_out, fox_b_f, mu, w0, w2, a0, a2, g2, k_k, k_a, r_k, ln_g, ln_b):
    rw_cols = w_in[:, 1544:3336][:, RW_PERM]
    zeros = jnp.zeros((D_MODEL, LANES - FOX_HEADS), F32)
    w = jnp.concatenate([w_in[:, 0:1536], rw_cols, w_in[:, 1536:1544], zeros], axis=1).astype(BF16)
    bias = jnp.concatenate([fox_b_f, jnp.zeros((LANES - FOX_HEADS,), F32)]).reshape(1, LANES)
    z64 = jnp.zeros((64, MIX), F32)
    hd = np.arange(MIX) // RWKV_N
    bd64 = jnp.asarray((hd[:, None] == hd[None, :]).astype(np.float32), BF16)
    P = dict(mu=mu[RW_PERM].reshape(1, -1), w0=w0.reshape(1, -1), a0=a0.reshape(1, -1),
             w2=jnp.concatenate([w2, z64], 0).astype(BF16), a2=jnp.concatenate([z64, a2], 0).astype(BF16),
             g2=g2.astype(BF16), k_k=k_k.reshape(1, -1), k_a=k_a.reshape(1, -1), r_k=r_k.reshape(1, -1),
             ln_g=ln_g.reshape(1, -1), ln_b=ln_b.reshape(1, -1), bd64=bd64)
    return dict(w_in=w, w_out_a=w_out[:MIX].astype(BF16), w_out_b=w_out[MIX:].astype(BF16), fox_bias=bias, P=P)


def _prep_moe(w_group, b_group, w_expert, b_expert, w1, w3, w2):
    padw = LANES - N_GROUPS - N_EXPERTS
    wr = jnp.concatenate([w_group, w_expert, jnp.zeros((D_MODEL, padw), F32)], axis=1)
    br = jnp.concatenate([b_group, b_expert, jnp.zeros((padw,), F32)]).reshape(1, LANES)
    return dict(wr=wr, br=br, w1=w1.astype(BF16), w3=w3.astype(BF16), w2=w2.astype(BF16))


def _even_odd_blocks(kcmp):
    return jnp.concatenate([kcmp[:, 0::2], kcmp[:, 1::2]], axis=1)


def _pages_t(cache):
    pool, page, h, dh = cache.shape
    return jnp.transpose(cache, (0, 2, 3, 1)).reshape(pool, h * dh, page)


def _tail_t(rows3):
    B, T, _ = rows3.shape
    return jnp.swapaxes(jnp.pad(rows3, ((0, 0), (0, PAGE - T), (0, 0))), 1, 2)


def _even_mixer(x3, q0, W, nsa_w, norm_g, c0, n0, m0, past, ln_g, ln_b):
    B, T, D = x3.shape
    N = B * T
    x2 = x3.reshape(N, D)
    pos = q0 + jnp.arange(T)
    cos, sin = _rope_tables(pos)
    rope = (jnp.tile(cos, (B, 1)), jnp.tile(sin, (B, 1)))
    sh = (B, T, NSA_GROUPS, NSA_DH)
    if past is None and T % 256 == 0:
        outs, outs_t = _proj_split(x2, W['w_in'], EVEN_SEGS, rope, t_segs=(5, 6, 7, 8), bt=(B, T))
        new_rows = [jnp.transpose(t.reshape(B, NSA_GROUPS, NSA_DH, T), (0, 3, 1, 2)) for t in outs_t]
    else:
        outs = _proj_split(x2, W['w_in'], EVEN_SEGS, rope)
        new_rows = [t.reshape(sh) for t in outs[5:9]]
    mq, mk, mv, mo, nq, kc, vc, ks, vs, kw, vw, small = outs
    h, c1, n1, m1 = _mlstm(mq, mk, mv, mo, small, W['gate_bias'], norm_g, c0, n0, m0, B, T)
    pe_k, pe_v, ck1, ck2, cv1, cv2 = nsa_w
    blk_w = CMP_BLOCK * LANES
    q3 = nq.reshape(B, T, MIX)
    if past is None:
        L = T
        nc = L // CMP_BLOCK
        kcmp = _compress(kc.reshape(B * nc, blk_w), pe_k, ck1, ck2).reshape(B, nc, LANES)
        vcmp = _compress(vc.reshape(B * nc, blk_w), pe_v, cv1, cv2).reshape(B, nc, LANES)
    else:
        pt = past['page_table']
        npages = pt.shape[1]
        L = npages * PAGE + T
        nc = L // CMP_BLOCK
        per_page = PAGE // CMP_BLOCK
        pool = past['kc'].shape[0]
        kcp = _compress_pool(_pages_t(past['kc']), pe_k, ck1, ck2).reshape(pool, per_page * LANES)
        vcp = _compress_pool(_pages_t(past['vc']), pe_v, cv1, cv2).reshape(pool, per_page * LANES)
        kcmp = jnp.take(kcp, pt, axis=0).reshape(B, npages * per_page, LANES)[:, :nc]
        vcmp = jnp.take(vcp, pt, axis=0).reshape(B, npages * per_page, LANES)[:, :nc]
    cpos = jnp.arange(nc) * CMP_BLOCK + CMP_BLOCK - 1
    ccos, csin = _rope_tables(cpos)
    kcmp = _rope_rows(kcmp.reshape(B * nc, LANES), jnp.tile(ccos, (B, 1)), jnp.tile(csin, (B, 1))).reshape(B, nc, LANES)
    ns = -(-L // SEL_BLOCK)
    ns_pad = -(-ns // LANES) * LANES
    cend = jnp.concatenate([cpos[0::2], cpos[1::2]]).astype(jnp.int32).reshape(1, nc)
    o_cmp, sel = _cmp_sel(q3, _even_odd_blocks(kcmp), _even_odd_blocks(vcmp), cend, q0, ns, ns_pad)
    ks3, vs3 = ks.reshape(B, T, LANES), vs.reshape(B, T, LANES)
    kw3, vw3 = kw.reshape(B, T, LANES), vw.reshape(B, T, LANES)
    if past is None:
        o_slc = _flash(q3, ks3, vs3, mode='sel', layout='nsa', tq=min(TQ_SLC, T), tk=min(TK_SLC, T), q0=0, sel=sel,
                       ns_pad=ns_pad)
        o_win = _flash(q3, kw3, vw3, mode='band', layout='nsa', tq=min(TQ_WIN, T), tk=min(TK_WIN, T), q0=0)
        kw_new, vw_new = kw3[:, -min(WINDOW, T):], vw3[:, -min(WINDOW, T):]
    else:
        o_slc = _flash(q3, _pages_t(past['ks']), _pages_t(past['vs']), mode='sel', layout='nsa', tq=T, tk=0, q0=q0,
                       page_table=pt, pps=_largest_divisor(npages, (16, 8, 4, 2, 1)), tail=(_tail_t(ks3), _tail_t(vs3)),
                       tail_pos0=q0, sel=sel, ns_pad=ns_pad)
        kw_ext = jnp.concatenate([past['kw_buf'], kw3], axis=1)
        vw_ext = jnp.concatenate([past['vw_buf'], vw3], axis=1)
        wb = past['kw_buf'].shape[1]
        o_win = _flash(q3, kw_ext, vw_ext, mode='band', layout='nsa', tq=T, tk=wb + T, q0=q0, kbase=q0 - wb)
        keep = min(WINDOW, wb + T)
        kw_new, vw_new = kw_ext[:, -keep:], vw_ext[:, -keep:]
    o = _nsa_combine(o_cmp.reshape(N, MIX), o_slc.reshape(N, MIX), o_win.reshape(N, MIX), small)
    x_new = _mm_res_ln([h, o], [W['w_out_a'], W['w_out_b']], x2, ln_g, ln_b).reshape(B, T, D)
    wsh = (B, -1, NSA_GROUPS, NSA_DH)
    state = (c1, n1, m1, *new_rows, kw_new.reshape(wsh), vw_new.reshape(wsh))
    return x_new, state


def _odd_mixer(x3, q0, W, s0, shift0, past, ln_g, ln_b):
    B, T, D = x3.shape
    N = B * T
    x2 = x3.reshape(N, D)
    hs = (B, T, FOX_HEADS, FOX_DH)
    if past is None and T % 256 == 0:
        outs, outs_t = _proj_split(x2, W['w_in'], ODD_SEGS, t_segs=(1, 2), bt=(B, T))
        new_kv = [jnp.transpose(t.reshape(B, FOX_HEADS, FOX_DH, T), (0, 3, 1, 2)) for t in outs_t]
    else:
        outs = _proj_split(x2, W['w_in'], ODD_SEGS)
        new_kv = [t.reshape(hs) for t in outs[1:3]]
    fq, fk, fv, rw, small = outs
    q3, k3, v3 = (t.reshape(B, T, MIX) for t in (fq, fk, fv))
    if past is None:
        lf_full, caug = _fox_cumsum(small, W['fox_bias'], B, T)
        lf = lf_full[:, :FOX_HEADS].reshape(B, T, FOX_HEADS)
        o_c = _flash(q3, k3, v3, mode='causal', layout='fox', tq=min(TQ_FOX, T), tk=min(TK_FOX, T), q0=0,
                     caug=caug.reshape(B, T, LANES))
    else:
        pt = past['page_table']
        npages = pt.shape[1]
        pps = _largest_divisor(npages, (16, 8, 4, 2, 1))
        padrows = ((0, 0), (0, PAGE - T), (0, 0))
        lf_pool_t = jnp.swapaxes(past['lf'], 1, 2)
        c_all, lf_new = _fox_cumsum_paged(lf_pool_t, pt, jnp.pad(small.reshape(B, T, LANES), padrows), W['fox_bias'], pps)
        lf = lf_new[:, :T, :FOX_HEADS]
        n_main = npages * PAGE
        o_c = _flash(q3, _pages_t(past['k']), _pages_t(past['v']), mode='causal', layout='fox_wide', tq=T, tk=0, q0=q0,
                     page_table=pt, pps=pps, tail=(_tail_t(k3), _tail_t(v3)), tail_pos0=q0,
                     crow=c_all, crow_tail=c_all[:, :, n_main:n_main + PAGE])
    rw3 = rw.reshape(B, T, RWKV_SHIFT)
    r, lw, k, v, kk, a, g = _rwkv_prep(rw3, shift0[:, RW_PERM], W['P'])
    y, s1 = _rwkv_scan(r, lw, k, v, kk, a, g, s0, W['P'], B, T)
    x_new = _mm_res_ln([o_c.reshape(N, MIX), y.reshape(N, MIX)], [W['w_out_a'], W['w_out_b']], x2, ln_g, ln_b)
    state = (*new_kv, lf, s1, rw3[:, -1][:, RW_INV])
    return x_new.reshape(B, T, D), state


def kernel(x_prompt, x_sample, state_mlstm_C, state_mlstm_n, state_mlstm_m, cache_nsa_kc, cache_nsa_vc, cache_nsa_ks, cache_nsa_vs, cache_nsa_kw, cache_nsa_vw, cache_fox_k, cache_fox_v, cache_fox_logf, state_rwkv_S, state_rwkv_shift, cache_mem_k, cache_mem_v, page_table, mem_prompt, even_w_in, even_w_out, mlstm_b_i, mlstm_b_f, mlstm_norm_g, nsa_pe_k, nsa_pe_v, nsa_cmp_k_w1, nsa_cmp_k_w2, nsa_cmp_v_w1, nsa_cmp_v_w2, odd_w_in, odd_w_out, fox_b_f, rwkv_mu, rwkv_w0, rwkv_w2, rwkv_a0, rwkv_a2, rwkv_g2, rwkv_k_k, rwkv_k_a, rwkv_r_k, rwkv_ln_g, rwkv_ln_b, mem_wq, mem_wk, mem_wv, mem_wo, moe_w_group, moe_b_group, moe_w_expert, moe_b_expert, moe_w1, moe_w3, moe_w2, ln_g, ln_b):
    bp, tp, d = x_prompt.shape
    bs, ts, _ = x_sample.shape
    depth = ln_g.shape[0]
    past = page_table.shape[1] * cache_nsa_kc.shape[2]
    xp, xs = x_prompt, x_sample
    acc = {}

    def push(prefix, names, vals):
        for nm, val in zip(names, vals):
            acc.setdefault(prefix + nm, []).append(val)

    even_names = ('mlstm_C', 'mlstm_n', 'mlstm_m', 'nsa_kc', 'nsa_vc', 'nsa_ks', 'nsa_vs', 'nsa_kw', 'nsa_vw')
    odd_names = ('fox_k', 'fox_v', 'fox_logf', 'rwkv_S', 'rwkv_shift')
    for layer in range(depth):
        e = layer // 2
        if layer % 2 == 0:
            W = _prep_even(even_w_in[e], even_w_out[e], mlstm_b_i[e], mlstm_b_f[e])
            nsa_w = (nsa_pe_k[e], nsa_pe_v[e], nsa_cmp_k_w1[e], nsa_cmp_k_w2[e], nsa_cmp_v_w1[e], nsa_cmp_v_w2[e])
            xp, sp = _even_mixer(xp, 0, W, nsa_w, mlstm_norm_g[e], jnp.zeros((bp, M_HEADS, M_DV, M_DK), F32),
                                 jnp.zeros((bp, M_HEADS, M_DK), F32), jnp.zeros((bp, M_HEADS), F32), None,
                                 ln_g[layer, 0], ln_b[layer, 0])
            wbuf = lambda c: c[e].reshape(bs, -1, LANES)
            past_d = dict(kc=cache_nsa_kc[e], vc=cache_nsa_vc[e], ks=cache_nsa_ks[e], vs=cache_nsa_vs[e],
                          page_table=page_table, kw_buf=wbuf(cache_nsa_kw), vw_buf=wbuf(cache_nsa_vw))
            xs, ss = _even_mixer(xs, past, W, nsa_w, mlstm_norm_g[e], state_mlstm_C[e], state_mlstm_n[e],
                                 state_mlstm_m[e], past_d, ln_g[layer, 0], ln_b[layer, 0])
            names = even_names
        else:
            W = _prep_odd(odd_w_in[e], odd_w_out[e], fox_b_f[e], rwkv_mu[e], rwkv_w0[e], rwkv_w2[e], rwkv_a0[e],
                          rwkv_a2[e], rwkv_g2[e], rwkv_k_k[e], rwkv_k_a[e], rwkv_r_k[e], rwkv_ln_g[e], rwkv_ln_b[e])
            xp, sp = _odd_mixer(xp, 0, W, jnp.zeros((bp, RWKV_HEADS, RWKV_N, RWKV_N), F32),
                                jnp.zeros((bp, RWKV_SHIFT), F32), None, ln_g[layer, 0], ln_b[layer, 0])
            past_d = dict(k=cache_fox_k[e], v=cache_fox_v[e], lf=cache_fox_logf[e], page_table=page_table)
            xs, ss = _odd_mixer(xs, past, W, state_rwkv_S[e], state_rwkv_shift[e], past_d,
                                ln_g[layer, 0], ln_b[layer, 0])
            names = odd_names
        push('p_', names, sp)
        push('s_', names, ss)
        wkv = jnp.concatenate([mem_wk[layer], mem_wv[layer]], axis=1).astype(BF16)
        mt = mem_prompt.shape[1]
        kmp, vmp = _proj_split(mem_prompt.reshape(bp * mt, d), wkv, [(d, False), (d, False)])
        push('p_', ('mem_k', 'mem_v'), (kmp.reshape(bp, mt, MEM_HEADS, MEM_DH), vmp.reshape(bp, mt, MEM_HEADS, MEM_DH)))
        wq, wo = mem_wq[layer].astype(BF16), mem_wo[layer].astype(BF16)
        xp = _mem_layer(xp, kmp.reshape(bp, mt, d), vmp.reshape(bp, mt, d), wq, wo, ln_g[layer, 1], ln_b[layer, 1])
        xs = _mem_layer(xs, cache_mem_k[layer].reshape(bs, -1, d), cache_mem_v[layer].reshape(bs, -1, d), wq, wo,
                        ln_g[layer, 1], ln_b[layer, 1])
        M = _prep_moe(moe_w_group[layer], moe_b_group[layer], moe_w_expert[layer], moe_b_expert[layer],
                      moe_w1[layer], moe_w3[layer], moe_w2[layer])
        xp = _moe_layer_grouped(xp.reshape(bp * tp, d), M, ln_g[layer, 2], ln_b[layer, 2]).reshape(bp, tp, d)
        xs = _moe_layer_grouped(xs.reshape(bs * ts, d), M, ln_g[layer, 2], ln_b[layer, 2]).reshape(bs, ts, d)

    st = {name: jnp.stack(vals) for name, vals in acc.items()}
    return (xp, xs,
            st['p_mlstm_C'], st['p_mlstm_n'], st['p_mlstm_m'],
            st['p_nsa_kc'], st['p_nsa_vc'], st['p_nsa_ks'], st['p_nsa_vs'], st['p_nsa_kw'], st['p_nsa_vw'],
            st['p_fox_k'], st['p_fox_v'], st['p_fox_logf'], st['p_rwkv_S'], st['p_rwkv_shift'],
            st['p_mem_k'], st['p_mem_v'],
            st['s_mlstm_C'], st['s_mlstm_n'], st['s_mlstm_m'],
            st['s_nsa_kc'], st['s_nsa_vc'], st['s_nsa_ks'], st['s_nsa_vs'], st['s_nsa_kw'], st['s_nsa_vw'],
            st['s_fox_k'], st['s_fox_v'], st['s_fox_logf'], st['s_rwkv_S'], st['s_rwkv_shift'])
```

```python
import functools

import numpy as np
import jax
import jax.numpy as jnp
from jax import lax
from jax.experimental import pallas as pl
from jax.experimental.pallas import tpu as pltpu

F32 = jnp.float32
BF16 = jnp.bfloat16
HI = lax.Precision.HIGHEST

D_MODEL = 1024
DEPTH = 2
PAGE = 128
M_HEADS, M_DK, M_DV, M_CHUNK = 4, 128, 128, 64
NSA_HEADS, NSA_GROUPS, NSA_DH = 8, 2, 64
CMP_BLOCK, SEL_BLOCK, N_SEL, WINDOW = 32, 64, 16, 512
FOX_HEADS, FOX_DH = 8, 64
RWKV_HEADS, RWKV_N, W_LORA, A_LORA, G_LORA = 8, 64, 64, 64, 128
RWKV_CHUNK = 64
MEM_HEADS = 4
MEM_DH = D_MODEL // MEM_HEADS
N_GROUPS, EXP_PER_GROUP, TOP_K, D_EXPERT = 4, 8, 2, 256
N_EXPERTS = N_GROUPS * EXP_PER_GROUP
ROPE_THETA = 10000.0
ALPHA = (2.0 * DEPTH) ** 0.25
LN_EPS = 1e-5
NEG = -1e30
FORCE = 1e4
MIX = 512
RWKV_SHIFT = 1792
SEL_OFF = 2.0 ** 30
LOG2E = 1.4426950408889634

LANES = 128
VMEM_LIMIT = 48 * 1024 * 1024
TQ_FOX, TK_FOX = 512, 512
TQ_SLC, TK_SLC = 256, 512
TQ_WIN, TK_WIN = 128, 128
TQ_CMP = 256
MOE_EXPERTS_PER_STEP = 2


def _cp(*sem):
    return pltpu.CompilerParams(dimension_semantics=sem, vmem_limit_bytes=VMEM_LIMIT)


def _dot(a, b, prec=None):
    return jnp.dot(a, b, preferred_element_type=F32, precision=prec)


def _dot_nt(a, b, prec=None):
    return lax.dot_general(a, b, (((1,), (1,)), ((), ())), preferred_element_type=F32, precision=prec)


def _dot_tn(a, b, prec=None):
    return lax.dot_general(a, b, (((0,), (0,)), ((), ())), preferred_element_type=F32, precision=prec)


def _bdot(a, b):
    return _dot(a.astype(BF16), b.astype(BF16))


def _bdot_nt(a, b):
    return _dot_nt(a.astype(BF16), b.astype(BF16))


def _bdot_tn(a, b):
    return _dot_tn(a.astype(BF16), b.astype(BF16))


def _split_bf16(a):
    hi = a.astype(BF16)
    return hi, (a - hi.astype(F32)).astype(BF16)


def _dot3(a, b):
    ah, al = _split_bf16(a)
    bh, bl = _split_bf16(b)
    return _dot(ah, bh) + (_dot(ah, bl) + _dot(al, bh))


def _split3_bf16(a):
    hi = a.astype(BF16)
    r1 = a - hi.astype(F32)
    mid = r1.astype(BF16)
    return hi, mid, (r1 - mid.astype(F32)).astype(BF16)


def _dot_r01(a, b01):
    b = b01.astype(BF16)
    hi, mid, lo = _split3_bf16(a)
    return _dot(hi, b) + (_dot(mid, b) + _dot(lo, b))


def _dot_l01(a01, b):
    a = a01.astype(BF16)
    hi, mid, lo = _split3_bf16(b)
    return _dot(a, hi) + (_dot(a, mid) + _dot(a, lo))


def _dot_nt_l01(a01, b):
    a = a01.astype(BF16)
    hi, mid, lo = _split3_bf16(b)
    return _dot_nt(a, hi) + (_dot_nt(a, mid) + _dot_nt(a, lo))


def _iota(shape, dim):
    return lax.broadcasted_iota(jnp.int32, shape, dim)


def _log_sigmoid(x):
    return jnp.minimum(x, 0.0) - jnp.log1p(jnp.exp(-jnp.abs(x)))


def _sigmoid(x):
    return 1.0 / (1.0 + jnp.exp(-x))


def _softplus(x):
    return jnp.maximum(x, 0.0) + jnp.log1p(jnp.exp(-jnp.abs(x)))


def _layer_norm_rows(z, g, b):
    mu = jnp.mean(z, axis=-1, keepdims=True)
    zc = z - mu
    var = jnp.mean(zc * zc, axis=-1, keepdims=True)
    return zc * lax.rsqrt(var + LN_EPS) * g + b


def _rope128(y, cos, sin_signed):
    lane = _iota(y.shape, 1)
    lo = (lane & 63) < 32
    sw = jnp.where(lo, pltpu.roll(y, 96, 1), pltpu.roll(y, 32, 1))
    return y * cos + sw * sin_signed


def _rope_tables(pos):
    half = NSA_DH // 2
    freq = ROPE_THETA ** (-jnp.arange(half, dtype=F32) / half)
    ang = pos.astype(F32)[:, None] * freq[None, :]
    cos, sin = jnp.cos(ang), jnp.sin(ang)
    return jnp.tile(cos, (1, 4)), jnp.tile(jnp.concatenate([-sin, sin], -1), (1, 2))


def _largest_divisor(n, options):
    for p in options:
        if n % p == 0:
            return p


def _proj_kernel(*refs, segs, use_rope, t_segs):
    if use_rope:
        x_ref, w_ref, cos_ref, sin_ref = refs[:4]
        outs = refs[4:]
    else:
        x_ref, w_ref = refs[:2]
        outs = refs[2:]
    t_outs = dict(zip(t_segs, outs[len(segs):]))
    xb = x_ref[...].astype(BF16)
    off = 0
    for si, ((wd, rp), o_ref) in enumerate(zip(segs, outs)):
        y = _dot(xb, w_ref[:, off:off + wd])
        if rp:
            y = jnp.concatenate([_rope128(y[:, c * LANES:(c + 1) * LANES], cos_ref[...], sin_ref[...])
                                 for c in range(wd // LANES)], axis=1) if wd > LANES else \
                _rope128(y, cos_ref[...], sin_ref[...])
        o_ref[...] = y
        if si in t_outs:
            t_outs[si][...] = y.T
        off += wd


def _proj_split(x, w, segs, rope=None, t_segs=(), bt=None):
    n, k = x.shape
    tm = min(256, n)
    wtot = sum(s[0] for s in segs)
    use_rope = rope is not None
    in_specs = [pl.BlockSpec((tm, k), lambda i: (i, 0)), pl.BlockSpec((k, wtot), lambda i: (0, 0))]
    args = [x, w]
    if use_rope:
        in_specs += [pl.BlockSpec((tm, LANES), lambda i: (i, 0))] * 2
        args += list(rope)
    out_shape = [jax.ShapeDtypeStruct((n, wd), F32) for wd, _ in segs]
    out_specs = [pl.BlockSpec((tm, wd), lambda i: (i, 0)) for wd, _ in segs]
    if t_segs:
        B, T = bt
        npt = T // tm
        for si in t_segs:
            wd = segs[si][0]
            out_shape.append(jax.ShapeDtypeStruct((B, wd, T), F32))
            out_specs.append(pl.BlockSpec((None, wd, tm), lambda i: (i // npt, 0, i % npt)))
    outs = pl.pallas_call(
        functools.partial(_proj_kernel, segs=tuple(segs), use_rope=use_rope, t_segs=tuple(t_segs)),
        out_shape=out_shape,
        grid=(n // tm,),
        in_specs=in_specs,
        out_specs=out_specs,
        compiler_params=_cp("parallel"),
    )(*args)
    return (outs[:len(segs)], outs[len(segs):]) if t_segs else outs


def _mm_res_ln_kernel(*refs, n_in):
    a_refs = refs[:n_in]
    w_refs = refs[n_in:2 * n_in]
    x_ref, g_ref, b_ref, o_ref = refs[2 * n_in:]
    y = _dot(a_refs[0][...].astype(BF16), w_refs[0][...])
    for a_ref, w_ref in zip(a_refs[1:], w_refs[1:]):
        y = y + _dot(a_ref[...].astype(BF16), w_ref[...])
    o_ref[...] = _layer_norm_rows(ALPHA * x_ref[...] + y, g_ref[...], b_ref[...])


def _mm_res_ln(a_list, w_list, x, g, b):
    n, d = x.shape
    tm = min(512, n)
    n_in = len(a_list)
    in_specs = ([pl.BlockSpec((tm, a.shape[1]), lambda i: (i, 0)) for a in a_list]
                + [pl.BlockSpec(w.shape, lambda i: (0, 0)) for w in w_list]
                + [pl.BlockSpec((tm, d), lambda i: (i, 0)), pl.BlockSpec((1, d), lambda i: (0, 0)),
                   pl.BlockSpec((1, d), lambda i: (0, 0))])
    return pl.pallas_call(
        functools.partial(_mm_res_ln_kernel, n_in=n_in),
        out_shape=jax.ShapeDtypeStruct((n, d), F32),
        grid=(n // tm,),
        in_specs=in_specs,
        out_specs=pl.BlockSpec((tm, d), lambda i: (i, 0)),
        compiler_params=_cp("parallel"),
    )(*a_list, *w_list, x, g.reshape(1, d), b.reshape(1, d))


def _mlstm_kernel(q_ref, k_ref, v_ref, o_ref, sm_ref, bias_ref, g_ref, c0_ref, n0_ref, m0_ref,
                  h_ref, c1_ref, n1_ref, m1_ref, c_scr, n_scr, m_scr, *, L):
    ci = pl.program_id(1)

    @pl.when(ci == 0)
    def _():
        c_scr[...] = c0_ref[...]
        n_scr[...] = n0_ref[...]
        m_scr[...] = m0_ref[...]

    sm = sm_ref[...] + bias_ref[...]
    lane = _iota(sm.shape, 1)
    gates = jnp.where(lane < M_HEADS, sm, _log_sigmoid(sm))
    eye8 = (_iota((8, LANES), 0) == _iota((8, LANES), 1)).astype(F32)
    gates_row = _dot_nt_l01(eye8, gates)
    r_i = _iota((L, L), 0)
    c_i = _iota((L, L), 1)
    tri = c_i <= r_i
    scale = M_DK ** -0.5
    hs = []
    for h in range(M_HEADS):
        ic_col = gates[:, h:h + 1]
        f_col = gates[:, M_HEADS + h:M_HEADS + h + 1]
        ic_row = gates_row[h:h + 1, :]
        f_row = gates_row[M_HEADS + h:M_HEADS + h + 1, :]
        b_col = jnp.sum(jnp.where(tri, jnp.broadcast_to(f_row, (L, L)), 0.0), axis=1, keepdims=True)
        b_row = jnp.sum(jnp.where(r_i <= c_i, jnp.broadcast_to(f_col, (L, L)), 0.0), axis=0, keepdims=True)
        m_prev = m_scr[h:h + 1, 0:1]
        qh = q_ref[:, h * M_DK:(h + 1) * M_DK]
        kh = k_ref[:, h * M_DK:(h + 1) * M_DK] * scale
        vh = v_ref[:, h * M_DV:(h + 1) * M_DV]
        hs.append(dict(ic_col=ic_col, ic_row=ic_row, b_col=b_col, b_row=b_row, m_prev=m_prev, qh=qh, kh=kh, vh=vh,
                       qb=qh.astype(BF16), kb=kh.astype(BF16), vb=vh.astype(BF16),
                       c_old=c_scr[h], n_old=n_scr[h:h + 1, :]))
    for d in hs:
        dmat = jnp.where(tri, d['b_col'] - d['b_row'] + d['ic_row'], NEG)
        inter = d['b_col'] + d['m_prev']
        d['mt'] = jnp.maximum(inter, jnp.max(dmat, axis=1, keepdims=True))
        d['wmat'] = jnp.where(tri, jnp.exp(dmat - d['mt']), 0.0)
        d['a'] = jnp.exp(inter - d['mt'])
    for d in hs:
        d['s'] = _dot_nt(d['qb'], d['kb']) * d['wmat']
        d['cq'] = _dot_nt(d['qb'], d['c_old'].astype(BF16))
    for d in hs:
        num = d['a'] * d['cq'] + _dot(d['s'].astype(BF16), d['vb'])
        den = d['a'] * jnp.sum(d['qh'] * d['n_old'], axis=1, keepdims=True) + jnp.sum(d['s'], axis=1, keepdims=True)
        d['hh'] = num / jnp.maximum(jnp.abs(den), jnp.exp(-d['mt']))
    for h, d in enumerate(hs):
        m_l = d['mt'][L - 1:L, :]
        b_l = d['b_col'][L - 1:L, :]
        g_l = jnp.exp(b_l - d['b_col'] + d['ic_col'] - m_l)
        a_l = jnp.exp(b_l + d['m_prev'] - m_l)
        c_scr[h] = a_l * d['c_old'] + _dot_tn((g_l * d['vh']).astype(BF16), d['kb'])
        n_scr[h:h + 1, :] = a_l * d['n_old'] + jnp.sum(g_l * d['kh'], axis=0, keepdims=True)
        m_scr[h:h + 1, :] = jnp.broadcast_to(m_l, (1, LANES))
    for h, d in enumerate(hs):
        hh = d['hh']
        mu = jnp.mean(hh, axis=1, keepdims=True)
        hc = hh - mu
        var = jnp.mean(hc * hc, axis=1, keepdims=True)
        hn = hc * lax.rsqrt(var + 1e-6) * g_ref[:, h * M_DV:(h + 1) * M_DV]
        h_ref[:, h * M_DV:(h + 1) * M_DV] = hn * _sigmoid(o_ref[:, h * M_DV:(h + 1) * M_DV])

    @pl.when(ci == pl.num_programs(1) - 1)
    def _():
        c1_ref[...] = c_scr[...]
        n1_ref[...] = n_scr[...]
        m1_ref[...] = m_scr[...]


def _mlstm(mq, mk, mv, mo, small, bias_row, norm_g, c0, n0, m0, B, T):
    L = M_CHUNK if T % M_CHUNK == 0 else T
    nch = T // L
    row = lambda b, c: (b * nch + c, 0)
    st4 = lambda b, c: (b, 0, 0, 0)
    st3 = lambda b, c: (b, 0, 0)
    m0b = jnp.broadcast_to(m0[:, :, None], (B, M_HEADS, LANES)).astype(F32)
    h, c1, n1, m1 = pl.pallas_call(
        functools.partial(_mlstm_kernel, L=L),
        out_shape=[jax.ShapeDtypeStruct((B * T, MIX), F32),
                   jax.ShapeDtypeStruct((B, M_HEADS, M_DV, M_DK), F32),
                   jax.ShapeDtypeStruct((B, M_HEADS, M_DK), F32),
                   jax.ShapeDtypeStruct((B, M_HEADS, LANES), F32)],
        grid=(B, nch),
        in_specs=[pl.BlockSpec((L, MIX), row)] * 4
        + [pl.BlockSpec((L, LANES), row), pl.BlockSpec((1, LANES), lambda b, c: (0, 0)),
           pl.BlockSpec((1, MIX), lambda b, c: (0, 0)),
           pl.BlockSpec((None, M_HEADS, M_DV, M_DK), st4), pl.BlockSpec((None, M_HEADS, M_DK), st3),
           pl.BlockSpec((None, M_HEADS, LANES), st3)],
        out_specs=[pl.BlockSpec((L, MIX), row), pl.BlockSpec((None, M_HEADS, M_DV, M_DK), st4),
                   pl.BlockSpec((None, M_HEADS, M_DK), st3), pl.BlockSpec((None, M_HEADS, LANES), st3)],
        scratch_shapes=[pltpu.VMEM((M_HEADS, M_DV, M_DK), F32), pltpu.VMEM((M_HEADS, M_DK), F32),
                        pltpu.VMEM((M_HEADS, LANES), F32)],
        compiler_params=_cp("parallel", "arbitrary"),
    )(mq, mk, mv, mo, small, bias_row, norm_g.reshape(1, MIX), c0.astype(F32), n0.astype(F32), m0b)
    return h, c1, n1, m1[:, :, 0]


def _compress_weights(pe, w1, w2):
    eye = jnp.eye(NSA_GROUPS, dtype=F32)
    w1j = jnp.einsum('jdo,gh->jgdho', w1.reshape(CMP_BLOCK, NSA_DH, NSA_DH), eye)
    w1j = w1j.reshape(CMP_BLOCK, LANES, LANES).astype(BF16)
    w2b = jnp.einsum('do,gh->gdho', w2, eye).reshape(LANES, LANES).astype(BF16)
    pe2 = jnp.tile(pe, (1, NSA_GROUPS))
    return pe2, w1j, w2b


def _compress_kernel(x_ref, pe_ref, w1_ref, w2_ref, o_ref):
    xb = (x_ref[...] + pe_ref[...]).astype(BF16)
    hid = jax.nn.gelu(_dot(xb, w1_ref[...]))
    o_ref[...] = _dot(hid.astype(BF16), w2_ref[...])


def _compress(blocks, pe, w1, w2):
    r = blocks.shape[0]
    tb = min(256, r)
    width = CMP_BLOCK * LANES
    pe2, w1j, w2b = _compress_weights(pe, w1, w2)
    return pl.pallas_call(
        _compress_kernel,
        out_shape=jax.ShapeDtypeStruct((r, LANES), F32),
        grid=(pl.cdiv(r, tb),),
        in_specs=[pl.BlockSpec((tb, width), lambda i: (i, 0)), pl.BlockSpec((1, width), lambda i: (0, 0)),
                  pl.BlockSpec((width, LANES), lambda i: (0, 0)), pl.BlockSpec((LANES, LANES), lambda i: (0, 0))],
        out_specs=pl.BlockSpec((tb, LANES), lambda i: (i, 0)),
        compiler_params=_cp("parallel"),
    )(blocks, pe2.reshape(1, width), w1j.reshape(width, LANES), w2b)


def _compress_pool_kernel(xt_ref, pe_ref, w1_ref, w2_ref, o_ref, xs_scr, *, pb):
    per_page = PAGE // CMP_BLOCK
    for p in range(pb):
        xs_scr[p * PAGE:(p + 1) * PAGE, :] = xt_ref[p].T + pe_ref[...]
    acc = None
    for j in range(CMP_BLOCK):
        rows = xs_scr[pl.ds(j, per_page * pb, stride=CMP_BLOCK), :]
        t = _dot(rows.astype(BF16), w1_ref[j])
        acc = t if acc is None else acc + t
    o_ref[...] = _dot(jax.nn.gelu(acc).astype(BF16), w2_ref[...])


def _compress_pool(pool_t, pe, w1, w2):
    npool = pool_t.shape[0]
    per_page = PAGE // CMP_BLOCK
    pb = _largest_divisor(npool, (32, 16, 8, 4, 2))
    pe2, w1j, w2b = _compress_weights(pe, w1, w2)
    pe_tile = jnp.tile(pe2, (per_page, 1))
    return pl.pallas_call(
        functools.partial(_compress_pool_kernel, pb=pb),
        out_shape=jax.ShapeDtypeStruct((npool * per_page, LANES), F32),
        grid=(npool // pb,),
        in_specs=[pl.BlockSpec((pb, LANES, PAGE), lambda i: (i, 0, 0)), pl.BlockSpec((PAGE, LANES), lambda i: (0, 0)),
                  pl.BlockSpec((CMP_BLOCK, LANES, LANES), lambda i: (0, 0, 0)),
                  pl.BlockSpec((LANES, LANES), lambda i: (0, 0))],
        out_specs=pl.BlockSpec((per_page * pb, LANES), lambda i: (i, 0)),
        scratch_shapes=[pltpu.VMEM((pb * PAGE, LANES), F32)],
        compiler_params=_cp("parallel"),
    )(pool_t, pe_tile, w1j, w2b)


def _rope_rows_kernel(x_ref, cos_ref, sin_ref, o_ref):
    o_ref[...] = _rope128(x_ref[...], cos_ref[...], sin_ref[...])


def _rope_rows(x, cos, sin):
    r = x.shape[0]
    tb = min(512, r)
    spec = pl.BlockSpec((tb, LANES), lambda i: (i, 0))
    return pl.pallas_call(_rope_rows_kernel, out_shape=jax.ShapeDtypeStruct((r, LANES), F32), grid=(r // tb,),
                          in_specs=[spec] * 3, out_specs=spec, compiler_params=_cp("parallel"))(x, cos, sin)


def _stack_heads(qblk, heads, scale):
    rows = []
    for src_blk, src_half, dst_half in heads:
        x = qblk[:, src_blk * LANES:(src_blk + 1) * LANES] * scale
        if src_half != dst_half:
            x = pltpu.roll(x, 64, 1)
        lane = _iota(x.shape, 1)
        keep = (lane >= 64) if dst_half else (lane < 64)
        rows.append(jnp.where(keep, x, 0.0))
    return jnp.concatenate(rows, axis=0)


def _unstack_heads(o_full_list, heads_per_kb, tq, n_out_blocks):
    outs = [None] * n_out_blocks
    for o_full, heads in zip(o_full_list, heads_per_kb):
        for idx, (src_blk, src_half, dst_half) in enumerate(heads):
            piece = o_full[idx * tq:(idx + 1) * tq]
            if src_half != dst_half:
                piece = pltpu.roll(piece, 64, 1)
            lane = _iota(piece.shape, 1)
            keep = (lane >= 64) if src_half else (lane < 64)
            prev = outs[src_blk]
            outs[src_blk] = jnp.where(keep, piece, 0.0 if prev is None else prev)
    return outs


def _lane_repeat(x, width):
    if width == LANES:
        return x
    if width % LANES == 0:
        return pltpu.repeat(x, width // LANES, axis=1)
    return jnp.broadcast_to(x[:, 0:1], (x.shape[0], width))


def _stack_wide(qblk, n_heads, scale):
    head_of_lane = _iota(qblk.shape, 1) >> 6
    return jnp.concatenate([jnp.where(head_of_lane == h, qblk * scale, 0.0) for h in range(n_heads)], axis=0)


def _unstack_wide(o_full, n_heads, tq):
    head_of_lane = _iota((tq, o_full.shape[1]), 1) >> 6
    out = jnp.zeros((tq, o_full.shape[1]), F32)
    for h in range(n_heads):
        out = jnp.where(head_of_lane == h, o_full[h * tq:(h + 1) * tq], out)
    return out


NSA_HEAD_MAP = [[(h // 2, h % 2, h // (NSA_HEADS // NSA_GROUPS)) for h in range(NSA_HEADS)]]
FOX_HEAD_MAP = [[(kb, 0, 0), (kb, 1, 1)] for kb in range(FOX_HEADS // 2)]


def _cmp_sel_kernel(q_ref, kc_ref, vc_ref, cend_ref, o_ref, sel_ref, *, tq, q0, nc, ns, ns_pad):
    qi = pl.program_id(1)
    heads = NSA_HEAD_MAP[0]
    hpg = NSA_HEADS // NSA_GROUPS
    qbd = _stack_heads(q_ref[...], heads, NSA_DH ** -0.5).astype(BF16)
    s = _dot_nt(qbd, kc_ref[...].astype(BF16))
    rows = NSA_HEADS * tq
    qpos = q0 + qi * tq + (_iota((rows, 1), 0) & (tq - 1))
    mask = cend_ref[...] <= qpos
    s = jnp.where(mask, s, NEG)
    e = jnp.where(mask, jnp.exp(s - jnp.max(s, axis=1, keepdims=True)), 0.0)
    p = e / jnp.maximum(jnp.sum(e, axis=1, keepdims=True), 1e-30)
    o_full = _dot(p.astype(BF16), vc_ref[...].astype(BF16))
    outs = _unstack_heads([o_full], NSA_HEAD_MAP, tq, MIX // LANES)
    for c, blk in enumerate(outs):
        o_ref[:, c * LANES:(c + 1) * LANES] = blk
    half = nc // 2
    qp = q0 + qi * tq + _iota((tq, 1), 0)
    blk_i = _iota((tq, ns_pad), 1)
    blk_f = blk_i.astype(F32)
    cur = qp >> 6
    forced = (blk_i == 0) | (blk_i == cur) | (blk_i == cur - 1)
    valid = (blk_i * SEL_BLOCK <= qp) & (blk_i < ns)
    n_sel = min(N_SEL, ns)
    scores, chosen = [], []
    for g in range(NSA_GROUPS):
        pg = p[g * hpg * tq:(g * hpg + 1) * tq]
        for j in range(1, hpg):
            pg = pg + p[(g * hpg + j) * tq:(g * hpg + j + 1) * tq]
        imp = pg[:, :half] + pg[:, half:]
        if ns_pad > half:
            imp = jnp.concatenate([imp, jnp.zeros((tq, ns_pad - half), F32)], axis=1)
        score = jnp.where(valid, imp + FORCE * forced.astype(F32), NEG)
        scores.append(jnp.where(blk_i < ns, score, -jnp.inf))
        chosen.append(jnp.zeros((tq, ns_pad), F32))
    for _ in range(n_sel):
        for g in range(NSA_GROUPS):
            mx = jnp.max(scores[g], axis=1, keepdims=True)
            first = jnp.min(jnp.where(scores[g] == mx, blk_f, 1e9), axis=1, keepdims=True)
            hit = blk_f == first
            chosen[g] = jnp.where(hit, 1.0, chosen[g])
            scores[g] = jnp.where(hit, -jnp.inf, scores[g])
    for g in range(NSA_GROUPS):
        sel_ref[:, g * ns_pad:(g + 1) * ns_pad] = jnp.where(valid, chosen[g], 0.0)


def _cmp_sel(q3, kcmp, vcmp, cend, q0, ns, ns_pad):
    B, T, _ = q3.shape
    nc = kcmp.shape[1]
    tq = min(TQ_CMP, T)
    return pl.pallas_call(
        functools.partial(_cmp_sel_kernel, tq=tq, q0=q0, nc=nc, ns=ns, ns_pad=ns_pad),
        out_shape=[jax.ShapeDtypeStruct((B, T, MIX), F32), jax.ShapeDtypeStruct((B, T, NSA_GROUPS * ns_pad), F32)],
        grid=(B, T // tq),
        in_specs=[pl.BlockSpec((None, tq, MIX), lambda b, i: (b, i, 0)),
                  pl.BlockSpec((None, nc, LANES), lambda b, i: (b, 0, 0)),
                  pl.BlockSpec((None, nc, LANES), lambda b, i: (b, 0, 0)),
                  pl.BlockSpec((1, nc), lambda b, i: (0, 0))],
        out_specs=[pl.BlockSpec((None, tq, MIX), lambda b, i: (b, i, 0)),
                   pl.BlockSpec((None, tq, NSA_GROUPS * ns_pad), lambda b, i: (b, i, 0))],
        compiler_params=_cp("parallel", "arbitrary"),
    )(q3, kcmp, vcmp, cend)


def _flash_kernel(*refs, cfg):
    mode, layout, tq, tk = cfg['mode'], cfg['layout'], cfg['tq'], cfg['tk']
    q0, kbase, pps, has_tail, tail_pos0 = cfg['q0'], cfg['kbase'], cfg['pps'], cfg['has_tail'], cfg['tail_pos0']
    ns_pad, n_pref, qw, kv_t = cfg['ns_pad'], cfg['n_pref'], cfg['qw'], cfg['kv_t']
    head_map = NSA_HEAD_MAP if layout == 'nsa' else FOX_HEAD_MAP
    nkb = 1 if layout in ('nsa', 'fox_wide') else FOX_HEADS // 2
    n_heads_kb = FOX_HEADS if layout == 'fox_wide' else len(head_map[0])
    rows = n_heads_kb * tq
    qi_ref, kj_ref, fl_ref = refs[:3]
    pos = n_pref
    q_ref = refs[pos]; pos += 1
    npg = max(pps, 1)
    k_refs = refs[pos:pos + npg]; pos += npg
    v_refs = refs[pos:pos + npg]; pos += npg
    if has_tail:
        tk_ref, tv_ref = refs[pos:pos + 2]; pos += 2
    c_ref = ct_ref = sel_ref = None
    if cfg['has_caug'] or cfg['has_crow']:
        c_ref = refs[pos]; pos += 1
        if has_tail:
            ct_ref = refs[pos]; pos += 1
    if mode == 'sel':
        sel_ref = refs[pos]; pos += 1
    o_ref = refs[pos]; pos += 1
    qst_scr, m_scr, l_scr, acc_scr = refs[pos:pos + 4]

    s_id = pl.program_id(1)
    qi = qi_ref[s_id]
    kj = kj_ref[s_id]
    fl = fl_ref[s_id]
    scale = NSA_DH ** -0.5 * LOG2E

    @pl.when((fl & 1) != 0)
    def _init():
        qblk = q_ref[...]
        if layout == 'fox_wide':
            qst_scr[0] = _stack_wide(qblk, FOX_HEADS, scale).astype(BF16)
        elif layout == 'nsa':
            qbd = _stack_heads(qblk, head_map[0], scale).astype(BF16)
            if mode == 'sel':
                selblk = sel_ref[...]
                hpg = NSA_HEADS // NSA_GROUPS
                parts = []
                for g in range(NSA_GROUPS):
                    off = ((selblk[:, g * ns_pad:(g + 1) * ns_pad] - 1.0) * SEL_OFF).astype(BF16)
                    parts += [off] * hpg
                qbd = jnp.concatenate([qbd, jnp.concatenate(parts, axis=0)], axis=1)
            qst_scr[0] = qbd
        else:
            for kb in range(nkb):
                qbd = _stack_heads(qblk, head_map[kb], scale).astype(BF16)
                if cfg['has_caug']:
                    lane = _iota((rows, LANES), 1)
                    head = 2 * kb + (_iota((rows, LANES), 0) >= tq).astype(jnp.int32)
                    aug = jnp.where((lane >= 3 * head) & (lane < 3 * head + 3), -1.0, 0.0).astype(BF16)
                    qbd = jnp.concatenate([qbd, aug], axis=1)
                qst_scr[kb] = qbd
        m_scr[...] = jnp.full(m_scr.shape, NEG, F32)
        l_scr[...] = jnp.zeros(l_scr.shape, F32)
        acc_scr[...] = jnp.zeros(acc_scr.shape, F32)

    def update(k_tile, v_tile, kpos0, c_tile, width, masked):
        if masked:
            qpos = q0 + qi * tq + (_iota((rows, 1), 0) & (tq - 1))
            kpos = kpos0 + _iota((1, width), 1)
            mask = kpos <= qpos
            if mode == 'band':
                mask = mask & (kpos > qpos - WINDOW)
        for kb in range(nkb):
            if kv_t:
                kt = k_tile.astype(BF16)
                if mode == 'sel':
                    onehot = (_iota((ns_pad, width), 0) == (kpos0 >> 6) + (_iota((ns_pad, width), 1) >> 6))
                    kt = jnp.concatenate([kt, onehot.astype(BF16)], axis=0)
                sc = _dot(qst_scr[kb], kt)
                if c_tile is not None:
                    sc = sc - LOG2E * jnp.concatenate([jnp.broadcast_to(c_tile[h:h + 1, :], (tq, width))
                                                       for h in range(n_heads_kb)], axis=0)
            else:
                kt = k_tile[:, kb * LANES:(kb + 1) * LANES].astype(BF16)
                if mode == 'sel':
                    onehot = (_iota((width, ns_pad), 1) == (kpos0 >> 6) + (_iota((width, ns_pad), 0) >> 6))
                    kt = jnp.concatenate([kt, onehot.astype(BF16)], axis=1)
                if c_tile is not None:
                    kt = jnp.concatenate([kt, c_tile], axis=1)
                sc = _dot_nt(qst_scr[kb], kt)
            if masked:
                sc = jnp.where(mask, sc, NEG)
            m_old = m_scr[kb]
            m_new = jnp.maximum(m_old, jnp.max(sc, axis=1, keepdims=True))
            alpha = jnp.exp2(m_old - m_new)
            p = jnp.exp2(sc - _lane_repeat(m_new, width))
            l_scr[kb] = alpha * l_scr[kb] + jnp.sum(p, axis=1, keepdims=True)
            if kv_t:
                pv = _dot_nt(p.astype(BF16), v_tile.astype(BF16))
            else:
                pv = _dot(p.astype(BF16), v_tile[:, kb * LANES:(kb + 1) * LANES].astype(BF16))
            acc_scr[kb] = _lane_repeat(alpha, pv.shape[1]) * acc_scr[kb] + pv
            m_scr[kb] = m_new

    def main_update(masked):
        if pps > 1:
            k_tile = jnp.concatenate([r[...] for r in k_refs], axis=1)
            v_tile = jnp.concatenate([r[...] for r in v_refs], axis=1)
        else:
            k_tile, v_tile = k_refs[0][...], v_refs[0][...]
        update(k_tile, v_tile, kbase + kj * tk, c_ref[...] if c_ref is not None else None, tk, masked)

    is_tail = (fl & 4) != 0
    need_mask = (fl & 8) != 0
    if has_tail:
        @pl.when(is_tail)
        def _():
            update(tk_ref[...], tv_ref[...], tail_pos0, ct_ref[...] if ct_ref is not None else None, PAGE, True)

    @pl.when(jnp.logical_and(jnp.logical_not(is_tail), need_mask))
    def _():
        main_update(True)

    @pl.when(jnp.logical_and(jnp.logical_not(is_tail), jnp.logical_not(need_mask)))
    def _():
        main_update(False)

    @pl.when((fl & 2) != 0)
    def _fin():
        fulls = [acc_scr[kb] / _lane_repeat(jnp.maximum(l_scr[kb], 1e-30), acc_scr.shape[2]) for kb in range(nkb)]
        if layout == 'fox_wide':
            o_ref[...] = _unstack_wide(fulls[0], FOX_HEADS, tq)
        else:
            outs = _unstack_heads(fulls, head_map, tq, qw // LANES)
            for c, blk in enumerate(outs):
                o_ref[:, c * LANES:(c + 1) * LANES] = blk


def _schedule(nq, tq, tk, q0, kbase, n_ktiles, mode, has_tail):
    qi_l, kj_l, fl_l = [], [], []
    for qi in range(nq):
        q_lo, q_hi = q0 + qi * tq, q0 + (qi + 1) * tq - 1
        entries = []
        for kj in range(n_ktiles):
            k_lo, k_hi = kbase + kj * tk, kbase + (kj + 1) * tk - 1
            if k_lo > q_hi:
                continue
            if mode == 'band' and k_hi <= q_lo - WINDOW:
                continue
            all_visible = k_hi <= q_lo and (mode != 'band' or k_lo > q_hi - WINDOW)
            entries.append((kj, 0 if all_visible else 8))
        if has_tail:
            entries.append((entries[-1][0] if entries else 0, 4 | 8))
        for n, (kj, f) in enumerate(entries):
            f |= (1 if n == 0 else 0) | (2 if n == len(entries) - 1 else 0)
            qi_l.append(qi); kj_l.append(kj); fl_l.append(f)
    return (np.asarray(qi_l, np.int32), np.asarray(kj_l, np.int32), np.asarray(fl_l, np.int32))


def _flash(q3, k3, v3, *, mode, layout, tq, tk, q0, kbase=0, page_table=None, pps=0, tail=None, tail_pos0=0,
           caug=None, crow=None, crow_tail=None, sel=None, ns_pad=0):
    B, Tq, qw = q3.shape
    nq = Tq // tq
    paged = page_table is not None
    has_tail = tail is not None
    if paged:
        kw = k3.shape[1]
        npages = page_table.shape[1]
        tk = pps * PAGE
        n_ktiles = npages // pps
    else:
        kw = k3.shape[-1]
        n_ktiles = k3.shape[1] // tk
    qi_t, kj_t, fl_t = _schedule(nq, tq, tk, q0, kbase, n_ktiles, mode, has_tail)
    nsteps = len(qi_t)
    prefetch = [jnp.asarray(qi_t), jnp.asarray(kj_t), jnp.asarray(fl_t)]
    if paged:
        prefetch.append(page_table.reshape(-1).astype(jnp.int32))
    n_pref = len(prefetch)
    nkb = 1 if layout in ('nsa', 'fox_wide') else FOX_HEADS // 2
    n_heads_kb = FOX_HEADS if layout == 'fox_wide' else (NSA_HEADS if layout == 'nsa' else 2)
    rows = n_heads_kb * tq
    cq = kw if layout == 'fox_wide' else LANES
    if mode == 'sel':
        cq += ns_pad
    if caug is not None:
        cq += LANES
    cv = kw if layout == 'fox_wide' else LANES

    in_specs = [pl.BlockSpec((None, tq, qw), lambda b, s, qi, kj, fl, *_: (b, qi[s], 0))]
    args = [q3]
    if paged:
        def page_map(i):
            return lambda b, s, qi, kj, fl, pt: (pt[b * npages + kj[s] * pps + i], 0, 0)
        for arr in (k3, v3):
            for i in range(pps):
                in_specs.append(pl.BlockSpec((None, kw, PAGE), page_map(i)))
                args.append(arr)
    else:
        for arr in (k3, v3):
            in_specs.append(pl.BlockSpec((None, tk, kw), lambda b, s, qi, kj, fl, *_: (b, kj[s], 0)))
            args.append(arr)
    if has_tail:
        for arr in tail:
            in_specs.append(pl.BlockSpec((None, kw, PAGE), lambda b, s, qi, kj, fl, *_: (b, 0, 0)))
            args.append(arr)
    if caug is not None:
        in_specs.append(pl.BlockSpec((None, tk, LANES), lambda b, s, qi, kj, fl, *_: (b, kj[s], 0)))
        args.append(caug)
    if crow is not None:
        in_specs.append(pl.BlockSpec((None, 8, tk), lambda b, s, qi, kj, fl, *_: (b, 0, kj[s])))
        args.append(crow)
        if has_tail:
            in_specs.append(pl.BlockSpec((None, 8, PAGE), lambda b, s, qi, kj, fl, *_: (b, 0, 0)))
            args.append(crow_tail)
    if mode == 'sel':
        in_specs.append(pl.BlockSpec((None, tq, NSA_GROUPS * ns_pad), lambda b, s, qi, kj, fl, *_: (b, qi[s], 0)))
        args.append(sel)
    cfg = dict(mode=mode, layout=layout, tq=tq, tk=tk, q0=q0, kbase=kbase, pps=pps if paged else 0,
               has_tail=has_tail, tail_pos0=tail_pos0, ns_pad=ns_pad, n_pref=n_pref, qw=qw, kv_t=paged,
               has_caug=caug is not None, has_crow=crow is not None)
    return pl.pallas_call(
        functools.partial(_flash_kernel, cfg=cfg),
        out_shape=jax.ShapeDtypeStruct((B, Tq, qw), F32),
        grid_spec=pltpu.PrefetchScalarGridSpec(
            num_scalar_prefetch=n_pref,
            grid=(B, nsteps),
            in_specs=in_specs,
            out_specs=pl.BlockSpec((None, tq, qw), lambda b, s, qi, kj, fl, *_: (b, qi[s], 0)),
            scratch_shapes=[pltpu.VMEM((nkb, rows, cq), BF16), pltpu.VMEM((nkb, rows, LANES), F32),
                            pltpu.VMEM((nkb, rows, LANES), F32), pltpu.VMEM((nkb, rows, cv), F32)]),
        compiler_params=_cp("parallel", "arbitrary"),
    )(*prefetch, *args)


def _nsa_combine_kernel(oc_ref, os_ref, ow_ref, sm_ref, e_ref, o_ref):
    gate = _sigmoid(sm_ref[...])
    acc = None
    for j, br in enumerate((oc_ref, os_ref, ow_ref)):
        gexp = _dot_r01(gate, e_ref[j])
        acc = gexp * br[...] if acc is None else acc + gexp * br[...]
    o_ref[...] = acc


def _nsa_combine(o_cmp, o_slc, o_win, small):
    n = o_cmp.shape[0]
    tm = min(512, n)
    src = np.zeros((3, LANES, MIX), np.float32)
    for h in range(NSA_HEADS):
        for j in range(3):
            src[j, 2 * M_HEADS + 3 * h + j, h * NSA_DH:(h + 1) * NSA_DH] = 1.0
    spec = pl.BlockSpec((tm, MIX), lambda i: (i, 0))
    return pl.pallas_call(
        _nsa_combine_kernel, out_shape=jax.ShapeDtypeStruct((n, MIX), F32), grid=(n // tm,),
        in_specs=[spec, spec, spec, pl.BlockSpec((tm, LANES), lambda i: (i, 0)),
                  pl.BlockSpec((3, LANES, MIX), lambda i: (0, 0, 0))],
        out_specs=spec, compiler_params=_cp("parallel"))(o_cmp, o_slc, o_win, small, jnp.asarray(src, BF16))


def _fox_cumsum_kernel(sm_ref, bias_ref, place_ref, lf_ref, caug_ref, carry, *, tt):
    @pl.when(pl.program_id(1) == 0)
    def _():
        carry[...] = jnp.zeros(carry.shape, F32)

    lf = _log_sigmoid(sm_ref[...] + bias_ref[...])
    lf_ref[...] = lf
    lower = (_iota((tt, tt), 1) <= _iota((tt, tt), 0)).astype(F32)
    c = _dot_l01(lower, lf) + carry[0:1, :]
    carry[...] = jnp.broadcast_to(c[tt - 1:tt, :], carry.shape)
    c = c * LOG2E
    c_hi = c.astype(BF16)
    r1 = c - c_hi.astype(F32)
    c_mid = r1.astype(BF16)
    c_lo = (r1 - c_mid.astype(F32)).astype(BF16)
    caug = _dot(c_hi, place_ref[0]) + _dot(c_mid, place_ref[1]) + _dot(c_lo, place_ref[2])
    caug_ref[...] = caug.astype(BF16)


def _fox_cumsum(small, bias_row, B, T):
    tt = min(256, T)
    row = lambda b, i: (b * (T // tt) + i, 0)
    place = np.zeros((3, LANES, LANES), np.float32)
    for h in range(FOX_HEADS):
        for j in range(3):
            place[j, h, 3 * h + j] = 1.0
    return pl.pallas_call(
        functools.partial(_fox_cumsum_kernel, tt=tt),
        out_shape=[jax.ShapeDtypeStruct((B * T, LANES), F32), jax.ShapeDtypeStruct((B * T, LANES), BF16)],
        grid=(B, T // tt),
        in_specs=[pl.BlockSpec((tt, LANES), row), pl.BlockSpec((1, LANES), lambda b, i: (0, 0)),
                  pl.BlockSpec((3, LANES, LANES), lambda b, i: (0, 0, 0))],
        out_specs=[pl.BlockSpec((tt, LANES), row), pl.BlockSpec((tt, LANES), row)],
        scratch_shapes=[pltpu.VMEM((8, LANES), F32)],
        compiler_params=_cp("parallel", "arbitrary"),
    )(small, bias_row, jnp.asarray(place, BF16))


def _fox_cumsum_paged_kernel(pt_ref, *refs, pps, n_main):
    lf_refs = refs[:pps]
    sm_ref, bias_ref, c_ref, lfn_ref, carry = refs[pps:]
    j = pl.program_id(1)

    @pl.when(j == 0)
    def _():
        carry[...] = jnp.zeros(carry.shape, F32)

    upper = (_iota((PAGE, PAGE), 0) <= _iota((PAGE, PAGE), 1)).astype(F32)

    @pl.when(j < n_main)
    def _():
        run = carry[:, 0:1]
        for i in range(pps):
            c = _dot_r01(lf_refs[i][...], upper) + run
            c_ref[:, i * PAGE:(i + 1) * PAGE] = c
            run = c[:, PAGE - 1:PAGE]
        carry[...] = jnp.broadcast_to(run, carry.shape)

    @pl.when(j == n_main)
    def _():
        lf = _log_sigmoid(sm_ref[...] + bias_ref[...])
        lfn_ref[...] = lf
        eye8 = (_iota((8, LANES), 0) == _iota((8, LANES), 1)).astype(F32)
        c_new = _dot_r01(_dot_nt_l01(eye8, lf), upper) + carry[:, 0:1]
        for i in range(pps):
            c_ref[:, i * PAGE:(i + 1) * PAGE] = c_new


def _fox_cumsum_paged(lf_pool_t, page_table, small_pad, bias_row, pps):
    B, npages = page_table.shape
    n_main = npages // pps

    def page_map(i):
        return lambda b, j, pt: (pt[b * npages + jnp.minimum(j, n_main - 1) * pps + i], 0, 0)

    return pl.pallas_call(
        functools.partial(_fox_cumsum_paged_kernel, pps=pps, n_main=n_main),
        out_shape=[jax.ShapeDtypeStruct((B, 8, (n_main + 1) * pps * PAGE), F32),
                   jax.ShapeDtypeStruct((B, PAGE, LANES), F32)],
        grid_spec=pltpu.PrefetchScalarGridSpec(
            num_scalar_prefetch=1, grid=(B, n_main + 1),
            in_specs=[pl.BlockSpec((None, 8, PAGE), page_map(i)) for i in range(pps)]
            + [pl.BlockSpec((None, PAGE, LANES), lambda b, j, pt: (b, 0, 0)),
               pl.BlockSpec((1, LANES), lambda b, j, pt: (0, 0))],
            out_specs=[pl.BlockSpec((None, 8, pps * PAGE), lambda b, j, pt: (b, 0, j)),
                       pl.BlockSpec((None, PAGE, LANES), lambda b, j, pt: (b, 0, 0))],
            scratch_shapes=[pltpu.VMEM((8, LANES), F32)]),
        compiler_params=_cp("parallel", "arbitrary"),
    )(page_table.reshape(-1).astype(jnp.int32), *([lf_pool_t] * pps), small_pad, bias_row)


def _rwkv_prep_kernel(rw_ref, prev_ref, sh_ref, mu_ref, w0_ref, w2_ref, a0_ref, a2_ref, g2_ref, kk_ref, ka_ref,
                      bd_ref, r_o, lw_o, k_o, v_o, kk_o, a_o, g_o, *, tt):
    i = pl.program_id(1)
    rw = rw_ref[...]
    first_prev = jnp.where(i == 0, sh_ref[...], prev_ref[7:8, :])
    prev = jnp.where(_iota(rw.shape, 0) == 0, first_prev, pltpu.roll(rw, 1, 0))
    m = rw + mu_ref[...] * (prev - rw)
    r, kr, vr = m[:, 0:MIX], m[:, MIX:2 * MIX], m[:, 2 * MIX:3 * MIX]
    lora = m[:, 3 * MIX:3 * MIX + LANES]
    gl = m[:, 3 * MIX + LANES:3 * MIX + 2 * LANES]
    w = -_softplus(-(w0_ref[...] + _dot(jnp.tanh(lora).astype(BF16), w2_ref[...]))) - 0.5
    a = _sigmoid(a0_ref[...] + _dot(lora.astype(BF16), a2_ref[...]))
    kk = kr * kk_ref[...]
    ss = _dot_r01(kk * kk, bd_ref[...])
    r_o[...] = r
    lw_o[...] = -jnp.exp(w)
    k_o[...] = kr * (1.0 + (a - 1.0) * ka_ref[...])
    v_o[...] = vr
    kk_o[...] = kk / jnp.maximum(jnp.sqrt(ss), 1e-12)
    a_o[...] = a
    g_o[...] = _dot(_sigmoid(gl).astype(BF16), g2_ref[...])


def _rwkv_prep(rw3, shift0, P):
    B, T, wdt = rw3.shape
    tt = min(256, T)
    nblk8 = tt // 8
    row3 = lambda b, i: (b, i, 0)
    vec = lambda n: pl.BlockSpec((1, n), lambda b, i: (0, 0))
    outs = pl.pallas_call(
        functools.partial(_rwkv_prep_kernel, tt=tt),
        out_shape=[jax.ShapeDtypeStruct((B, T, MIX), F32)] * 7,
        grid=(B, T // tt),
        in_specs=[pl.BlockSpec((None, tt, wdt), row3),
                  pl.BlockSpec((None, 8, wdt), lambda b, i: (b, jnp.maximum(i * nblk8 - 1, 0), 0)),
                  pl.BlockSpec((None, 1, wdt), lambda b, i: (b, 0, 0)),
                  vec(wdt), vec(MIX), pl.BlockSpec((LANES, MIX), lambda b, i: (0, 0)),
                  vec(MIX), pl.BlockSpec((LANES, MIX), lambda b, i: (0, 0)),
                  pl.BlockSpec((LANES, MIX), lambda b, i: (0, 0)), vec(MIX), vec(MIX),
                  pl.BlockSpec((MIX, MIX), lambda b, i: (0, 0))],
        out_specs=[pl.BlockSpec((None, tt, MIX), row3)] * 7,
        compiler_params=_cp("parallel", "arbitrary"),
    )(rw3, rw3, shift0[:, None, :], P['mu'], P['w0'], P['w2'], P['a0'], P['a2'], P['g2'], P['k_k'], P['k_a'], P['bd64'])
    return outs


def _rwkv_scan_kernel(r_ref, lw_ref, k_ref, v_ref, kk_ref, a_ref, g_ref, s0_ref, rk_ref, lng_ref, lnb_ref,
                      y_ref, s1_ref, s_scr, *, L):
    ci = pl.program_id(1)

    @pl.when(ci == 0)
    def _():
        s_scr[...] = s0_ref[...]

    r_i = _iota((L, L), 0)
    c_i = _iota((L, L), 1)
    lower = (c_i <= r_i).astype(F32)
    strict = c_i < r_i
    incl = c_i <= r_i
    eye_l = (c_i == r_i).astype(F32)
    lane = _iota((L, LANES), 1)
    half_masks = (lane < 64, lane >= 64)
    rr = _iota((LANES, LANES), 0)
    cc = _iota((LANES, LANES), 1)
    bdmask = (rr >= 64) == (cc >= 64)
    eye128 = rr == cc
    bd_f = bdmask.astype(F32)
    pairs = range(RWKV_HEADS // 2)
    sls = [slice(pr * LANES, (pr + 1) * LANES) for pr in pairs]
    P = []
    for sl in sls:
        lw = lw_ref[:, sl]
        cl = _dot_l01(lower, lw)
        w_in = jnp.exp(cl)
        w_out = jnp.exp(-cl)
        kk = kk_ref[:, sl]
        P.append(dict(w_in=w_in, kap=kk * jnp.exp(cl - lw), bet=kk * a_ref[:, sl] * w_out,
                      kt=k_ref[:, sl] * w_out, rt=r_ref[:, sl] * w_in, v=v_ref[:, sl]))
    for pr, d in zip(pairs, P):
        d['sbd'] = s_scr[pr]
        d['rhs'] = _bdot(d['kap'], d['sbd'])
        d['ys'] = _bdot(d['rt'], d['sbd'])
    H = []
    for d in P:
        for hm in half_masks:
            kap_h = jnp.where(hm, d['kap'], 0.0)
            rt_h = jnp.where(hm, d['rt'], 0.0)
            amat = jnp.where(strict, _bdot_nt(kap_h, d['bet']), 0.0)
            H.append(dict(d=d, hm=hm, pw=amat, tinv=eye_l - amat,
                          cmat=jnp.where(strict, _bdot_nt(kap_h, d['kt']), 0.0),
                          rb=jnp.where(incl, _bdot_nt(rt_h, d['bet']), 0.0),
                          rk=jnp.where(incl, _bdot_nt(rt_h, d['kt']), 0.0)))
    def same_block(shift):
        return (r_i >> shift) == (c_i >> shift)

    for h in H:
        h['amat'] = h['pw']
        h['pw'] = jnp.where(same_block(3), h['amat'], 0.0)
        h['tinv'] = eye_l - h['pw']
    for _ in range(2):
        for h in H:
            h['pw'] = _dot3(h['pw'], h['pw'])
        for h in H:
            h['tinv'] = h['tinv'] + _dot3(h['tinv'], h['pw'])
    shift = 3
    while (1 << shift) < L:
        off_diag = jnp.logical_and(same_block(shift + 1), jnp.logical_not(same_block(shift)))
        for h in H:
            h['x'] = _dot3(jnp.where(off_diag, h['amat'], 0.0), h['tinv'])
        for h in H:
            h['tinv'] = h['tinv'] - _dot3(h['tinv'], h['x'])
        shift += 1
    for h in H:
        d, hm = h['d'], h['hm']
        h['v_h'] = jnp.where(hm, d['v'], 0.0)
        h['rhs_h'] = jnp.where(hm, d['rhs'], 0.0) + _bdot(h['cmat'], h['v_h'])
    for h in H:
        h['u_h'] = -_dot3(h['tinv'], h['rhs_h'])
    for h in H:
        h['y_h'] = _bdot(h['rb'], h['u_h']) + _bdot(h['rk'], h['v_h'])
    for pr, (d, sl) in enumerate(zip(P, sls)):
        h0, h1 = H[2 * pr], H[2 * pr + 1]
        u = h0['u_h'] + h1['u_h']
        y = d['ys'] + h0['y_h'] + h1['y_h']
        upd = jnp.where(bdmask, _bdot_tn(d['bet'], u) + _bdot_tn(d['kt'], d['v']), 0.0)
        w_end = jnp.broadcast_to(d['w_in'][L - 1:L, :], (LANES, LANES))
        w_end_col = jnp.sum(jnp.where(eye128, w_end, 0.0), axis=1, keepdims=True)
        s_scr[pr] = (d['sbd'] + upd) * w_end_col
        d['y'] = y
    for d, sl in zip(P, sls):
        y = d['y']
        mean = _dot_r01(y, bd_f) * (1.0 / RWKV_N)
        yc = y - mean
        var = _dot_r01(yc * yc, bd_f) * (1.0 / RWKV_N)
        yn = yc * lax.rsqrt(var + 64e-5) * lng_ref[:, sl] + lnb_ref[:, sl]
        bonus = _dot_r01(r_ref[:, sl] * k_ref[:, sl] * rk_ref[:, sl], bd_f)
        y_ref[:, sl] = (yn + bonus * d['v']) * g_ref[:, sl]

    @pl.when(ci == pl.num_programs(1) - 1)
    def _():
        s1_ref[...] = s_scr[...]


def _rwkv_scan(r, lw, k, v, kk, a, g, s0, P, B, T):
    L = RWKV_CHUNK
    tp = -(-T // L) * L
    if tp != T:
        pad = lambda t: jnp.pad(t, ((0, 0), (0, tp - T), (0, 0)))
        r, lw, k, v, kk, a, g = (pad(t) for t in (r, lw, k, v, kk, a, g))
    nch = tp // L
    st = jnp.swapaxes(s0.astype(F32), -1, -2).reshape(B, RWKV_HEADS // 2, 2, RWKV_N, RWKV_N)
    z = jnp.zeros_like(st[:, :, 0])
    sbd0 = jnp.concatenate([jnp.concatenate([st[:, :, 0], z], -1), jnp.concatenate([z, st[:, :, 1]], -1)], -2)
    row3 = lambda b, c: (b, c, 0)
    st_spec = pl.BlockSpec((None, RWKV_HEADS // 2, LANES, LANES), lambda b, c: (b, 0, 0, 0))
    vec = pl.BlockSpec((1, MIX), lambda b, c: (0, 0))
    y, sbd1 = pl.pallas_call(
        functools.partial(_rwkv_scan_kernel, L=L),
        out_shape=[jax.ShapeDtypeStruct((B, tp, MIX), F32),
                   jax.ShapeDtypeStruct((B, RWKV_HEADS // 2, LANES, LANES), F32)],
        grid=(B, nch),
        in_specs=[pl.BlockSpec((None, L, MIX), row3)] * 7 + [st_spec, vec, vec, vec],
        out_specs=[pl.BlockSpec((None, L, MIX), row3), st_spec],
        scratch_shapes=[pltpu.VMEM((RWKV_HEADS // 2, LANES, LANES), F32)],
        compiler_params=_cp("parallel", "arbitrary"),
    )(r, lw, k, v, kk, a, g, sbd0, P['r_k'], P['ln_g'], P['ln_b'])
    s_e = sbd1[:, :, :RWKV_N, :RWKV_N]
    s_o = sbd1[:, :, RWKV_N:, RWKV_N:]
    s1 = jnp.swapaxes(jnp.stack([s_e, s_o], 2).reshape(B, RWKV_HEADS, RWKV_N, RWKV_N), -1, -2)
    return y[:, :T], s1


def _mem_kernel(x_ref, km_ref, vm_ref, wq_ref, wo_ref, g_ref, b_ref, o_ref):
    x = x_ref[...]
    q = _dot(x.astype(BF16), wq_ref[...])
    heads = []
    for h in range(MEM_HEADS):
        sl = slice(h * MEM_DH, (h + 1) * MEM_DH)
        s = _dot_nt((q[:, sl] * (MEM_DH ** -0.5)).astype(BF16), km_ref[:, sl].astype(BF16))
        e = jnp.exp(s - jnp.max(s, axis=1, keepdims=True))
        p = e / jnp.sum(e, axis=1, keepdims=True)
        heads.append(_dot(p.astype(BF16), vm_ref[:, sl].astype(BF16)))
    o = jnp.concatenate(heads, axis=1)
    y = _dot(o.astype(BF16), wo_ref[...])
    o_ref[...] = _layer_norm_rows(ALPHA * x + y, g_ref[...], b_ref[...])


def _mem_layer(x3, km, vm, wq, wo, g, b):
    B, T, d = x3.shape
    tq = min(512, T)
    mt = km.shape[1]
    full = lambda shape: pl.BlockSpec(shape, lambda bb, i: (0,) * len(shape))
    return pl.pallas_call(
        _mem_kernel, out_shape=jax.ShapeDtypeStruct((B, T, d), F32), grid=(B, T // tq),
        in_specs=[pl.BlockSpec((None, tq, d), lambda bb, i: (bb, i, 0)),
                  pl.BlockSpec((None, mt, d), lambda bb, i: (bb, 0, 0)),
                  pl.BlockSpec((None, mt, d), lambda bb, i: (bb, 0, 0)),
                  full((d, d)), full((d, d)), full((1, d)), full((1, d))],
        out_specs=pl.BlockSpec((None, tq, d), lambda bb, i: (bb, i, 0)),
        compiler_params=_cp("parallel", "arbitrary"),
    )(x3, km, vm, wq, wo, g.reshape(1, d), b.reshape(1, d))


def _moe_route_kernel(x_ref, wr_ref, br_ref, gw_ref, cnt_ref):
    x = x_ref[...]
    logit = _dot(x, wr_ref[...], HI) + br_ref[...]
    lane_f = _iota(logit.shape, 1).astype(F32)
    is_g = lane_f < N_GROUPS
    gl = jnp.where(is_g, logit, -jnp.inf)
    gmax = jnp.max(gl, axis=1, keepdims=True)
    grp = jnp.min(jnp.where(gl == gmax, lane_f, 1e9), axis=1, keepdims=True)
    g_w = 1.0 / jnp.sum(jnp.where(is_g, jnp.exp(gl - gmax), 0.0), axis=1, keepdims=True)
    lo = N_GROUPS + EXP_PER_GROUP * grp
    in_grp = (lane_f >= lo) & (lane_f < lo + EXP_PER_GROUP)
    el = jnp.where(in_grp, logit, -jnp.inf)
    emax = jnp.max(el, axis=1, keepdims=True)
    ee = jnp.where(in_grp, jnp.exp(el - emax), 0.0)
    p = jnp.where(in_grp, ee / jnp.sum(ee, axis=1, keepdims=True), -1.0)
    p1 = jnp.max(p, axis=1, keepdims=True)
    i1 = jnp.min(jnp.where(p == p1, lane_f, 1e9), axis=1, keepdims=True)
    pr = jnp.where(lane_f == i1, -1.0, p)
    p2 = jnp.max(pr, axis=1, keepdims=True)
    i2 = jnp.min(jnp.where(pr == p2, lane_f, 1e9), axis=1, keepdims=True)
    den = p1 + p2
    member = jnp.where(lane_f == grp, 1.0, 0.0)
    gw_ref[...] = (member + jnp.where(lane_f == i1, g_w * p1 / den, 0.0)
                   + jnp.where(lane_f == i2, g_w * p2 / den, 0.0))
    cnt_ref[...] = jnp.broadcast_to(jnp.sum(member, axis=0, keepdims=True), cnt_ref.shape)


def _moe_group_kernel(cnt_ref, x_ref, gw_ref, w1_ref, w3_ref, w2_ref, g_ref, b_ref, o_ref,
                      xb_scr, gt_scr, rk_scr, xs_scr, gws_scr, ys_scr, acc_scr, *, tm, ch):
    i = pl.program_id(0)
    e = pl.program_id(1) * MOE_EXPERTS_PER_STEP
    grp = e // EXP_PER_GROUP
    cnt = cnt_ref[i * N_GROUPS + grp]
    slots = [slice(c * ch, (c + 1) * ch) for c in range(tm // ch)]

    @pl.when(e == 0)
    def _():
        xb_scr[...] = x_ref[...].astype(BF16)
        acc_scr[...] = jnp.zeros(acc_scr.shape, F32)
        member = gw_ref[...].T[0:8, :]
        gt_scr[...] = member
        before = (_iota((tm, tm), 0) < _iota((tm, tm), 1)).astype(BF16)
        rk_scr[...] = _dot(member.astype(BF16), before)

    def one_hot(c):
        member = gt_scr[pl.ds(grp, 1), :]
        rank = rk_scr[pl.ds(grp, 1), :]
        want = (c * ch + _iota((ch, 1), 0)).astype(F32)
        return jnp.where((member > 0.5) & (rank == want), 1.0, 0.0).astype(BF16)

    @pl.when(e % EXP_PER_GROUP == 0)
    def _():
        for c, rs in enumerate(slots):
            @pl.when(c * ch < cnt)
            def _():
                p = one_hot(c)
                xs_scr[rs, :] = _dot(p, xb_scr[...]).astype(BF16)
                gws_scr[rs, :] = _dot_l01(p, gw_ref[...])
                ys_scr[rs, :] = jnp.zeros((ch, ys_scr.shape[1]), F32)

    for c, rs in enumerate(slots):
        @pl.when(c * ch < cnt)
        def _():
            xs = xs_scr[rs, :]
            gws = gws_scr[rs, :]
            y = None
            for j in range(MOE_EXPERTS_PER_STEP):
                h1 = _dot(xs, w1_ref[j])
                h3 = _dot(xs, w3_ref[j])
                hid = h1 * _sigmoid(h1) * h3
                gcol = jnp.sum(jnp.where(_iota(gws.shape, 1) == e + j + N_GROUPS, gws, 0.0), axis=1, keepdims=True)
                yj = gcol * _dot(hid.astype(BF16), w2_ref[j])
                y = yj if y is None else y + yj
            ys_scr[rs, :] += y

    @pl.when(e % EXP_PER_GROUP == EXP_PER_GROUP - MOE_EXPERTS_PER_STEP)
    def _():
        for c, rs in enumerate(slots):
            @pl.when(c * ch < cnt)
            def _():
                p = one_hot(c)
                hi, lo = _split_bf16(ys_scr[rs, :])
                acc_scr[...] += _dot_tn(p, hi) + _dot_tn(p, lo)

    @pl.when(e == N_EXPERTS - MOE_EXPERTS_PER_STEP)
    def _():
        o_ref[...] = _layer_norm_rows(ALPHA * x_ref[...] + acc_scr[...], g_ref[...], b_ref[...])


def _moe_layer_grouped(x, M, g, b):
    n, d = x.shape
    tm = min(1024, n)
    ch = min(256, tm)
    nt = n // tm
    gw, cnt = pl.pallas_call(
        _moe_route_kernel,
        out_shape=[jax.ShapeDtypeStruct((n, LANES), F32), jax.ShapeDtypeStruct((nt, 8, LANES), F32)],
        grid=(nt,),
        in_specs=[pl.BlockSpec((tm, d), lambda i: (i, 0)), pl.BlockSpec((d, LANES), lambda i: (0, 0)),
                  pl.BlockSpec((1, LANES), lambda i: (0, 0))],
        out_specs=[pl.BlockSpec((tm, LANES), lambda i: (i, 0)), pl.BlockSpec((None, 8, LANES), lambda i: (i, 0, 0))],
        compiler_params=_cp("parallel"),
    )(x, M['wr'], M['br'])
    counts = cnt[:, 0, :N_GROUPS].astype(jnp.int32).reshape(-1)
    full2 = lambda shape: pl.BlockSpec(shape, lambda i, e, c: (0, 0))
    return pl.pallas_call(
        functools.partial(_moe_group_kernel, tm=tm, ch=ch),
        out_shape=jax.ShapeDtypeStruct((n, d), F32),
        grid_spec=pltpu.PrefetchScalarGridSpec(
            num_scalar_prefetch=1, grid=(nt, N_EXPERTS // MOE_EXPERTS_PER_STEP),
            in_specs=[pl.BlockSpec((tm, d), lambda i, e, c: (i, 0)), pl.BlockSpec((tm, LANES), lambda i, e, c: (i, 0)),
                      pl.BlockSpec((MOE_EXPERTS_PER_STEP, d, D_EXPERT), lambda i, e, c: (e, 0, 0)),
                      pl.BlockSpec((MOE_EXPERTS_PER_STEP, d, D_EXPERT), lambda i, e, c: (e, 0, 0)),
                      pl.BlockSpec((MOE_EXPERTS_PER_STEP, D_EXPERT, d), lambda i, e, c: (e, 0, 0)),
                      full2((1, d)), full2((1, d))],
            out_specs=pl.BlockSpec((tm, d), lambda i, e, c: (i, 0)),
            scratch_shapes=[pltpu.VMEM((tm, d), BF16), pltpu.VMEM((8, tm), F32), pltpu.VMEM((8, tm), F32),
                            pltpu.VMEM((tm, d), BF16), pltpu.VMEM((tm, LANES), F32), pltpu.VMEM((tm, d), F32),
                            pltpu.VMEM((tm, d), F32)]),
        compiler_params=_cp("parallel", "arbitrary"),
    )(counts, x, gw, M['w1'], M['w3'], M['w2'], g.reshape(1, d), b.reshape(1, d))


EVEN_SEGS = ([(MIX, False)] * 4 + [(MIX, True), (LANES, False), (LANES, False), (LANES, True), (LANES, False),
                                   (LANES, True), (LANES, False), (LANES, False)])
ODD_SEGS = [(MIX, False)] * 3 + [(RWKV_SHIFT, False), (LANES, False)]
RW_PERM = np.concatenate([np.arange(0, 512), np.arange(576, 1088), np.arange(1088, 1600), np.arange(512, 576),
                          np.arange(1600, 1664), np.arange(1664, 1792)])
RW_INV = np.argsort(RW_PERM)


def _prep_even(w_in, w_out, b_i, b_f):
    zeros = jnp.zeros((D_MODEL, LANES - 2 * M_HEADS - 3 * NSA_HEADS), F32)
    w = jnp.concatenate([w_in[:, 0:1536], w_in[:, 1544:2056], w_in[:, 2056:3336], w_in[:, 1536:1544],
                         w_in[:, 3336:3360], zeros], axis=1).astype(BF16)
    bias = jnp.concatenate([b_i, b_f, jnp.zeros((LANES - 2 * M_HEADS,), F32)]).reshape(1, LANES)
    return dict(w_in=w, w_out_a=w_out[:MIX].astype(BF16), w_out_b=w_out[MIX:].astype(BF16), gate_bias=bias)


def _prep_odd(w_in, w_out, fox_b_f, mu, w0, w2, a0, a2, g2, k_k, k_a, r_k, ln_g, ln_b):
    rw_cols = w_in[:, 1544:3336][:, RW_PERM]
    zeros = jnp.zeros((D_MODEL, LANES - FOX_HEADS), F32)
    w = jnp.concatenate([w_in[:, 0:1536], rw_cols, w_in[:, 1536:1544], zeros], axis=1).astype(BF16)
    bias = jnp.concatenate([fox_b_f, jnp.zeros((LANES - FOX_HEADS,), F32)]).reshape(1, LANES)
    z64 = jnp.zeros((64, MIX), F32)
    hd = np.arange(MIX) // RWKV_N
    bd64 = jnp.asarray((hd[:, None] == hd[None, :]).astype(np.float32), BF16)
    P = dict(mu=mu[RW_PERM].reshape(1, -1), w0=w0.reshape(1, -1), a0=a0.reshape(1, -1),
             w2=jnp.concatenate([w2, z64], 0).astype(BF16), a2=jnp.concatenate([z64, a2], 0).astype(BF16),
             g2=g2.astype(BF16), k_k=k_k.reshape(1, -1), k_a=k_a.reshape(1, -1), r_k=r_k.reshape(1, -1),
             ln_g=ln_g.reshape(1, -1), ln_b=ln_b.reshape(1, -1), bd64=bd64)
    return dict(w_in=w, w_out_a=w_out[:MIX].astype(BF16), w_out_b=w_out[MIX:].astype(BF16), fox_bias=bias, P=P)


def _prep_moe(w_group, b_group, w_expert, b_expert, w1, w3, w2):
    padw = LANES - N_GROUPS - N_EXPERTS
    wr = jnp.concatenate([w_group, w_expert, jnp.zeros((D_MODEL, padw), F32)], axis=1)
    br = jnp.concatenate([b_group, b_expert, jnp.zeros((padw,), F32)]).reshape(1, LANES)
    return dict(wr=wr, br=br, w1=w1.astype(BF16), w3=w3.astype(BF16), w2=w2.astype(BF16))


def _even_odd_blocks(kcmp):
    return jnp.concatenate([kcmp[:, 0::2], kcmp[:, 1::2]], axis=1)


def _pages_t(cache):
    pool, page, h, dh = cache.shape
    return jnp.transpose(cache, (0, 2, 3, 1)).reshape(pool, h * dh, page)


def _tail_t(rows3):
    B, T, _ = rows3.shape
    return jnp.swapaxes(jnp.pad(rows3, ((0, 0), (0, PAGE - T), (0, 0))), 1, 2)


def _even_mixer(x3, q0, W, nsa_w, norm_g, c0, n0, m0, past, ln_g, ln_b):
    B, T, D = x3.shape
    N = B * T
    x2 = x3.reshape(N, D)
    pos = q0 + jnp.arange(T)
    cos, sin = _rope_tables(pos)
    rope = (jnp.tile(cos, (B, 1)), jnp.tile(sin, (B, 1)))
    sh = (B, T, NSA_GROUPS, NSA_DH)
    if past is None and T % 256 == 0:
        outs, outs_t = _proj_split(x2, W['w_in'], EVEN_SEGS, rope, t_segs=(5, 6, 7, 8), bt=(B, T))
        new_rows = [jnp.transpose(t.reshape(B, NSA_GROUPS, NSA_DH, T), (0, 3, 1, 2)) for t in outs_t]
    else:
        outs = _proj_split(x2, W['w_in'], EVEN_SEGS, rope)
        new_rows = [t.reshape(sh) for t in outs[5:9]]
    mq, mk, mv, mo, nq, kc, vc, ks, vs, kw, vw, small = outs
    h, c1, n1, m1 = _mlstm(mq, mk, mv, mo, small, W['gate_bias'], norm_g, c0, n0, m0, B, T)
    pe_k, pe_v, ck1, ck2, cv1, cv2 = nsa_w
    blk_w = CMP_BLOCK * LANES
    q3 = nq.reshape(B, T, MIX)
    if past is None:
        L = T
        nc = L // CMP_BLOCK
        kcmp = _compress(kc.reshape(B * nc, blk_w), pe_k, ck1, ck2).reshape(B, nc, LANES)
        vcmp = _compress(vc.reshape(B * nc, blk_w), pe_v, cv1, cv2).reshape(B, nc, LANES)
    else:
        pt = past['page_table']
        npages = pt.shape[1]
        L = npages * PAGE + T
        nc = L // CMP_BLOCK
        per_page = PAGE // CMP_BLOCK
        pool = past['kc'].shape[0]
        kcp = _compress_pool(_pages_t(past['kc']), pe_k, ck1, ck2).reshape(pool, per_page * LANES)
        vcp = _compress_pool(_pages_t(past['vc']), pe_v, cv1, cv2).reshape(pool, per_page * LANES)
        kcmp = jnp.take(kcp, pt, axis=0).reshape(B, npages * per_page, LANES)[:, :nc]
        vcmp = jnp.take(vcp, pt, axis=0).reshape(B, npages * per_page, LANES)[:, :nc]
    cpos = jnp.arange(nc) * CMP_BLOCK + CMP_BLOCK - 1
    ccos, csin = _rope_tables(cpos)
    kcmp = _rope_rows(kcmp.reshape(B * nc, LANES), jnp.tile(ccos, (B, 1)), jnp.tile(csin, (B, 1))).reshape(B, nc, LANES)
    ns = -(-L // SEL_BLOCK)
    ns_pad = -(-ns // LANES) * LANES
    cend = jnp.concatenate([cpos[0::2], cpos[1::2]]).astype(jnp.int32).reshape(1, nc)
    o_cmp, sel = _cmp_sel(q3, _even_odd_blocks(kcmp), _even_odd_blocks(vcmp), cend, q0, ns, ns_pad)
    ks3, vs3 = ks.reshape(B, T, LANES), vs.reshape(B, T, LANES)
    kw3, vw3 = kw.reshape(B, T, LANES), vw.reshape(B, T, LANES)
    if past is None:
        o_slc = _flash(q3, ks3, vs3, mode='sel', layout='nsa', tq=min(TQ_SLC, T), tk=min(TK_SLC, T), q0=0, sel=sel,
                       ns_pad=ns_pad)
        o_win = _flash(q3, kw3, vw3, mode='band', layout='nsa', tq=min(TQ_WIN, T), tk=min(TK_WIN, T), q0=0)
        kw_new, vw_new = kw3[:, -min(WINDOW, T):], vw3[:, -min(WINDOW, T):]
    else:
        o_slc = _flash(q3, _pages_t(past['ks']), _pages_t(past['vs']), mode='sel', layout='nsa', tq=T, tk=0, q0=q0,
                       page_table=pt, pps=_largest_divisor(npages, (16, 8, 4, 2, 1)), tail=(_tail_t(ks3), _tail_t(vs3)),
                       tail_pos0=q0, sel=sel, ns_pad=ns_pad)
        kw_ext = jnp.concatenate([past['kw_buf'], kw3], axis=1)
        vw_ext = jnp.concatenate([past['vw_buf'], vw3], axis=1)
        wb = past['kw_buf'].shape[1]
        o_win = _flash(q3, kw_ext, vw_ext, mode='band', layout='nsa', tq=T, tk=wb + T, q0=q0, kbase=q0 - wb)
        keep = min(WINDOW, wb + T)
        kw_new, vw_new = kw_ext[:, -keep:], vw_ext[:, -keep:]
    o = _nsa_combine(o_cmp.reshape(N, MIX), o_slc.reshape(N, MIX), o_win.reshape(N, MIX), small)
    x_new = _mm_res_ln([h, o], [W['w_out_a'], W['w_out_b']], x2, ln_g, ln_b).reshape(B, T, D)
    wsh = (B, -1, NSA_GROUPS, NSA_DH)
    state = (c1, n1, m1, *new_rows, kw_new.reshape(wsh), vw_new.reshape(wsh))
    return x_new, state


def _odd_mixer(x3, q0, W, s0, shift0, past, ln_g, ln_b):
    B, T, D = x3.shape
    N = B * T
    x2 = x3.reshape(N, D)
    hs = (B, T, FOX_HEADS, FOX_DH)
    if past is None and T % 256 == 0:
        outs, outs_t = _proj_split(x2, W['w_in'], ODD_SEGS, t_segs=(1, 2), bt=(B, T))
        new_kv = [jnp.transpose(t.reshape(B, FOX_HEADS, FOX_DH, T), (0, 3, 1, 2)) for t in outs_t]
    else:
        outs = _proj_split(x2, W['w_in'], ODD_SEGS)
        new_kv = [t.reshape(hs) for t in outs[1:3]]
    fq, fk, fv, rw, small = outs
    q3, k3, v3 = (t.reshape(B, T, MIX) for t in (fq, fk, fv))
    if past is None:
        lf_full, caug = _fox_cumsum(small, W['fox_bias'], B, T)
        lf = lf_full[:, :FOX_HEADS].reshape(B, T, FOX_HEADS)
        o_c = _flash(q3, k3, v3, mode='causal', layout='fox', tq=min(TQ_FOX, T), tk=min(TK_FOX, T), q0=0,
                     caug=caug.reshape(B, T, LANES))
    else:
        pt = past['page_table']
        npages = pt.shape[1]
        pps = _largest_divisor(npages, (16, 8, 4, 2, 1))
        padrows = ((0, 0), (0, PAGE - T), (0, 0))
        lf_pool_t = jnp.swapaxes(past['lf'], 1, 2)
        c_all, lf_new = _fox_cumsum_paged(lf_pool_t, pt, jnp.pad(small.reshape(B, T, LANES), padrows), W['fox_bias'], pps)
        lf = lf_new[:, :T, :FOX_HEADS]
        n_main = npages * PAGE
        o_c = _flash(q3, _pages_t(past['k']), _pages_t(past['v']), mode='causal', layout='fox_wide', tq=T, tk=0, q0=q0,
                     page_table=pt, pps=pps, tail=(_tail_t(k3), _tail_t(v3)), tail_pos0=q0,
                     crow=c_all, crow_tail=c_all[:, :, n_main:n_main + PAGE])
    rw3 = rw.reshape(B, T, RWKV_SHIFT)
    r, lw, k, v, kk, a, g = _rwkv_prep(rw3, shift0[:, RW_PERM], W['P'])
    y, s1 = _rwkv_scan(r, lw, k, v, kk, a, g, s0, W['P'], B, T)
    x_new = _mm_res_ln([o_c.reshape(N, MIX), y.reshape(N, MIX)], [W['w_out_a'], W['w_out_b']], x2, ln_g, ln_b)
    state = (*new_kv, lf, s1, rw3[:, -1][:, RW_INV])
    return x_new.reshape(B, T, D), state


def kernel(x_prompt, x_sample, state_mlstm_C, state_mlstm_n, state_mlstm_m, cache_nsa_kc, cache_nsa_vc, cache_nsa_ks, cache_nsa_vs, cache_nsa_kw, cache_nsa_vw, cache_fox_k, cache_fox_v, cache_fox_logf, state_rwkv_S, state_rwkv_shift, cache_mem_k, cache_mem_v, page_table, mem_prompt, even_w_in, even_w_out, mlstm_b_i, mlstm_b_f, mlstm_norm_g, nsa_pe_k, nsa_pe_v, nsa_cmp_k_w1, nsa_cmp_k_w2, nsa_cmp_v_w1, nsa_cmp_v_w2, odd_w_in, odd_w_out, fox_b_f, rwkv_mu, rwkv_w0, rwkv_w2, rwkv_a0, rwkv_a2, rwkv_g2, rwkv_k_k, rwkv_k_a, rwkv_r_k, rwkv_ln_g, rwkv_ln_b, mem_wq, mem_wk, mem_wv, mem_wo, moe_w_group, moe_b_group, moe_w_expert, moe_b_expert, moe_w1, moe_w3, moe_w2, ln_g, ln_b):
    bp, tp, d = x_prompt.shape
    bs, ts, _ = x_sample.shape
    depth = ln_g.shape[0]
    past = page_table.shape[1] * cache_nsa_kc.shape[2]
    xp, xs = x_prompt, x_sample
    acc = {}

    def push(prefix, names, vals):
        for nm, val in zip(names, vals):
            acc.setdefault(prefix + nm, []).append(val)

    even_names = ('mlstm_C', 'mlstm_n', 'mlstm_m', 'nsa_kc', 'nsa_vc', 'nsa_ks', 'nsa_vs', 'nsa_kw', 'nsa_vw')
    odd_names = ('fox_k', 'fox_v', 'fox_logf', 'rwkv_S', 'rwkv_shift')
    for layer in range(depth):
        e = layer // 2
        if layer % 2 == 0:
            W = _prep_even(even_w_in[e], even_w_out[e], mlstm_b_i[e], mlstm_b_f[e])
            nsa_w = (nsa_pe_k[e], nsa_pe_v[e], nsa_cmp_k_w1[e], nsa_cmp_k_w2[e], nsa_cmp_v_w1[e], nsa_cmp_v_w2[e])
            xp, sp = _even_mixer(xp, 0, W, nsa_w, mlstm_norm_g[e], jnp.zeros((bp, M_HEADS, M_DV, M_DK), F32),
                                 jnp.zeros((bp, M_HEADS, M_DK), F32), jnp.zeros((bp, M_HEADS), F32), None,
                                 ln_g[layer, 0], ln_b[layer, 0])
            wbuf = lambda c: c[e].reshape(bs, -1, LANES)
            past_d = dict(kc=cache_nsa_kc[e], vc=cache_nsa_vc[e], ks=cache_nsa_ks[e], vs=cache_nsa_vs[e],
                          page_table=page_table, kw_buf=wbuf(cache_nsa_kw), vw_buf=wbuf(cache_nsa_vw))
            xs, ss = _even_mixer(xs, past, W, nsa_w, mlstm_norm_g[e], state_mlstm_C[e], state_mlstm_n[e],
                                 state_mlstm_m[e], past_d, ln_g[layer, 0], ln_b[layer, 0])
            names = even_names
        else:
            W = _prep_odd(odd_w_in[e], odd_w_out[e], fox_b_f[e], rwkv_mu[e], rwkv_w0[e], rwkv_w2[e], rwkv_a0[e],
                          rwkv_a2[e], rwkv_g2[e], rwkv_k_k[e], rwkv_k_a[e], rwkv_r_k[e], rwkv_ln_g[e], rwkv_ln_b[e])
            xp, sp = _odd_mixer(xp, 0, W, jnp.zeros((bp, RWKV_HEADS, RWKV_N, RWKV_N), F32),
                                jnp.zeros((bp, RWKV_SHIFT), F32), None, ln_g[layer, 0], ln_b[layer, 0])
            past_d = dict(k=cache_fox_k[e], v=cache_fox_v[e], lf=cache_fox_logf[e], page_table=page_table)
            xs, ss = _odd_mixer(xs, past, W, state_rwkv_S[e], state_rwkv_shift[e], past_d,
                                ln_g[layer, 0], ln_b[layer, 0])
            names = odd_names
        push('p_', names, sp)
        push('s_', names, ss)
        wkv = jnp.concatenate([mem_wk[layer], mem_wv[layer]], axis=1).astype(BF16)
        mt = mem_prompt.shape[1]
        kmp, vmp = _proj_split(mem_prompt.reshape(bp * mt, d), wkv, [(d, False), (d, False)])
        push('p_', ('mem_k', 'mem_v'), (kmp.reshape(bp, mt, MEM_HEADS, MEM_DH), vmp.reshape(bp, mt, MEM_HEADS, MEM_DH)))
        wq, wo = mem_wq[layer].astype(BF16), mem_wo[layer].astype(BF16)
        xp = _mem_layer(xp, kmp.reshape(bp, mt, d), vmp.reshape(bp, mt, d), wq, wo, ln_g[layer, 1], ln_b[layer, 1])
        xs = _mem_layer(xs, cache_mem_k[layer].reshape(bs, -1, d), cache_mem_v[layer].reshape(bs, -1, d), wq, wo,
                        ln_g[layer, 1], ln_b[layer, 1])
        M = _prep_moe(moe_w_group[layer], moe_b_group[layer], moe_w_expert[layer], moe_b_expert[layer],
                      moe_w1[layer], moe_w3[layer], moe_w2[layer])
        xp = _moe_layer_grouped(xp.reshape(bp * tp, d), M, ln_g[layer, 2], ln_b[layer, 2]).reshape(bp, tp, d)
        xs = _moe_layer_grouped(xs.reshape(bs * ts, d), M, ln_g[layer, 2], ln_b[layer, 2]).reshape(bs, ts, d)

    st = {name: jnp.stack(vals) for name, vals in acc.items()}
    return (xp, xs,
            st['p_mlstm_C'], st['p_mlstm_n'], st['p_mlstm_m'],
            st['p_nsa_kc'], st['p_nsa_vc'], st['p_nsa_ks'], st['p_nsa_vs'], st['p_nsa_kw'], st['p_nsa_vw'],
            st['p_fox_k'], st['p_fox_v'], st['p_fox_logf'], st['p_rwkv_S'], st['p_rwkv_shift'],
            st['p_mem_k'], st['p_mem_v'],
            st['s_mlstm_C'], st['s_mlstm_n'], st['s_mlstm_m'],
            st['s_nsa_kc'], st['s_nsa_vc'], st['s_nsa_ks'], st['s_nsa_vs'], st['s_nsa_kw'], st['s_nsa_vw'],
            st['s_fox_k'], st['s_fox_v'], st['s_fox_logf'], st['s_rwkv_S'], st['s_rwkv_shift'])
```

```python
import functools

import numpy as np
import jax
import jax.numpy as jnp
from jax import lax
from jax.experimental import pallas as pl
from jax.experimental.pallas import tpu as pltpu

F32 = jnp.float32
BF16 = jnp.bfloat16
HI = lax.Precision.HIGHEST

D_MODEL = 1024
DEPTH = 2
PAGE = 128
M_HEADS, M_DK, M_DV, M_CHUNK = 4, 128, 128, 64
NSA_HEADS, NSA_GROUPS, NSA_DH = 8, 2, 64
CMP_BLOCK, SEL_BLOCK, N_SEL, WINDOW = 32, 64, 16, 512
FOX_HEADS, FOX_DH = 8, 64
RWKV_HEADS, RWKV_N, W_LORA, A_LORA, G_LORA = 8, 64, 64, 64, 128
RWKV_CHUNK = 64
MEM_HEADS = 4
MEM_DH = D_MODEL // MEM_HEADS
N_GROUPS, EXP_PER_GROUP, TOP_K, D_EXPERT = 4, 8, 2, 256
N_EXPERTS = N_GROUPS * EXP_PER_GROUP
ROPE_THETA = 10000.0
ALPHA = (2.0 * DEPTH) ** 0.25
LN_EPS = 1e-5
NEG = -1e30
FORCE = 1e4
MIX = 512
RWKV_SHIFT = 1792
SEL_OFF = 2.0 ** 30
LOG2E = 1.4426950408889634

LANES = 128
VMEM_LIMIT = 48 * 1024 * 1024
TQ_FOX, TK_FOX = 512, 512
TQ_SLC, TK_SLC = 256, 512
TQ_WIN, TK_WIN = 128, 128
TQ_CMP = 256
MOE_EXPERTS_PER_STEP = 4


def _cp(*sem):
    return pltpu.CompilerParams(dimension_semantics=sem, vmem_limit_bytes=VMEM_LIMIT)


def _dot(a, b, prec=None):
    return jnp.dot(a, b, preferred_element_type=F32, precision=prec)


def _dot_nt(a, b, prec=None):
    return lax.dot_general(a, b, (((1,), (1,)), ((), ())), preferred_element_type=F32, precision=prec)


def _dot_tn(a, b, prec=None):
    return lax.dot_general(a, b, (((0,), (0,)), ((), ())), preferred_element_type=F32, precision=prec)


def _bdot(a, b):
    return _dot(a.astype(BF16), b.astype(BF16))


def _bdot_nt(a, b):
    return _dot_nt(a.astype(BF16), b.astype(BF16))


def _bdot_tn(a, b):
    return _dot_tn(a.astype(BF16), b.astype(BF16))


def _split_bf16(a):
    hi = a.astype(BF16)
    return hi, (a - hi.astype(F32)).astype(BF16)


def _dot3(a, b):
    ah, al = _split_bf16(a)
    bh, bl = _split_bf16(b)
    return _dot(ah, bh) + (_dot(ah, bl) + _dot(al, bh))


def _split3_bf16(a):
    hi = a.astype(BF16)
    r1 = a - hi.astype(F32)
    mid = r1.astype(BF16)
    return hi, mid, (r1 - mid.astype(F32)).astype(BF16)


def _dot_r01(a, b01):
    b = b01.astype(BF16)
    hi, mid, lo = _split3_bf16(a)
    return _dot(hi, b) + (_dot(mid, b) + _dot(lo, b))


def _dot_l01(a01, b):
    a = a01.astype(BF16)
    hi, mid, lo = _split3_bf16(b)
    return _dot(a, hi) + (_dot(a, mid) + _dot(a, lo))


def _dot_nt_l01(a01, b):
    a = a01.astype(BF16)
    hi, mid, lo = _split3_bf16(b)
    return _dot_nt(a, hi) + (_dot_nt(a, mid) + _dot_nt(a, lo))


def _iota(shape, dim):
    return lax.broadcasted_iota(jnp.int32, shape, dim)


def _log_sigmoid(x):
    return jnp.minimum(x, 0.0) - jnp.log1p(jnp.exp(-jnp.abs(x)))


def _sigmoid(x):
    return 1.0 / (1.0 + jnp.exp(-x))


def _softplus(x):
    return jnp.maximum(x, 0.0) + jnp.log1p(jnp.exp(-jnp.abs(x)))


def _layer_norm_rows(z, g, b):
    mu = jnp.mean(z, axis=-1, keepdims=True)
    zc = z - mu
    var = jnp.mean(zc * zc, axis=-1, keepdims=True)
    return zc * lax.rsqrt(var + LN_EPS) * g + b


def _rope128(y, cos, sin_signed):
    lane = _iota(y.shape, 1)
    lo = (lane & 63) < 32
    sw = jnp.where(lo, pltpu.roll(y, 96, 1), pltpu.roll(y, 32, 1))
    return y * cos + sw * sin_signed


def _rope_tables(pos):
    half = NSA_DH // 2
    freq = ROPE_THETA ** (-jnp.arange(half, dtype=F32) / half)
    ang = pos.astype(F32)[:, None] * freq[None, :]
    cos, sin = jnp.cos(ang), jnp.sin(ang)
    return jnp.tile(cos, (1, 4)), jnp.tile(jnp.concatenate([-sin, sin], -1), (1, 2))


def _largest_divisor(n, options):
    for p in options:
        if n % p == 0:
            return p


def _proj_kernel(*refs, segs, use_rope, t_segs):
    if use_rope:
        x_ref, w_ref, cos_ref, sin_ref = refs[:4]
        outs = refs[4:]
    else:
        x_ref, w_ref = refs[:2]
        outs = refs[2:]
    t_outs = dict(zip(t_segs, outs[len(segs):]))
    xb = x_ref[...].astype(BF16)
    off = 0
    for si, ((wd, rp), o_ref) in enumerate(zip(segs, outs)):
        y = _dot(xb, w_ref[:, off:off + wd])
        if rp:
            y = jnp.concatenate([_rope128(y[:, c * LANES:(c + 1) * LANES], cos_ref[...], sin_ref[...])
                                 for c in range(wd // LANES)], axis=1) if wd > LANES else \
                _rope128(y, cos_ref[...], sin_ref[...])
        o_ref[...] = y
        if si in t_outs:
            t_outs[si][...] = y.T
        off += wd


def _proj_split(x, w, segs, rope=None, t_segs=(), bt=None):
    n, k = x.shape
    tm = min(256, n)
    wtot = sum(s[0] for s in segs)
    use_rope = rope is not None
    in_specs = [pl.BlockSpec((tm, k), lambda i: (i, 0)), pl.BlockSpec((k, wtot), lambda i: (0, 0))]
    args = [x, w]
    if use_rope:
        in_specs += [pl.BlockSpec((tm, LANES), lambda i: (i, 0))] * 2
        args += list(rope)
    out_shape = [jax.ShapeDtypeStruct((n, wd), F32) for wd, _ in segs]
    out_specs = [pl.BlockSpec((tm, wd), lambda i: (i, 0)) for wd, _ in segs]
    if t_segs:
        B, T = bt
        npt = T // tm
        for si in t_segs:
            wd = segs[si][0]
            out_shape.append(jax.ShapeDtypeStruct((B, wd, T), F32))
            out_specs.append(pl.BlockSpec((None, wd, tm), lambda i: (i // npt, 0, i % npt)))
    outs = pl.pallas_call(
        functools.partial(_proj_kernel, segs=tuple(segs), use_rope=use_rope, t_segs=tuple(t_segs)),
        out_shape=out_shape,
        grid=(n // tm,),
        in_specs=in_specs,
        out_specs=out_specs,
        compiler_params=_cp("parallel"),
    )(*args)
    return (outs[:len(segs)], outs[len(segs):]) if t_segs else outs


def _mm_res_ln_kernel(*refs, n_in):
    a_refs = refs[:n_in]
    w_refs = refs[n_in:2 * n_in]
    x_ref, g_ref, b_ref, o_ref = refs[2 * n_in:]
    y = _dot(a_refs[0][...].astype(BF16), w_refs[0][...])
    for a_ref, w_ref in zip(a_refs[1:], w_refs[1:]):
        y = y + _dot(a_ref[...].astype(BF16), w_ref[...])
    o_ref[...] = _layer_norm_rows(ALPHA * x_ref[...] + y, g_ref[...], b_ref[...])


def _mm_res_ln(a_list, w_list, x, g, b):
    n, d = x.shape
    tm = min(512, n)
    n_in = len(a_list)
    in_specs = ([pl.BlockSpec((tm, a.shape[1]), lambda i: (i, 0)) for a in a_list]
                + [pl.BlockSpec(w.shape, lambda i: (0, 0)) for w in w_list]
                + [pl.BlockSpec((tm, d), lambda i: (i, 0)), pl.BlockSpec((1, d), lambda i: (0, 0)),
                   pl.BlockSpec((1, d), lambda i: (0, 0))])
    return pl.pallas_call(
        functools.partial(_mm_res_ln_kernel, n_in=n_in),
        out_shape=jax.ShapeDtypeStruct((n, d), F32),
        grid=(n // tm,),
        in_specs=in_specs,
        out_specs=pl.BlockSpec((tm, d), lambda i: (i, 0)),
        compiler_params=_cp("parallel"),
    )(*a_list, *w_list, x, g.reshape(1, d), b.reshape(1, d))


def _mlstm_kernel(q_ref, k_ref, v_ref, o_ref, sm_ref, bias_ref, g_ref, c0_ref, n0_ref, m0_ref,
                  h_ref, c1_ref, n1_ref, m1_ref, c_scr, n_scr, m_scr, *, L):
    ci = pl.program_id(1)

    @pl.when(ci == 0)
    def _():
        c_scr[...] = c0_ref[...]
        n_scr[...] = n0_ref[...]
        m_scr[...] = m0_ref[...]

    sm = sm_ref[...] + bias_ref[...]
    lane = _iota(sm.shape, 1)
    gates = jnp.where(lane < M_HEADS, sm, _log_sigmoid(sm))
    eye8 = (_iota((8, LANES), 0) == _iota((8, LANES), 1)).astype(F32)
    gates_row = _dot_nt_l01(eye8, gates)
    r_i = _iota((L, L), 0)
    c_i = _iota((L, L), 1)
    tri = c_i <= r_i
    scale = M_DK ** -0.5
    hs = []
    for h in range(M_HEADS):
        ic_col = gates[:, h:h + 1]
        f_col = gates[:, M_HEADS + h:M_HEADS + h + 1]
        ic_row = gates_row[h:h + 1, :]
        f_row = gates_row[M_HEADS + h:M_HEADS + h + 1, :]
        b_col = jnp.sum(jnp.where(tri, jnp.broadcast_to(f_row, (L, L)), 0.0), axis=1, keepdims=True)
        b_row = jnp.sum(jnp.where(r_i <= c_i, jnp.broadcast_to(f_col, (L, L)), 0.0), axis=0, keepdims=True)
        m_prev = m_scr[h:h + 1, 0:1]
        qh = q_ref[:, h * M_DK:(h + 1) * M_DK]
        kh = k_ref[:, h * M_DK:(h + 1) * M_DK] * scale
        vh = v_ref[:, h * M_DV:(h + 1) * M_DV]
        hs.append(dict(ic_col=ic_col, ic_row=ic_row, b_col=b_col, b_row=b_row, m_prev=m_prev, qh=qh, kh=kh, vh=vh,
                       qb=qh.astype(BF16), kb=kh.astype(BF16), vb=vh.astype(BF16),
                       c_old=c_scr[h], n_old=n_scr[h:h + 1, :]))
    for d in hs:
        dmat = jnp.where(tri, d['b_col'] - d['b_row'] + d['ic_row'], NEG)
        inter = d['b_col'] + d['m_prev']
        d['mt'] = jnp.maximum(inter, jnp.max(dmat, axis=1, keepdims=True))
        d['wmat'] = jnp.where(tri, jnp.exp(dmat - d['mt']), 0.0)
        d['a'] = jnp.exp(inter - d['mt'])
    for d in hs:
        d['s'] = _dot_nt(d['qb'], d['kb']) * d['wmat']
        d['cq'] = _dot_nt(d['qb'], d['c_old'].astype(BF16))
    for d in hs:
        num = d['a'] * d['cq'] + _dot(d['s'].astype(BF16), d['vb'])
        den = d['a'] * jnp.sum(d['qh'] * d['n_old'], axis=1, keepdims=True) + jnp.sum(d['s'], axis=1, keepdims=True)
        d['hh'] = num / jnp.maximum(jnp.abs(den), jnp.exp(-d['mt']))
    for h, d in enumerate(hs):
        m_l = d['mt'][L - 1:L, :]
        b_l = d['b_col'][L - 1:L, :]
        g_l = jnp.exp(b_l - d['b_col'] + d['ic_col'] - m_l)
        a_l = jnp.exp(b_l + d['m_prev'] - m_l)
        c_scr[h] = a_l * d['c_old'] + _dot_tn((g_l * d['vh']).astype(BF16), d['kb'])
        n_scr[h:h + 1, :] = a_l * d['n_old'] + jnp.sum(g_l * d['kh'], axis=0, keepdims=True)
        m_scr[h:h + 1, :] = jnp.broadcast_to(m_l, (1, LANES))
    for h, d in enumerate(hs):
        hh = d['hh']
        mu = jnp.mean(hh, axis=1, keepdims=True)
        hc = hh - mu
        var = jnp.mean(hc * hc, axis=1, keepdims=True)
        hn = hc * lax.rsqrt(var + 1e-6) * g_ref[:, h * M_DV:(h + 1) * M_DV]
        h_ref[:, h * M_DV:(h + 1) * M_DV] = hn * _sigmoid(o_ref[:, h * M_DV:(h + 1) * M_DV])

    @pl.when(ci == pl.num_programs(1) - 1)
    def _():
        c1_ref[...] = c_scr[...]
        n1_ref[...] = n_scr[...]
        m1_ref[...] = m_scr[...]


def _mlstm(mq, mk, mv, mo, small, bias_row, norm_g, c0, n0, m0, B, T):
    L = M_CHUNK if T % M_CHUNK == 0 else T
    nch = T // L
    row = lambda b, c: (b * nch + c, 0)
    st4 = lambda b, c: (b, 0, 0, 0)
    st3 = lambda b, c: (b, 0, 0)
    m0b = jnp.broadcast_to(m0[:, :, None], (B, M_HEADS, LANES)).astype(F32)
    h, c1, n1, m1 = pl.pallas_call(
        functools.partial(_mlstm_kernel, L=L),
        out_shape=[jax.ShapeDtypeStruct((B * T, MIX), F32),
                   jax.ShapeDtypeStruct((B, M_HEADS, M_DV, M_DK), F32),
                   jax.ShapeDtypeStruct((B, M_HEADS, M_DK), F32),
                   jax.ShapeDtypeStruct((B, M_HEADS, LANES), F32)],
        grid=(B, nch),
        in_specs=[pl.BlockSpec((L, MIX), row)] * 4
        + [pl.BlockSpec((L, LANES), row), pl.BlockSpec((1, LANES), lambda b, c: (0, 0)),
           pl.BlockSpec((1, MIX), lambda b, c: (0, 0)),
           pl.BlockSpec((None, M_HEADS, M_DV, M_DK), st4), pl.BlockSpec((None, M_HEADS, M_DK), st3),
           pl.BlockSpec((None, M_HEADS, LANES), st3)],
        out_specs=[pl.BlockSpec((L, MIX), row), pl.BlockSpec((None, M_HEADS, M_DV, M_DK), st4),
                   pl.BlockSpec((None, M_HEADS, M_DK), st3), pl.BlockSpec((None, M_HEADS, LANES), st3)],
        scratch_shapes=[pltpu.VMEM((M_HEADS, M_DV, M_DK), F32), pltpu.VMEM((M_HEADS, M_DK), F32),
                        pltpu.VMEM((M_HEADS, LANES), F32)],
        compiler_params=_cp("parallel", "arbitrary"),
    )(mq, mk, mv, mo, small, bias_row, norm_g.reshape(1, MIX), c0.astype(F32), n0.astype(F32), m0b)
    return h, c1, n1, m1[:, :, 0]


def _compress_weights(pe, w1, w2):
    eye = jnp.eye(NSA_GROUPS, dtype=F32)
    w1j = jnp.einsum('jdo,gh->jgdho', w1.reshape(CMP_BLOCK, NSA_DH, NSA_DH), eye)
    w1j = w1j.reshape(CMP_BLOCK, LANES, LANES).astype(BF16)
    w2b = jnp.einsum('do,gh->gdho', w2, eye).reshape(LANES, LANES).astype(BF16)
    pe2 = jnp.tile(pe, (1, NSA_GROUPS))
    return pe2, w1j, w2b


def _compress_kernel(x_ref, pe_ref, w1_ref, w2_ref, o_ref):
    xb = (x_ref[...] + pe_ref[...]).astype(BF16)
    hid = jax.nn.gelu(_dot(xb, w1_ref[...]))
    o_ref[...] = _dot(hid.astype(BF16), w2_ref[...])


def _compress(blocks, pe, w1, w2):
    r = blocks.shape[0]
    tb = min(256, r)
    width = CMP_BLOCK * LANES
    pe2, w1j, w2b = _compress_weights(pe, w1, w2)
    return pl.pallas_call(
        _compress_kernel,
        out_shape=jax.ShapeDtypeStruct((r, LANES), F32),
        grid=(pl.cdiv(r, tb),),
        in_specs=[pl.BlockSpec((tb, width), lambda i: (i, 0)), pl.BlockSpec((1, width), lambda i: (0, 0)),
                  pl.BlockSpec((width, LANES), lambda i: (0, 0)), pl.BlockSpec((LANES, LANES), lambda i: (0, 0))],
        out_specs=pl.BlockSpec((tb, LANES), lambda i: (i, 0)),
        compiler_params=_cp("parallel"),
    )(blocks, pe2.reshape(1, width), w1j.reshape(width, LANES), w2b)


def _compress_pool_kernel(xt_ref, pe_ref, w1_ref, w2_ref, o_ref, xs_scr, *, pb):
    per_page = PAGE // CMP_BLOCK
    for p in range(pb):
        xs_scr[p * PAGE:(p + 1) * PAGE, :] = xt_ref[p].T + pe_ref[...]
    acc = None
    for j in range(CMP_BLOCK):
        rows = xs_scr[pl.ds(j, per_page * pb, stride=CMP_BLOCK), :]
        t = _dot(rows.astype(BF16), w1_ref[j])
        acc = t if acc is None else acc + t
    o_ref[...] = _dot(jax.nn.gelu(acc).astype(BF16), w2_ref[...])


def _compress_pool(pool_t, pe, w1, w2):
    npool = pool_t.shape[0]
    per_page = PAGE // CMP_BLOCK
    pb = _largest_divisor(npool, (32, 16, 8, 4, 2))
    pe2, w1j, w2b = _compress_weights(pe, w1, w2)
    pe_tile = jnp.tile(pe2, (per_page, 1))
    return pl.pallas_call(
        functools.partial(_compress_pool_kernel, pb=pb),
        out_shape=jax.ShapeDtypeStruct((npool * per_page, LANES), F32),
        grid=(npool // pb,),
        in_specs=[pl.BlockSpec((pb, LANES, PAGE), lambda i: (i, 0, 0)), pl.BlockSpec((PAGE, LANES), lambda i: (0, 0)),
                  pl.BlockSpec((CMP_BLOCK, LANES, LANES), lambda i: (0, 0, 0)),
                  pl.BlockSpec((LANES, LANES), lambda i: (0, 0))],
        out_specs=pl.BlockSpec((per_page * pb, LANES), lambda i: (i, 0)),
        scratch_shapes=[pltpu.VMEM((pb * PAGE, LANES), F32)],
        compiler_params=_cp("parallel"),
    )(pool_t, pe_tile, w1j, w2b)


def _rope_rows_kernel(x_ref, cos_ref, sin_ref, o_ref):
    o_ref[...] = _rope128(x_ref[...], cos_ref[...], sin_ref[...])


def _rope_rows(x, cos, sin):
    r = x.shape[0]
    tb = min(512, r)
    spec = pl.BlockSpec((tb, LANES), lambda i: (i, 0))
    return pl.pallas_call(_rope_rows_kernel, out_shape=jax.ShapeDtypeStruct((r, LANES), F32), grid=(r // tb,),
                          in_specs=[spec] * 3, out_specs=spec, compiler_params=_cp("parallel"))(x, cos, sin)


def _stack_heads(qblk, heads, scale):
    rows = []
    for src_blk, src_half, dst_half in heads:
        x = qblk[:, src_blk * LANES:(src_blk + 1) * LANES] * scale
        if src_half != dst_half:
            x = pltpu.roll(x, 64, 1)
        lane = _iota(x.shape, 1)
        keep = (lane >= 64) if dst_half else (lane < 64)
        rows.append(jnp.where(keep, x, 0.0))
    return jnp.concatenate(rows, axis=0)


def _unstack_heads(o_full_list, heads_per_kb, tq, n_out_blocks):
    outs = [None] * n_out_blocks
    for o_full, heads in zip(o_full_list, heads_per_kb):
        for idx, (src_blk, src_half, dst_half) in enumerate(heads):
            piece = o_full[idx * tq:(idx + 1) * tq]
            if src_half != dst_half:
                piece = pltpu.roll(piece, 64, 1)
            lane = _iota(piece.shape, 1)
            keep = (lane >= 64) if src_half else (lane < 64)
            prev = outs[src_blk]
            outs[src_blk] = jnp.where(keep, piece, 0.0 if prev is None else prev)
    return outs


def _lane_repeat(x, width):
    if width == LANES:
        return x
    if width % LANES == 0:
        return pltpu.repeat(x, width // LANES, axis=1)
    return jnp.broadcast_to(x[:, 0:1], (x.shape[0], width))


def _stack_wide(qblk, n_heads, scale):
    head_of_lane = _iota(qblk.shape, 1) >> 6
    return jnp.concatenate([jnp.where(head_of_lane == h, qblk * scale, 0.0) for h in range(n_heads)], axis=0)


def _unstack_wide(o_full, n_heads, tq):
    head_of_lane = _iota((tq, o_full.shape[1]), 1) >> 6
    out = jnp.zeros((tq, o_full.shape[1]), F32)
    for h in range(n_heads):
        out = jnp.where(head_of_lane == h, o_full[h * tq:(h + 1) * tq], out)
    return out


NSA_HEAD_MAP = [[(h // 2, h % 2, h // (NSA_HEADS // NSA_GROUPS)) for h in range(NSA_HEADS)]]
FOX_HEAD_MAP = [[(kb, 0, 0), (kb, 1, 1)] for kb in range(FOX_HEADS // 2)]


def _cmp_sel_kernel(q_ref, kc_ref, vc_ref, cend_ref, o_ref, sel_ref, *, tq, q0, nc, ns, ns_pad):
    qi = pl.program_id(1)
    heads = NSA_HEAD_MAP[0]
    hpg = NSA_HEADS // NSA_GROUPS
    qbd = _stack_heads(q_ref[...], heads, NSA_DH ** -0.5).astype(BF16)
    s = _dot_nt(qbd, kc_ref[...].astype(BF16))
    rows = NSA_HEADS * tq
    qpos = q0 + qi * tq + (_iota((rows, 1), 0) & (tq - 1))
    mask = cend_ref[...] <= qpos
    s = jnp.where(mask, s, NEG)
    e = jnp.where(mask, jnp.exp(s - jnp.max(s, axis=1, keepdims=True)), 0.0)
    p = e / jnp.maximum(jnp.sum(e, axis=1, keepdims=True), 1e-30)
    o_full = _dot(p.astype(BF16), vc_ref[...].astype(BF16))
    outs = _unstack_heads([o_full], NSA_HEAD_MAP, tq, MIX // LANES)
    for c, blk in enumerate(outs):
        o_ref[:, c * LANES:(c + 1) * LANES] = blk
    half = nc // 2
    qp = q0 + qi * tq + _iota((tq, 1), 0)
    blk_i = _iota((tq, ns_pad), 1)
    blk_f = blk_i.astype(F32)
    cur = qp >> 6
    forced = (blk_i == 0) | (blk_i == cur) | (blk_i == cur - 1)
    valid = (blk_i * SEL_BLOCK <= qp) & (blk_i < ns)
    n_sel = min(N_SEL, ns)
    scores, chosen = [], []
    for g in range(NSA_GROUPS):
        pg = p[g * hpg * tq:(g * hpg + 1) * tq]
        for j in range(1, hpg):
            pg = pg + p[(g * hpg + j) * tq:(g * hpg + j + 1) * tq]
        imp = pg[:, :half] + pg[:, half:]
        if ns_pad > half:
            imp = jnp.concatenate([imp, jnp.zeros((tq, ns_pad - half), F32)], axis=1)
        score = jnp.where(valid, imp + FORCE * forced.astype(F32), NEG)
        scores.append(jnp.where(blk_i < ns, score, -jnp.inf))
        chosen.append(jnp.zeros((tq, ns_pad), F32))
    for _ in range(n_sel):
        for g in range(NSA_GROUPS):
            mx = jnp.max(scores[g], axis=1, keepdims=True)
            first = jnp.min(jnp.where(scores[g] == mx, blk_f, 1e9), axis=1, keepdims=True)
            hit = blk_f == first
            chosen[g] = jnp.where(hit, 1.0, chosen[g])
            scores[g] = jnp.where(hit, -jnp.inf, scores[g])
    for g in range(NSA_GROUPS):
        sel_ref[:, g * ns_pad:(g + 1) * ns_pad] = jnp.where(valid, chosen[g], 0.0)


def _cmp_sel(q3, kcmp, vcmp, cend, q0, ns, ns_pad):
    B, T, _ = q3.shape
    nc = kcmp.shape[1]
    tq = min(TQ_CMP, T)
    return pl.pallas_call(
        functools.partial(_cmp_sel_kernel, tq=tq, q0=q0, nc=nc, ns=ns, ns_pad=ns_pad),
        out_shape=[jax.ShapeDtypeStruct((B, T, MIX), F32), jax.ShapeDtypeStruct((B, T, NSA_GROUPS * ns_pad), F32)],
        grid=(B, T // tq),
        in_specs=[pl.BlockSpec((None, tq, MIX), lambda b, i: (b, i, 0)),
                  pl.BlockSpec((None, nc, LANES), lambda b, i: (b, 0, 0)),
                  pl.BlockSpec((None, nc, LANES), lambda b, i: (b, 0, 0)),
                  pl.BlockSpec((1, nc), lambda b, i: (0, 0))],
        out_specs=[pl.BlockSpec((None, tq, MIX), lambda b, i: (b, i, 0)),
                   pl.BlockSpec((None, tq, NSA_GROUPS * ns_pad), lambda b, i: (b, i, 0))],
        compiler_params=_cp("parallel", "arbitrary"),
    )(q3, kcmp, vcmp, cend)


def _flash_kernel(*refs, cfg):
    mode, layout, tq, tk = cfg['mode'], cfg['layout'], cfg['tq'], cfg['tk']
    q0, kbase, pps, has_tail, tail_pos0 = cfg['q0'], cfg['kbase'], cfg['pps'], cfg['has_tail'], cfg['tail_pos0']
    ns_pad, n_pref, qw, kv_t = cfg['ns_pad'], cfg['n_pref'], cfg['qw'], cfg['kv_t']
    head_map = NSA_HEAD_MAP if layout == 'nsa' else FOX_HEAD_MAP
    nkb = 1 if layout in ('nsa', 'fox_wide') else FOX_HEADS // 2
    n_heads_kb = FOX_HEADS if layout == 'fox_wide' else len(head_map[0])
    rows = n_heads_kb * tq
    qi_ref, kj_ref, fl_ref = refs[:3]
    pos = n_pref
    q_ref = refs[pos]; pos += 1
    npg = max(pps, 1)
    k_refs = refs[pos:pos + npg]; pos += npg
    v_refs = refs[pos:pos + npg]; pos += npg
    if has_tail:
        tk_ref, tv_ref = refs[pos:pos + 2]; pos += 2
    c_ref = ct_ref = sel_ref = None
    if cfg['has_caug'] or cfg['has_crow']:
        c_ref = refs[pos]; pos += 1
        if has_tail:
            ct_ref = refs[pos]; pos += 1
    if mode == 'sel':
        sel_ref = refs[pos]; pos += 1
    o_ref = refs[pos]; pos += 1
    qst_scr, m_scr, l_scr, acc_scr = refs[pos:pos + 4]

    s_id = pl.program_id(1)
    qi = qi_ref[s_id]
    kj = kj_ref[s_id]
    fl = fl_ref[s_id]
    scale = NSA_DH ** -0.5 * LOG2E

    @pl.when((fl & 1) != 0)
    def _init():
        qblk = q_ref[...]
        if layout == 'fox_wide':
            qst_scr[0] = _stack_wide(qblk, FOX_HEADS, scale).astype(BF16)
        elif layout == 'nsa':
            qbd = _stack_heads(qblk, head_map[0], scale).astype(BF16)
            if mode == 'sel':
                selblk = sel_ref[...]
                hpg = NSA_HEADS // NSA_GROUPS
                parts = []
                for g in range(NSA_GROUPS):
                    off = ((selblk[:, g * ns_pad:(g + 1) * ns_pad] - 1.0) * SEL_OFF).astype(BF16)
                    parts += [off] * hpg
                qbd = jnp.concatenate([qbd, jnp.concatenate(parts, axis=0)], axis=1)
            qst_scr[0] = qbd
        else:
            for kb in range(nkb):
                qbd = _stack_heads(qblk, head_map[kb], scale).astype(BF16)
                if cfg['has_caug']:
                    lane = _iota((rows, LANES), 1)
                    head = 2 * kb + (_iota((rows, LANES), 0) >= tq).astype(jnp.int32)
                    aug = jnp.where((lane >= 3 * head) & (lane < 3 * head + 3), -1.0, 0.0).astype(BF16)
                    qbd = jnp.concatenate([qbd, aug], axis=1)
                qst_scr[kb] = qbd
        m_scr[...] = jnp.full(m_scr.shape, NEG, F32)
        l_scr[...] = jnp.zeros(l_scr.shape, F32)
        acc_scr[...] = jnp.zeros(acc_scr.shape, F32)

    def update(k_tile, v_tile, kpos0, c_tile, width, masked):
        if masked:
            qpos = q0 + qi * tq + (_iota((rows, 1), 0) & (tq - 1))
            kpos = kpos0 + _iota((1, width), 1)
            mask = kpos <= qpos
            if mode == 'band':
                mask = mask & (kpos > qpos - WINDOW)
        for kb in range(nkb):
            if kv_t:
                kt = k_tile.astype(BF16)
                if mode == 'sel':
                    onehot = (_iota((ns_pad, width), 0) == (kpos0 >> 6) + (_iota((ns_pad, width), 1) >> 6))
                    kt = jnp.concatenate([kt, onehot.astype(BF16)], axis=0)
                sc = _dot(qst_scr[kb], kt)
                if c_tile is not None:
                    sc = sc - LOG2E * jnp.concatenate([jnp.broadcast_to(c_tile[h:h + 1, :], (tq, width))
                                                       for h in range(n_heads_kb)], axis=0)
            else:
                kt = k_tile[:, kb * LANES:(kb + 1) * LANES].astype(BF16)
                if mode == 'sel':
                    onehot = (_iota((width, ns_pad), 1) == (kpos0 >> 6) + (_iota((width, ns_pad), 0) >> 6))
                    kt = jnp.concatenate([kt, onehot.astype(BF16)], axis=1)
                if c_tile is not None:
                    kt = jnp.concatenate([kt, c_tile], axis=1)
                sc = _dot_nt(qst_scr[kb], kt)
            if masked:
                sc = jnp.where(mask, sc, NEG)
            m_old = m_scr[kb]
            m_new = jnp.maximum(m_old, jnp.max(sc, axis=1, keepdims=True))
            alpha = jnp.exp2(m_old - m_new)
            p = jnp.exp2(sc - _lane_repeat(m_new, width))
            l_scr[kb] = alpha * l_scr[kb] + jnp.sum(p, axis=1, keepdims=True)
            if kv_t:
                pv = _dot_nt(p.astype(BF16), v_tile.astype(BF16))
            else:
                pv = _dot(p.astype(BF16), v_tile[:, kb * LANES:(kb + 1) * LANES].astype(BF16))
            acc_scr[kb] = _lane_repeat(alpha, pv.shape[1]) * acc_scr[kb] + pv
            m_scr[kb] = m_new

    def main_update(masked):
        if pps > 1:
            k_tile = jnp.concatenate([r[...] for r in k_refs], axis=1)
            v_tile = jnp.concatenate([r[...] for r in v_refs], axis=1)
        else:
            k_tile, v_tile = k_refs[0][...], v_refs[0][...]
        update(k_tile, v_tile, kbase + kj * tk, c_ref[...] if c_ref is not None else None, tk, masked)

    is_tail = (fl & 4) != 0
    need_mask = (fl & 8) != 0
    if has_tail:
        @pl.when(is_tail)
        def _():
            update(tk_ref[...], tv_ref[...], tail_pos0, ct_ref[...] if ct_ref is not None else None, PAGE, True)

    @pl.when(jnp.logical_and(jnp.logical_not(is_tail), need_mask))
    def _():
        main_update(True)

    @pl.when(jnp.logical_and(jnp.logical_not(is_tail), jnp.logical_not(need_mask)))
    def _():
        main_update(False)

    @pl.when((fl & 2) != 0)
    def _fin():
        fulls = [acc_scr[kb] / _lane_repeat(jnp.maximum(l_scr[kb], 1e-30), acc_scr.shape[2]) for kb in range(nkb)]
        if layout == 'fox_wide':
            o_ref[...] = _unstack_wide(fulls[0], FOX_HEADS, tq)
        else:
            outs = _unstack_heads(fulls, head_map, tq, qw // LANES)
            for c, blk in enumerate(outs):
                o_ref[:, c * LANES:(c + 1) * LANES] = blk


def _schedule(nq, tq, tk, q0, kbase, n_ktiles, mode, has_tail):
    qi_l, kj_l, fl_l = [], [], []
    for qi in range(nq):
        q_lo, q_hi = q0 + qi * tq, q0 + (qi + 1) * tq - 1
        entries = []
        for kj in range(n_ktiles):
            k_lo, k_hi = kbase + kj * tk, kbase + (kj + 1) * tk - 1
            if k_lo > q_hi:
                continue
            if mode == 'band' and k_hi <= q_lo - WINDOW:
                continue
            all_visible = k_hi <= q_lo and (mode != 'band' or k_lo > q_hi - WINDOW)
            entries.append((kj, 0 if all_visible else 8))
        if has_tail:
            entries.append((entries[-1][0] if entries else 0, 4 | 8))
        for n, (kj, f) in enumerate(entries):
            f |= (1 if n == 0 else 0) | (2 if n == len(entries) - 1 else 0)
            qi_l.append(qi); kj_l.append(kj); fl_l.append(f)
    return (np.asarray(qi_l, np.int32), np.asarray(kj_l, np.int32), np.asarray(fl_l, np.int32))


def _flash(q3, k3, v3, *, mode, layout, tq, tk, q0, kbase=0, page_table=None, pps=0, tail=None, tail_pos0=0,
           caug=None, crow=None, crow_tail=None, sel=None, ns_pad=0):
    B, Tq, qw = q3.shape
    nq = Tq // tq
    paged = page_table is not None
    has_tail = tail is not None
    if paged:
        kw = k3.shape[1]
        npages = page_table.shape[1]
        tk = pps * PAGE
        n_ktiles = npages // pps
    else:
        kw = k3.shape[-1]
        n_ktiles = k3.shape[1] // tk
    qi_t, kj_t, fl_t = _schedule(nq, tq, tk, q0, kbase, n_ktiles, mode, has_tail)
    nsteps = len(qi_t)
    prefetch = [jnp.asarray(qi_t), jnp.asarray(kj_t), jnp.asarray(fl_t)]
    if paged:
        prefetch.append(page_table.reshape(-1).astype(jnp.int32))
    n_pref = len(prefetch)
    nkb = 1 if layout in ('nsa', 'fox_wide') else FOX_HEADS // 2
    n_heads_kb = FOX_HEADS if layout == 'fox_wide' else (NSA_HEADS if layout == 'nsa' else 2)
    rows = n_heads_kb * tq
    cq = kw if layout == 'fox_wide' else LANES
    if mode == 'sel':
        cq += ns_pad
    if caug is not None:
        cq += LANES
    cv = kw if layout == 'fox_wide' else LANES

    in_specs = [pl.BlockSpec((None, tq, qw), lambda b, s, qi, kj, fl, *_: (b, qi[s], 0))]
    args = [q3]
    if paged:
        def page_map(i):
            return lambda b, s, qi, kj, fl, pt: (pt[b * npages + kj[s] * pps + i], 0, 0)
        for arr in (k3, v3):
            for i in range(pps):
                in_specs.append(pl.BlockSpec((None, kw, PAGE), page_map(i)))
                args.append(arr)
    else:
        for arr in (k3, v3):
            in_specs.append(pl.BlockSpec((None, tk, kw), lambda b, s, qi, kj, fl, *_: (b, kj[s], 0)))
            args.append(arr)
    if has_tail:
        for arr in tail:
            in_specs.append(pl.BlockSpec((None, kw, PAGE), lambda b, s, qi, kj, fl, *_: (b, 0, 0)))
            args.append(arr)
    if caug is not None:
        in_specs.append(pl.BlockSpec((None, tk, LANES), lambda b, s, qi, kj, fl, *_: (b, kj[s], 0)))
        args.append(caug)
    if crow is not None:
        in_specs.append(pl.BlockSpec((None, 8, tk), lambda b, s, qi, kj, fl, *_: (b, 0, kj[s])))
        args.append(crow)
        if has_tail:
            in_specs.append(pl.BlockSpec((None, 8, PAGE), lambda b, s, qi, kj, fl, *_: (b, 0, 0)))
            args.append(crow_tail)
    if mode == 'sel':
        in_specs.append(pl.BlockSpec((None, tq, NSA_GROUPS * ns_pad), lambda b, s, qi, kj, fl, *_: (b, qi[s], 0)))
        args.append(sel)
    cfg = dict(mode=mode, layout=layout, tq=tq, tk=tk, q0=q0, kbase=kbase, pps=pps if paged else 0,
               has_tail=has_tail, tail_pos0=tail_pos0, ns_pad=ns_pad, n_pref=n_pref, qw=qw, kv_t=paged,
               has_caug=caug is not None, has_crow=crow is not None)
    return pl.pallas_call(
        functools.partial(_flash_kernel, cfg=cfg),
        out_shape=jax.ShapeDtypeStruct((B, Tq, qw), F32),
        grid_spec=pltpu.PrefetchScalarGridSpec(
            num_scalar_prefetch=n_pref,
            grid=(B, nsteps),
            in_specs=in_specs,
            out_specs=pl.BlockSpec((None, tq, qw), lambda b, s, qi, kj, fl, *_: (b, qi[s], 0)),
            scratch_shapes=[pltpu.VMEM((nkb, rows, cq), BF16), pltpu.VMEM((nkb, rows, LANES), F32),
                            pltpu.VMEM((nkb, rows, LANES), F32), pltpu.VMEM((nkb, rows, cv), F32)]),
        compiler_params=_cp("parallel", "arbitrary"),
    )(*prefetch, *args)


def _nsa_combine_kernel(oc_ref, os_ref, ow_ref, sm_ref, e_ref, o_ref):
    gate = _sigmoid(sm_ref[...])
    acc = None
    for j, br in enumerate((oc_ref, os_ref, ow_ref)):
        gexp = _dot_r01(gate, e_ref[j])
        acc = gexp * br[...] if acc is None else acc + gexp * br[...]
    o_ref[...] = acc


def _nsa_combine(o_cmp, o_slc, o_win, small):
    n = o_cmp.shape[0]
    tm = min(512, n)
    src = np.zeros((3, LANES, MIX), np.float32)
    for h in range(NSA_HEADS):
        for j in range(3):
            src[j, 2 * M_HEADS + 3 * h + j, h * NSA_DH:(h + 1) * NSA_DH] = 1.0
    spec = pl.BlockSpec((tm, MIX), lambda i: (i, 0))
    return pl.pallas_call(
        _nsa_combine_kernel, out_shape=jax.ShapeDtypeStruct((n, MIX), F32), grid=(n // tm,),
        in_specs=[spec, spec, spec, pl.BlockSpec((tm, LANES), lambda i: (i, 0)),
                  pl.BlockSpec((3, LANES, MIX), lambda i: (0, 0, 0))],
        out_specs=spec, compiler_params=_cp("parallel"))(o_cmp, o_slc, o_win, small, jnp.asarray(src, BF16))


def _fox_cumsum_kernel(sm_ref, bias_ref, place_ref, lf_ref, caug_ref, carry, *, tt):
    @pl.when(pl.program_id(1) == 0)
    def _():
        carry[...] = jnp.zeros(carry.shape, F32)

    lf = _log_sigmoid(sm_ref[...] + bias_ref[...])
    lf_ref[...] = lf
    lower = (_iota((tt, tt), 1) <= _iota((tt, tt), 0)).astype(F32)
    c = _dot_l01(lower, lf) + carry[0:1, :]
    carry[...] = jnp.broadcast_to(c[tt - 1:tt, :], carry.shape)
    c = c * LOG2E
    c_hi = c.astype(BF16)
    r1 = c - c_hi.astype(F32)
    c_mid = r1.astype(BF16)
    c_lo = (r1 - c_mid.astype(F32)).astype(BF16)
    caug = _dot(c_hi, place_ref[0]) + _dot(c_mid, place_ref[1]) + _dot(c_lo, place_ref[2])
    caug_ref[...] = caug.astype(BF16)


def _fox_cumsum(small, bias_row, B, T):
    tt = min(256, T)
    row = lambda b, i: (b * (T // tt) + i, 0)
    place = np.zeros((3, LANES, LANES), np.float32)
    for h in range(FOX_HEADS):
        for j in range(3):
            place[j, h, 3 * h + j] = 1.0
    return pl.pallas_call(
        functools.partial(_fox_cumsum_kernel, tt=tt),
        out_shape=[jax.ShapeDtypeStruct((B * T, LANES), F32), jax.ShapeDtypeStruct((B * T, LANES), BF16)],
        grid=(B, T // tt),
        in_specs=[pl.BlockSpec((tt, LANES), row), pl.BlockSpec((1, LANES), lambda b, i: (0, 0)),
                  pl.BlockSpec((3, LANES, LANES), lambda b, i: (0, 0, 0))],
        out_specs=[pl.BlockSpec((tt, LANES), row), pl.BlockSpec((tt, LANES), row)],
        scratch_shapes=[pltpu.VMEM((8, LANES), F32)],
        compiler_params=_cp("parallel", "arbitrary"),
    )(small, bias_row, jnp.asarray(place, BF16))


def _fox_cumsum_paged_kernel(pt_ref, *refs, pps, n_main):
    lf_refs = refs[:pps]
    sm_ref, bias_ref, c_ref, lfn_ref, carry = refs[pps:]
    j = pl.program_id(1)

    @pl.when(j == 0)
    def _():
        carry[...] = jnp.zeros(carry.shape, F32)

    upper = (_iota((PAGE, PAGE), 0) <= _iota((PAGE, PAGE), 1)).astype(F32)

    @pl.when(j < n_main)
    def _():
        run = carry[:, 0:1]
        for i in range(pps):
            c = _dot_r01(lf_refs[i][...], upper) + run
            c_ref[:, i * PAGE:(i + 1) * PAGE] = c
            run = c[:, PAGE - 1:PAGE]
        carry[...] = jnp.broadcast_to(run, carry.shape)

    @pl.when(j == n_main)
    def _():
        lf = _log_sigmoid(sm_ref[...] + bias_ref[...])
        lfn_ref[...] = lf
        eye8 = (_iota((8, LANES), 0) == _iota((8, LANES), 1)).astype(F32)
        c_new = _dot_r01(_dot_nt_l01(eye8, lf), upper) + carry[:, 0:1]
        for i in range(pps):
            c_ref[:, i * PAGE:(i + 1) * PAGE] = c_new


def _fox_cumsum_paged(lf_pool_t, page_table, small_pad, bias_row, pps):
    B, npages = page_table.shape
    n_main = npages // pps

    def page_map(i):
        return lambda b, j, pt: (pt[b * npages + jnp.minimum(j, n_main - 1) * pps + i], 0, 0)

    return pl.pallas_call(
        functools.partial(_fox_cumsum_paged_kernel, pps=pps, n_main=n_main),
        out_shape=[jax.ShapeDtypeStruct((B, 8, (n_main + 1) * pps * PAGE), F32),
                   jax.ShapeDtypeStruct((B, PAGE, LANES), F32)],
        grid_spec=pltpu.PrefetchScalarGridSpec(
            num_scalar_prefetch=1, grid=(B, n_main + 1),
            in_specs=[pl.BlockSpec((None, 8, PAGE), page_map(i)) for i in range(pps)]
            + [pl.BlockSpec((None, PAGE, LANES), lambda b, j, pt: (b, 0, 0)),
               pl.BlockSpec((1, LANES), lambda b, j, pt: (0, 0))],
            out_specs=[pl.BlockSpec((None, 8, pps * PAGE), lambda b, j, pt: (b, 0, j)),
                       pl.BlockSpec((None, PAGE, LANES), lambda b, j, pt: (b, 0, 0))],
            scratch_shapes=[pltpu.VMEM((8, LANES), F32)]),
        compiler_params=_cp("parallel", "arbitrary"),
    )(page_table.reshape(-1).astype(jnp.int32), *([lf_pool_t] * pps), small_pad, bias_row)


def _rwkv_prep_kernel(rw_ref, prev_ref, sh_ref, mu_ref, w0_ref, w2_ref, a0_ref, a2_ref, g2_ref, kk_ref, ka_ref,
                      bd_ref, r_o, lw_o, k_o, v_o, kk_o, a_o, g_o, *, tt):
    i = pl.program_id(1)
    rw = rw_ref[...]
    first_prev = jnp.where(i == 0, sh_ref[...], prev_ref[7:8, :])
    prev = jnp.where(_iota(rw.shape, 0) == 0, first_prev, pltpu.roll(rw, 1, 0))
    m = rw + mu_ref[...] * (prev - rw)
    r, kr, vr = m[:, 0:MIX], m[:, MIX:2 * MIX], m[:, 2 * MIX:3 * MIX]
    lora = m[:, 3 * MIX:3 * MIX + LANES]
    gl = m[:, 3 * MIX + LANES:3 * MIX + 2 * LANES]
    w = -_softplus(-(w0_ref[...] + _dot(jnp.tanh(lora).astype(BF16), w2_ref[...]))) - 0.5
    a = _sigmoid(a0_ref[...] + _dot(lora.astype(BF16), a2_ref[...]))
    kk = kr * kk_ref[...]
    ss = _dot_r01(kk * kk, bd_ref[...])
    r_o[...] = r
    lw_o[...] = -jnp.exp(w)
    k_o[...] = kr * (1.0 + (a - 1.0) * ka_ref[...])
    v_o[...] = vr
    kk_o[...] = kk / jnp.maximum(jnp.sqrt(ss), 1e-12)
    a_o[...] = a
    g_o[...] = _dot(_sigmoid(gl).astype(BF16), g2_ref[...])


def _rwkv_prep(rw3, shift0, P):
    B, T, wdt = rw3.shape
    tt = min(256, T)
    nblk8 = tt // 8
    row3 = lambda b, i: (b, i, 0)
    vec = lambda n: pl.BlockSpec((1, n), lambda b, i: (0, 0))
    outs = pl.pallas_call(
        functools.partial(_rwkv_prep_kernel, tt=tt),
        out_shape=[jax.ShapeDtypeStruct((B, T, MIX), F32)] * 7,
        grid=(B, T // tt),
        in_specs=[pl.BlockSpec((None, tt, wdt), row3),
                  pl.BlockSpec((None, 8, wdt), lambda b, i: (b, jnp.maximum(i * nblk8 - 1, 0), 0)),
                  pl.BlockSpec((None, 1, wdt), lambda b, i: (b, 0, 0)),
                  vec(wdt), vec(MIX), pl.BlockSpec((LANES, MIX), lambda b, i: (0, 0)),
                  vec(MIX), pl.BlockSpec((LANES, MIX), lambda b, i: (0, 0)),
                  pl.BlockSpec((LANES, MIX), lambda b, i: (0, 0)), vec(MIX), vec(MIX),
                  pl.BlockSpec((MIX, MIX), lambda b, i: (0, 0))],
        out_specs=[pl.BlockSpec((None, tt, MIX), row3)] * 7,
        compiler_params=_cp("parallel", "arbitrary"),
    )(rw3, rw3, shift0[:, None, :], P['mu'], P['w0'], P['w2'], P['a0'], P['a2'], P['g2'], P['k_k'], P['k_a'], P['bd64'])
    return outs


def _rwkv_scan_kernel(r_ref, lw_ref, k_ref, v_ref, kk_ref, a_ref, g_ref, s0_ref, rk_ref, lng_ref, lnb_ref,
                      y_ref, s1_ref, s_scr, *, L):
    ci = pl.program_id(1)

    @pl.when(ci == 0)
    def _():
        s_scr[...] = s0_ref[...]

    r_i = _iota((L, L), 0)
    c_i = _iota((L, L), 1)
    lower = (c_i <= r_i).astype(F32)
    strict = c_i < r_i
    incl = c_i <= r_i
    eye_l = (c_i == r_i).astype(F32)
    lane = _iota((L, LANES), 1)
    half_masks = (lane < 64, lane >= 64)
    rr = _iota((LANES, LANES), 0)
    cc = _iota((LANES, LANES), 1)
    bdmask = (rr >= 64) == (cc >= 64)
    eye128 = rr == cc
    bd_f = bdmask.astype(F32)
    pairs = range(RWKV_HEADS // 2)
    sls = [slice(pr * LANES, (pr + 1) * LANES) for pr in pairs]
    P = []
    for sl in sls:
        lw = lw_ref[:, sl]
        cl = _dot_l01(lower, lw)
        w_in = jnp.exp(cl)
        w_out = jnp.exp(-cl)
        kk = kk_ref[:, sl]
        P.append(dict(w_in=w_in, kap=kk * jnp.exp(cl - lw), bet=kk * a_ref[:, sl] * w_out,
                      kt=k_ref[:, sl] * w_out, rt=r_ref[:, sl] * w_in, v=v_ref[:, sl]))
    for pr, d in zip(pairs, P):
        d['sbd'] = s_scr[pr]
        d['rhs'] = _bdot(d['kap'], d['sbd'])
        d['ys'] = _bdot(d['rt'], d['sbd'])
    H = []
    for d in P:
        for hm in half_masks:
            kap_h = jnp.where(hm, d['kap'], 0.0)
            rt_h = jnp.where(hm, d['rt'], 0.0)
            amat = jnp.where(strict, _bdot_nt(kap_h, d['bet']), 0.0)
            H.append(dict(d=d, hm=hm, pw=amat, tinv=eye_l - amat,
                          cmat=jnp.where(strict, _bdot_nt(kap_h, d['kt']), 0.0),
                          rb=jnp.where(incl, _bdot_nt(rt_h, d['bet']), 0.0),
                          rk=jnp.where(incl, _bdot_nt(rt_h, d['kt']), 0.0)))
    def same_block(shift):
        return (r_i >> shift) == (c_i >> shift)

    for h in H:
        h['amat'] = h['pw']
        h['pw'] = jnp.where(same_block(3), h['amat'], 0.0)
        h['tinv'] = eye_l - h['pw']
    for _ in range(2):
        for h in H:
            h['pw'] = _dot3(h['pw'], h['pw'])
        for h in H:
            h['tinv'] = h['tinv'] + _dot3(h['tinv'], h['pw'])
    shift = 3
    while (1 << shift) < L:
        off_diag = jnp.logical_and(same_block(shift + 1), jnp.logical_not(same_block(shift)))
        for h in H:
            h['x'] = _dot3(jnp.where(off_diag, h['amat'], 0.0), h['tinv'])
        for h in H:
            h['tinv'] = h['tinv'] - _dot3(h['tinv'], h['x'])
        shift += 1
    for h in H:
        d, hm = h['d'], h['hm']
        h['v_h'] = jnp.where(hm, d['v'], 0.0)
        h['rhs_h'] = jnp.where(hm, d['rhs'], 0.0) + _bdot(h['cmat'], h['v_h'])
    for h in H:
        h['u_h'] = -_dot3(h['tinv'], h['rhs_h'])
    for h in H:
        h['y_h'] = _bdot(h['rb'], h['u_h']) + _bdot(h['rk'], h['v_h'])
    for pr, (d, sl) in enumerate(zip(P, sls)):
        h0, h1 = H[2 * pr], H[2 * pr + 1]
        u = h0['u_h'] + h1['u_h']
        y = d['ys'] + h0['y_h'] + h1['y_h']
        upd = jnp.where(bdmask, _bdot_tn(d['bet'], u) + _bdot_tn(d['kt'], d['v']), 0.0)
        w_end = jnp.broadcast_to(d['w_in'][L - 1:L, :], (LANES, LANES))
        w_end_col = jnp.sum(jnp.where(eye128, w_end, 0.0), axis=1, keepdims=True)
        s_scr[pr] = (d['sbd'] + upd) * w_end_col
        d['y'] = y
    for d, sl in zip(P, sls):
        y = d['y']
        mean = _dot_r01(y, bd_f) * (1.0 / RWKV_N)
        yc = y - mean
        var = _dot_r01(yc * yc, bd_f) * (1.0 / RWKV_N)
        yn = yc * lax.rsqrt(var + 64e-5) * lng_ref[:, sl] + lnb_ref[:, sl]
        bonus = _dot_r01(r_ref[:, sl] * k_ref[:, sl] * rk_ref[:, sl], bd_f)
        y_ref[:, sl] = (yn + bonus * d['v']) * g_ref[:, sl]

    @pl.when(ci == pl.num_programs(1) - 1)
    def _():
        s1_ref[...] = s_scr[...]


def _rwkv_scan(r, lw, k, v, kk, a, g, s0, P, B, T):
    L = RWKV_CHUNK
    tp = -(-T // L) * L
    if tp != T:
        pad = lambda t: jnp.pad(t, ((0, 0), (0, tp - T), (0, 0)))
        r, lw, k, v, kk, a, g = (pad(t) for t in (r, lw, k, v, kk, a, g))
    nch = tp // L
    st = jnp.swapaxes(s0.astype(F32), -1, -2).reshape(B, RWKV_HEADS // 2, 2, RWKV_N, RWKV_N)
    z = jnp.zeros_like(st[:, :, 0])
    sbd0 = jnp.concatenate([jnp.concatenate([st[:, :, 0], z], -1), jnp.concatenate([z, st[:, :, 1]], -1)], -2)
    row3 = lambda b, c: (b, c, 0)
    st_spec = pl.BlockSpec((None, RWKV_HEADS // 2, LANES, LANES), lambda b, c: (b, 0, 0, 0))
    vec = pl.BlockSpec((1, MIX), lambda b, c: (0, 0))
    y, sbd1 = pl.pallas_call(
        functools.partial(_rwkv_scan_kernel, L=L),
        out_shape=[jax.ShapeDtypeStruct((B, tp, MIX), F32),
                   jax.ShapeDtypeStruct((B, RWKV_HEADS // 2, LANES, LANES), F32)],
        grid=(B, nch),
        in_specs=[pl.BlockSpec((None, L, MIX), row3)] * 7 + [st_spec, vec, vec, vec],
        out_specs=[pl.BlockSpec((None, L, MIX), row3), st_spec],
        scratch_shapes=[pltpu.VMEM((RWKV_HEADS // 2, LANES, LANES), F32)],
        compiler_params=_cp("parallel", "arbitrary"),
    )(r, lw, k, v, kk, a, g, sbd0, P['r_k'], P['ln_g'], P['ln_b'])
    s_e = sbd1[:, :, :RWKV_N, :RWKV_N]
    s_o = sbd1[:, :, RWKV_N:, RWKV_N:]
    s1 = jnp.swapaxes(jnp.stack([s_e, s_o], 2).reshape(B, RWKV_HEADS, RWKV_N, RWKV_N), -1, -2)
    return y[:, :T], s1


def _mem_kernel(x_ref, km_ref, vm_ref, wq_ref, wo_ref, g_ref, b_ref, o_ref):
    x = x_ref[...]
    q = _dot(x.astype(BF16), wq_ref[...])
    heads = []
    for h in range(MEM_HEADS):
        sl = slice(h * MEM_DH, (h + 1) * MEM_DH)
        s = _dot_nt((q[:, sl] * (MEM_DH ** -0.5)).astype(BF16), km_ref[:, sl].astype(BF16))
        e = jnp.exp(s - jnp.max(s, axis=1, keepdims=True))
        p = e / jnp.sum(e, axis=1, keepdims=True)
        heads.append(_dot(p.astype(BF16), vm_ref[:, sl].astype(BF16)))
    o = jnp.concatenate(heads, axis=1)
    y = _dot(o.astype(BF16), wo_ref[...])
    o_ref[...] = _layer_norm_rows(ALPHA * x + y, g_ref[...], b_ref[...])


def _mem_layer(x3, km, vm, wq, wo, g, b):
    B, T, d = x3.shape
    tq = min(512, T)
    mt = km.shape[1]
    full = lambda shape: pl.BlockSpec(shape, lambda bb, i: (0,) * len(shape))
    return pl.pallas_call(
        _mem_kernel, out_shape=jax.ShapeDtypeStruct((B, T, d), F32), grid=(B, T // tq),
        in_specs=[pl.BlockSpec((None, tq, d), lambda bb, i: (bb, i, 0)),
                  pl.BlockSpec((None, mt, d), lambda bb, i: (bb, 0, 0)),
                  pl.BlockSpec((None, mt, d), lambda bb, i: (bb, 0, 0)),
                  full((d, d)), full((d, d)), full((1, d)), full((1, d))],
        out_specs=pl.BlockSpec((None, tq, d), lambda bb, i: (bb, i, 0)),
        compiler_params=_cp("parallel", "arbitrary"),
    )(x3, km, vm, wq, wo, g.reshape(1, d), b.reshape(1, d))


def _moe_route_kernel(x_ref, wr_ref, br_ref, gw_ref, cnt_ref):
    x = x_ref[...]
    logit = _dot(x, wr_ref[...], HI) + br_ref[...]
    lane_f = _iota(logit.shape, 1).astype(F32)
    is_g = lane_f < N_GROUPS
    gl = jnp.where(is_g, logit, -jnp.inf)
    gmax = jnp.max(gl, axis=1, keepdims=True)
    grp = jnp.min(jnp.where(gl == gmax, lane_f, 1e9), axis=1, keepdims=True)
    g_w = 1.0 / jnp.sum(jnp.where(is_g, jnp.exp(gl - gmax), 0.0), axis=1, keepdims=True)
    lo = N_GROUPS + EXP_PER_GROUP * grp
    in_grp = (lane_f >= lo) & (lane_f < lo + EXP_PER_GROUP)
    el = jnp.where(in_grp, logit, -jnp.inf)
    emax = jnp.max(el, axis=1, keepdims=True)
    ee = jnp.where(in_grp, jnp.exp(el - emax), 0.0)
    p = jnp.where(in_grp, ee / jnp.sum(ee, axis=1, keepdims=True), -1.0)
    p1 = jnp.max(p, axis=1, keepdims=True)
    i1 = jnp.min(jnp.where(p == p1, lane_f, 1e9), axis=1, keepdims=True)
    pr = jnp.where(lane_f == i1, -1.0, p)
    p2 = jnp.max(pr, axis=1, keepdims=True)
    i2 = jnp.min(jnp.where(pr == p2, lane_f, 1e9), axis=1, keepdims=True)
    den = p1 + p2
    member = jnp.where(lane_f == grp, 1.0, 0.0)
    gw_ref[...] = (member + jnp.where(lane_f == i1, g_w * p1 / den, 0.0)
                   + jnp.where(lane_f == i2, g_w * p2 / den, 0.0))
    cnt_ref[...] = jnp.broadcast_to(jnp.sum(member, axis=0, keepdims=True), cnt_ref.shape)


def _moe_group_kernel(cnt_ref, x_ref, gw_ref, w1_ref, w3_ref, w2_ref, g_ref, b_ref, o_ref,
                      xb_scr, gt_scr, rk_scr, xs_scr, gws_scr, ys_scr, acc_scr, *, tm, ch):
    i = pl.program_id(0)
    e = pl.program_id(1) * MOE_EXPERTS_PER_STEP
    grp = e // EXP_PER_GROUP
    cnt = cnt_ref[i * N_GROUPS + grp]
    slots = [slice(c * ch, (c + 1) * ch) for c in range(tm // ch)]

    @pl.when(e == 0)
    def _():
        xb_scr[...] = x_ref[...].astype(BF16)
        acc_scr[...] = jnp.zeros(acc_scr.shape, F32)
        member = gw_ref[...].T[0:8, :]
        gt_scr[...] = member
        before = (_iota((tm, tm), 0) < _iota((tm, tm), 1)).astype(BF16)
        rk_scr[...] = _dot(member.astype(BF16), before)

    def one_hot(c):
        member = gt_scr[pl.ds(grp, 1), :]
        rank = rk_scr[pl.ds(grp, 1), :]
        want = (c * ch + _iota((ch, 1), 0)).astype(F32)
        return jnp.where((member > 0.5) & (rank == want), 1.0, 0.0).astype(BF16)

    @pl.when(e % EXP_PER_GROUP == 0)
    def _():
        for c, rs in enumerate(slots):
            @pl.when(c * ch < cnt)
            def _():
                p = one_hot(c)
                xs_scr[rs, :] = _dot(p, xb_scr[...]).astype(BF16)
                gws_scr[rs, :] = _dot_l01(p, gw_ref[...])
                ys_scr[rs, :] = jnp.zeros((ch, ys_scr.shape[1]), F32)

    for c, rs in enumerate(slots):
        @pl.when(c * ch < cnt)
        def _():
            xs = xs_scr[rs, :]
            gws = gws_scr[rs, :]
            y = None
            for j in range(MOE_EXPERTS_PER_STEP):
                h1 = _dot(xs, w1_ref[j])
                h3 = _dot(xs, w3_ref[j])
                hid = h1 * _sigmoid(h1) * h3
                gcol = jnp.sum(jnp.where(_iota(gws.shape, 1) == e + j + N_GROUPS, gws, 0.0), axis=1, keepdims=True)
                yj = gcol * _dot(hid.astype(BF16), w2_ref[j])
                y = yj if y is None else y + yj
            ys_scr[rs, :] += y

    @pl.when(e % EXP_PER_GROUP == EXP_PER_GROUP - MOE_EXPERTS_PER_STEP)
    def _():
        for c, rs in enumerate(slots):
            @pl.when(c * ch < cnt)
            def _():
                p = one_hot(c)
                hi, lo = _split_bf16(ys_scr[rs, :])
                acc_scr[...] += _dot_tn(p, hi) + _dot_tn(p, lo)

    @pl.when(e == N_EXPERTS - MOE_EXPERTS_PER_STEP)
    def _():
        o_ref[...] = _layer_norm_rows(ALPHA * x_ref[...] + acc_scr[...], g_ref[...], b_ref[...])


def _moe_layer_grouped(x, M, g, b):
    n, d = x.shape
    tm = min(1024, n)
    ch = min(256, tm)
    nt = n // tm
    gw, cnt = pl.pallas_call(
        _moe_route_kernel,
        out_shape=[jax.ShapeDtypeStruct((n, LANES), F32), jax.ShapeDtypeStruct((nt, 8, LANES), F32)],
        grid=(nt,),
        in_specs=[pl.BlockSpec((tm, d), lambda i: (i, 0)), pl.BlockSpec((d, LANES), lambda i: (0, 0)),
                  pl.BlockSpec((1, LANES), lambda i: (0, 0))],
        out_specs=[pl.BlockSpec((tm, LANES), lambda i: (i, 0)), pl.BlockSpec((None, 8, LANES), lambda i: (i, 0, 0))],
        compiler_params=_cp("parallel"),
    )(x, M['wr'], M['br'])
    counts = cnt[:, 0, :N_GROUPS].astype(jnp.int32).reshape(-1)
    full2 = lambda shape: pl.BlockSpec(shape, lambda i, e, c: (0, 0))
    return pl.pallas_call(
        functools.partial(_moe_group_kernel, tm=tm, ch=ch),
        out_shape=jax.ShapeDtypeStruct((n, d), F32),
        grid_spec=pltpu.PrefetchScalarGridSpec(
            num_scalar_prefetch=1, grid=(nt, N_EXPERTS // MOE_EXPERTS_PER_STEP),
            in_specs=[pl.BlockSpec((tm, d), lambda i, e, c: (i, 0)), pl.BlockSpec((tm, LANES), lambda i, e, c: (i, 0)),
                      pl.BlockSpec((MOE_EXPERTS_PER_STEP, d, D_EXPERT), lambda i, e, c: (e, 0, 0)),
                      pl.BlockSpec((MOE_EXPERTS_PER_STEP, d, D_EXPERT), lambda i, e, c: (e, 0, 0)),
                      pl.BlockSpec((MOE_EXPERTS_PER_STEP, D_EXPERT, d), lambda i, e, c: (e, 0, 0)),
                      full2((1, d)), full2((1, d))],
            out_specs=pl.BlockSpec((tm, d), lambda i, e, c: (i, 0)),
            scratch_shapes=[pltpu.VMEM((tm, d), BF16), pltpu.VMEM((8, tm), F32), pltpu.VMEM((8, tm), F32),
                            pltpu.VMEM((tm, d), BF16), pltpu.VMEM((tm, LANES), F32), pltpu.VMEM((tm, d), F32),
                            pltpu.VMEM((tm, d), F32)]),
        compiler_params=_cp("parallel", "arbitrary"),
    )(counts, x, gw, M['w1'], M['w3'], M['w2'], g.reshape(1, d), b.reshape(1, d))


EVEN_SEGS = ([(MIX, False)] * 4 + [(MIX, True), (LANES, False), (LANES, False), (LANES, True), (LANES, False),
                                   (LANES, True), (LANES, False), (LANES, False)])
ODD_SEGS = [(MIX, False)] * 3 + [(RWKV_SHIFT, False), (LANES, False)]
RW_PERM = np.concatenate([np.arange(0, 512), np.arange(576, 1088), np.arange(1088, 1600), np.arange(512, 576),
                          np.arange(1600, 1664), np.arange(1664, 1792)])
RW_INV = np.argsort(RW_PERM)


def _prep_even(w_in, w_out, b_i, b_f):
    zeros = jnp.zeros((D_MODEL, LANES - 2 * M_HEADS - 3 * NSA_HEADS), F32)
    w = jnp.concatenate([w_in[:, 0:1536], w_in[:, 1544:2056], w_in[:, 2056:3336], w_in[:, 1536:1544],
                         w_in[:, 3336:3360], zeros], axis=1).astype(BF16)
    bias = jnp.concatenate([b_i, b_f, jnp.zeros((LANES - 2 * M_HEADS,), F32)]).reshape(1, LANES)
    return dict(w_in=w, w_out_a=w_out[:MIX].astype(BF16), w_out_b=w_out[MIX:].astype(BF16), gate_bias=bias)


def _prep_odd(w_in, w_out, fox_b_f, mu, w0, w2, a0, a2, g2, k_k, k_a, r_k, ln_g, ln_b):
    rw_cols = w_in[:, 1544:3336][:, RW_PERM]
    zeros = jnp.zeros((D_MODEL, LANES - FOX_HEADS), F32)
    w = jnp.concatenate([w_in[:, 0:1536], rw_cols, w_in[:, 1536:1544], zeros], axis=1).astype(BF16)
    bias = jnp.concatenate([fox_b_f, jnp.zeros((LANES - FOX_HEADS,), F32)]).reshape(1, LANES)
    z64 = jnp.zeros((64, MIX), F32)
    hd = np.arange(MIX) // RWKV_N
    bd64 = jnp.asarray((hd[:, None] == hd[None, :]).astype(np.float32), BF16)
    P = dict(mu=mu[RW_PERM].reshape(1, -1), w0=w0.reshape(1, -1), a0=a0.reshape(1, -1),
             w2=jnp.concatenate([w2, z64], 0).astype(BF16), a2=jnp.concatenate([z64, a2], 0).astype(BF16),
             g2=g2.astype(BF16), k_k=k_k.reshape(1, -1), k_a=k_a.reshape(1, -1), r_k=r_k.reshape(1, -1),
             ln_g=ln_g.reshape(1, -1), ln_b=ln_b.reshape(1, -1), bd64=bd64)
    return dict(w_in=w, w_out_a=w_out[:MIX].astype(BF16), w_out_b=w_out[MIX:].astype(BF16), fox_bias=bias, P=P)


def _prep_moe(w_group, b_group, w_expert, b_expert, w1, w3, w2):
    padw = LANES - N_GROUPS - N_EXPERTS
    wr = jnp.concatenate([w_group, w_expert, jnp.zeros((D_MODEL, padw), F32)], axis=1)
    br = jnp.concatenate([b_group, b_expert, jnp.zeros((padw,), F32)]).reshape(1, LANES)
    return dict(wr=wr, br=br, w1=w1.astype(BF16), w3=w3.astype(BF16), w2=w2.astype(BF16))


def _even_odd_blocks(kcmp):
    return jnp.concatenate([kcmp[:, 0::2], kcmp[:, 1::2]], axis=1)


def _pages_t(cache):
    pool, page, h, dh = cache.shape
    return jnp.transpose(cache, (0, 2, 3, 1)).reshape(pool, h * dh, page)


def _tail_t(rows3):
    B, T, _ = rows3.shape
    return jnp.swapaxes(jnp.pad(rows3, ((0, 0), (0, PAGE - T), (0, 0))), 1, 2)


def _even_mixer(x3, q0, W, nsa_w, norm_g, c0, n0, m0, past, ln_g, ln_b):
    B, T, D = x3.shape
    N = B * T
    x2 = x3.reshape(N, D)
    pos = q0 + jnp.arange(T)
    cos, sin = _rope_tables(pos)
    rope = (jnp.tile(cos, (B, 1)), jnp.tile(sin, (B, 1)))
    sh = (B, T, NSA_GROUPS, NSA_DH)
    if past is None and T % 256 == 0:
        outs, outs_t = _proj_split(x2, W['w_in'], EVEN_SEGS, rope, t_segs=(5, 6, 7, 8), bt=(B, T))
        new_rows = [jnp.transpose(t.reshape(B, NSA_GROUPS, NSA_DH, T), (0, 3, 1, 2)) for t in outs_t]
    else:
        outs = _proj_split(x2, W['w_in'], EVEN_SEGS, rope)
        new_rows = [t.reshape(sh) for t in outs[5:9]]
    mq, mk, mv, mo, nq, kc, vc, ks, vs, kw, vw, small = outs
    h, c1, n1, m1 = _mlstm(mq, mk, mv, mo, small, W['gate_bias'], norm_g, c0, n0, m0, B, T)
    pe_k, pe_v, ck1, ck2, cv1, cv2 = nsa_w
    blk_w = CMP_BLOCK * LANES
    q3 = nq.reshape(B, T, MIX)
    if past is None:
        L = T
        nc = L // CMP_BLOCK
        kcmp = _compress(kc.reshape(B * nc, blk_w), pe_k, ck1, ck2).reshape(B, nc, LANES)
        vcmp = _compress(vc.reshape(B * nc, blk_w), pe_v, cv1, cv2).reshape(B, nc, LANES)
    else:
        pt = past['page_table']
        npages = pt.shape[1]
        L = npages * PAGE + T
        nc = L // CMP_BLOCK
        per_page = PAGE // CMP_BLOCK
        pool = past['kc'].shape[0]
        kcp = _compress_pool(_pages_t(past['kc']), pe_k, ck1, ck2).reshape(pool, per_page * LANES)
        vcp = _compress_pool(_pages_t(past['vc']), pe_v, cv1, cv2).reshape(pool, per_page * LANES)
        kcmp = jnp.take(kcp, pt, axis=0).reshape(B, npages * per_page, LANES)[:, :nc]
        vcmp = jnp.take(vcp, pt, axis=0).reshape(B, npages * per_page, LANES)[:, :nc]
    cpos = jnp.arange(nc) * CMP_BLOCK + CMP_BLOCK - 1
    ccos, csin = _rope_tables(cpos)
    kcmp = _rope_rows(kcmp.reshape(B * nc, LANES), jnp.tile(ccos, (B, 1)), jnp.tile(csin, (B, 1))).reshape(B, nc, LANES)
    ns = -(-L // SEL_BLOCK)
    ns_pad = -(-ns // LANES) * LANES
    cend = jnp.concatenate([cpos[0::2], cpos[1::2]]).astype(jnp.int32).reshape(1, nc)
    o_cmp, sel = _cmp_sel(q3, _even_odd_blocks(kcmp), _even_odd_blocks(vcmp), cend, q0, ns, ns_pad)
    ks3, vs3 = ks.reshape(B, T, LANES), vs.reshape(B, T, LANES)
    kw3, vw3 = kw.reshape(B, T, LANES), vw.reshape(B, T, LANES)
    if past is None:
        o_slc = _flash(q3, ks3, vs3, mode='sel', layout='nsa', tq=min(TQ_SLC, T), tk=min(TK_SLC, T), q0=0, sel=sel,
                       ns_pad=ns_pad)
        o_win = _flash(q3, kw3, vw3, mode='band', layout='nsa', tq=min(TQ_WIN, T), tk=min(TK_WIN, T), q0=0)
        kw_new, vw_new = kw3[:, -min(WINDOW, T):], vw3[:, -min(WINDOW, T):]
    else:
        o_slc = _flash(q3, _pages_t(past['ks']), _pages_t(past['vs']), mode='sel', layout='nsa', tq=T, tk=0, q0=q0,
                       page_table=pt, pps=_largest_divisor(npages, (16, 8, 4, 2, 1)), tail=(_tail_t(ks3), _tail_t(vs3)),
                       tail_pos0=q0, sel=sel, ns_pad=ns_pad)
        kw_ext = jnp.concatenate([past['kw_buf'], kw3], axis=1)
        vw_ext = jnp.concatenate([past['vw_buf'], vw3], axis=1)
        wb = past['kw_buf'].shape[1]
        o_win = _flash(q3, kw_ext, vw_ext, mode='band', layout='nsa', tq=T, tk=wb + T, q0=q0, kbase=q0 - wb)
        keep = min(WINDOW, wb + T)
        kw_new, vw_new = kw_ext[:, -keep:], vw_ext[:, -keep:]
    o = _nsa_combine(o_cmp.reshape(N, MIX), o_slc.reshape(N, MIX), o_win.reshape(N, MIX), small)
    x_new = _mm_res_ln([h, o], [W['w_out_a'], W['w_out_b']], x2, ln_g, ln_b).reshape(B, T, D)
    wsh = (B, -1, NSA_GROUPS, NSA_DH)
    state = (c1, n1, m1, *new_rows, kw_new.reshape(wsh), vw_new.reshape(wsh))
    return x_new, state


def _odd_mixer(x3, q0, W, s0, shift0, past, ln_g, ln_b):
    B, T, D = x3.shape
    N = B * T
    x2 = x3.reshape(N, D)
    hs = (B, T, FOX_HEADS, FOX_DH)
    if past is None and T % 256 == 0:
        outs, outs_t = _proj_split(x2, W['w_in'], ODD_SEGS, t_segs=(1, 2), bt=(B, T))
        new_kv = [jnp.transpose(t.reshape(B, FOX_HEADS, FOX_DH, T), (0, 3, 1, 2)) for t in outs_t]
    else:
        outs = _proj_split(x2, W['w_in'], ODD_SEGS)
        new_kv = [t.reshape(hs) for t in outs[1:3]]
    fq, fk, fv, rw, small = outs
    q3, k3, v3 = (t.reshape(B, T, MIX) for t in (fq, fk, fv))
    if past is None:
        lf_full, caug = _fox_cumsum(small, W['fox_bias'], B, T)
        lf = lf_full[:, :FOX_HEADS].reshape(B, T, FOX_HEADS)
        o_c = _flash(q3, k3, v3, mode='causal', layout='fox', tq=min(TQ_FOX, T), tk=min(TK_FOX, T), q0=0,
                     caug=caug.reshape(B, T, LANES))
    else:
        pt = past['page_table']
        npages = pt.shape[1]
        pps = _largest_divisor(npages, (16, 8, 4, 2, 1))
        padrows = ((0, 0), (0, PAGE - T), (0, 0))
        lf_pool_t = jnp.swapaxes(past['lf'], 1, 2)
        c_all, lf_new = _fox_cumsum_paged(lf_pool_t, pt, jnp.pad(small.reshape(B, T, LANES), padrows), W['fox_bias'], pps)
        lf = lf_new[:, :T, :FOX_HEADS]
        n_main = npages * PAGE
        o_c = _flash(q3, _pages_t(past['k']), _pages_t(past['v']), mode='causal', layout='fox_wide', tq=T, tk=0, q0=q0,
                     page_table=pt, pps=pps, tail=(_tail_t(k3), _tail_t(v3)), tail_pos0=q0,
                     crow=c_all, crow_tail=c_all[:, :, n_main:n_main + PAGE])
    rw3 = rw.reshape(B, T, RWKV_SHIFT)
    r, lw, k, v, kk, a, g = _rwkv_prep(rw3, shift0[:, RW_PERM], W['P'])
    y, s1 = _rwkv_scan(r, lw, k, v, kk, a, g, s0, W['P'], B, T)
    x_new = _mm_res_ln([o_c.reshape(N, MIX), y.reshape(N, MIX)], [W['w_out_a'], W['w_out_b']], x2, ln_g, ln_b)
    state = (*new_kv, lf, s1, rw3[:, -1][:, RW_INV])
    return x_new.reshape(B, T, D), state


def kernel(x_prompt, x_sample, state_mlstm_C, state_mlstm_n, state_mlstm_m, cache_nsa_kc, cache_nsa_vc, cache_nsa_ks, cache_nsa_vs, cache_nsa_kw, cache_nsa_vw, cache_fox_k, cache_fox_v, cache_fox_logf, state_rwkv_S, state_rwkv_shift, cache_mem_k, cache_mem_v, page_table, mem_prompt, even_w_in, even_w_out, mlstm_b_i, mlstm_b_f, mlstm_norm_g, nsa_pe_k, nsa_pe_v, nsa_cmp_k_w1, nsa_cmp_k_w2, nsa_cmp_v_w1, nsa_cmp_v_w2, odd_w_in, odd_w_out, fox_b_f, rwkv_mu, rwkv_w0, rwkv_w2, rwkv_a0, rwkv_a2, rwkv_g2, rwkv_k_k, rwkv_k_a, rwkv_r_k, rwkv_ln_g, rwkv_ln_b, mem_wq, mem_wk, mem_wv, mem_wo, moe_w_group, moe_b_group, moe_w_expert, moe_b_expert, moe_w1, moe_w3, moe_w2, ln_g, ln_b):
    bp, tp, d = x_prompt.shape
    bs, ts, _ = x_sample.shape
    depth = ln_g.shape[0]
    past = page_table.shape[1] * cache_nsa_kc.shape[2]
    xp, xs = x_prompt, x_sample
    acc = {}

    def push(prefix, names, vals):
        for nm, val in zip(names, vals):
            acc.setdefault(prefix + nm, []).append(val)

    even_names = ('mlstm_C', 'mlstm_n', 'mlstm_m', 'nsa_kc', 'nsa_vc', 'nsa_ks', 'nsa_vs', 'nsa_kw', 'nsa_vw')
    odd_names = ('fox_k', 'fox_v', 'fox_logf', 'rwkv_S', 'rwkv_shift')
    for layer in range(depth):
        e = layer // 2
        if layer % 2 == 0:
            W = _prep_even(even_w_in[e], even_w_out[e], mlstm_b_i[e], mlstm_b_f[e])
            nsa_w = (nsa_pe_k[e], nsa_pe_v[e], nsa_cmp_k_w1[e], nsa_cmp_k_w2[e], nsa_cmp_v_w1[e], nsa_cmp_v_w2[e])
            xp, sp = _even_mixer(xp, 0, W, nsa_w, mlstm_norm_g[e], jnp.zeros((bp, M_HEADS, M_DV, M_DK), F32),
                                 jnp.zeros((bp, M_HEADS, M_DK), F32), jnp.zeros((bp, M_HEADS), F32), None,
                                 ln_g[layer, 0], ln_b[layer, 0])
            wbuf = lambda c: c[e].reshape(bs, -1, LANES)
            past_d = dict(kc=cache_nsa_kc[e], vc=cache_nsa_vc[e], ks=cache_nsa_ks[e], vs=cache_nsa_vs[e],
                          page_table=page_table, kw_buf=wbuf(cache_nsa_kw), vw_buf=wbuf(cache_nsa_vw))
            xs, ss = _even_mixer(xs, past, W, nsa_w, mlstm_norm_g[e], state_mlstm_C[e], state_mlstm_n[e],
                                 state_mlstm_m[e], past_d, ln_g[layer, 0], ln_b[layer, 0])
            names = even_names
        else:
            W = _prep_odd(odd_w_in[e], odd_w_out[e], fox_b_f[e], rwkv_mu[e], rwkv_w0[e], rwkv_w2[e], rwkv_a0[e],
                          rwkv_a2[e], rwkv_g2[e], rwkv_k_k[e], rwkv_k_a[e], rwkv_r_k[e], rwkv_ln_g[e], rwkv_ln_b[e])
            xp, sp = _odd_mixer(xp, 0, W, jnp.zeros((bp, RWKV_HEADS, RWKV_N, RWKV_N), F32),
                                jnp.zeros((bp, RWKV_SHIFT), F32), None, ln_g[layer, 0], ln_b[layer, 0])
            past_d = dict(k=cache_fox_k[e], v=cache_fox_v[e], lf=cache_fox_logf[e], page_table=page_table)
            xs, ss = _odd_mixer(xs, past, W, state_rwkv_S[e], state_rwkv_shift[e], past_d,
                                ln_g[layer, 0], ln_b[layer, 0])
            names = odd_names
        push('p_', names, sp)
        push('s_', names, ss)
        wkv = jnp.concatenate([mem_wk[layer], mem_wv[layer]], axis=1).astype(BF16)
        mt = mem_prompt.shape[1]
        kmp, vmp = _proj_split(mem_prompt.reshape(bp * mt, d), wkv, [(d, False), (d, False)])
        push('p_', ('mem_k', 'mem_v'), (kmp.reshape(bp, mt, MEM_HEADS, MEM_DH), vmp.reshape(bp, mt, MEM_HEADS, MEM_DH)))
        wq, wo = mem_wq[layer].astype(BF16), mem_wo[layer].astype(BF16)
        xp = _mem_layer(xp, kmp.reshape(bp, mt, d), vmp.reshape(bp, mt, d), wq, wo, ln_g[layer, 1], ln_b[layer, 1])
        xs = _mem_layer(xs, cache_mem_k[layer].reshape(bs, -1, d), cache_mem_v[layer].reshape(bs, -1, d), wq, wo,
                        ln_g[layer, 1], ln_b[layer, 1])
        M = _prep_moe(moe_w_group[layer], moe_b_group[layer], moe_w_expert[layer], moe_b_expert[layer],
                      moe_w1[layer], moe_w3[layer], moe_w2[layer])
        xp = _moe_layer_grouped(xp.reshape(bp * tp, d), M, ln_g[layer, 2], ln_b[layer, 2]).reshape(bp, tp, d)
        xs = _moe_layer_grouped(xs.reshape(bs * ts, d), M, ln_g[layer, 2], ln_b[layer, 2]).reshape(bs, ts, d)

    st = {name: jnp.stack(vals) for name, vals in acc.items()}
    return (xp, xs,
            st['p_mlstm_C'], st['p_mlstm_n'], st['p_mlstm_m'],
            st['p_nsa_kc'], st['p_nsa_vc'], st['p_nsa_ks'], st['p_nsa_vs'], st['p_nsa_kw'], st['p_nsa_vw'],
            st['p_fox_k'], st['p_fox_v'], st['p_fox_logf'], st['p_rwkv_S'], st['p_rwkv_shift'],
            st['p_mem_k'], st['p_mem_v'],
            st['s_mlstm_C'], st['s_mlstm_n'], st['s_mlstm_m'],
            st['s_nsa_kc'], st['s_nsa_vc'], st['s_nsa_ks'], st['s_nsa_vs'], st['s_nsa_kw'], st['s_nsa_vw'],
            st['s_fox_k'], st['s_fox_v'], st['s_fox_logf'], st['s_rwkv_S'], st['s_rwkv_shift'])
```
